```python
import math
import jax, jax.numpy as jnp
from jax import lax
import numpy as np

D_MODEL = 1024
BATCH = 2
SEQ = 16384
DEPTH = 2
DEC_BATCH = 8
DEC_SEQ = 64
PAST_LEN = 1024

CHUNK = 64
N_EVEN = (DEPTH + 1) // 2
N_ODD = DEPTH // 2
DIFF_HEADS = 4
DIFF_HEAD_DIM = 64
DIFF_Q_WIDTH = 2 * DIFF_HEADS * DIFF_HEAD_DIM
DIFF_V_WIDTH = DIFF_HEADS * 2 * DIFF_HEAD_DIM
DIFF_SUBLN_EPS = 1e-5
Q_BLOCK = 128
BAND_HEADS = 8
BAND_HEAD_DIM = 64
BAND_WIDTH = BAND_HEADS * BAND_HEAD_DIM
BAND_PREV_CHUNKS = 8
BAND_PAST_MAX = BAND_PREV_CHUNKS * CHUNK
BAND_LEN = BAND_PAST_MAX + CHUNK
REL_CLIP = 256
N_REL = (CHUNK - 1) + REL_CLIP + 1
QKV_WIDTH = 2 * DIFF_Q_WIDTH + DIFF_V_WIDTH + 3 * BAND_WIDTH
MIX_WIDTH = DIFF_V_WIDTH + BAND_WIDTH
GMLP_DIM = 2048
GMLP_GROUPS = 8
GMLP_GROUP_DIM = GMLP_DIM // GMLP_GROUPS
GMLP_CHUNK = 128
D_FF = 2816
N_EXPERTS = 8
TOP_K = 2
RMS_EPS = 1e-6
LN_EPS = 1e-5
MASK_VALUE = -1e30

kernel_name = "hybrid_streaming_encoder_step"


def rmsnorm(x, g, eps=RMS_EPS):
    xf = x.astype(jnp.float32)
    y = xf * lax.rsqrt(jnp.mean(xf * xf, axis=-1, keepdims=True) + eps)
    return (y * g.astype(jnp.float32)).astype(x.dtype)


def layernorm(x, g, b, eps=LN_EPS):
    xf = x.astype(jnp.float32)
    mu = jnp.mean(xf, axis=-1, keepdims=True)
    var = jnp.mean(jnp.square(xf - mu), axis=-1, keepdims=True)
    y = (xf - mu) * lax.rsqrt(var + eps)
    return (y * g.astype(jnp.float32) + b.astype(jnp.float32)).astype(x.dtype)


def alibi_slopes(n):
    return 2.0 ** (-8.0 * jnp.arange(1, n + 1, dtype=jnp.float32) / n)


def diff_lambda(lam_params, lambda_init):
    lp = lam_params.astype(jnp.float32)
    return jnp.exp(jnp.sum(lp[0] * lp[1])) - jnp.exp(jnp.sum(lp[2] * lp[3])) + lambda_init


def split_qkv(h, w_in):
    B, T, _ = h.shape
    proj = h @ w_in
    o1 = DIFF_Q_WIDTH
    o2 = o1 + DIFF_Q_WIDTH
    o3 = o2 + DIFF_V_WIDTH
    o4 = o3 + BAND_WIDTH
    o5 = o4 + BAND_WIDTH
    q_a = proj[..., :o1].reshape(B, T, 2 * DIFF_HEADS, DIFF_HEAD_DIM)
    k_a = proj[..., o1:o2].reshape(B, T, 2 * DIFF_HEADS, DIFF_HEAD_DIM)
    v_a = proj[..., o2:o3].reshape(B, T, DIFF_HEADS, 2 * DIFF_HEAD_DIM)
    q_b = proj[..., o3:o4].reshape(B, T, BAND_HEADS, BAND_HEAD_DIM)
    k_b = proj[..., o4:o5].reshape(B, T, BAND_HEADS, BAND_HEAD_DIM)
    v_b = proj[..., o5:].reshape(B, T, BAND_HEADS, BAND_HEAD_DIM)
    return q_a, k_a, v_a, q_b, k_b, v_b


def diff_attn_block(q, k, v, q_pos, k_pos, lam, subln_g, lambda_init):
    s = jnp.einsum('bthd,bshd->bhts', q, k).astype(jnp.float32) * (DIFF_HEAD_DIM ** -0.5)
    dist = jnp.abs(q_pos[:, None] - k_pos[None, :]).astype(jnp.float32)
    slopes = jnp.repeat(alibi_slopes(DIFF_HEADS), 2)
    s = s - slopes[:, None, None] * dist
    visible = (k_pos[None, :] // CHUNK) <= (q_pos[:, None] // CHUNK)
    s = jnp.where(visible, s, MASK_VALUE)
    p = jax.nn.softmax(s, axis=-1)
    B, _, Tq, S = p.shape
    p = p.reshape(B, DIFF_HEADS, 2, Tq, S)
    a = p[:, :, 0] - lam * p[:, :, 1]
    o = jnp.einsum('bhts,bshe->bthe', a.astype(v.dtype), v)
    o = rmsnorm(o, subln_g, DIFF_SUBLN_EPS) * (1.0 - lambda_init)
    return o.reshape(B, Tq, DIFF_V_WIDTH)


def band_attn_block(q, k, v, q_pos, k_pos, rel_table):
    s = jnp.einsum('bthd,bshd->bhts', q, k).astype(jnp.float32) * (BAND_HEAD_DIM ** -0.5)
    rel = jnp.clip(q_pos[:, None] - k_pos[None, :], -(CHUNK - 1), REL_CLIP) + (CHUNK - 1)
    s = s + rel_table.astype(jnp.float32)[:, rel][None]
    qc = q_pos // CHUNK
    kc = k_pos // CHUNK
    visible = ((k_pos[None, :] >= 0) & (kc[None, :] <= qc[:, None])
               & (qc[:, None] - kc[None, :] <= BAND_PREV_CHUNKS))
    s = jnp.where(visible, s, MASK_VALUE)
    p = jax.nn.softmax(s, axis=-1)
    o = jnp.einsum('bhts,bshd->bthd', p.astype(v.dtype), v)
    return o.reshape(q.shape[0], q.shape[1], BAND_WIDTH)


def attn_mixer_prompt(h, w_in, lam, subln_g, rel_table, w_out, lambda_init):
    q_a, k_a, v_a, q_b, k_b, v_b = split_qkv(h, w_in)
    B, T, _ = h.shape
    pos = jnp.arange(T, dtype=jnp.int32)

    def diff_one(i):
        start = i * Q_BLOCK
        qi = lax.dynamic_slice_in_dim(q_a, start, Q_BLOCK, axis=1)
        qp = start + jnp.arange(Q_BLOCK, dtype=jnp.int32)
        return diff_attn_block(qi, k_a, v_a, qp, pos, lam, subln_g, lambda_init)

    o_a = lax.map(diff_one, jnp.arange(T // Q_BLOCK, dtype=jnp.int32))
    o_a = o_a.transpose(1, 0, 2, 3).reshape(B, T, DIFF_V_WIDTH)

    pad = ((0, 0), (BAND_PAST_MAX, 0), (0, 0), (0, 0))
    kp = jnp.pad(k_b, pad)
    vp = jnp.pad(v_b, pad)

    def band_one(c):
        start = c * CHUNK
        qc = lax.dynamic_slice_in_dim(q_b, start, CHUNK, axis=1)
        kb = lax.dynamic_slice_in_dim(kp, start, BAND_LEN, axis=1)
        vb = lax.dynamic_slice_in_dim(vp, start, BAND_LEN, axis=1)
        qp = start + jnp.arange(CHUNK, dtype=jnp.int32)
        kpos = start - BAND_PAST_MAX + jnp.arange(BAND_LEN, dtype=jnp.int32)
        return band_attn_block(qc, kb, vb, qp, kpos, rel_table)

    o_b = lax.map(band_one, jnp.arange(T // CHUNK, dtype=jnp.int32))
    o_b = o_b.transpose(1, 0, 2, 3).reshape(B, T, BAND_WIDTH)
    out = jnp.concatenate([o_a, o_b], axis=-1) @ w_out
    n_band = min(BAND_PAST_MAX, T)
    return out, k_a, v_a, k_b[:, T - n_band:], v_b[:, T - n_band:]


def attn_mixer_sample(h, ck_a, cv_a, ck_b, cv_b, w_in, lam, subln_g, rel_table, w_out, lambda_init):
    q_a, k_a, v_a, q_b, k_b, v_b = split_qkv(h, w_in)
    T = h.shape[1]
    P = ck_a.shape[1]
    PB = ck_b.shape[1]
    q_pos = P + jnp.arange(T, dtype=jnp.int32)
    k_pos_a = jnp.arange(P + T, dtype=jnp.int32)
    o_a = diff_attn_block(q_a, jnp.concatenate([ck_a, k_a], axis=1), jnp.concatenate([cv_a, v_a], axis=1),
                          q_pos, k_pos_a, lam, subln_g, lambda_init)
    k_pos_b = jnp.concatenate([P - PB + jnp.arange(PB, dtype=jnp.int32), q_pos])
    o_b = band_attn_block(q_b, jnp.concatenate([ck_b, k_b], axis=1), jnp.concatenate([cv_b, v_b], axis=1),
                          q_pos, k_pos_b, rel_table)
    out = jnp.concatenate([o_a, o_b], axis=-1) @ w_out
    return out, k_a, v_a, k_b, v_b


def gmlp_mixer(h, w_in, b_in, ln_g, ln_b, w_s, b_s, w_out):
    B, T, _ = h.shape
    z = jax.nn.gelu(h @ w_in + b_in, approximate=False)
    u, v = jnp.split(z, 2, axis=-1)
    v = layernorm(v, ln_g, ln_b)
    L = min(T, GMLP_CHUNK)
    vg = v.reshape(B, T // L, L, GMLP_GROUPS, GMLP_GROUP_DIM)
    w = jnp.tril(w_s[:, :L, :L])
    sv = jnp.einsum('gts,bnsgc->bntgc', w, vg) + b_s[:, :L].T[:, :, None]
    out = u * sv.reshape(B, T, GMLP_DIM)
    return out @ w_out, v


def swiglu(h, w_gu, w_down):
    g, u = jnp.split(h @ w_gu, 2, axis=-1)
    return (jax.nn.silu(g) * u) @ w_down


def moe_ffn(h, w_router, b_router, w_gu, w_down):
    logits = (h @ w_router + b_router).astype(jnp.float32)
    top_val, top_idx = lax.top_k(logits, TOP_K)
    gates = jax.nn.softmax(top_val, axis=-1)
    out = jnp.zeros_like(h)
    for e in range(N_EXPERTS):
        g_e = jnp.sum(jnp.where(top_idx == e, gates, 0.0), axis=-1, keepdims=True).astype(h.dtype)
        out = out + g_e * swiglu(h, w_gu[e], w_down[e])
    return out


def forward(x, caches, p):
    new_dk, new_dv, new_bk, new_bv, new_gv = [], [], [], [], []
    for layer in range(DEPTH):
        if layer % 2 == 0:
            e = layer // 2
            lambda_init = 0.8 - 0.6 * math.exp(-0.3 * layer)
            lam = diff_lambda(p['diff_lambda'][e], lambda_init)
            h = rmsnorm(x, p['norm_attn'][e])
            if caches is None:
                mix, ka, va, kb, vb = attn_mixer_prompt(
                    h, p['attn_w_in'][e], lam, p['diff_subln'][e], p['band_rel_bias'][e],
                    p['attn_w_out'][e], lambda_init)
            else:
                mix, ka, va, kb, vb = attn_mixer_sample(
                    h, caches[0][e], caches[1][e], caches[2][e], caches[3][e],
                    p['attn_w_in'][e], lam, p['diff_subln'][e], p['band_rel_bias'][e],
                    p['attn_w_out'][e], lambda_init)
            x = x + mix
            x = x + swiglu(rmsnorm(x, p['norm_ffn'][e]), p['ffn_w_gu'][e], p['ffn_w_down'][e])
            new_dk.append(ka)
            new_dv.append(va)
            new_bk.append(kb)
            new_bv.append(vb)
        else:
            o = layer // 2
            mix, v_rows = gmlp_mixer(rmsnorm(x, p['norm_gmlp'][o]), p['gmlp_w_in'][o], p['gmlp_b_in'][o],
                                     p['gmlp_ln_g'][o], p['gmlp_ln_b'][o], p['gmlp_w_s'][o],
                                     p['gmlp_b_s'][o], p['gmlp_w_out'][o])
            x = x + mix
            x = x + moe_ffn(rmsnorm(x, p['norm_moe'][o]), p['moe_w_router'][o], p['moe_b_router'][o],
                            p['moe_w_gu'][o], p['moe_w_down'][o])
            new_gv.append(v_rows)
    y = rmsnorm(x, p['norm_final'])
    return y, jnp.stack(new_dk), jnp.stack(new_dv), jnp.stack(new_bk), jnp.stack(new_bv), jnp.stack(new_gv)


def setup_inputs(seed: int = 0) -> dict:
    key = jax.random.key(seed)
    ks = jax.random.split(key, 29)
    nrm = lambda k, shape, scale: scale * jax.random.normal(k, shape, jnp.float32)
    gain = lambda k, shape: 1.0 + 0.02 * jax.random.normal(k, shape, jnp.float32)
    band_past = min(BAND_PAST_MAX, PAST_LEN)
    return {
        'x_prompt': nrm(ks[0], (BATCH, SEQ, D_MODEL), 1.0),
        'x_sample': nrm(ks[1], (DEC_BATCH, DEC_SEQ, D_MODEL), 1.0),
        'cache_diff_k': nrm(ks[2], (N_EVEN, DEC_BATCH, PAST_LEN, 2 * DIFF_HEADS, DIFF_HEAD_DIM), 1.0),
        'cache_diff_v': nrm(ks[3], (N_EVEN, DEC_BATCH, PAST_LEN, DIFF_HEADS, 2 * DIFF_HEAD_DIM), 1.0),
        'cache_band_k': nrm(ks[4], (N_EVEN, DEC_BATCH, band_past, BAND_HEADS, BAND_HEAD_DIM), 1.0),
        'cache_band_v': nrm(ks[5], (N_EVEN, DEC_BATCH, band_past, BAND_HEADS, BAND_HEAD_DIM), 1.0),
        'norm_attn': gain(ks[6], (N_EVEN, D_MODEL)),
        'attn_w_in': nrm(ks[7], (N_EVEN, D_MODEL, QKV_WIDTH), D_MODEL ** -0.5),
        'diff_lambda': nrm(ks[8], (N_EVEN, 4, DIFF_HEAD_DIM), 0.1),
        'diff_subln': gain(ks[9], (N_EVEN, 2 * DIFF_HEAD_DIM)),
        'band_rel_bias': nrm(ks[10], (N_EVEN, BAND_HEADS, N_REL), 0.5),
        'attn_w_out': nrm(ks[11], (N_EVEN, MIX_WIDTH, D_MODEL), MIX_WIDTH ** -0.5),
        'norm_ffn': gain(ks[12], (N_EVEN, D_MODEL)),
        'ffn_w_gu': nrm(ks[13], (N_EVEN, D_MODEL, 2 * D_FF), D_MODEL ** -0.5),
        'ffn_w_down': nrm(ks[14], (N_EVEN, D_FF, D_MODEL), D_FF ** -0.5),
        'norm_gmlp': gain(ks[15], (N_ODD, D_MODEL)),
        'gmlp_w_in': nrm(ks[16], (N_ODD, D_MODEL, 2 * GMLP_DIM), D_MODEL ** -0.5),
        'gmlp_b_in': nrm(ks[17], (N_ODD, 2 * GMLP_DIM), 0.02),
        'gmlp_ln_g': gain(ks[18], (N_ODD, GMLP_DIM)),
        'gmlp_ln_b': nrm(ks[19], (N_ODD, GMLP_DIM), 0.02),
        'gmlp_w_s': nrm(ks[20], (N_ODD, GMLP_GROUPS, GMLP_CHUNK, GMLP_CHUNK), GMLP_CHUNK ** -0.5),
        'gmlp_b_s': gain(ks[21], (N_ODD, GMLP_GROUPS, GMLP_CHUNK)),
        'gmlp_w_out': nrm(ks[22], (N_ODD, GMLP_DIM, D_MODEL), GMLP_DIM ** -0.5),
        'norm_moe': gain(ks[23], (N_ODD, D_MODEL)),
        'moe_w_router': nrm(ks[24], (N_ODD, D_MODEL, N_EXPERTS), D_MODEL ** -0.5),
        'moe_b_router': nrm(ks[25], (N_ODD, N_EXPERTS), 0.01),
        'moe_w_gu': nrm(ks[26], (N_ODD, N_EXPERTS, D_MODEL, 2 * D_FF), D_MODEL ** -0.5),
        'moe_w_down': nrm(ks[27], (N_ODD, N_EXPERTS, D_FF, D_MODEL), D_FF ** -0.5),
        'norm_final': gain(ks[28], (D_MODEL,)),
    }


def reference(x_prompt, x_sample, cache_diff_k, cache_diff_v, cache_band_k, cache_band_v,
              norm_attn, attn_w_in, diff_lambda, diff_subln, band_rel_bias, attn_w_out,
              norm_ffn, ffn_w_gu, ffn_w_down,
              norm_gmlp, gmlp_w_in, gmlp_b_in, gmlp_ln_g, gmlp_ln_b, gmlp_w_s, gmlp_b_s, gmlp_w_out,
              norm_moe, moe_w_router, moe_b_router, moe_w_gu, moe_w_down, norm_final):
    p = {
        'norm_attn': norm_attn, 'attn_w_in': attn_w_in, 'diff_lambda': diff_lambda,
        'diff_subln': diff_subln, 'band_rel_bias': band_rel_bias, 'attn_w_out': attn_w_out,
        'norm_ffn': norm_ffn, 'ffn_w_gu': ffn_w_gu, 'ffn_w_down': ffn_w_down,
        'norm_gmlp': norm_gmlp, 'gmlp_w_in': gmlp_w_in, 'gmlp_b_in': gmlp_b_in,
        'gmlp_ln_g': gmlp_ln_g, 'gmlp_ln_b': gmlp_ln_b, 'gmlp_w_s': gmlp_w_s, 'gmlp_b_s': gmlp_b_s,
        'gmlp_w_out': gmlp_w_out, 'norm_moe': norm_moe, 'moe_w_router': moe_w_router,
        'moe_b_router': moe_b_router, 'moe_w_gu': moe_w_gu, 'moe_w_down': moe_w_down,
        'norm_final': norm_final,
    }
    y_prompt, dk_p, dv_p, bk_p, bv_p, _gv_p = forward(x_prompt, None, p)
    y_sample, dk_s, dv_s, bk_s, bv_s, gv_s = forward(
        x_sample, (cache_diff_k, cache_diff_v, cache_band_k, cache_band_v), p)
    return (y_prompt, y_sample, dk_p, dv_p, bk_p, bv_p, dk_s, dv_s, bk_s, bv_s, gv_s)
```

```python
import functools
import math

import numpy as np
import jax
import jax.numpy as jnp
from jax import lax
from jax.experimental import pallas as pl
from jax.experimental.pallas import tpu as pltpu

CHUNK = 64
DIFF_HEADS = 4
DIFF_HEAD_DIM = 64
DIFF_SUBLN_EPS = 1e-5
BAND_HEADS = 8
BAND_HEAD_DIM = 64
BAND_PREV_CHUNKS = 8
BAND_PAST_MAX = BAND_PREV_CHUNKS * CHUNK
REL_CLIP = 256
GMLP_GROUPS = 8
GMLP_CHUNK = 128
N_EXPERTS = 8
RMS_EPS = 1e-6
LN_EPS = 1e-5
MASK_VALUE = -1e30

LANES = 128
VMEM_LIMIT_BYTES = 56 << 20

F32 = jnp.float32
BF16 = jnp.bfloat16
NT_DIMS = (((1,), (1,)), ((), ()))


def _params(*sem):
    return pltpu.CompilerParams(dimension_semantics=sem, vmem_limit_bytes=VMEM_LIMIT_BYTES)


def _const_spec(shape):
    nd = len(shape)
    return pl.BlockSpec(shape, lambda *_: (0,) * nd, pipeline_mode=pl.Buffered(1))


def _rmsnorm(x, g, eps):
    return (x * lax.rsqrt(jnp.mean(x * x, axis=-1, keepdims=True) + eps)) * g


def _dot(a, b):
    return jnp.dot(a, b, preferred_element_type=F32)


def _swiglu(h, wgu_ref, wd_ref, d_ff, tf):
    y = None
    for c in range(d_ff // tf):
        g = _dot(h, wgu_ref[:, c * tf:(c + 1) * tf])
        u = _dot(h, wgu_ref[:, d_ff + c * tf:d_ff + (c + 1) * tf])
        a = (g * jax.nn.sigmoid(g) * u).astype(BF16)
        part = _dot(a, wd_ref[c * tf:(c + 1) * tf, :])
        y = part if y is None else y + part
    return y


def _qkv_kernel(x_ref, g_ref, w_ref, pbf_ref, ka_ref, va_ref, kb_ref, vb_ref, *, width):
    h = _rmsnorm(x_ref[...], g_ref[...], RMS_EPS).astype(BF16)
    f32_outs = {1: ka_ref, 2: va_ref, 4: kb_ref, 5: vb_ref}
    for c in range(6):
        r = _dot(h, w_ref[:, c * width:(c + 1) * width])
        if c in f32_outs:
            f32_outs[c][...] = r
        else:
            r = r * (DIFF_HEAD_DIM ** -0.5)
        pbf_ref[:, c * width:(c + 1) * width] = r.astype(BF16)


def _qkv_proj(x2d, gain, w_bf, tm):
    m, d = x2d.shape
    width = w_bf.shape[1] // 6
    row = lambda i: (i, 0)
    return pl.pallas_call(
        functools.partial(_qkv_kernel, width=width),
        grid=(m // tm,),
        in_specs=[pl.BlockSpec((tm, d), row), _const_spec((1, d)), _const_spec(w_bf.shape)],
        out_specs=[pl.BlockSpec((tm, 6 * width), row)] + [pl.BlockSpec((tm, width), row)] * 4,
        out_shape=[jax.ShapeDtypeStruct((m, 6 * width), BF16)]
        + [jax.ShapeDtypeStruct((m, width), F32)] * 4,
        compiler_params=_params("parallel"),
        name="qkv_proj",
    )(x2d, gain.reshape(1, d), w_bf)


def _split_heads_rows(q, tq):
    qf = q.astype(F32)
    lane = lax.broadcasted_iota(jnp.int32, qf.shape, 1)
    lo = jnp.where(lane < DIFF_HEAD_DIM, qf, 0.0)
    hi = jnp.where(lane >= DIFF_HEAD_DIM, qf, 0.0)
    return jnp.concatenate([lo, hi], axis=0).astype(BF16)


def _diff_attn_kernel(ii_ref, jj_ref, last_ref, q_ref, k_ref, v_ref, slope_ref, lam_ref, g_ref,
                      o_ref, qz_ref, m_ref, l_ref, acc_ref, *, tq, tk, q_off, lambda_init):
    s = pl.program_id(2)
    i = ii_ref[s]
    j = jj_ref[s]

    @pl.when(j == 0)
    def _init():
        qz_ref[...] = _split_heads_rows(q_ref[...], tq)
        m_ref[...] = jnp.full(m_ref.shape, MASK_VALUE, F32)
        l_ref[...] = jnp.zeros(l_ref.shape, F32)
        acc_ref[...] = jnp.zeros(acc_ref.shape, F32)

    sc = lax.dot_general(qz_ref[...], k_ref[...], NT_DIMS, preferred_element_type=F32)
    row = lax.broadcasted_iota(jnp.int32, (2 * tq, 1), 0)
    row = jnp.where(row >= tq, row - tq, row)
    qpos = q_off + i * tq + row
    kpos = j * tk + lax.broadcasted_iota(jnp.int32, (1, tk), 1)
    dist = jnp.abs(qpos - kpos).astype(F32)
    sc = sc - slope_ref[:, 0:1] * dist
    visible = (kpos // CHUNK) <= (qpos // CHUNK)
    sc = jnp.where(visible, sc, MASK_VALUE)

    m_prev = m_ref[...]
    m_new = jnp.maximum(m_prev, jnp.max(sc, axis=1, keepdims=True))
    alpha = jnp.exp(m_prev - m_new)
    p = jnp.exp(sc - m_new)
    l_ref[...] = alpha * l_ref[...] + jnp.sum(p, axis=1, keepdims=True)
    acc_ref[...] = alpha * acc_ref[...] + _dot(p.astype(BF16), v_ref[...])
    m_ref[...] = m_new

    @pl.when(last_ref[s] == 1)
    def _finish():
        lp = lam_ref[...]
        lam = (jnp.exp(jnp.sum(lp[0:1] * lp[1:2], axis=1, keepdims=True))
               - jnp.exp(jnp.sum(lp[2:3] * lp[3:4], axis=1, keepdims=True)) + lambda_init)
        o_all = acc_ref[...] / l_ref[...]
        o = o_all[:tq] - lam * o_all[tq:]
        o = _rmsnorm(o, g_ref[...], DIFF_SUBLN_EPS) * (1.0 - lambda_init)
        o_ref[...] = o.astype(o_ref.dtype)


def _diff_attn(q_arr, k_arr, v_arr, q_cb, k_cb, v_cb, lam_p, subln_g, *, tq, tk, q_off, lambda_init):
    b, t_q, _ = q_arr.shape
    t_k = k_arr.shape[1]
    nq, nk = t_q // tq, t_k // tk
    pairs = [(i, j) for i in range(nq) for j in range(nk)
             if (j * tk) // CHUNK <= (q_off + i * tq + tq - 1) // CHUNK]
    ii = np.array([p[0] for p in pairs], np.int32)
    jj = np.array([p[1] for p in pairs], np.int32)
    last = np.array([1 if (n + 1 == len(pairs) or pairs[n + 1][0] != pairs[n][0]) else 0
                     for n in range(len(pairs))], np.int32)
    slopes = 2.0 ** (-8.0 * np.arange(1, DIFF_HEADS + 1, dtype=np.float32) / DIFF_HEADS)
    slopes = jnp.asarray(np.broadcast_to(slopes[:, None, None], (DIFF_HEADS, 1, LANES)).copy())

    grid_spec = pltpu.PrefetchScalarGridSpec(
        num_scalar_prefetch=3,
        grid=(b, DIFF_HEADS, len(pairs)),
        in_specs=[
            pl.BlockSpec((None, tq, LANES), lambda b_, h, s, ii, jj, la: (b_, ii[s], q_cb + h)),
            pl.BlockSpec((None, tk, LANES), lambda b_, h, s, ii, jj, la: (b_, jj[s], k_cb + h)),
            pl.BlockSpec((None, tk, LANES), lambda b_, h, s, ii, jj, la: (b_, jj[s], v_cb + h)),
            pl.BlockSpec((None, 1, LANES), lambda b_, h, s, ii, jj, la: (h, 0, 0)),
            pl.BlockSpec((4, DIFF_HEAD_DIM), lambda b_, h, s, ii, jj, la: (0, 0)),
            pl.BlockSpec((1, LANES), lambda b_, h, s, ii, jj, la: (0, 0)),
        ],
        out_specs=pl.BlockSpec((None, tq, LANES), lambda b_, h, s, ii, jj, la: (b_, ii[s], h)),
        scratch_shapes=[
            pltpu.VMEM((2 * tq, LANES), BF16),
            pltpu.VMEM((2 * tq, 1), F32),
            pltpu.VMEM((2 * tq, 1), F32),
            pltpu.VMEM((2 * tq, LANES), F32),
        ],
    )
    return pl.pallas_call(
        functools.partial(_diff_attn_kernel, tq=tq, tk=tk, q_off=q_off, lambda_init=lambda_init),
        grid_spec=grid_spec,
        out_shape=jax.ShapeDtypeStruct((b, t_q, DIFF_HEADS * LANES), BF16),
        compiler_params=_params("parallel", "parallel", "arbitrary"),
        name="diff_attn",
    )(jnp.asarray(ii), jnp.asarray(jj), jnp.asarray(last), q_arr, k_arr, v_arr, slopes,
      lam_p, subln_g.reshape(1, LANES))


def _band_attn_kernel(*refs, tq, n_parts, tkp):
    q_ref = refs[0]
    k_refs = refs[1:1 + n_parts]
    v_refs = refs[1 + n_parts:1 + 2 * n_parts]
    bias_ref = refs[1 + 2 * n_parts]
    o_ref = refs[2 + 2 * n_parts]
    i = pl.program_id(2)
    qz = _split_heads_rows(q_ref[...], tq)
    scores = []
    for m in range(n_parts):
        sc = lax.dot_general(qz, k_refs[m][...], NT_DIMS, preferred_element_type=F32)
        sc = sc + bias_ref[:, m * tkp:(m + 1) * tkp]
        if n_parts > 1:
            sc = jnp.where(i - (n_parts - 1) + m >= 0, sc, MASK_VALUE)
        scores.append(sc)
    mx = functools.reduce(jnp.maximum, [jnp.max(sc, axis=1, keepdims=True) for sc in scores])
    den = None
    num = None
    for m in range(n_parts):
        p = jnp.exp(scores[m] - mx)
        d = jnp.sum(p, axis=1, keepdims=True)
        r = _dot(p.astype(BF16), v_refs[m][...])
        den = d if den is None else den + d
        num = r if num is None else num + r
    r = num / den
    lane = lax.broadcasted_iota(jnp.int32, (tq, LANES), 1)
    o_ref[...] = jnp.where(lane < BAND_HEAD_DIM, r[:tq], r[tq:]).astype(o_ref.dtype)


def _band_bias(rel_table, q_pos, k_pos, tq):
    d = q_pos[:, None] - k_pos[None, :]
    rel = np.clip(d, -(CHUNK - 1), REL_CLIP) + (CHUNK - 1)
    qc, kc = q_pos // CHUNK, k_pos // CHUNK
    visible = (kc[None, :] <= qc[:, None]) & (qc[:, None] - kc[None, :] <= BAND_PREV_CHUNKS)
    bias = jnp.where(jnp.asarray(visible)[None], rel_table.astype(F32)[:, jnp.asarray(rel)], MASK_VALUE)
    return bias.reshape(BAND_HEADS // 2, 2 * tq, k_pos.shape[0])


def _band_attn(q_arr, k_arr, v_arr, q_cb, k_cb, v_cb, bias, *, tq, n_parts, tkp):
    b, t_q, _ = q_arr.shape
    n_pairs = BAND_HEADS // 2

    def kv_spec(cb, m):
        return pl.BlockSpec((None, tkp, LANES),
                            lambda p, b_, i: (b_, jnp.maximum(i - (n_parts - 1) + m, 0), cb + p))

    return pl.pallas_call(
        functools.partial(_band_attn_kernel, tq=tq, n_parts=n_parts, tkp=tkp),
        grid=(n_pairs, b, t_q // tq),
        in_specs=[pl.BlockSpec((None, tq, LANES), lambda p, b_, i: (b_, i, q_cb + p))]
        + [kv_spec(k_cb, m) for m in range(n_parts)]
        + [kv_spec(v_cb, m) for m in range(n_parts)]
        + [pl.BlockSpec((None, 2 * tq, n_parts * tkp), lambda p, b_, i: (p, 0, 0))],
        out_specs=pl.BlockSpec((None, tq, LANES), lambda p, b_, i: (b_, i, p)),
        out_shape=jax.ShapeDtypeStruct((b, t_q, n_pairs * LANES), BF16),
        compiler_params=_params("parallel", "parallel", "parallel"),
        name="band_attn",
    )(q_arr, *([k_arr] * n_parts), *([v_arr] * n_parts), bias)


def _attn_out_ffn_kernel(oa_ref, ob_ref, x_ref, wo_ref, g_ref, wgu_ref, wd_ref, o_ref, *, d_ff, tf):
    half = oa_ref.shape[1]
    x1 = x_ref[...] + _dot(oa_ref[...], wo_ref[0:half, :]) + _dot(ob_ref[...], wo_ref[half:2 * half, :])
    h = _rmsnorm(x1, g_ref[...], RMS_EPS).astype(BF16)
    o_ref[...] = x1 + _swiglu(h, wgu_ref, wd_ref, d_ff, tf)


def _attn_out_ffn(oa, ob, x2d, wo_bf, gain, wgu_bf, wd_bf, tm):
    m, d = x2d.shape
    d_ff = wd_bf.shape[0]
    row = lambda i: (i, 0)
    return pl.pallas_call(
        functools.partial(_attn_out_ffn_kernel, d_ff=d_ff, tf=256),
        grid=(m // tm,),
        in_specs=[pl.BlockSpec((tm, oa.shape[1]), row), pl.BlockSpec((tm, ob.shape[1]), row),
                  pl.BlockSpec((tm, d), row), _const_spec(wo_bf.shape), _const_spec((1, d)),
                  _const_spec(wgu_bf.shape), _const_spec(wd_bf.shape)],
        out_specs=pl.BlockSpec((tm, d), row),
        out_shape=jax.ShapeDtypeStruct((m, d), F32),
        compiler_params=_params("parallel"),
        name="attn_out_ffn",
    )(oa, ob, x2d, wo_bf, gain.reshape(1, d), wgu_bf, wd_bf)


def _gelu(z):
    return 0.5 * z * (1.0 + lax.erf(z * (2.0 ** -0.5)))


def _gmlp_kernel(x_ref, g_ref, win_ref, bin_ref, lng_ref, lnb_ref, ws_ref, bs_ref, wout_ref,
                 o_ref, v_ref, u_s, act_s, *, seg, cw):
    tm, gd = v_ref.shape
    gw = gd // GMLP_GROUPS
    x = x_ref[...]
    h = _rmsnorm(x, g_ref[...], RMS_EPS).astype(BF16)
    for c in range(gd // cw):
        lo, hi = c * cw, (c + 1) * cw
        u_s[:, lo:hi] = _gelu(_dot(h, win_ref[:, lo:hi]) + bin_ref[:, lo:hi])
        v_ref[:, lo:hi] = _gelu(_dot(h, win_ref[:, gd + lo:gd + hi]) + bin_ref[:, gd + lo:gd + hi])
    v = v_ref[...]
    mu = jnp.mean(v, axis=-1, keepdims=True)
    var = jnp.mean(jnp.square(v - mu), axis=-1, keepdims=True)
    v_ref[...] = (v - mu) * lax.rsqrt(var + LN_EPS) * lng_ref[...] + lnb_ref[...]
    r_i = lax.broadcasted_iota(jnp.int32, (seg, seg), 0)
    c_i = lax.broadcasted_iota(jnp.int32, (seg, seg), 1)
    for g in range(GMLP_GROUPS):
        w = jnp.where(r_i >= c_i, ws_ref[g], 0.0).astype(BF16)
        for n in range(tm // seg):
            rows = slice(n * seg, (n + 1) * seg)
            cols = slice(g * gw, (g + 1) * gw)
            sv = _dot(w, v_ref[rows, cols].astype(BF16)) + bs_ref[g]
            act_s[rows, cols] = (u_s[rows, cols] * sv).astype(BF16)
    o_ref[...] = x + _dot(act_s[...], wout_ref[...])


def _gmlp(x2d, gain, win_bf, b_in, ln_g, ln_b, w_s, b_s, wout_bf, tm, seg):
    m, d = x2d.shape
    gd = wout_bf.shape[0]
    row = lambda i: (i, 0)
    ws = w_s[:, :seg, :seg]
    bs = b_s[:, :seg, None]
    return pl.pallas_call(
        functools.partial(_gmlp_kernel, seg=seg, cw=512),
        grid=(m // tm,),
        in_specs=[pl.BlockSpec((tm, d), row), _const_spec((1, d)), _const_spec(win_bf.shape),
                  _const_spec((1, 2 * gd)), _const_spec((1, gd)), _const_spec((1, gd)),
                  _const_spec(ws.shape), _const_spec(bs.shape), _const_spec(wout_bf.shape)],
        out_specs=[pl.BlockSpec((tm, d), row), pl.BlockSpec((tm, gd), row)],
        out_shape=[jax.ShapeDtypeStruct((m, d), F32), jax.ShapeDtypeStruct((m, gd), F32)],
        scratch_shapes=[pltpu.VMEM((tm, gd), F32), pltpu.VMEM((tm, gd), BF16)],
        compiler_params=_params("parallel"),
        name="gmlp",
    )(x2d, gain.reshape(1, d), win_bf, b_in.reshape(1, 2 * gd), ln_g.reshape(1, gd),
      ln_b.reshape(1, gd), ws, bs, wout_bf)


def _router_kernel(x_ref, g_ref, whi_ref, wlo_ref, b_ref, h_ref, gate_ref):
    h = _rmsnorm(x_ref[...], g_ref[...], RMS_EPS)
    h_hi = h.astype(BF16)
    h_lo = (h - h_hi.astype(F32)).astype(BF16)
    logits = (_dot(h_hi, whi_ref[...]) + _dot(h_hi, wlo_ref[...]) + _dot(h_lo, whi_ref[...])
              + b_ref[...])
    lane = lax.broadcasted_iota(jnp.int32, logits.shape, 1)
    logits = jnp.where(lane < N_EXPERTS, logits, -jnp.inf)
    v1 = jnp.max(logits, axis=1, keepdims=True)
    i1 = jnp.min(jnp.where(logits == v1, lane, LANES), axis=1, keepdims=True)
    rest = jnp.where(lane == i1, -jnp.inf, logits)
    v2 = jnp.max(rest, axis=1, keepdims=True)
    i2 = jnp.min(jnp.where(rest == v2, lane, LANES), axis=1, keepdims=True)
    e2 = jnp.exp(v2 - v1)
    den = 1.0 + e2
    gate_ref[...] = jnp.where(lane == i1, 1.0 / den, 0.0) + jnp.where(lane == i2, e2 / den, 0.0)
    h_ref[...] = h_hi


def _router(x2d, gain, w_router, b_router, tm):
    m, d = x2d.shape
    w_pad = jnp.pad(w_router, ((0, 0), (0, LANES - N_EXPERTS)))
    w_hi = w_pad.astype(BF16)
    w_lo = (w_pad - w_hi.astype(F32)).astype(BF16)
    b_pad = jnp.pad(b_router, (0, LANES - N_EXPERTS)).reshape(1, LANES)
    row = lambda i: (i, 0)
    return pl.pallas_call(
        _router_kernel,
        grid=(m // tm,),
        in_specs=[pl.BlockSpec((tm, d), row), _const_spec((1, d)), _const_spec((d, LANES)),
                  _const_spec((d, LANES)), _const_spec((1, LANES))],
        out_specs=[pl.BlockSpec((tm, d), row), pl.BlockSpec((tm, LANES), row)],
        out_shape=[jax.ShapeDtypeStruct((m, d), BF16), jax.ShapeDtypeStruct((m, LANES), F32)],
        compiler_params=_params("parallel"),
        name="moe_router",
    )(x2d, gain.reshape(1, d), w_hi, w_lo, b_pad)


def _moe_kernel(h_ref, gate_ref, x_ref, wgu_ref, wd_ref, gf_ref, o_ref, acc_ref, *, d_ff, tf):
    e = pl.program_id(1)

    @pl.when(e == 0)
    def _init():
        acc_ref[...] = jnp.zeros(acc_ref.shape, F32)

    gate = gate_ref[...]
    lane = lax.broadcasted_iota(jnp.int32, gate.shape, 1)
    g_e = jnp.sum(jnp.where(lane == e, gate, 0.0), axis=1, keepdims=True)
    acc_ref[...] += g_e * _swiglu(h_ref[...], wgu_ref, wd_ref, d_ff, tf)

    @pl.when(e == pl.num_programs(1) - 1)
    def _finish():
        o_ref[...] = _rmsnorm(x_ref[...] + acc_ref[...], gf_ref[...], RMS_EPS)


def _moe(h_bf, gates, x2d, wgu_bf, wd_bf, gain_final, tm):
    m, d = x2d.shape
    n_e, d_ff, _ = wd_bf.shape
    row = lambda i, e: (i, 0)
    return pl.pallas_call(
        functools.partial(_moe_kernel, d_ff=d_ff, tf=256),
        grid=(m // tm, n_e),
        in_specs=[pl.BlockSpec((tm, d), row), pl.BlockSpec((tm, LANES), row), pl.BlockSpec((tm, d), row),
                  pl.BlockSpec((None, d, 2 * d_ff), lambda i, e: (e, 0, 0)),
                  pl.BlockSpec((None, d_ff, d), lambda i, e: (e, 0, 0)),
                  pl.BlockSpec((1, d), lambda i, e: (0, 0))],
        out_specs=pl.BlockSpec((tm, d), row),
        out_shape=jax.ShapeDtypeStruct((m, d), F32),
        scratch_shapes=[pltpu.VMEM((tm, d), F32)],
        compiler_params=_params("parallel", "arbitrary"),
        name="moe_experts",
    )(h_bf, gates, x2d, wgu_bf, wd_bf, gain_final.reshape(1, d))


def _forward(x, caches, w):
    b, t, d = x.shape
    m = b * t
    tm = min(512, m)
    x2d = x.reshape(m, d)
    lambda_init = 0.8 - 0.6 * math.exp(-0.3 * 0)
    width = DIFF_HEADS * 2 * DIFF_HEAD_DIM
    ncb = width // LANES

    pbf, ka, va, kb, vb = _qkv_proj(x2d, w["norm_attn"], w["attn_w_in"], tm)
    pbf3 = pbf.reshape(b, t, 6 * width)
    if caches is None:
        tile = min(512, t)
        oa = _diff_attn(pbf3, pbf3, pbf3, 0, ncb, 2 * ncb, w["diff_lambda"], w["diff_subln"],
                        tq=tile, tk=tile, q_off=0, lambda_init=lambda_init)
        tqb = min(256, t)
        n_parts = BAND_PAST_MAX // tqb + 1
        q_pos = np.arange(tqb)
        k_pos = np.arange(n_parts * tqb) - BAND_PAST_MAX
        bias = _band_bias(w["band_rel_bias"], q_pos, k_pos, tqb)
        ob = _band_attn(pbf3, pbf3, pbf3, 3 * ncb, 4 * ncb, 5 * ncb, bias,
                        tq=tqb, n_parts=n_parts, tkp=tqb)
        n_band = min(BAND_PAST_MAX, t)
        new_bk = kb.reshape(b, t, BAND_HEADS, BAND_HEAD_DIM)[:, t - n_band:]
        new_bv = vb.reshape(b, t, BAND_HEADS, BAND_HEAD_DIM)[:, t - n_band:]
    else:
        ck_a, cv_a, ck_b, cv_b = caches
        p_len, pb_len = ck_a.shape[1], ck_b.shape[1]
        cat = lambda c, lo: jnp.concatenate(
            [c.reshape(b, c.shape[1], width).astype(BF16), pbf3[:, :, lo:lo + width]], axis=1)
        k_a, v_a = cat(ck_a, width), cat(cv_a, 2 * width)
        k_b, v_b = cat(ck_b, 4 * width), cat(cv_b, 5 * width)
        oa = _diff_attn(pbf3, k_a, v_a, 0, 0, 0, w["diff_lambda"], w["diff_subln"],
                        tq=t, tk=p_len + t, q_off=p_len, lambda_init=lambda_init)
        q_pos = p_len + np.arange(t)
        k_pos = np.concatenate([p_len - pb_len + np.arange(pb_len), q_pos])
        bias = _band_bias(w["band_rel_bias"], q_pos, k_pos, t)
        ob = _band_attn(pbf3, k_b, v_b, 3 * ncb, 0, 0, bias, tq=t, n_parts=1, tkp=pb_len + t)
        new_bk = kb.reshape(b, t, BAND_HEADS, BAND_HEAD_DIM)
        new_bv = vb.reshape(b, t, BAND_HEADS, BAND_HEAD_DIM)
    new_dk = ka.reshape(b, t, 2 * DIFF_HEADS, DIFF_HEAD_DIM)
    new_dv = va.reshape(b, t, DIFF_HEADS, 2 * DIFF_HEAD_DIM)

    x2d = _attn_out_ffn(oa.reshape(m, width), ob.reshape(m, width), x2d, w["attn_w_out"],
                        w["norm_ffn"], w["ffn_w_gu"], w["ffn_w_down"], tm)

    seg = min(t, GMLP_CHUNK)
    x2d, v_rows = _gmlp(x2d, w["norm_gmlp"], w["gmlp_w_in"], w["gmlp_b_in"], w["gmlp_ln_g"],
                        w["gmlp_ln_b"], w["gmlp_w_s"], w["gmlp_b_s"], w["gmlp_w_out"], tm, seg)
    h_bf, gates = _router(x2d, w["norm_moe"], w["moe_w_router"], w["moe_b_router"], tm)
    y = _moe(h_bf, gates, x2d, w["moe_w_gu"], w["moe_w_down"], w["norm_final"], tm)
    return (y.reshape(b, t, d), new_dk[None], new_dv[None], new_bk[None], new_bv[None],
            v_rows.reshape(b, t, -1)[None])


def kernel(x_prompt, x_sample, cache_diff_k, cache_diff_v, cache_band_k, cache_band_v,
           norm_attn, attn_w_in, diff_lambda, diff_subln, band_rel_bias, attn_w_out,
           norm_ffn, ffn_w_gu, ffn_w_down,
           norm_gmlp, gmlp_w_in, gmlp_b_in, gmlp_ln_g, gmlp_ln_b, gmlp_w_s, gmlp_b_s, gmlp_w_out,
           norm_moe, moe_w_router, moe_b_router, moe_w_gu, moe_w_down, norm_final):
    w = {
        "norm_attn": norm_attn[0], "attn_w_in": attn_w_in[0].astype(BF16),
        "diff_lambda": diff_lambda[0], "diff_subln": diff_subln[0],
        "band_rel_bias": band_rel_bias[0], "attn_w_out": attn_w_out[0].astype(BF16),
        "norm_ffn": norm_ffn[0], "ffn_w_gu": ffn_w_gu[0].astype(BF16),
        "ffn_w_down": ffn_w_down[0].astype(BF16),
        "norm_gmlp": norm_gmlp[0], "gmlp_w_in": gmlp_w_in[0].astype(BF16),
        "gmlp_b_in": gmlp_b_in[0], "gmlp_ln_g": gmlp_ln_g[0], "gmlp_ln_b": gmlp_ln_b[0],
        "gmlp_w_s": gmlp_w_s[0], "gmlp_b_s": gmlp_b_s[0], "gmlp_w_out": gmlp_w_out[0].astype(BF16),
        "norm_moe": norm_moe[0], "moe_w_router": moe_w_router[0], "moe_b_router": moe_b_router[0],
        "moe_w_gu": moe_w_gu[0].astype(BF16), "moe_w_down": moe_w_down[0].astype(BF16),
        "norm_final": norm_final,
    }
    y_p, dk_p, dv_p, bk_p, bv_p, _ = _forward(x_prompt, None, w)
    y_s, dk_s, dv_s, bk_s, bv_s, gv_s = _forward(
        x_sample, (cache_diff_k[0], cache_diff_v[0], cache_band_k[0], cache_band_v[0]), w)
    return (y_p, y_s, dk_p, dv_p, bk_p, bv_p, dk_s, dv_s, bk_s, bv_s, gv_s)
```

```python
import functools
import math

import numpy as np
import jax
import jax.numpy as jnp
from jax import lax
from jax.experimental import pallas as pl
from jax.experimental.pallas import tpu as pltpu

CHUNK = 64
DIFF_HEADS = 4
DIFF_HEAD_DIM = 64
DIFF_SUBLN_EPS = 1e-5
BAND_HEADS = 8
BAND_HEAD_DIM = 64
BAND_PREV_CHUNKS = 8
BAND_PAST_MAX = BAND_PREV_CHUNKS * CHUNK
REL_CLIP = 256
GMLP_GROUPS = 8
GMLP_CHUNK = 128
N_EXPERTS = 8
RMS_EPS = 1e-6
LN_EPS = 1e-5
MASK_VALUE = -1e30

LANES = 128
VMEM_LIMIT_BYTES = 56 << 20

F32 = jnp.float32
BF16 = jnp.bfloat16
LOG2E = math.log2(math.e)
Q_SCALE = DIFF_HEAD_DIM ** -0.5 * LOG2E
NT_DIMS = (((1,), (1,)), ((), ()))


def _params(*sem):
    return pltpu.CompilerParams(dimension_semantics=sem, vmem_limit_bytes=VMEM_LIMIT_BYTES)


def _const_spec(shape):
    nd = len(shape)
    return pl.BlockSpec(shape, lambda *_: (0,) * nd, pipeline_mode=pl.Buffered(1))


def _rmsnorm(x, g, eps):
    return (x * lax.rsqrt(jnp.mean(x * x, axis=-1, keepdims=True) + eps)) * g


def _dot(a, b):
    return jnp.dot(a, b, preferred_element_type=F32)


def _swiglu(h, wgu_ref, wd_ref, d_ff, tf):
    y = None
    for c in range(d_ff // tf):
        g = _dot(h, wgu_ref[:, c * tf:(c + 1) * tf])
        u = _dot(h, wgu_ref[:, d_ff + c * tf:d_ff + (c + 1) * tf])
        a = (g * jax.nn.sigmoid(g) * u).astype(BF16)
        part = _dot(a, wd_ref[c * tf:(c + 1) * tf, :])
        y = part if y is None else y + part
    return y


def _qkv_kernel(x_ref, g_ref, w_ref, *rest, width, with_vt, tiles_per_seq):
    if with_vt:
        wvt_ref, pbf_ref, ka_ref, va_ref, kb_ref, vb_ref, vt_ref, kaug_ref = rest
    else:
        pbf_ref, ka_ref, va_ref, kb_ref, vb_ref = rest
    tm = x_ref.shape[0]
    h = _rmsnorm(x_ref[...], g_ref[...], RMS_EPS).astype(BF16)
    f32_outs = {1: ka_ref, 2: va_ref, 4: kb_ref, 5: vb_ref}
    for c in range(6):
        r = _dot(h, w_ref[:, c * width:(c + 1) * width])
        if c in f32_outs:
            f32_outs[c][...] = r
        else:
            r = r * Q_SCALE
        pbf_ref[:, c * width:(c + 1) * width] = r.astype(BF16)
        if with_vt and c == 1:
            pos = (pl.program_id(0) % tiles_per_seq) * tm + lax.broadcasted_iota(jnp.int32, (tm, LANES), 0)
            lane = lax.broadcasted_iota(jnp.int32, (tm, LANES), 1)
            hi = (pos // LANES).astype(F32)
            lo = (pos % LANES).astype(F32)
            feats = jnp.where(lane < 3, hi, jnp.where(lane < 6, lo, 0.0)).astype(BF16)
            for hd in range(DIFF_HEADS):
                kaug_ref[:, 2 * hd * LANES:(2 * hd + 1) * LANES] = r[:, hd * LANES:(hd + 1) * LANES].astype(BF16)
                kaug_ref[:, (2 * hd + 1) * LANES:(2 * hd + 2) * LANES] = feats
    if with_vt:
        vt_ref[...] = lax.dot_general(wvt_ref[...], h, NT_DIMS, preferred_element_type=F32).astype(BF16)


def _qkv_proj(x2d, gain, w_bf, tm, wvt_bf=None, seq=None):
    m, d = x2d.shape
    width = w_bf.shape[1] // 6
    row = lambda i: (i, 0)
    with_vt = wvt_bf is not None
    in_specs = [pl.BlockSpec((tm, d), row), _const_spec((1, d)), _const_spec(w_bf.shape)]
    out_specs = [pl.BlockSpec((tm, 6 * width), row)] + [pl.BlockSpec((tm, width), row)] * 4
    out_shape = [jax.ShapeDtypeStruct((m, 6 * width), BF16)] + [jax.ShapeDtypeStruct((m, width), F32)] * 4
    args = [x2d, gain.reshape(1, d), w_bf]
    nt = 1
    if with_vt:
        nt = seq // tm
        assert seq <= 256 * LANES
        in_specs.append(_const_spec(wvt_bf.shape))
        out_specs.append(pl.BlockSpec((None, width, tm), lambda i: (i // nt, 0, i % nt)))
        out_shape.append(jax.ShapeDtypeStruct((m // seq, width, seq), BF16))
        out_specs.append(pl.BlockSpec((tm, 2 * width), row))
        out_shape.append(jax.ShapeDtypeStruct((m, 2 * width), BF16))
        args.append(wvt_bf)
    return pl.pallas_call(
        functools.partial(_qkv_kernel, width=width, with_vt=with_vt, tiles_per_seq=nt),
        grid=(m // tm,),
        in_specs=in_specs, out_specs=out_specs, out_shape=out_shape,
        compiler_params=_params("parallel"),
        name="qkv_proj",
    )(*args)


def _split_heads_rows(q, tq):
    qf = q.astype(F32)
    lane = lax.broadcasted_iota(jnp.int32, qf.shape, 1)
    lo = jnp.where(lane < DIFF_HEAD_DIM, qf, 0.0)
    hi = jnp.where(lane >= DIFF_HEAD_DIM, qf, 0.0)
    return jnp.concatenate([lo, hi], axis=0).astype(BF16)


def _diff_attn_kernel(ii_ref, jj_ref, last_ref, q_ref, k_ref, v_ref, slope_ref, lam_ref, g_ref,
                      o_ref, qz_ref, m_ref, l_ref, acc_ref, *, tq, tk, q_off, lambda_init):
    s = pl.program_id(2)
    i = ii_ref[s]
    j = jj_ref[s]

    @pl.when(j == 0)
    def _init():
        qz_ref[...] = _split_heads_rows(q_ref[...], tq)
        m_ref[...] = jnp.full(m_ref.shape, MASK_VALUE, F32)
        l_ref[...] = jnp.zeros(l_ref.shape, F32)
        acc_ref[...] = jnp.zeros(acc_ref.shape, F32)

    sc = lax.dot_general(qz_ref[...], k_ref[...], NT_DIMS, preferred_element_type=F32)
    row = lax.broadcasted_iota(jnp.int32, (2 * tq, 1), 0)
    row = jnp.where(row >= tq, row - tq, row)
    qpos = q_off + i * tq + row
    kpos = j * tk + lax.broadcasted_iota(jnp.int32, (1, tk), 1)
    dist = jnp.abs(qpos - kpos).astype(F32)
    sc = sc - slope_ref[:, 0:1] * dist
    visible = (kpos // CHUNK) <= (qpos // CHUNK)
    sc = jnp.where(visible, sc, MASK_VALUE)

    m_prev = m_ref[...]
    m_new = jnp.maximum(m_prev, jnp.max(sc, axis=1, keepdims=True))
    alpha = jnp.exp2(m_prev - m_new)
    p = jnp.exp2(sc - m_new)
    l_ref[...] = alpha * l_ref[...] + jnp.sum(p, axis=1, keepdims=True)
    acc_ref[...] = alpha * acc_ref[...] + _dot(p.astype(BF16), v_ref[...])
    m_ref[...] = m_new

    @pl.when(last_ref[s] == 1)
    def _finish():
        lp = lam_ref[...]
        lam = (jnp.exp(jnp.sum(lp[0:1] * lp[1:2], axis=1, keepdims=True))
               - jnp.exp(jnp.sum(lp[2:3] * lp[3:4], axis=1, keepdims=True)) + lambda_init)
        o_all = acc_ref[...] / l_ref[...]
        o = o_all[:tq] - lam * o_all[tq:]
        o = _rmsnorm(o, g_ref[...], DIFF_SUBLN_EPS) * (1.0 - lambda_init)
        o_ref[...] = o.astype(o_ref.dtype)


def _diff_attn(q_arr, k_arr, v_arr, q_cb, k_cb, v_cb, lam_p, subln_g, *, tq, tk, q_off, lambda_init):
    b, t_q, _ = q_arr.shape
    t_k = k_arr.shape[1]
    nq, nk = t_q // tq, t_k // tk
    pairs = [(i, j) for i in range(nq) for j in range(nk)
             if (j * tk) // CHUNK <= (q_off + i * tq + tq - 1) // CHUNK]
    ii = np.array([p[0] for p in pairs], np.int32)
    jj = np.array([p[1] for p in pairs], np.int32)
    last = np.array([1 if (n + 1 == len(pairs) or pairs[n + 1][0] != pairs[n][0]) else 0
                     for n in range(len(pairs))], np.int32)
    slopes = jnp.asarray(_alibi_slopes_log2()[:, None, None] * np.ones((1, 1, LANES), np.float32))

    grid_spec = pltpu.PrefetchScalarGridSpec(
        num_scalar_prefetch=3,
        grid=(b, DIFF_HEADS, len(pairs)),
        in_specs=[
            pl.BlockSpec((None, tq, LANES), lambda b_, h, s, ii, jj, la: (b_, ii[s], q_cb + h)),
            pl.BlockSpec((None, tk, LANES), lambda b_, h, s, ii, jj, la: (b_, jj[s], k_cb + h)),
            pl.BlockSpec((None, tk, LANES), lambda b_, h, s, ii, jj, la: (b_, jj[s], v_cb + h)),
            pl.BlockSpec((None, 1, LANES), lambda b_, h, s, ii, jj, la: (h, 0, 0)),
            pl.BlockSpec((4, DIFF_HEAD_DIM), lambda b_, h, s, ii, jj, la: (0, 0)),
            pl.BlockSpec((1, LANES), lambda b_, h, s, ii, jj, la: (0, 0)),
        ],
        out_specs=pl.BlockSpec((None, tq, LANES), lambda b_, h, s, ii, jj, la: (b_, ii[s], h)),
        scratch_shapes=[
            pltpu.VMEM((2 * tq, LANES), BF16),
            pltpu.VMEM((2 * tq, 1), F32),
            pltpu.VMEM((2 * tq, 1), F32),
            pltpu.VMEM((2 * tq, LANES), F32),
        ],
    )
    return pl.pallas_call(
        functools.partial(_diff_attn_kernel, tq=tq, tk=tk, q_off=q_off, lambda_init=lambda_init),
        grid_spec=grid_spec,
        out_shape=jax.ShapeDtypeStruct((b, t_q, DIFF_HEADS * LANES), BF16),
        compiler_params=_params("parallel", "parallel", "arbitrary"),
        name="diff_attn",
    )(jnp.asarray(ii), jnp.asarray(jj), jnp.asarray(last), q_arr, k_arr, v_arr, slopes,
      lam_p, subln_g.reshape(1, LANES))


def _alibi_slopes_log2():
    return (2.0 ** (-8.0 * np.arange(1, DIFF_HEADS + 1, dtype=np.float64) / DIFF_HEADS) * LOG2E).astype(np.float32)


def _diff_prompt_kernel(ii_ref, jj_ref, last_ref, lin_ref, q_ref, k_ref, vt_ref, dbias_ref, qfeat_ref,
                        lam_ref, g_ref, o_ref, qzt_ref, m_ref, l_ref, acc_ref,
                        *, tq, tk, cb, lambda_init):
    s = pl.program_id(2)
    j = jj_ref[s]

    @pl.when(j == 0)
    def _init():
        qf = q_ref[...].astype(F32)
        lane = lax.broadcasted_iota(jnp.int32, qf.shape, 1)
        qzt_ref[0:LANES, 0:tq] = jnp.where(lane < DIFF_HEAD_DIM, qf, 0.0).T.astype(BF16)
        qzt_ref[0:LANES, tq:2 * tq] = jnp.where(lane >= DIFF_HEAD_DIM, qf, 0.0).T.astype(BF16)
        qzt_ref[LANES:2 * LANES, :] = jnp.broadcast_to(qfeat_ref[...], (LANES, 2 * tq)).astype(BF16)
        m_ref[...] = jnp.full(m_ref.shape, MASK_VALUE, F32)
        l_ref[...] = jnp.zeros(l_ref.shape, F32)
        acc_ref[...] = jnp.zeros(acc_ref.shape, F32)

    def step(bias_of_block):
        k = k_ref[...]
        vt = vt_ref[...]
        nblk = 2 * tq // cb
        blk = lambda c: slice(c * cb, (c + 1) * cb)
        scs = [_dot(k, qzt_ref[:, blk(c)]) for c in range(nblk)]
        if bias_of_block is not None:
            scs = [scs[c] + bias_of_block(c) for c in range(nblk)]
        m_prev = m_ref[...]
        m_new = jnp.maximum(m_prev, jnp.concatenate(
            [jnp.max(sc, axis=0, keepdims=True) for sc in scs], axis=1))
        alpha = jnp.exp2(m_prev - m_new)
        ps = [jnp.exp2(scs[c] - m_new[:, blk(c)]) for c in range(nblk)]
        l_ref[...] = alpha * l_ref[...] + jnp.concatenate(
            [jnp.sum(p, axis=0, keepdims=True) for p in ps], axis=1)
        pv = jnp.concatenate([_dot(vt, p.astype(BF16)) for p in ps], axis=1)
        acc_ref[...] = alpha * acc_ref[...] + pv
        m_ref[...] = m_new

    @pl.when(lin_ref[s] == 1)
    def _earlier_chunks():
        step(None)

    @pl.when(lin_ref[s] == 0)
    def _diagonal():
        step(lambda c: dbias_ref[:, (c * cb) % tq:(c * cb) % tq + cb])

    @pl.when(last_ref[s] == 1)
    def _finish():
        lp = lam_ref[...]
        lam = (jnp.exp(jnp.sum(lp[0:1] * lp[1:2], axis=1, keepdims=True))
               - jnp.exp(jnp.sum(lp[2:3] * lp[3:4], axis=1, keepdims=True)) + lambda_init)
        o_all = acc_ref[...] / l_ref[...]
        o = o_all[:, 0:tq] - lam * o_all[:, tq:2 * tq]
        o = o * lax.rsqrt(jnp.mean(o * o, axis=0, keepdims=True) + DIFF_SUBLN_EPS)
        o = o * g_ref[...] * (1.0 - lambda_init)
        o_ref[...] = o.T.astype(o_ref.dtype)


def _bf16_split3(x):
    import ml_dtypes
    rnd = lambda v: v.astype(ml_dtypes.bfloat16).astype(np.float32)
    x = np.asarray(x, np.float32)
    hi = rnd(x)
    mid = rnd(x - hi)
    lo = rnd(x - hi - mid)
    return hi, mid, lo


def _diff_attn_prompt(pbf3, kaug3, vt_arr, q_cb, lam_p, subln_g, *, tile, lambda_init):
    b, t, _ = pbf3.shape
    tq = tk = tile
    n = t // tile
    pairs = [(i, j) for i in range(n) for j in range(i + 1)]
    ii = np.array([p[0] for p in pairs], np.int32)
    jj = np.array([p[1] for p in pairs], np.int32)
    last = (ii == jj).astype(np.int32)
    lin = (ii != jj).astype(np.int32)
    c2 = _alibi_slopes_log2()
    qfeat = np.zeros((DIFF_HEADS, LANES, 1), np.float32)
    for r, part in enumerate(_bf16_split3(c2)):
        qfeat[:, r, 0] = part * LANES
        qfeat[:, 3 + r, 0] = part
    pos = np.arange(tile)
    d = (pos[None, :] - pos[:, None]).astype(np.float32)
    visible = (pos[:, None] // CHUNK) <= (pos[None, :] // CHUNK)
    dbias = jnp.asarray(np.where(visible[None], 2.0 * c2[:, None, None] * np.minimum(d, 0.0)[None],
                                 MASK_VALUE).astype(np.float32))
    cb = min(256, 2 * tq)
    idx = lambda f: (lambda b_, h, s, ii, jj, la, li: f(b_, h, s, ii, jj))
    grid_spec = pltpu.PrefetchScalarGridSpec(
        num_scalar_prefetch=4,
        grid=(b, DIFF_HEADS, len(pairs)),
        in_specs=[
            pl.BlockSpec((None, tq, LANES), idx(lambda b_, h, s, ii, jj: (b_, ii[s], q_cb + h))),
            pl.BlockSpec((None, tk, 2 * LANES), idx(lambda b_, h, s, ii, jj: (b_, jj[s], h))),
            pl.BlockSpec((None, LANES, tk), idx(lambda b_, h, s, ii, jj: (b_, h, jj[s]))),
            pl.BlockSpec((None, tk, tq), idx(lambda b_, h, s, ii, jj: (h, 0, 0))),
            pl.BlockSpec((None, LANES, 1), idx(lambda b_, h, s, ii, jj: (h, 0, 0))),
            pl.BlockSpec((4, DIFF_HEAD_DIM), idx(lambda b_, h, s, ii, jj: (0, 0))),
            pl.BlockSpec((LANES, 1), idx(lambda b_, h, s, ii, jj: (0, 0))),
        ],
        out_specs=pl.BlockSpec((None, tq, LANES), idx(lambda b_, h, s, ii, jj: (b_, ii[s], h))),
        scratch_shapes=[
            pltpu.VMEM((2 * LANES, 2 * tq), BF16),
            pltpu.VMEM((1, 2 * tq), F32),
            pltpu.VMEM((1, 2 * tq), F32),
            pltpu.VMEM((LANES, 2 * tq), F32),
        ],
    )
    return pl.pallas_call(
        functools.partial(_diff_prompt_kernel, tq=tq, tk=tk, cb=cb, lambda_init=lambda_init),
        grid_spec=grid_spec,
        out_shape=jax.ShapeDtypeStruct((b, t, DIFF_HEADS * LANES), BF16),
        compiler_params=_params("parallel", "parallel", "arbitrary"),
        name="diff_attn_prompt",
    )(jnp.asarray(ii), jnp.asarray(jj), jnp.asarray(last), jnp.asarray(lin), pbf3, kaug3, vt_arr, dbias,
      jnp.asarray(qfeat), lam_p, subln_g.reshape(LANES, 1))


def _band_attn_kernel(*refs, tq, n_parts, tkp):
    q_ref = refs[0]
    k_refs = refs[1:1 + n_parts]
    v_refs = refs[1 + n_parts:1 + 2 * n_parts]
    bias_ref = refs[1 + 2 * n_parts]
    o_ref = refs[2 + 2 * n_parts]
    i = pl.program_id(2)
    qz = _split_heads_rows(q_ref[...], tq)
    scores = []
    for m in range(n_parts):
        sc = lax.dot_general(qz, k_refs[m][...], NT_DIMS, preferred_element_type=F32)
        sc = sc + bias_ref[:, m * tkp:(m + 1) * tkp]
        if n_parts > 1:
            sc = jnp.where(i - (n_parts - 1) + m >= 0, sc, MASK_VALUE)
        scores.append(sc)
    mx = functools.reduce(jnp.maximum, [jnp.max(sc, axis=1, keepdims=True) for sc in scores])
    den = None
    num = None
    for m in range(n_parts):
        p = jnp.exp2(scores[m] - mx)
        d = jnp.sum(p, axis=1, keepdims=True)
        r = _dot(p.astype(BF16), v_refs[m][...])
        den = d if den is None else den + d
        num = r if num is None else num + r
    r = num / den
    lane = lax.broadcasted_iota(jnp.int32, (tq, LANES), 1)
    o_ref[...] = jnp.where(lane < BAND_HEAD_DIM, r[:tq], r[tq:]).astype(o_ref.dtype)


def _band_bias(rel_table, q_pos, k_pos, tq):
    d = q_pos[:, None] - k_pos[None, :]
    rel = np.clip(d, -(CHUNK - 1), REL_CLIP) + (CHUNK - 1)
    qc, kc = q_pos // CHUNK, k_pos // CHUNK
    visible = (kc[None, :] <= qc[:, None]) & (qc[:, None] - kc[None, :] <= BAND_PREV_CHUNKS)
    table = rel_table.astype(F32) * LOG2E
    bias = jnp.where(jnp.asarray(visible)[None], table[:, jnp.asarray(rel)], MASK_VALUE)
    return bias.reshape(BAND_HEADS // 2, 2 * tq, k_pos.shape[0])


def _band_attn(q_arr, k_arr, v_arr, q_cb, k_cb, v_cb, bias, *, tq, n_parts, tkp):
    b, t_q, _ = q_arr.shape
    n_pairs = BAND_HEADS // 2

    def kv_spec(cb, m):
        return pl.BlockSpec((None, tkp, LANES),
                            lambda p, b_, i: (b_, jnp.maximum(i - (n_parts - 1) + m, 0), cb + p))

    return pl.pallas_call(
        functools.partial(_band_attn_kernel, tq=tq, n_parts=n_parts, tkp=tkp),
        grid=(n_pairs, b, t_q // tq),
        in_specs=[pl.BlockSpec((None, tq, LANES), lambda p, b_, i: (b_, i, q_cb + p))]
        + [kv_spec(k_cb, m) for m in range(n_parts)]
        + [kv_spec(v_cb, m) for m in range(n_parts)]
        + [pl.BlockSpec((None, 2 * tq, n_parts * tkp), lambda p, b_, i: (p, 0, 0))],
        out_specs=pl.BlockSpec((None, tq, LANES), lambda p, b_, i: (b_, i, p)),
        out_shape=jax.ShapeDtypeStruct((b, t_q, n_pairs * LANES), BF16),
        compiler_params=_params("parallel", "parallel", "parallel"),
        name="band_attn",
    )(q_arr, *([k_arr] * n_parts), *([v_arr] * n_parts), bias)


def _attn_out_ffn_kernel(oa_ref, ob_ref, x_ref, wo_ref, g_ref, wgu_ref, wd_ref, o_ref, *, d_ff, tf):
    half = oa_ref.shape[1]
    x1 = x_ref[...] + _dot(oa_ref[...], wo_ref[0:half, :]) + _dot(ob_ref[...], wo_ref[half:2 * half, :])
    h = _rmsnorm(x1, g_ref[...], RMS_EPS).astype(BF16)
    o_ref[...] = x1 + _swiglu(h, wgu_ref, wd_ref, d_ff, tf)


def _attn_out_ffn(oa, ob, x2d, wo_bf, gain, wgu_bf, wd_bf, tm):
    m, d = x2d.shape
    d_ff = wd_bf.shape[0]
    row = lambda i: (i, 0)
    return pl.pallas_call(
        functools.partial(_attn_out_ffn_kernel, d_ff=d_ff, tf=256),
        grid=(m // tm,),
        in_specs=[pl.BlockSpec((tm, oa.shape[1]), row), pl.BlockSpec((tm, ob.shape[1]), row),
                  pl.BlockSpec((tm, d), row), _const_spec(wo_bf.shape), _const_spec((1, d)),
                  _const_spec(wgu_bf.shape), _const_spec(wd_bf.shape)],
        out_specs=pl.BlockSpec((tm, d), row),
        out_shape=jax.ShapeDtypeStruct((m, d), F32),
        compiler_params=_params("parallel"),
        name="attn_out_ffn",
    )(oa, ob, x2d, wo_bf, gain.reshape(1, d), wgu_bf, wd_bf)


def _gelu(z):
    return 0.5 * z * (1.0 + lax.erf(z * (2.0 ** -0.5)))


def _gmlp_kernel(x_ref, g_ref, win_ref, bin_ref, lng_ref, lnb_ref, ws_ref, bs_ref, wout_ref,
                 o_ref, v_ref, u_s, act_s, *, seg, cw):
    tm, gd = v_ref.shape
    gw = gd // GMLP_GROUPS
    x = x_ref[...]
    h = _rmsnorm(x, g_ref[...], RMS_EPS).astype(BF16)
    for c in range(gd // cw):
        lo, hi = c * cw, (c + 1) * cw
        u_s[:, lo:hi] = _gelu(_dot(h, win_ref[:, lo:hi]) + bin_ref[:, lo:hi])
        v_ref[:, lo:hi] = _gelu(_dot(h, win_ref[:, gd + lo:gd + hi]) + bin_ref[:, gd + lo:gd + hi])
    v = v_ref[...]
    mu = jnp.mean(v, axis=-1, keepdims=True)
    var = jnp.mean(jnp.square(v - mu), axis=-1, keepdims=True)
    v_ref[...] = (v - mu) * lax.rsqrt(var + LN_EPS) * lng_ref[...] + lnb_ref[...]
    r_i = lax.broadcasted_iota(jnp.int32, (seg, seg), 0)
    c_i = lax.broadcasted_iota(jnp.int32, (seg, seg), 1)
    for g in range(GMLP_GROUPS):
        w = jnp.where(r_i >= c_i, ws_ref[g], 0.0).astype(BF16)
        for n in range(tm // seg):
            rows = slice(n * seg, (n + 1) * seg)
            cols = slice(g * gw, (g + 1) * gw)
            sv = _dot(w, v_ref[rows, cols].astype(BF16)) + bs_ref[g]
            act_s[rows, cols] = (u_s[rows, cols] * sv).astype(BF16)
    o_ref[...] = x + _dot(act_s[...], wout_ref[...])


def _gmlp(x2d, gain, win_bf, b_in, ln_g, ln_b, w_s, b_s, wout_bf, tm, seg):
    m, d = x2d.shape
    gd = wout_bf.shape[0]
    row = lambda i: (i, 0)
    ws = w_s[:, :seg, :seg]
    bs = b_s[:, :seg, None]
    return pl.pallas_call(
        functools.partial(_gmlp_kernel, seg=seg, cw=512),
        grid=(m // tm,),
        in_specs=[pl.BlockSpec((tm, d), row), _const_spec((1, d)), _const_spec(win_bf.shape),
                  _const_spec((1, 2 * gd)), _const_spec((1, gd)), _const_spec((1, gd)),
                  _const_spec(ws.shape), _const_spec(bs.shape), _const_spec(wout_bf.shape)],
        out_specs=[pl.BlockSpec((tm, d), row), pl.BlockSpec((tm, gd), row)],
        out_shape=[jax.ShapeDtypeStruct((m, d), F32), jax.ShapeDtypeStruct((m, gd), F32)],
        scratch_shapes=[pltpu.VMEM((tm, gd), F32), pltpu.VMEM((tm, gd), BF16)],
        compiler_params=_params("parallel"),
        name="gmlp",
    )(x2d, gain.reshape(1, d), win_bf, b_in.reshape(1, 2 * gd), ln_g.reshape(1, gd),
      ln_b.reshape(1, gd), ws, bs, wout_bf)


def _router_kernel(x_ref, g_ref, whi_ref, wlo_ref, b_ref, h_ref, gate_ref):
    h = _rmsnorm(x_ref[...], g_ref[...], RMS_EPS)
    h_hi = h.astype(BF16)
    h_lo = (h - h_hi.astype(F32)).astype(BF16)
    logits = (_dot(h_hi, whi_ref[...]) + _dot(h_hi, wlo_ref[...]) + _dot(h_lo, whi_ref[...])
              + b_ref[...])
    lane = lax.broadcasted_iota(jnp.int32, logits.shape, 1)
    logits = jnp.where(lane < N_EXPERTS, logits, -jnp.inf)
    v1 = jnp.max(logits, axis=1, keepdims=True)
    i1 = jnp.min(jnp.where(logits == v1, lane, LANES), axis=1, keepdims=True)
    rest = jnp.where(lane == i1, -jnp.inf, logits)
    v2 = jnp.max(rest, axis=1, keepdims=True)
    i2 = jnp.min(jnp.where(rest == v2, lane, LANES), axis=1, keepdims=True)
    e2 = jnp.exp(v2 - v1)
    den = 1.0 + e2
    gate_ref[...] = jnp.where(lane == i1, 1.0 / den, 0.0) + jnp.where(lane == i2, e2 / den, 0.0)
    h_ref[...] = h_hi


def _router(x2d, gain, w_router, b_router, tm):
    m, d = x2d.shape
    w_pad = jnp.pad(w_router, ((0, 0), (0, LANES - N_EXPERTS)))
    w_hi = w_pad.astype(BF16)
    w_lo = (w_pad - w_hi.astype(F32)).astype(BF16)
    b_pad = jnp.pad(b_router, (0, LANES - N_EXPERTS)).reshape(1, LANES)
    row = lambda i: (i, 0)
    return pl.pallas_call(
        _router_kernel,
        grid=(m // tm,),
        in_specs=[pl.BlockSpec((tm, d), row), _const_spec((1, d)), _const_spec((d, LANES)),
                  _const_spec((d, LANES)), _const_spec((1, LANES))],
        out_specs=[pl.BlockSpec((tm, d), row), pl.BlockSpec((tm, LANES), row)],
        out_shape=[jax.ShapeDtypeStruct((m, d), BF16), jax.ShapeDtypeStruct((m, LANES), F32)],
        compiler_params=_params("parallel"),
        name="moe_router",
    )(x2d, gain.reshape(1, d), w_hi, w_lo, b_pad)


def _moe_kernel(h_ref, gate_ref, x_ref, wgu_ref, wd_ref, gf_ref, o_ref, acc_ref, *, d_ff, tf):
    e = pl.program_id(1)

    @pl.when(e == 0)
    def _init():
        acc_ref[...] = jnp.zeros(acc_ref.shape, F32)

    gate = gate_ref[...]
    lane = lax.broadcasted_iota(jnp.int32, gate.shape, 1)
    g_e = jnp.sum(jnp.where(lane == e, gate, 0.0), axis=1, keepdims=True)
    acc_ref[...] += g_e * _swiglu(h_ref[...], wgu_ref, wd_ref, d_ff, tf)

    @pl.when(e == pl.num_programs(1) - 1)
    def _finish():
        o_ref[...] = _rmsnorm(x_ref[...] + acc_ref[...], gf_ref[...], RMS_EPS)


def _moe(h_bf, gates, x2d, wgu_bf, wd_bf, gain_final, tm):
    m, d = x2d.shape
    n_e, d_ff, _ = wd_bf.shape
    row = lambda i, e: (i, 0)
    return pl.pallas_call(
        functools.partial(_moe_kernel, d_ff=d_ff, tf=256),
        grid=(m // tm, n_e),
        in_specs=[pl.BlockSpec((tm, d), row), pl.BlockSpec((tm, LANES), row), pl.BlockSpec((tm, d), row),
                  pl.BlockSpec((None, d, 2 * d_ff), lambda i, e: (e, 0, 0)),
                  pl.BlockSpec((None, d_ff, d), lambda i, e: (e, 0, 0)),
                  pl.BlockSpec((1, d), lambda i, e: (0, 0))],
        out_specs=pl.BlockSpec((tm, d), row),
        out_shape=jax.ShapeDtypeStruct((m, d), F32),
        scratch_shapes=[pltpu.VMEM((tm, d), F32)],
        compiler_params=_params("parallel", "arbitrary"),
        name="moe_experts",
    )(h_bf, gates, x2d, wgu_bf, wd_bf, gain_final.reshape(1, d))


def _forward(x, caches, w):
    b, t, d = x.shape
    m = b * t
    tm = min(512, m)
    x2d = x.reshape(m, d)
    lambda_init = 0.8 - 0.6 * math.exp(-0.3 * 0)
    width = DIFF_HEADS * 2 * DIFF_HEAD_DIM
    ncb = width // LANES

    if caches is None:
        pbf, ka, va, kb, vb, vt, kaug = _qkv_proj(x2d, w["norm_attn"], w["attn_w_in"], tm,
                                                  w["attn_w_vt"], t)
    else:
        pbf, ka, va, kb, vb = _qkv_proj(x2d, w["norm_attn"], w["attn_w_in"], tm)
    pbf3 = pbf.reshape(b, t, 6 * width)
    if caches is None:
        oa = _diff_attn_prompt(pbf3, kaug.reshape(b, t, 2 * width), vt, 0, w["diff_lambda"],
                               w["diff_subln"], tile=min(1024, t), lambda_init=lambda_init)
        tqb = min(256, t)
        n_parts = BAND_PAST_MAX // tqb + 1
        q_pos = np.arange(tqb)
        k_pos = np.arange(n_parts * tqb) - BAND_PAST_MAX
        bias = _band_bias(w["band_rel_bias"], q_pos, k_pos, tqb)
        ob = _band_attn(pbf3, pbf3, pbf3, 3 * ncb, 4 * ncb, 5 * ncb, bias,
                        tq=tqb, n_parts=n_parts, tkp=tqb)
        n_band = min(BAND_PAST_MAX, t)
        new_bk = kb.reshape(b, t, BAND_HEADS, BAND_HEAD_DIM)[:, t - n_band:]
        new_bv = vb.reshape(b, t, BAND_HEADS, BAND_HEAD_DIM)[:, t - n_band:]
    else:
        ck_a, cv_a, ck_b, cv_b = caches
        p_len, pb_len = ck_a.shape[1], ck_b.shape[1]
        cat = lambda c, lo: jnp.concatenate(
            [c.reshape(b, c.shape[1], width).astype(BF16), pbf3[:, :, lo:lo + width]], axis=1)
        k_a, v_a = cat(ck_a, width), cat(cv_a, 2 * width)
        k_b, v_b = cat(ck_b, 4 * width), cat(cv_b, 5 * width)
        oa = _diff_attn(pbf3, k_a, v_a, 0, 0, 0, w["diff_lambda"], w["diff_subln"],
                        tq=t, tk=p_len + t, q_off=p_len, lambda_init=lambda_init)
        q_pos = p_len + np.arange(t)
        k_pos = np.concatenate([p_len - pb_len + np.arange(pb_len), q_pos])
        bias = _band_bias(w["band_rel_bias"], q_pos, k_pos, t)
        ob = _band_attn(pbf3, k_b, v_b, 3 * ncb, 0, 0, bias, tq=t, n_parts=1, tkp=pb_len + t)
        new_bk = kb.reshape(b, t, BAND_HEADS, BAND_HEAD_DIM)
        new_bv = vb.reshape(b, t, BAND_HEADS, BAND_HEAD_DIM)
    new_dk = ka.reshape(b, t, 2 * DIFF_HEADS, DIFF_HEAD_DIM)
    new_dv = va.reshape(b, t, DIFF_HEADS, 2 * DIFF_HEAD_DIM)

    x2d = _attn_out_ffn(oa.reshape(m, width), ob.reshape(m, width), x2d, w["attn_w_out"],
                        w["norm_ffn"], w["ffn_w_gu"], w["ffn_w_down"], tm)

    seg = min(t, GMLP_CHUNK)
    x2d, v_rows = _gmlp(x2d, w["norm_gmlp"], w["gmlp_w_in"], w["gmlp_b_in"], w["gmlp_ln_g"],
                        w["gmlp_ln_b"], w["gmlp_w_s"], w["gmlp_b_s"], w["gmlp_w_out"], tm, seg)
    h_bf, gates = _router(x2d, w["norm_moe"], w["moe_w_router"], w["moe_b_router"], tm)
    y = _moe(h_bf, gates, x2d, w["moe_w_gu"], w["moe_w_down"], w["norm_final"], tm)
    return (y.reshape(b, t, d), new_dk[None], new_dv[None], new_bk[None], new_bv[None],
            v_rows.reshape(b, t, -1)[None])


def kernel(x_prompt, x_sample, cache_diff_k, cache_diff_v, cache_band_k, cache_band_v,
           norm_attn, attn_w_in, diff_lambda, diff_subln, band_rel_bias, attn_w_out,
           norm_ffn, ffn_w_gu, ffn_w_down,
           norm_gmlp, gmlp_w_in, gmlp_b_in, gmlp_ln_g, gmlp_ln_b, gmlp_w_s, gmlp_b_s, gmlp_w_out,
           norm_moe, moe_w_router, moe_b_router, moe_w_gu, moe_w_down, norm_final):
    w = {
        "norm_attn": norm_attn[0], "attn_w_in": attn_w_in[0].astype(BF16),
        "attn_w_vt": attn_w_in[0][:, 1024:1536].T.astype(BF16),
        "diff_lambda": diff_lambda[0], "diff_subln": diff_subln[0],
        "band_rel_bias": band_rel_bias[0], "attn_w_out": attn_w_out[0].astype(BF16),
        "norm_ffn": norm_ffn[0], "ffn_w_gu": ffn_w_gu[0].astype(BF16),
        "ffn_w_down": ffn_w_down[0].astype(BF16),
        "norm_gmlp": norm_gmlp[0], "gmlp_w_in": gmlp_w_in[0].astype(BF16),
        "gmlp_b_in": gmlp_b_in[0], "gmlp_ln_g": gmlp_ln_g[0], "gmlp_ln_b": gmlp_ln_b[0],
        "gmlp_w_s": gmlp_w_s[0], "gmlp_b_s": gmlp_b_s[0], "gmlp_w_out": gmlp_w_out[0].astype(BF16),
        "norm_moe": norm_moe[0], "moe_w_router": moe_w_router[0], "moe_b_router": moe_b_router[0],
        "moe_w_gu": moe_w_gu[0].astype(BF16), "moe_w_down": moe_w_down[0].astype(BF16),
        "norm_final": norm_final,
    }
    y_p, dk_p, dv_p, bk_p, bv_p, _ = _forward(x_prompt, None, w)
    y_s, dk_s, dv_s, bk_s, bv_s, gv_s = _forward(
        x_sample, (cache_diff_k[0], cache_diff_v[0], cache_band_k[0], cache_band_v[0]), w)
    return (y_p, y_s, dk_p, dv_p, bk_p, bv_p, dk_s, dv_s, bk_s, bv_s, gv_s)
```

```python
import functools
import math

import numpy as np
import jax
import jax.numpy as jnp
from jax import lax
from jax.experimental import pallas as pl
from jax.experimental.pallas import tpu as pltpu

CHUNK = 64
DIFF_HEADS = 4
DIFF_HEAD_DIM = 64
DIFF_SUBLN_EPS = 1e-5
BAND_HEADS = 8
BAND_HEAD_DIM = 64
BAND_PREV_CHUNKS = 8
BAND_PAST_MAX = BAND_PREV_CHUNKS * CHUNK
REL_CLIP = 256
GMLP_GROUPS = 8
GMLP_CHUNK = 128
N_EXPERTS = 8
RMS_EPS = 1e-6
LN_EPS = 1e-5
MASK_VALUE = -1e30

LANES = 128
VMEM_LIMIT_BYTES = 56 << 20

F32 = jnp.float32
BF16 = jnp.bfloat16
LOG2E = math.log2(math.e)
Q_SCALE = DIFF_HEAD_DIM ** -0.5 * LOG2E
NT_DIMS = (((1,), (1,)), ((), ()))


def _params(*sem):
    return pltpu.CompilerParams(dimension_semantics=sem, vmem_limit_bytes=VMEM_LIMIT_BYTES)


def _const_spec(shape):
    nd = len(shape)
    return pl.BlockSpec(shape, lambda *_: (0,) * nd, pipeline_mode=pl.Buffered(1))


def _rmsnorm(x, g, eps):
    return (x * lax.rsqrt(jnp.mean(x * x, axis=-1, keepdims=True) + eps)) * g


def _dot(a, b):
    return jnp.dot(a, b, preferred_element_type=F32)


def _swiglu(h, wgu_ref, wd_ref, d_ff, tf):
    y = None
    for c in range(d_ff // tf):
        g = _dot(h, wgu_ref[:, c * tf:(c + 1) * tf])
        u = _dot(h, wgu_ref[:, d_ff + c * tf:d_ff + (c + 1) * tf])
        a = (g * jax.nn.sigmoid(g) * u).astype(BF16)
        part = _dot(a, wd_ref[c * tf:(c + 1) * tf, :])
        y = part if y is None else y + part
    return y


def _qkv_kernel(x_ref, g_ref, w_ref, *rest, width, with_vt, tiles_per_seq):
    if with_vt:
        wvt_ref, pbf_ref, ka_ref, va_ref, kb_ref, vb_ref, vt_ref, kaug_ref = rest
    else:
        pbf_ref, ka_ref, va_ref, kb_ref, vb_ref = rest
    tm = x_ref.shape[0]
    h = _rmsnorm(x_ref[...], g_ref[...], RMS_EPS).astype(BF16)
    f32_outs = {1: ka_ref, 2: va_ref, 4: kb_ref, 5: vb_ref}
    for c in range(6):
        r = _dot(h, w_ref[:, c * width:(c + 1) * width])
        if c in f32_outs:
            f32_outs[c][...] = r
        else:
            r = r * Q_SCALE
        pbf_ref[:, c * width:(c + 1) * width] = r.astype(BF16)
        if with_vt and c == 1:
            pos = (pl.program_id(0) % tiles_per_seq) * tm + lax.broadcasted_iota(jnp.int32, (tm, LANES), 0)
            lane = lax.broadcasted_iota(jnp.int32, (tm, LANES), 1)
            hi = (pos // LANES).astype(F32)
            lo = (pos % LANES).astype(F32)
            feats = jnp.where(lane < 3, hi, jnp.where(lane < 6, lo, 0.0)).astype(BF16)
            for hd in range(DIFF_HEADS):
                kaug_ref[:, 2 * hd * LANES:(2 * hd + 1) * LANES] = r[:, hd * LANES:(hd + 1) * LANES].astype(BF16)
                kaug_ref[:, (2 * hd + 1) * LANES:(2 * hd + 2) * LANES] = feats
    if with_vt:
        vt_ref[...] = lax.dot_general(wvt_ref[...], h, NT_DIMS, preferred_element_type=F32).astype(BF16)


def _qkv_proj(x2d, gain, w_bf, tm, wvt_bf=None, seq=None):
    m, d = x2d.shape
    width = w_bf.shape[1] // 6
    row = lambda i: (i, 0)
    with_vt = wvt_bf is not None
    in_specs = [pl.BlockSpec((tm, d), row), _const_spec((1, d)), _const_spec(w_bf.shape)]
    out_specs = [pl.BlockSpec((tm, 6 * width), row)] + [pl.BlockSpec((tm, width), row)] * 4
    out_shape = [jax.ShapeDtypeStruct((m, 6 * width), BF16)] + [jax.ShapeDtypeStruct((m, width), F32)] * 4
    args = [x2d, gain.reshape(1, d), w_bf]
    nt = 1
    if with_vt:
        nt = seq // tm
        assert seq <= 256 * LANES
        in_specs.append(_const_spec(wvt_bf.shape))
        out_specs.append(pl.BlockSpec((None, width, tm), lambda i: (i // nt, 0, i % nt)))
        out_shape.append(jax.ShapeDtypeStruct((m // seq, width, seq), BF16))
        out_specs.append(pl.BlockSpec((tm, 2 * width), row))
        out_shape.append(jax.ShapeDtypeStruct((m, 2 * width), BF16))
        args.append(wvt_bf)
    return pl.pallas_call(
        functools.partial(_qkv_kernel, width=width, with_vt=with_vt, tiles_per_seq=nt),
        grid=(m // tm,),
        in_specs=in_specs, out_specs=out_specs, out_shape=out_shape,
        compiler_params=_params("parallel"),
        name="qkv_proj",
    )(*args)


def _split_heads_rows(q, tq):
    qf = q.astype(F32)
    lane = lax.broadcasted_iota(jnp.int32, qf.shape, 1)
    lo = jnp.where(lane < DIFF_HEAD_DIM, qf, 0.0)
    hi = jnp.where(lane >= DIFF_HEAD_DIM, qf, 0.0)
    return jnp.concatenate([lo, hi], axis=0).astype(BF16)


def _diff_attn_kernel(ii_ref, jj_ref, last_ref, q_ref, k_ref, v_ref, slope_ref, lam_ref, g_ref,
                      o_ref, qz_ref, m_ref, l_ref, acc_ref, *, tq, tk, q_off, lambda_init):
    s = pl.program_id(2)
    i = ii_ref[s]
    j = jj_ref[s]

    @pl.when(j == 0)
    def _init():
        qz_ref[...] = _split_heads_rows(q_ref[...], tq)
        m_ref[...] = jnp.full(m_ref.shape, MASK_VALUE, F32)
        l_ref[...] = jnp.zeros(l_ref.shape, F32)
        acc_ref[...] = jnp.zeros(acc_ref.shape, F32)

    sc = lax.dot_general(qz_ref[...], k_ref[...], NT_DIMS, preferred_element_type=F32)
    row = lax.broadcasted_iota(jnp.int32, (2 * tq, 1), 0)
    row = jnp.where(row >= tq, row - tq, row)
    qpos = q_off + i * tq + row
    kpos = j * tk + lax.broadcasted_iota(jnp.int32, (1, tk), 1)
    dist = jnp.abs(qpos - kpos).astype(F32)
    sc = sc - slope_ref[:, 0:1] * dist
    visible = (kpos // CHUNK) <= (qpos // CHUNK)
    sc = jnp.where(visible, sc, MASK_VALUE)

    m_prev = m_ref[...]
    m_new = jnp.maximum(m_prev, jnp.max(sc, axis=1, keepdims=True))
    alpha = jnp.exp2(m_prev - m_new)
    p = jnp.exp2(sc - m_new)
    l_ref[...] = alpha * l_ref[...] + jnp.sum(p, axis=1, keepdims=True)
    acc_ref[...] = alpha * acc_ref[...] + _dot(p.astype(BF16), v_ref[...])
    m_ref[...] = m_new

    @pl.when(last_ref[s] == 1)
    def _finish():
        lp = lam_ref[...]
        lam = (jnp.exp(jnp.sum(lp[0:1] * lp[1:2], axis=1, keepdims=True))
               - jnp.exp(jnp.sum(lp[2:3] * lp[3:4], axis=1, keepdims=True)) + lambda_init)
        o_all = acc_ref[...] / l_ref[...]
        o = o_all[:tq] - lam * o_all[tq:]
        o = _rmsnorm(o, g_ref[...], DIFF_SUBLN_EPS) * (1.0 - lambda_init)
        o_ref[...] = o.astype(o_ref.dtype)


def _diff_attn(q_arr, k_arr, v_arr, q_cb, k_cb, v_cb, lam_p, subln_g, *, tq, tk, q_off, lambda_init):
    b, t_q, _ = q_arr.shape
    t_k = k_arr.shape[1]
    nq, nk = t_q // tq, t_k // tk
    pairs = [(i, j) for i in range(nq) for j in range(nk)
             if (j * tk) // CHUNK <= (q_off + i * tq + tq - 1) // CHUNK]
    ii = np.array([p[0] for p in pairs], np.int32)
    jj = np.array([p[1] for p in pairs], np.int32)
    last = np.array([1 if (n + 1 == len(pairs) or pairs[n + 1][0] != pairs[n][0]) else 0
                     for n in range(len(pairs))], np.int32)
    slopes = jnp.asarray(_alibi_slopes_log2()[:, None, None] * np.ones((1, 1, LANES), np.float32))

    grid_spec = pltpu.PrefetchScalarGridSpec(
        num_scalar_prefetch=3,
        grid=(b, DIFF_HEADS, len(pairs)),
        in_specs=[
            pl.BlockSpec((None, tq, LANES), lambda b_, h, s, ii, jj, la: (b_, ii[s], q_cb + h)),
            pl.BlockSpec((None, tk, LANES), lambda b_, h, s, ii, jj, la: (b_, jj[s], k_cb + h)),
            pl.BlockSpec((None, tk, LANES), lambda b_, h, s, ii, jj, la: (b_, jj[s], v_cb + h)),
            pl.BlockSpec((None, 1, LANES), lambda b_, h, s, ii, jj, la: (h, 0, 0)),
            pl.BlockSpec((4, DIFF_HEAD_DIM), lambda b_, h, s, ii, jj, la: (0, 0)),
            pl.BlockSpec((1, LANES), lambda b_, h, s, ii, jj, la: (0, 0)),
        ],
        out_specs=pl.BlockSpec((None, tq, LANES), lambda b_, h, s, ii, jj, la: (b_, ii[s], h)),
        scratch_shapes=[
            pltpu.VMEM((2 * tq, LANES), BF16),
            pltpu.VMEM((2 * tq, 1), F32),
            pltpu.VMEM((2 * tq, 1), F32),
            pltpu.VMEM((2 * tq, LANES), F32),
        ],
    )
    return pl.pallas_call(
        functools.partial(_diff_attn_kernel, tq=tq, tk=tk, q_off=q_off, lambda_init=lambda_init),
        grid_spec=grid_spec,
        out_shape=jax.ShapeDtypeStruct((b, t_q, DIFF_HEADS * LANES), BF16),
        compiler_params=_params("parallel", "parallel", "arbitrary"),
        name="diff_attn",
    )(jnp.asarray(ii), jnp.asarray(jj), jnp.asarray(last), q_arr, k_arr, v_arr, slopes,
      lam_p, subln_g.reshape(1, LANES))


def _alibi_slopes_log2():
    return (2.0 ** (-8.0 * np.arange(1, DIFF_HEADS + 1, dtype=np.float64) / DIFF_HEADS) * LOG2E).astype(np.float32)


def _diff_prompt_kernel(ii_ref, jj_ref, last_ref, lin_ref, q_ref, k_ref, vt_ref, dbias_ref, qfeat_ref,
                        lam_ref, g_ref, o_ref, qzt_ref, m_ref, l_ref, acc_ref,
                        *, tq, tk, cb, lambda_init):
    s = pl.program_id(2)
    j = jj_ref[s]

    @pl.when(j == 0)
    def _init():
        qf = q_ref[...].astype(F32)
        lane = lax.broadcasted_iota(jnp.int32, qf.shape, 1)
        qzt_ref[0:LANES, 0:tq] = jnp.where(lane < DIFF_HEAD_DIM, qf, 0.0).T.astype(BF16)
        qzt_ref[0:LANES, tq:2 * tq] = jnp.where(lane >= DIFF_HEAD_DIM, qf, 0.0).T.astype(BF16)
        qzt_ref[LANES:2 * LANES, :] = jnp.broadcast_to(qfeat_ref[...], (LANES, 2 * tq)).astype(BF16)
        m_ref[...] = jnp.full(m_ref.shape, MASK_VALUE, F32)
        l_ref[...] = jnp.zeros(l_ref.shape, F32)
        acc_ref[...] = jnp.zeros(acc_ref.shape, F32)

    def step(bias_of_block):
        k = k_ref[...]
        vt = vt_ref[...]
        nblk = 2 * tq // cb
        blk = lambda c: slice(c * cb, (c + 1) * cb)
        scs = [_dot(k, qzt_ref[:, blk(c)]) for c in range(nblk)]
        if bias_of_block is not None:
            scs = [scs[c] + bias_of_block(c) for c in range(nblk)]
        m_prev = m_ref[...]
        m_new = jnp.maximum(m_prev, jnp.concatenate(
            [jnp.max(sc, axis=0, keepdims=True) for sc in scs], axis=1))
        alpha = jnp.exp2(m_prev - m_new)
        ps = [jnp.exp2(scs[c] - m_new[:, blk(c)]) for c in range(nblk)]
        l_ref[...] = alpha * l_ref[...] + jnp.concatenate(
            [jnp.sum(p, axis=0, keepdims=True) for p in ps], axis=1)
        pv = jnp.concatenate([_dot(vt, p.astype(BF16)) for p in ps], axis=1)
        acc_ref[...] = alpha * acc_ref[...] + pv
        m_ref[...] = m_new

    @pl.when(lin_ref[s] == 1)
    def _earlier_chunks():
        step(None)

    @pl.when(lin_ref[s] == 0)
    def _diagonal():
        step(lambda c: dbias_ref[:, (c * cb) % tq:(c * cb) % tq + cb])

    @pl.when(last_ref[s] == 1)
    def _finish():
        lp = lam_ref[...]
        lam = (jnp.exp(jnp.sum(lp[0:1] * lp[1:2], axis=1, keepdims=True))
               - jnp.exp(jnp.sum(lp[2:3] * lp[3:4], axis=1, keepdims=True)) + lambda_init)
        o_all = acc_ref[...] / l_ref[...]
        o = o_all[:, 0:tq] - lam * o_all[:, tq:2 * tq]
        o = o * lax.rsqrt(jnp.mean(o * o, axis=0, keepdims=True) + DIFF_SUBLN_EPS)
        o = o * g_ref[...] * (1.0 - lambda_init)
        o_ref[...] = o.T.astype(o_ref.dtype)


def _bf16_split3(x):
    import ml_dtypes
    rnd = lambda v: v.astype(ml_dtypes.bfloat16).astype(np.float32)
    x = np.asarray(x, np.float32)
    hi = rnd(x)
    mid = rnd(x - hi)
    lo = rnd(x - hi - mid)
    return hi, mid, lo


def _diff_attn_prompt(pbf3, kaug3, vt_arr, q_cb, lam_p, subln_g, *, tile, lambda_init):
    b, t, _ = pbf3.shape
    tq = tk = tile
    n = t // tile
    pairs = [(i, j) for i in range(n) for j in range(i + 1)]
    ii = np.array([p[0] for p in pairs], np.int32)
    jj = np.array([p[1] for p in pairs], np.int32)
    last = (ii == jj).astype(np.int32)
    lin = (ii != jj).astype(np.int32)
    c2 = _alibi_slopes_log2()
    qfeat = np.zeros((DIFF_HEADS, LANES, 1), np.float32)
    for r, part in enumerate(_bf16_split3(c2)):
        qfeat[:, r, 0] = part * LANES
        qfeat[:, 3 + r, 0] = part
    pos = np.arange(tile)
    d = (pos[None, :] - pos[:, None]).astype(np.float32)
    visible = (pos[:, None] // CHUNK) <= (pos[None, :] // CHUNK)
    dbias = jnp.asarray(np.where(visible[None], 2.0 * c2[:, None, None] * np.minimum(d, 0.0)[None],
                                 MASK_VALUE).astype(np.float32))
    cb = min(256, 2 * tq)
    idx = lambda f: (lambda b_, h, s, ii, jj, la, li: f(b_, h, s, ii, jj))
    grid_spec = pltpu.PrefetchScalarGridSpec(
        num_scalar_prefetch=4,
        grid=(b, DIFF_HEADS, len(pairs)),
        in_specs=[
            pl.BlockSpec((None, tq, LANES), idx(lambda b_, h, s, ii, jj: (b_, ii[s], q_cb + h))),
            pl.BlockSpec((None, tk, 2 * LANES), idx(lambda b_, h, s, ii, jj: (b_, jj[s], h))),
            pl.BlockSpec((None, LANES, tk), idx(lambda b_, h, s, ii, jj: (b_, h, jj[s]))),
            pl.BlockSpec((None, tk, tq), idx(lambda b_, h, s, ii, jj: (h, 0, 0))),
            pl.BlockSpec((None, LANES, 1), idx(lambda b_, h, s, ii, jj: (h, 0, 0))),
            pl.BlockSpec((4, DIFF_HEAD_DIM), idx(lambda b_, h, s, ii, jj: (0, 0))),
            pl.BlockSpec((LANES, 1), idx(lambda b_, h, s, ii, jj: (0, 0))),
        ],
        out_specs=pl.BlockSpec((None, tq, LANES), idx(lambda b_, h, s, ii, jj: (b_, ii[s], h))),
        scratch_shapes=[
            pltpu.VMEM((2 * LANES, 2 * tq), BF16),
            pltpu.VMEM((1, 2 * tq), F32),
            pltpu.VMEM((1, 2 * tq), F32),
            pltpu.VMEM((LANES, 2 * tq), F32),
        ],
    )
    return pl.pallas_call(
        functools.partial(_diff_prompt_kernel, tq=tq, tk=tk, cb=cb, lambda_init=lambda_init),
        grid_spec=grid_spec,
        out_shape=jax.ShapeDtypeStruct((b, t, DIFF_HEADS * LANES), BF16),
        compiler_params=_params("parallel", "parallel", "arbitrary"),
        name="diff_attn_prompt",
    )(jnp.asarray(ii), jnp.asarray(jj), jnp.asarray(last), jnp.asarray(lin), pbf3, kaug3, vt_arr, dbias,
      jnp.asarray(qfeat), lam_p, subln_g.reshape(LANES, 1))


def _band_attn_kernel(*refs, tq, n_parts, tkp):
    q_ref = refs[0]
    k_refs = refs[1:1 + n_parts]
    v_refs = refs[1 + n_parts:1 + 2 * n_parts]
    bias_ref = refs[1 + 2 * n_parts]
    o_ref = refs[2 + 2 * n_parts]
    i = pl.program_id(2)
    qz = _split_heads_rows(q_ref[...], tq)
    scores = []
    for m in range(n_parts):
        sc = lax.dot_general(qz, k_refs[m][...], NT_DIMS, preferred_element_type=F32)
        sc = sc + bias_ref[:, m * tkp:(m + 1) * tkp]
        if n_parts > 1:
            sc = jnp.where(i - (n_parts - 1) + m >= 0, sc, MASK_VALUE)
        scores.append(sc)
    mx = functools.reduce(jnp.maximum, [jnp.max(sc, axis=1, keepdims=True) for sc in scores])
    den = None
    num = None
    for m in range(n_parts):
        p = jnp.exp2(scores[m] - mx)
        d = jnp.sum(p, axis=1, keepdims=True)
        r = _dot(p.astype(BF16), v_refs[m][...])
        den = d if den is None else den + d
        num = r if num is None else num + r
    r = num / den
    lane = lax.broadcasted_iota(jnp.int32, (tq, LANES), 1)
    o_ref[...] = jnp.where(lane < BAND_HEAD_DIM, r[:tq], r[tq:]).astype(o_ref.dtype)


def _band_bias_kernel(line_ref, o_ref, *, q0, k0):
    tq, nk = o_ref.shape
    line = jnp.broadcast_to(line_ref[...], (tq, line_ref.shape[1]))
    rolled = pltpu.roll(line, 0, 1, stride=1, stride_axis=0)
    qc = (q0 + lax.broadcasted_iota(jnp.int32, (tq, nk), 0)) >> 6
    kc = (k0 + lax.broadcasted_iota(jnp.int32, (tq, nk), 1)) >> 6
    visible = (kc <= qc) & (qc - kc <= BAND_PREV_CHUNKS)
    o_ref[...] = jnp.where(visible, rolled[:, 0:nk], MASK_VALUE)


def _band_bias(rel_table, q0, k0, tq, nk):
    assert CHUNK == 64
    width = pl.next_power_of_2(tq + nk - 1)
    y = np.arange(width)
    c_minus_r = np.where(y < nk, y, y - width)
    rel = np.clip((q0 - k0) - c_minus_r, -(CHUNK - 1), REL_CLIP) + (CHUNK - 1)
    line = (rel_table.astype(F32) * LOG2E)[:, rel].reshape(BAND_HEADS, 1, width)
    return pl.pallas_call(
        functools.partial(_band_bias_kernel, q0=q0, k0=k0),
        grid=(BAND_HEADS,),
        in_specs=[pl.BlockSpec((None, 1, width), lambda h: (h, 0, 0))],
        out_specs=pl.BlockSpec((None, tq, nk), lambda h: (h // 2, h % 2, 0)),
        out_shape=jax.ShapeDtypeStruct((BAND_HEADS // 2, 2 * tq, nk), F32),
        compiler_params=_params("parallel"),
        name="band_bias",
    )(line)


def _band_attn(q_arr, k_arr, v_arr, q_cb, k_cb, v_cb, bias, *, tq, n_parts, tkp):
    b, t_q, _ = q_arr.shape
    n_pairs = BAND_HEADS // 2

    def kv_spec(cb, m):
        return pl.BlockSpec((None, tkp, LANES),
                            lambda p, b_, i: (b_, jnp.maximum(i - (n_parts - 1) + m, 0), cb + p))

    return pl.pallas_call(
        functools.partial(_band_attn_kernel, tq=tq, n_parts=n_parts, tkp=tkp),
        grid=(n_pairs, b, t_q // tq),
        in_specs=[pl.BlockSpec((None, tq, LANES), lambda p, b_, i: (b_, i, q_cb + p))]
        + [kv_spec(k_cb, m) for m in range(n_parts)]
        + [kv_spec(v_cb, m) for m in range(n_parts)]
        + [pl.BlockSpec((None, 2 * tq, n_parts * tkp), lambda p, b_, i: (p, 0, 0))],
        out_specs=pl.BlockSpec((None, tq, LANES), lambda p, b_, i: (b_, i, p)),
        out_shape=jax.ShapeDtypeStruct((b, t_q, n_pairs * LANES), BF16),
        compiler_params=_params("parallel", "parallel", "parallel"),
        name="band_attn",
    )(q_arr, *([k_arr] * n_parts), *([v_arr] * n_parts), bias)


def _attn_out_ffn_kernel(oa_ref, ob_ref, x_ref, wo_ref, g_ref, wgu_ref, wd_ref, o_ref, *, d_ff, tf):
    half = oa_ref.shape[1]
    x1 = x_ref[...] + _dot(oa_ref[...], wo_ref[0:half, :]) + _dot(ob_ref[...], wo_ref[half:2 * half, :])
    h = _rmsnorm(x1, g_ref[...], RMS_EPS).astype(BF16)
    o_ref[...] = x1 + _swiglu(h, wgu_ref, wd_ref, d_ff, tf)


def _attn_out_ffn(oa, ob, x2d, wo_bf, gain, wgu_bf, wd_bf, tm):
    m, d = x2d.shape
    d_ff = wd_bf.shape[0]
    row = lambda i: (i, 0)
    return pl.pallas_call(
        functools.partial(_attn_out_ffn_kernel, d_ff=d_ff, tf=256),
        grid=(m // tm,),
        in_specs=[pl.BlockSpec((tm, oa.shape[1]), row), pl.BlockSpec((tm, ob.shape[1]), row),
                  pl.BlockSpec((tm, d), row), _const_spec(wo_bf.shape), _const_spec((1, d)),
                  _const_spec(wgu_bf.shape), _const_spec(wd_bf.shape)],
        out_specs=pl.BlockSpec((tm, d), row),
        out_shape=jax.ShapeDtypeStruct((m, d), F32),
        compiler_params=_params("parallel"),
        name="attn_out_ffn",
    )(oa, ob, x2d, wo_bf, gain.reshape(1, d), wgu_bf, wd_bf)


def _gelu(z):
    return 0.5 * z * (1.0 + lax.erf(z * (2.0 ** -0.5)))


def _gmlp_kernel(x_ref, g_ref, win_ref, bin_ref, lng_ref, lnb_ref, ws_ref, bs_ref, wout_ref,
                 o_ref, v_ref, u_s, act_s, *, seg, cw):
    tm, gd = v_ref.shape
    gw = gd // GMLP_GROUPS
    x = x_ref[...]
    h = _rmsnorm(x, g_ref[...], RMS_EPS).astype(BF16)
    for c in range(gd // cw):
        lo, hi = c * cw, (c + 1) * cw
        u_s[:, lo:hi] = _gelu(_dot(h, win_ref[:, lo:hi]) + bin_ref[:, lo:hi])
        v_ref[:, lo:hi] = _gelu(_dot(h, win_ref[:, gd + lo:gd + hi]) + bin_ref[:, gd + lo:gd + hi])
    v = v_ref[...]
    mu = jnp.mean(v, axis=-1, keepdims=True)
    var = jnp.mean(jnp.square(v - mu), axis=-1, keepdims=True)
    v_ref[...] = (v - mu) * lax.rsqrt(var + LN_EPS) * lng_ref[...] + lnb_ref[...]
    r_i = lax.broadcasted_iota(jnp.int32, (seg, seg), 0)
    c_i = lax.broadcasted_iota(jnp.int32, (seg, seg), 1)
    for g in range(GMLP_GROUPS):
        w = jnp.where(r_i >= c_i, ws_ref[g], 0.0).astype(BF16)
        for n in range(tm // seg):
            rows = slice(n * seg, (n + 1) * seg)
            cols = slice(g * gw, (g + 1) * gw)
            sv = _dot(w, v_ref[rows, cols].astype(BF16)) + bs_ref[g]
            act_s[rows, cols] = (u_s[rows, cols] * sv).astype(BF16)
    o_ref[...] = x + _dot(act_s[...], wout_ref[...])


def _gmlp(x2d, gain, win_bf, b_in, ln_g, ln_b, w_s, b_s, wout_bf, tm, seg):
    m, d = x2d.shape
    gd = wout_bf.shape[0]
    row = lambda i: (i, 0)
    ws = w_s[:, :seg, :seg]
    bs = b_s[:, :seg, None]
    return pl.pallas_call(
        functools.partial(_gmlp_kernel, seg=seg, cw=512),
        grid=(m // tm,),
        in_specs=[pl.BlockSpec((tm, d), row), _const_spec((1, d)), _const_spec(win_bf.shape),
                  _const_spec((1, 2 * gd)), _const_spec((1, gd)), _const_spec((1, gd)),
                  _const_spec(ws.shape), _const_spec(bs.shape), _const_spec(wout_bf.shape)],
        out_specs=[pl.BlockSpec((tm, d), row), pl.BlockSpec((tm, gd), row)],
        out_shape=[jax.ShapeDtypeStruct((m, d), F32), jax.ShapeDtypeStruct((m, gd), F32)],
        scratch_shapes=[pltpu.VMEM((tm, gd), F32), pltpu.VMEM((tm, gd), BF16)],
        compiler_params=_params("parallel"),
        name="gmlp",
    )(x2d, gain.reshape(1, d), win_bf, b_in.reshape(1, 2 * gd), ln_g.reshape(1, gd),
      ln_b.reshape(1, gd), ws, bs, wout_bf)


def _router_kernel(x_ref, g_ref, whi_ref, wlo_ref, b_ref, h_ref, meta_ref, cnt_ref):
    h = _rmsnorm(x_ref[...], g_ref[...], RMS_EPS)
    h_hi = h.astype(BF16)
    h_lo = (h - h_hi.astype(F32)).astype(BF16)
    logits = (_dot(h_hi, whi_ref[...]) + _dot(h_hi, wlo_ref[...]) + _dot(h_lo, whi_ref[...])
              + b_ref[...])
    tm = logits.shape[0]
    lane = lax.broadcasted_iota(jnp.int32, logits.shape, 1)
    logits = jnp.where(lane < N_EXPERTS, logits, -jnp.inf)
    v1 = jnp.max(logits, axis=1, keepdims=True)
    i1 = jnp.min(jnp.where(logits == v1, lane, LANES), axis=1, keepdims=True)
    rest = jnp.where(lane == i1, -jnp.inf, logits)
    v2 = jnp.max(rest, axis=1, keepdims=True)
    i2 = jnp.min(jnp.where(rest == v2, lane, LANES), axis=1, keepdims=True)
    e2 = jnp.exp(v2 - v1)
    den = 1.0 + e2

    @pl.when(pl.program_id(0) == 0)
    def _init():
        cnt_ref[...] = jnp.zeros(cnt_ref.shape, F32)

    oh1 = lane == i1
    oh2 = lane == i2
    r_i = lax.broadcasted_iota(jnp.int32, (tm, tm), 0)
    c_i = lax.broadcasted_iota(jnp.int32, (tm, tm), 1)
    before = jnp.where(c_i < r_i, 1.0, 0.0).astype(BF16)
    cum1 = _dot(before, jnp.where(oh1, 1.0, 0.0).astype(BF16))
    cum2 = _dot(before, jnp.where(oh2, 1.0, 0.0).astype(BF16))
    n1 = jnp.sum(jnp.where(oh1, 1.0, 0.0), axis=0, keepdims=True)
    n2 = jnp.sum(jnp.where(oh2, 1.0, 0.0), axis=0, keepdims=True)
    base = cnt_ref[...]
    rank1 = jnp.sum(jnp.where(oh1, base + cum1, 0.0), axis=1, keepdims=True)
    rank2 = jnp.sum(jnp.where(oh2, base + n1 + cum2, 0.0), axis=1, keepdims=True)
    cnt_ref[...] = base + n1 + n2
    meta = jnp.where(lane == 0, i1.astype(F32), 0.0)
    meta = jnp.where(lane == 1, i2.astype(F32), meta)
    meta = jnp.where(lane == 2, 1.0 / den, meta)
    meta = jnp.where(lane == 3, e2 / den, meta)
    meta = jnp.where(lane == 4, rank1, meta)
    meta = jnp.where(lane == 5, rank2, meta)
    meta_ref[...] = meta
    h_ref[...] = h


def _router(x2d, gain, w_router, b_router, tm):
    m, d = x2d.shape
    assert 2 * m < 2 ** 24
    w_pad = jnp.pad(w_router, ((0, 0), (0, LANES - N_EXPERTS)))
    w_hi = w_pad.astype(BF16)
    w_lo = (w_pad - w_hi.astype(F32)).astype(BF16)
    b_pad = jnp.pad(b_router, (0, LANES - N_EXPERTS)).reshape(1, LANES)
    row = lambda i: (i, 0)
    return pl.pallas_call(
        _router_kernel,
        grid=(m // tm,),
        in_specs=[pl.BlockSpec((tm, d), row), _const_spec((1, d)), _const_spec((d, LANES)),
                  _const_spec((d, LANES)), _const_spec((1, LANES))],
        out_specs=[pl.BlockSpec((tm, d), row), pl.BlockSpec((tm, LANES), row),
                   pl.BlockSpec((1, LANES), lambda i: (0, 0))],
        out_shape=[jax.ShapeDtypeStruct((m, d), F32), jax.ShapeDtypeStruct((m, LANES), F32),
                   jax.ShapeDtypeStruct((1, LANES), F32)],
        compiler_params=_params("arbitrary"),
        name="moe_router",
    )(x2d, gain.reshape(1, d), w_hi, w_lo, b_pad)


def _row_copy(src_hbm, src_row, dst, dst_row, sem):
    return pltpu.make_async_copy(src_hbm.at[pl.ds(src_row, 1)], dst.at[pl.ds(dst_row, 1)], sem)


def _dispatch_kernel(pos1_ref, pos2_ref, h_hbm, xs_in_hbm, xs_hbm, sem, *, tm):
    del xs_in_hbm
    t0 = pl.program_id(0) * tm

    def start(r, carry):
        _row_copy(h_hbm, t0 + r, xs_hbm, pos1_ref[r], sem).start()
        _row_copy(h_hbm, t0 + r, xs_hbm, pos2_ref[r], sem).start()
        return carry

    def wait(r, carry):
        _row_copy(h_hbm, 0, xs_hbm, 0, sem).wait()
        _row_copy(h_hbm, 0, xs_hbm, 0, sem).wait()
        return carry

    lax.fori_loop(0, tm, start, 0)
    lax.fori_loop(0, tm, wait, 0)


def _dispatch(h2d, pos1, pos2, n_slots, tm):
    m, d = h2d.shape
    smem = lambda: pl.BlockSpec((tm,), lambda i: (i,), memory_space=pltpu.SMEM)
    return pl.pallas_call(
        functools.partial(_dispatch_kernel, tm=tm),
        grid=(m // tm,),
        in_specs=[smem(), smem(), pl.BlockSpec(memory_space=pl.ANY), pl.BlockSpec(memory_space=pl.ANY)],
        out_specs=pl.BlockSpec(memory_space=pl.ANY),
        out_shape=jax.ShapeDtypeStruct((n_slots, d), h2d.dtype),
        scratch_shapes=[pltpu.SemaphoreType.DMA(())],
        input_output_aliases={3: 0},
        compiler_params=_params("arbitrary"),
        name="moe_dispatch",
    )(pos1, pos2, h2d, jnp.zeros((n_slots, d), h2d.dtype))


def _expert_ffn_kernel(te_ref, nt_ref, xs_ref, wgu_ref, wd_ref, ys_ref, *, d_ff, tf):
    @pl.when(pl.program_id(0) < nt_ref[0])
    def _():
        ys_ref[...] = _swiglu(xs_ref[...].astype(BF16), wgu_ref, wd_ref, d_ff, tf)

    @pl.when(pl.program_id(0) >= nt_ref[0])
    def _():
        ys_ref[...] = jnp.zeros(ys_ref.shape, F32)


def _expert_ffn(xs, tile_expert, n_tiles, wgu_bf, wd_bf, tmx):
    n_slots, d = xs.shape
    _, d_ff, _ = wd_bf.shape
    row = lambda i, te, nt: (jnp.minimum(i, nt[0] - 1), 0)
    grid_spec = pltpu.PrefetchScalarGridSpec(
        num_scalar_prefetch=2,
        grid=(n_slots // tmx,),
        in_specs=[pl.BlockSpec((tmx, d), row),
                  pl.BlockSpec((None, d, 2 * d_ff), lambda i, te, nt: (te[i], 0, 0)),
                  pl.BlockSpec((None, d_ff, d), lambda i, te, nt: (te[i], 0, 0))],
        out_specs=pl.BlockSpec((tmx, d), lambda i, te, nt: (i, 0)),
    )
    return pl.pallas_call(
        functools.partial(_expert_ffn_kernel, d_ff=d_ff, tf=256),
        grid_spec=grid_spec,
        out_shape=jax.ShapeDtypeStruct((n_slots, d), F32),
        compiler_params=_params("arbitrary"),
        name="moe_experts",
    )(tile_expert, n_tiles, xs, wgu_bf, wd_bf)


def _combine_kernel(pos1_ref, pos2_ref, ys_hbm, x_ref, meta_ref, gf_ref, o_ref, y1_ref, y2_ref, sem, *, tm):
    def start(r, carry):
        _row_copy(ys_hbm, pos1_ref[r], y1_ref, r, sem).start()
        _row_copy(ys_hbm, pos2_ref[r], y2_ref, r, sem).start()
        return carry

    def wait(r, carry):
        _row_copy(ys_hbm, 0, y1_ref, 0, sem).wait()
        _row_copy(ys_hbm, 0, y2_ref, 0, sem).wait()
        return carry

    lax.fori_loop(0, tm, start, 0)
    lax.fori_loop(0, tm, wait, 0)
    meta = meta_ref[...]
    out = meta[:, 2:3] * y1_ref[...] + meta[:, 3:4] * y2_ref[...]
    o_ref[...] = _rmsnorm(x_ref[...] + out, gf_ref[...], RMS_EPS)


def _combine(ys, pos1, pos2, x2d, meta, gain_final, tm):
    m, d = x2d.shape
    smem = lambda: pl.BlockSpec((tm,), lambda i: (i,), memory_space=pltpu.SMEM)
    row = lambda i: (i, 0)
    return pl.pallas_call(
        functools.partial(_combine_kernel, tm=tm),
        grid=(m // tm,),
        in_specs=[smem(), smem(), pl.BlockSpec(memory_space=pl.ANY), pl.BlockSpec((tm, d), row),
                  pl.BlockSpec((tm, LANES), row), _const_spec((1, d))],
        out_specs=pl.BlockSpec((tm, d), row),
        out_shape=jax.ShapeDtypeStruct((m, d), F32),
        scratch_shapes=[pltpu.VMEM((tm, d), F32), pltpu.VMEM((tm, d), F32), pltpu.SemaphoreType.DMA(())],
        compiler_params=_params("arbitrary"),
        name="moe_combine",
    )(pos1, pos2, ys, x2d, meta, gain_final.reshape(1, d))


def _moe(x2d, norm_gain, w_router, b_router, wgu_bf, wd_bf, gain_final, tm):
    m, d = x2d.shape
    tmx = 512 if m >= 4096 else 128
    h, meta, counts = _router(x2d, norm_gain, w_router, b_router, tm)
    counts = counts[0, :N_EXPERTS].astype(jnp.int32)
    padded = (counts + tmx - 1) // tmx * tmx
    ends = jnp.cumsum(padded)
    starts = ends - padded
    n_slots = 2 * m + N_EXPERTS * tmx
    n_tiles_max = n_slots // tmx
    tile_start = jnp.arange(n_tiles_max, dtype=jnp.int32) * tmx
    tile_expert = jnp.minimum(jnp.sum((tile_start[:, None] >= ends[None, :]).astype(jnp.int32), axis=1),
                              N_EXPERTS - 1).astype(jnp.int32)
    n_tiles = (ends[-1:] // tmx).astype(jnp.int32)
    experts = jnp.arange(N_EXPERTS, dtype=jnp.int32)[None, :]
    start_of = lambda e: jnp.sum(jnp.where(e[:, None] == experts, starts[None, :], 0), axis=1)
    e1, e2 = meta[:, 0].astype(jnp.int32), meta[:, 1].astype(jnp.int32)
    pos1 = start_of(e1) + meta[:, 4].astype(jnp.int32)
    pos2 = start_of(e2) + meta[:, 5].astype(jnp.int32)
    td = min(256, m)
    xs = _dispatch(h, pos1, pos2, n_slots, td)
    ys = _expert_ffn(xs, tile_expert, n_tiles, wgu_bf, wd_bf, tmx)
    return _combine(ys, pos1, pos2, x2d, meta, gain_final, td)


def _forward(x, caches, w):
    b, t, d = x.shape
    m = b * t
    tm = min(512, m)
    x2d = x.reshape(m, d)
    lambda_init = 0.8 - 0.6 * math.exp(-0.3 * 0)
    width = DIFF_HEADS * 2 * DIFF_HEAD_DIM
    ncb = width // LANES

    if caches is None:
        pbf, ka, va, kb, vb, vt, kaug = _qkv_proj(x2d, w["norm_attn"], w["attn_w_in"], tm,
                                                  w["attn_w_vt"], t)
    else:
        pbf, ka, va, kb, vb = _qkv_proj(x2d, w["norm_attn"], w["attn_w_in"], tm)
    pbf3 = pbf.reshape(b, t, 6 * width)
    if caches is None:
        oa = _diff_attn_prompt(pbf3, kaug.reshape(b, t, 2 * width), vt, 0, w["diff_lambda"],
                               w["diff_subln"], tile=min(1024, t), lambda_init=lambda_init)
        tqb = min(256, t)
        n_parts = BAND_PAST_MAX // tqb + 1
        bias = _band_bias(w["band_rel_bias"], 0, -BAND_PAST_MAX, tqb, n_parts * tqb)
        ob = _band_attn(pbf3, pbf3, pbf3, 3 * ncb, 4 * ncb, 5 * ncb, bias,
                        tq=tqb, n_parts=n_parts, tkp=tqb)
        n_band = min(BAND_PAST_MAX, t)
        new_bk = kb.reshape(b, t, BAND_HEADS, BAND_HEAD_DIM)[:, t - n_band:]
        new_bv = vb.reshape(b, t, BAND_HEADS, BAND_HEAD_DIM)[:, t - n_band:]
    else:
        ck_a, cv_a, ck_b, cv_b = caches
        p_len, pb_len = ck_a.shape[1], ck_b.shape[1]
        cat = lambda c, lo: jnp.concatenate(
            [c.reshape(b, c.shape[1], width).astype(BF16), pbf3[:, :, lo:lo + width]], axis=1)
        k_a, v_a = cat(ck_a, width), cat(cv_a, 2 * width)
        k_b, v_b = cat(ck_b, 4 * width), cat(cv_b, 5 * width)
        oa = _diff_attn(pbf3, k_a, v_a, 0, 0, 0, w["diff_lambda"], w["diff_subln"],
                        tq=t, tk=p_len + t, q_off=p_len, lambda_init=lambda_init)
        bias = _band_bias(w["band_rel_bias"], p_len, p_len - pb_len, t, pb_len + t)
        ob = _band_attn(pbf3, k_b, v_b, 3 * ncb, 0, 0, bias, tq=t, n_parts=1, tkp=pb_len + t)
        new_bk = kb.reshape(b, t, BAND_HEADS, BAND_HEAD_DIM)
        new_bv = vb.reshape(b, t, BAND_HEADS, BAND_HEAD_DIM)
    new_dk = ka.reshape(b, t, 2 * DIFF_HEADS, DIFF_HEAD_DIM)
    new_dv = va.reshape(b, t, DIFF_HEADS, 2 * DIFF_HEAD_DIM)

    x2d = _attn_out_ffn(oa.reshape(m, width), ob.reshape(m, width), x2d, w["attn_w_out"],
                        w["norm_ffn"], w["ffn_w_gu"], w["ffn_w_down"], tm)

    seg = min(t, GMLP_CHUNK)
    x2d, v_rows = _gmlp(x2d, w["norm_gmlp"], w["gmlp_w_in"], w["gmlp_b_in"], w["gmlp_ln_g"],
                        w["gmlp_ln_b"], w["gmlp_w_s"], w["gmlp_b_s"], w["gmlp_w_out"], tm, seg)
    y = _moe(x2d, w["norm_moe"], w["moe_w_router"], w["moe_b_router"], w["moe_w_gu"], w["moe_w_down"],
             w["norm_final"], tm)
    return (y.reshape(b, t, d), new_dk[None], new_dv[None], new_bk[None], new_bv[None],
            v_rows.reshape(b, t, -1)[None])


def kernel(x_prompt, x_sample, cache_diff_k, cache_diff_v, cache_band_k, cache_band_v,
           norm_attn, attn_w_in, diff_lambda, diff_subln, band_rel_bias, attn_w_out,
           norm_ffn, ffn_w_gu, ffn_w_down,
           norm_gmlp, gmlp_w_in, gmlp_b_in, gmlp_ln_g, gmlp_ln_b, gmlp_w_s, gmlp_b_s, gmlp_w_out,
           norm_moe, moe_w_router, moe_b_router, moe_w_gu, moe_w_down, norm_final):
    w = {
        "norm_attn": norm_attn[0], "attn_w_in": attn_w_in[0].astype(BF16),
        "attn_w_vt": attn_w_in[0][:, 1024:1536].T.astype(BF16),
        "diff_lambda": diff_lambda[0], "diff_subln": diff_subln[0],
        "band_rel_bias": band_rel_bias[0], "attn_w_out": attn_w_out[0].astype(BF16),
        "norm_ffn": norm_ffn[0], "ffn_w_gu": ffn_w_gu[0].astype(BF16),
        "ffn_w_down": ffn_w_down[0].astype(BF16),
        "norm_gmlp": norm_gmlp[0], "gmlp_w_in": gmlp_w_in[0].astype(BF16),
        "gmlp_b_in": gmlp_b_in[0], "gmlp_ln_g": gmlp_ln_g[0], "gmlp_ln_b": gmlp_ln_b[0],
        "gmlp_w_s": gmlp_w_s[0], "gmlp_b_s": gmlp_b_s[0], "gmlp_w_out": gmlp_w_out[0].astype(BF16),
        "norm_moe": norm_moe[0], "moe_w_router": moe_w_router[0], "moe_b_router": moe_b_router[0],
        "moe_w_gu": moe_w_gu[0].astype(BF16), "moe_w_down": moe_w_down[0].astype(BF16),
        "norm_final": norm_final,
    }
    y_p, dk_p, dv_p, bk_p, bv_p, _ = _forward(x_prompt, None, w)
    y_s, dk_s, dv_s, bk_s, bv_s, gv_s = _forward(
        x_sample, (cache_diff_k[0], cache_diff_v[0], cache_band_k[0], cache_band_v[0]), w)
    return (y_p, y_s, dk_p, dv_p, bk_p, bv_p, dk_s, dv_s, bk_s, bv_s, gv_s)
```

```python
import functools
import math

import numpy as np
import jax
import jax.numpy as jnp
from jax import lax
from jax.experimental import pallas as pl
from jax.experimental.pallas import tpu as pltpu

CHUNK = 64
DIFF_HEADS = 4
DIFF_HEAD_DIM = 64
DIFF_SUBLN_EPS = 1e-5
BAND_HEADS = 8
BAND_HEAD_DIM = 64
BAND_PREV_CHUNKS = 8
BAND_PAST_MAX = BAND_PREV_CHUNKS * CHUNK
REL_CLIP = 256
GMLP_GROUPS = 8
GMLP_CHUNK = 128
N_EXPERTS = 8
RMS_EPS = 1e-6
LN_EPS = 1e-5
MASK_VALUE = -1e30

LANES = 128
VMEM_LIMIT_BYTES = 56 << 20

F32 = jnp.float32
BF16 = jnp.bfloat16
LOG2E = math.log2(math.e)
Q_SCALE = DIFF_HEAD_DIM ** -0.5 * LOG2E
NT_DIMS = (((1,), (1,)), ((), ()))


def _params(*sem):
    return pltpu.CompilerParams(dimension_semantics=sem, vmem_limit_bytes=VMEM_LIMIT_BYTES)


def _const_spec(shape):
    nd = len(shape)
    return pl.BlockSpec(shape, lambda *_: (0,) * nd, pipeline_mode=pl.Buffered(1))


def _rmsnorm(x, g, eps):
    return (x * lax.rsqrt(jnp.mean(x * x, axis=-1, keepdims=True) + eps)) * g


def _dot(a, b):
    return jnp.dot(a, b, preferred_element_type=F32)


def _swiglu(h, wgu_ref, wd_ref, d_ff, tf):
    y = None
    for c in range(d_ff // tf):
        g = _dot(h, wgu_ref[:, c * tf:(c + 1) * tf])
        u = _dot(h, wgu_ref[:, d_ff + c * tf:d_ff + (c + 1) * tf])
        a = (g * jax.nn.sigmoid(g) * u).astype(BF16)
        part = _dot(a, wd_ref[c * tf:(c + 1) * tf, :])
        y = part if y is None else y + part
    return y


def _qkv_kernel(x_ref, g_ref, w_ref, *rest, width, with_vt, tiles_per_seq):
    if with_vt:
        wvt_ref, pbf_ref, ka_ref, va_ref, kb_ref, vb_ref, vt_ref, kaug_ref = rest
    else:
        pbf_ref, ka_ref, va_ref, kb_ref, vb_ref = rest
    tm = x_ref.shape[0]
    h = _rmsnorm(x_ref[...], g_ref[...], RMS_EPS).astype(BF16)
    f32_outs = {1: ka_ref, 2: va_ref, 4: kb_ref, 5: vb_ref}
    for c in range(6):
        r = _dot(h, w_ref[:, c * width:(c + 1) * width])
        if c in f32_outs:
            f32_outs[c][...] = r
        else:
            r = r * Q_SCALE
        pbf_ref[:, c * width:(c + 1) * width] = r.astype(BF16)
        if with_vt and c == 1:
            pos = (pl.program_id(0) % tiles_per_seq) * tm + lax.broadcasted_iota(jnp.int32, (tm, LANES), 0)
            lane = lax.broadcasted_iota(jnp.int32, (tm, LANES), 1)
            hi = (pos // LANES).astype(F32)
            lo = (pos % LANES).astype(F32)
            feats = jnp.where(lane < 3, hi, jnp.where(lane < 6, lo, 0.0)).astype(BF16)
            for hd in range(DIFF_HEADS):
                kaug_ref[:, 2 * hd * LANES:(2 * hd + 1) * LANES] = r[:, hd * LANES:(hd + 1) * LANES].astype(BF16)
                kaug_ref[:, (2 * hd + 1) * LANES:(2 * hd + 2) * LANES] = feats
    if with_vt:
        vt_ref[...] = lax.dot_general(wvt_ref[...], h, NT_DIMS, preferred_element_type=F32).astype(BF16)


def _qkv_proj(x2d, gain, w_bf, tm, wvt_bf=None, seq=None):
    m, d = x2d.shape
    width = w_bf.shape[1] // 6
    row = lambda i: (i, 0)
    with_vt = wvt_bf is not None
    in_specs = [pl.BlockSpec((tm, d), row), _const_spec((1, d)), _const_spec(w_bf.shape)]
    out_specs = [pl.BlockSpec((tm, 6 * width), row)] + [pl.BlockSpec((tm, width), row)] * 4
    out_shape = [jax.ShapeDtypeStruct((m, 6 * width), BF16)] + [jax.ShapeDtypeStruct((m, width), F32)] * 4
    args = [x2d, gain.reshape(1, d), w_bf]
    nt = 1
    if with_vt:
        nt = seq // tm
        assert seq <= 256 * LANES
        in_specs.append(_const_spec(wvt_bf.shape))
        out_specs.append(pl.BlockSpec((None, width, tm), lambda i: (i // nt, 0, i % nt)))
        out_shape.append(jax.ShapeDtypeStruct((m // seq, width, seq), BF16))
        out_specs.append(pl.BlockSpec((tm, 2 * width), row))
        out_shape.append(jax.ShapeDtypeStruct((m, 2 * width), BF16))
        args.append(wvt_bf)
    return pl.pallas_call(
        functools.partial(_qkv_kernel, width=width, with_vt=with_vt, tiles_per_seq=nt),
        grid=(m // tm,),
        in_specs=in_specs, out_specs=out_specs, out_shape=out_shape,
        compiler_params=_params("parallel"),
        name="qkv_proj",
    )(*args)


def _split_heads_rows(q, tq):
    qf = q.astype(F32)
    lane = lax.broadcasted_iota(jnp.int32, qf.shape, 1)
    lo = jnp.where(lane < DIFF_HEAD_DIM, qf, 0.0)
    hi = jnp.where(lane >= DIFF_HEAD_DIM, qf, 0.0)
    return jnp.concatenate([lo, hi], axis=0).astype(BF16)


def _diff_attn_kernel(ii_ref, jj_ref, last_ref, q_ref, k_ref, v_ref, slope_ref, lam_ref, g_ref,
                      o_ref, qz_ref, m_ref, l_ref, acc_ref, *, tq, tk, q_off, lambda_init):
    s = pl.program_id(2)
    i = ii_ref[s]
    j = jj_ref[s]

    @pl.when(j == 0)
    def _init():
        qz_ref[...] = _split_heads_rows(q_ref[...], tq)
        m_ref[...] = jnp.full(m_ref.shape, MASK_VALUE, F32)
        l_ref[...] = jnp.zeros(l_ref.shape, F32)
        acc_ref[...] = jnp.zeros(acc_ref.shape, F32)

    sc = lax.dot_general(qz_ref[...], k_ref[...], NT_DIMS, preferred_element_type=F32)
    row = lax.broadcasted_iota(jnp.int32, (2 * tq, 1), 0)
    row = jnp.where(row >= tq, row - tq, row)
    qpos = q_off + i * tq + row
    kpos = j * tk + lax.broadcasted_iota(jnp.int32, (1, tk), 1)
    dist = jnp.abs(qpos - kpos).astype(F32)
    sc = sc - slope_ref[:, 0:1] * dist
    visible = (kpos // CHUNK) <= (qpos // CHUNK)
    sc = jnp.where(visible, sc, MASK_VALUE)

    m_prev = m_ref[...]
    m_new = jnp.maximum(m_prev, jnp.max(sc, axis=1, keepdims=True))
    alpha = jnp.exp2(m_prev - m_new)
    p = jnp.exp2(sc - m_new)
    l_ref[...] = alpha * l_ref[...] + jnp.sum(p, axis=1, keepdims=True)
    acc_ref[...] = alpha * acc_ref[...] + _dot(p.astype(BF16), v_ref[...])
    m_ref[...] = m_new

    @pl.when(last_ref[s] == 1)
    def _finish():
        lp = lam_ref[...]
        lam = (jnp.exp(jnp.sum(lp[0:1] * lp[1:2], axis=1, keepdims=True))
               - jnp.exp(jnp.sum(lp[2:3] * lp[3:4], axis=1, keepdims=True)) + lambda_init)
        o_all = acc_ref[...] / l_ref[...]
        o = o_all[:tq] - lam * o_all[tq:]
        o = _rmsnorm(o, g_ref[...], DIFF_SUBLN_EPS) * (1.0 - lambda_init)
        o_ref[...] = o.astype(o_ref.dtype)


def _diff_attn(q_arr, k_arr, v_arr, q_cb, k_cb, v_cb, lam_p, subln_g, *, tq, tk, q_off, lambda_init):
    b, t_q, _ = q_arr.shape
    t_k = k_arr.shape[1]
    nq, nk = t_q // tq, t_k // tk
    pairs = [(i, j) for i in range(nq) for j in range(nk)
             if (j * tk) // CHUNK <= (q_off + i * tq + tq - 1) // CHUNK]
    ii = np.array([p[0] for p in pairs], np.int32)
    jj = np.array([p[1] for p in pairs], np.int32)
    last = np.array([1 if (n + 1 == len(pairs) or pairs[n + 1][0] != pairs[n][0]) else 0
                     for n in range(len(pairs))], np.int32)
    slopes = jnp.asarray(_alibi_slopes_log2()[:, None, None] * np.ones((1, 1, LANES), np.float32))

    grid_spec = pltpu.PrefetchScalarGridSpec(
        num_scalar_prefetch=3,
        grid=(b, DIFF_HEADS, len(pairs)),
        in_specs=[
            pl.BlockSpec((None, tq, LANES), lambda b_, h, s, ii, jj, la: (b_, ii[s], q_cb + h)),
            pl.BlockSpec((None, tk, LANES), lambda b_, h, s, ii, jj, la: (b_, jj[s], k_cb + h)),
            pl.BlockSpec((None, tk, LANES), lambda b_, h, s, ii, jj, la: (b_, jj[s], v_cb + h)),
            pl.BlockSpec((None, 1, LANES), lambda b_, h, s, ii, jj, la: (h, 0, 0)),
            pl.BlockSpec((4, DIFF_HEAD_DIM), lambda b_, h, s, ii, jj, la: (0, 0)),
            pl.BlockSpec((1, LANES), lambda b_, h, s, ii, jj, la: (0, 0)),
        ],
        out_specs=pl.BlockSpec((None, tq, LANES), lambda b_, h, s, ii, jj, la: (b_, ii[s], h)),
        scratch_shapes=[
            pltpu.VMEM((2 * tq, LANES), BF16),
            pltpu.VMEM((2 * tq, 1), F32),
            pltpu.VMEM((2 * tq, 1), F32),
            pltpu.VMEM((2 * tq, LANES), F32),
        ],
    )
    return pl.pallas_call(
        functools.partial(_diff_attn_kernel, tq=tq, tk=tk, q_off=q_off, lambda_init=lambda_init),
        grid_spec=grid_spec,
        out_shape=jax.ShapeDtypeStruct((b, t_q, DIFF_HEADS * LANES), BF16),
        compiler_params=_params("parallel", "parallel", "arbitrary"),
        name="diff_attn",
    )(jnp.asarray(ii), jnp.asarray(jj), jnp.asarray(last), q_arr, k_arr, v_arr, slopes,
      lam_p, subln_g.reshape(1, LANES))


def _alibi_slopes_log2():
    return (2.0 ** (-8.0 * np.arange(1, DIFF_HEADS + 1, dtype=np.float64) / DIFF_HEADS) * LOG2E).astype(np.float32)


def _diff_prompt_kernel(ii_ref, jj_ref, last_ref, lin_ref, q_ref, k_ref, vt_ref, dbias_ref, qfeat_ref,
                        lam_ref, g_ref, o_ref, qzt_ref, m_ref, l_ref, acc_ref,
                        *, tq, tk, cb, lambda_init):
    s = pl.program_id(2)
    j = jj_ref[s]

    @pl.when(j == 0)
    def _init():
        qf = q_ref[...].astype(F32)
        lane = lax.broadcasted_iota(jnp.int32, qf.shape, 1)
        qzt_ref[0:LANES, 0:tq] = jnp.where(lane < DIFF_HEAD_DIM, qf, 0.0).T.astype(BF16)
        qzt_ref[0:LANES, tq:2 * tq] = jnp.where(lane >= DIFF_HEAD_DIM, qf, 0.0).T.astype(BF16)
        qzt_ref[LANES:2 * LANES, :] = jnp.broadcast_to(qfeat_ref[...], (LANES, 2 * tq)).astype(BF16)
        m_ref[...] = jnp.full(m_ref.shape, MASK_VALUE, F32)
        l_ref[...] = jnp.zeros(l_ref.shape, F32)
        acc_ref[...] = jnp.zeros(acc_ref.shape, F32)

    def step(bias_of_block):
        k = k_ref[...]
        vt = vt_ref[...]
        nblk = 2 * tq // cb
        blk = lambda c: slice(c * cb, (c + 1) * cb)
        scs = [_dot(k, qzt_ref[:, blk(c)]) for c in range(nblk)]
        if bias_of_block is not None:
            scs = [scs[c] + bias_of_block(c) for c in range(nblk)]
        m_prev = m_ref[...]
        m_new = jnp.maximum(m_prev, jnp.concatenate(
            [jnp.max(sc, axis=0, keepdims=True) for sc in scs], axis=1))
        alpha = jnp.exp2(m_prev - m_new)
        ps = [jnp.exp2(scs[c] - m_new[:, blk(c)]) for c in range(nblk)]
        l_ref[...] = alpha * l_ref[...] + jnp.concatenate(
            [jnp.sum(p, axis=0, keepdims=True) for p in ps], axis=1)
        pv = jnp.concatenate([_dot(vt, p.astype(BF16)) for p in ps], axis=1)
        acc_ref[...] = alpha * acc_ref[...] + pv
        m_ref[...] = m_new

    @pl.when(lin_ref[s] == 1)
    def _earlier_chunks():
        step(None)

    @pl.when(lin_ref[s] == 0)
    def _diagonal():
        step(lambda c: dbias_ref[:, (c * cb) % tq:(c * cb) % tq + cb])

    @pl.when(last_ref[s] == 1)
    def _finish():
        lp = lam_ref[...]
        lam = (jnp.exp(jnp.sum(lp[0:1] * lp[1:2], axis=1, keepdims=True))
               - jnp.exp(jnp.sum(lp[2:3] * lp[3:4], axis=1, keepdims=True)) + lambda_init)
        o_all = acc_ref[...] / l_ref[...]
        o = o_all[:, 0:tq] - lam * o_all[:, tq:2 * tq]
        o = o * lax.rsqrt(jnp.mean(o * o, axis=0, keepdims=True) + DIFF_SUBLN_EPS)
        o = o * g_ref[...] * (1.0 - lambda_init)
        o_ref[...] = o.T.astype(o_ref.dtype)


def _bf16_split3(x):
    import ml_dtypes
    rnd = lambda v: v.astype(ml_dtypes.bfloat16).astype(np.float32)
    x = np.asarray(x, np.float32)
    hi = rnd(x)
    mid = rnd(x - hi)
    lo = rnd(x - hi - mid)
    return hi, mid, lo


def _diff_attn_prompt(pbf3, kaug3, vt_arr, q_cb, lam_p, subln_g, *, tile, lambda_init):
    b, t, _ = pbf3.shape
    tq = tk = tile
    n = t // tile
    pairs = [(i, j) for i in range(n) for j in range(i + 1)]
    ii = np.array([p[0] for p in pairs], np.int32)
    jj = np.array([p[1] for p in pairs], np.int32)
    last = (ii == jj).astype(np.int32)
    lin = (ii != jj).astype(np.int32)
    c2 = _alibi_slopes_log2()
    qfeat = np.zeros((DIFF_HEADS, LANES, 1), np.float32)
    for r, part in enumerate(_bf16_split3(c2)):
        qfeat[:, r, 0] = part * LANES
        qfeat[:, 3 + r, 0] = part
    pos = np.arange(tile)
    d = (pos[None, :] - pos[:, None]).astype(np.float32)
    visible = (pos[:, None] // CHUNK) <= (pos[None, :] // CHUNK)
    dbias = jnp.asarray(np.where(visible[None], 2.0 * c2[:, None, None] * np.minimum(d, 0.0)[None],
                                 MASK_VALUE).astype(np.float32))
    cb = min(256, 2 * tq)
    idx = lambda f: (lambda b_, h, s, ii, jj, la, li: f(b_, h, s, ii, jj))
    grid_spec = pltpu.PrefetchScalarGridSpec(
        num_scalar_prefetch=4,
        grid=(b, DIFF_HEADS, len(pairs)),
        in_specs=[
            pl.BlockSpec((None, tq, LANES), idx(lambda b_, h, s, ii, jj: (b_, ii[s], q_cb + h))),
            pl.BlockSpec((None, tk, 2 * LANES), idx(lambda b_, h, s, ii, jj: (b_, jj[s], h))),
            pl.BlockSpec((None, LANES, tk), idx(lambda b_, h, s, ii, jj: (b_, h, jj[s]))),
            pl.BlockSpec((None, tk, tq), idx(lambda b_, h, s, ii, jj: (h, 0, 0))),
            pl.BlockSpec((None, LANES, 1), idx(lambda b_, h, s, ii, jj: (h, 0, 0))),
            pl.BlockSpec((4, DIFF_HEAD_DIM), idx(lambda b_, h, s, ii, jj: (0, 0))),
            pl.BlockSpec((LANES, 1), idx(lambda b_, h, s, ii, jj: (0, 0))),
        ],
        out_specs=pl.BlockSpec((None, tq, LANES), idx(lambda b_, h, s, ii, jj: (b_, ii[s], h))),
        scratch_shapes=[
            pltpu.VMEM((2 * LANES, 2 * tq), BF16),
            pltpu.VMEM((1, 2 * tq), F32),
            pltpu.VMEM((1, 2 * tq), F32),
            pltpu.VMEM((LANES, 2 * tq), F32),
        ],
    )
    return pl.pallas_call(
        functools.partial(_diff_prompt_kernel, tq=tq, tk=tk, cb=cb, lambda_init=lambda_init),
        grid_spec=grid_spec,
        out_shape=jax.ShapeDtypeStruct((b, t, DIFF_HEADS * LANES), BF16),
        compiler_params=_params("parallel", "parallel", "arbitrary"),
        name="diff_attn_prompt",
    )(jnp.asarray(ii), jnp.asarray(jj), jnp.asarray(last), jnp.asarray(lin), pbf3, kaug3, vt_arr, dbias,
      jnp.asarray(qfeat), lam_p, subln_g.reshape(LANES, 1))


def _band_attn_kernel(*refs, tq, n_parts, tkp):
    q_ref = refs[0]
    k_refs = refs[1:1 + n_parts]
    v_refs = refs[1 + n_parts:1 + 2 * n_parts]
    bias_ref = refs[1 + 2 * n_parts]
    o_ref = refs[2 + 2 * n_parts]
    i = pl.program_id(2)
    qz = _split_heads_rows(q_ref[...], tq)
    scores = []
    for m in range(n_parts):
        sc = lax.dot_general(qz, k_refs[m][...], NT_DIMS, preferred_element_type=F32)
        sc = sc + bias_ref[:, m * tkp:(m + 1) * tkp]
        if n_parts > 1:
            sc = jnp.where(i - (n_parts - 1) + m >= 0, sc, MASK_VALUE)
        scores.append(sc)
    mx = functools.reduce(jnp.maximum, [jnp.max(sc, axis=1, keepdims=True) for sc in scores])
    den = None
    num = None
    for m in range(n_parts):
        p = jnp.exp2(scores[m] - mx)
        d = jnp.sum(p, axis=1, keepdims=True)
        r = _dot(p.astype(BF16), v_refs[m][...])
        den = d if den is None else den + d
        num = r if num is None else num + r
    r = num / den
    lane = lax.broadcasted_iota(jnp.int32, (tq, LANES), 1)
    o_ref[...] = jnp.where(lane < BAND_HEAD_DIM, r[:tq], r[tq:]).astype(o_ref.dtype)


def _band_bias_kernel(line_ref, o_ref, *, q0, k0):
    tq, nk = o_ref.shape
    line = jnp.broadcast_to(line_ref[...], (tq, line_ref.shape[1]))
    rolled = pltpu.roll(line, 0, 1, stride=1, stride_axis=0)
    qc = (q0 + lax.broadcasted_iota(jnp.int32, (tq, nk), 0)) >> 6
    kc = (k0 + lax.broadcasted_iota(jnp.int32, (tq, nk), 1)) >> 6
    visible = (kc <= qc) & (qc - kc <= BAND_PREV_CHUNKS)
    o_ref[...] = jnp.where(visible, rolled[:, 0:nk], MASK_VALUE)


def _band_bias(rel_table, q0, k0, tq, nk):
    assert CHUNK == 64
    width = pl.next_power_of_2(tq + nk - 1)
    y = np.arange(width)
    c_minus_r = np.where(y < nk, y, y - width)
    rel = np.clip((q0 - k0) - c_minus_r, -(CHUNK - 1), REL_CLIP) + (CHUNK - 1)
    line = (rel_table.astype(F32) * LOG2E)[:, rel].reshape(BAND_HEADS, 1, width)
    return pl.pallas_call(
        functools.partial(_band_bias_kernel, q0=q0, k0=k0),
        grid=(BAND_HEADS,),
        in_specs=[pl.BlockSpec((None, 1, width), lambda h: (h, 0, 0))],
        out_specs=pl.BlockSpec((None, tq, nk), lambda h: (h // 2, h % 2, 0)),
        out_shape=jax.ShapeDtypeStruct((BAND_HEADS // 2, 2 * tq, nk), F32),
        compiler_params=_params("parallel"),
        name="band_bias",
    )(line)


def _band_attn(q_arr, k_arr, v_arr, q_cb, k_cb, v_cb, bias, *, tq, n_parts, tkp):
    b, t_q, _ = q_arr.shape
    n_pairs = BAND_HEADS // 2

    def kv_spec(cb, m):
        return pl.BlockSpec((None, tkp, LANES),
                            lambda p, b_, i: (b_, jnp.maximum(i - (n_parts - 1) + m, 0), cb + p))

    return pl.pallas_call(
        functools.partial(_band_attn_kernel, tq=tq, n_parts=n_parts, tkp=tkp),
        grid=(n_pairs, b, t_q // tq),
        in_specs=[pl.BlockSpec((None, tq, LANES), lambda p, b_, i: (b_, i, q_cb + p))]
        + [kv_spec(k_cb, m) for m in range(n_parts)]
        + [kv_spec(v_cb, m) for m in range(n_parts)]
        + [pl.BlockSpec((None, 2 * tq, n_parts * tkp), lambda p, b_, i: (p, 0, 0))],
        out_specs=pl.BlockSpec((None, tq, LANES), lambda p, b_, i: (b_, i, p)),
        out_shape=jax.ShapeDtypeStruct((b, t_q, n_pairs * LANES), BF16),
        compiler_params=_params("parallel", "parallel", "parallel"),
        name="band_attn",
    )(q_arr, *([k_arr] * n_parts), *([v_arr] * n_parts), bias)


def _attn_out_ffn_kernel(oa_ref, ob_ref, x_ref, wo_ref, g_ref, wgu_ref, wd_ref, o_ref, *, d_ff, tf):
    half = oa_ref.shape[1]
    x1 = x_ref[...] + _dot(oa_ref[...], wo_ref[0:half, :]) + _dot(ob_ref[...], wo_ref[half:2 * half, :])
    h = _rmsnorm(x1, g_ref[...], RMS_EPS).astype(BF16)
    o_ref[...] = x1 + _swiglu(h, wgu_ref, wd_ref, d_ff, tf)


def _attn_out_ffn(oa, ob, x2d, wo_bf, gain, wgu_bf, wd_bf, tm):
    m, d = x2d.shape
    d_ff = wd_bf.shape[0]
    row = lambda i: (i, 0)
    return pl.pallas_call(
        functools.partial(_attn_out_ffn_kernel, d_ff=d_ff, tf=256),
        grid=(m // tm,),
        in_specs=[pl.BlockSpec((tm, oa.shape[1]), row), pl.BlockSpec((tm, ob.shape[1]), row),
                  pl.BlockSpec((tm, d), row), _const_spec(wo_bf.shape), _const_spec((1, d)),
                  _const_spec(wgu_bf.shape), _const_spec(wd_bf.shape)],
        out_specs=pl.BlockSpec((tm, d), row),
        out_shape=jax.ShapeDtypeStruct((m, d), F32),
        compiler_params=_params("parallel"),
        name="attn_out_ffn",
    )(oa, ob, x2d, wo_bf, gain.reshape(1, d), wgu_bf, wd_bf)


def _gelu(z):
    return 0.5 * z * (1.0 + lax.erf(z * (2.0 ** -0.5)))


def _gmlp_kernel(x_ref, g_ref, win_ref, bin_ref, lng_ref, lnb_ref, ws_ref, bs_ref, wout_ref,
                 o_ref, v_ref, u_s, act_s, *, seg, cw):
    tm, gd = v_ref.shape
    gw = gd // GMLP_GROUPS
    x = x_ref[...]
    h = _rmsnorm(x, g_ref[...], RMS_EPS).astype(BF16)
    for c in range(gd // cw):
        lo, hi = c * cw, (c + 1) * cw
        u_s[:, lo:hi] = _gelu(_dot(h, win_ref[:, lo:hi]) + bin_ref[:, lo:hi])
        v_ref[:, lo:hi] = _gelu(_dot(h, win_ref[:, gd + lo:gd + hi]) + bin_ref[:, gd + lo:gd + hi])
    v = v_ref[...]
    mu = jnp.mean(v, axis=-1, keepdims=True)
    var = jnp.mean(jnp.square(v - mu), axis=-1, keepdims=True)
    v_ref[...] = (v - mu) * lax.rsqrt(var + LN_EPS) * lng_ref[...] + lnb_ref[...]
    r_i = lax.broadcasted_iota(jnp.int32, (seg, seg), 0)
    c_i = lax.broadcasted_iota(jnp.int32, (seg, seg), 1)
    for g in range(GMLP_GROUPS):
        w = jnp.where(r_i >= c_i, ws_ref[g], 0.0).astype(BF16)
        for n in range(tm // seg):
            rows = slice(n * seg, (n + 1) * seg)
            cols = slice(g * gw, (g + 1) * gw)
            sv = _dot(w, v_ref[rows, cols].astype(BF16)) + bs_ref[g]
            act_s[rows, cols] = (u_s[rows, cols] * sv).astype(BF16)
    o_ref[...] = x + _dot(act_s[...], wout_ref[...])


def _gmlp(x2d, gain, win_bf, b_in, ln_g, ln_b, w_s, b_s, wout_bf, tm, seg):
    m, d = x2d.shape
    gd = wout_bf.shape[0]
    row = lambda i: (i, 0)
    ws = w_s[:, :seg, :seg]
    bs = b_s[:, :seg, None]
    return pl.pallas_call(
        functools.partial(_gmlp_kernel, seg=seg, cw=512),
        grid=(m // tm,),
        in_specs=[pl.BlockSpec((tm, d), row), _const_spec((1, d)), _const_spec(win_bf.shape),
                  _const_spec((1, 2 * gd)), _const_spec((1, gd)), _const_spec((1, gd)),
                  _const_spec(ws.shape), _const_spec(bs.shape), _const_spec(wout_bf.shape)],
        out_specs=[pl.BlockSpec((tm, d), row), pl.BlockSpec((tm, gd), row)],
        out_shape=[jax.ShapeDtypeStruct((m, d), F32), jax.ShapeDtypeStruct((m, gd), F32)],
        scratch_shapes=[pltpu.VMEM((tm, gd), F32), pltpu.VMEM((tm, gd), BF16)],
        compiler_params=_params("parallel"),
        name="gmlp",
    )(x2d, gain.reshape(1, d), win_bf, b_in.reshape(1, 2 * gd), ln_g.reshape(1, gd),
      ln_b.reshape(1, gd), ws, bs, wout_bf)


def _router_kernel(x_ref, g_ref, whi_ref, wlo_ref, b_ref, h_ref, meta_ref, cnt_ref):
    h = _rmsnorm(x_ref[...], g_ref[...], RMS_EPS)
    h_hi = h.astype(BF16)
    h_lo = (h - h_hi.astype(F32)).astype(BF16)
    logits = (_dot(h_hi, whi_ref[...]) + _dot(h_hi, wlo_ref[...]) + _dot(h_lo, whi_ref[...])
              + b_ref[...])
    tm = logits.shape[0]
    lane = lax.broadcasted_iota(jnp.int32, logits.shape, 1)
    logits = jnp.where(lane < N_EXPERTS, logits, -jnp.inf)
    v1 = jnp.max(logits, axis=1, keepdims=True)
    i1 = jnp.min(jnp.where(logits == v1, lane, LANES), axis=1, keepdims=True)
    rest = jnp.where(lane == i1, -jnp.inf, logits)
    v2 = jnp.max(rest, axis=1, keepdims=True)
    i2 = jnp.min(jnp.where(rest == v2, lane, LANES), axis=1, keepdims=True)
    e2 = jnp.exp(v2 - v1)
    den = 1.0 + e2

    @pl.when(pl.program_id(0) == 0)
    def _init():
        cnt_ref[...] = jnp.zeros(cnt_ref.shape, F32)

    oh1 = lane == i1
    oh2 = lane == i2
    r_i = lax.broadcasted_iota(jnp.int32, (tm, tm), 0)
    c_i = lax.broadcasted_iota(jnp.int32, (tm, tm), 1)
    before = jnp.where(c_i < r_i, 1.0, 0.0).astype(BF16)
    cum1 = _dot(before, jnp.where(oh1, 1.0, 0.0).astype(BF16))
    cum2 = _dot(before, jnp.where(oh2, 1.0, 0.0).astype(BF16))
    n1 = jnp.sum(jnp.where(oh1, 1.0, 0.0), axis=0, keepdims=True)
    n2 = jnp.sum(jnp.where(oh2, 1.0, 0.0), axis=0, keepdims=True)
    base = cnt_ref[...]
    rank1 = jnp.sum(jnp.where(oh1, base + cum1, 0.0), axis=1, keepdims=True)
    rank2 = jnp.sum(jnp.where(oh2, base + n1 + cum2, 0.0), axis=1, keepdims=True)
    cnt_ref[...] = base + n1 + n2
    meta = jnp.where(lane == 0, i1.astype(F32), 0.0)
    meta = jnp.where(lane == 1, i2.astype(F32), meta)
    meta = jnp.where(lane == 2, 1.0 / den, meta)
    meta = jnp.where(lane == 3, e2 / den, meta)
    meta = jnp.where(lane == 4, rank1, meta)
    meta = jnp.where(lane == 5, rank2, meta)
    meta_ref[...] = meta
    h_ref[...] = h


def _router(x2d, gain, w_router, b_router, tm):
    m, d = x2d.shape
    assert 2 * m < 2 ** 24
    w_pad = jnp.pad(w_router, ((0, 0), (0, LANES - N_EXPERTS)))
    w_hi = w_pad.astype(BF16)
    w_lo = (w_pad - w_hi.astype(F32)).astype(BF16)
    b_pad = jnp.pad(b_router, (0, LANES - N_EXPERTS)).reshape(1, LANES)
    row = lambda i: (i, 0)
    return pl.pallas_call(
        _router_kernel,
        grid=(m // tm,),
        in_specs=[pl.BlockSpec((tm, d), row), _const_spec((1, d)), _const_spec((d, LANES)),
                  _const_spec((d, LANES)), _const_spec((1, LANES))],
        out_specs=[pl.BlockSpec((tm, d), row), pl.BlockSpec((tm, LANES), row),
                   pl.BlockSpec((1, LANES), lambda i: (0, 0))],
        out_shape=[jax.ShapeDtypeStruct((m, d), F32), jax.ShapeDtypeStruct((m, LANES), F32),
                   jax.ShapeDtypeStruct((1, LANES), F32)],
        compiler_params=_params("arbitrary"),
        name="moe_router",
    )(x2d, gain.reshape(1, d), w_hi, w_lo, b_pad)


def _row_copy(src_hbm, src_row, dst, dst_row, sem):
    return pltpu.make_async_copy(src_hbm.at[pl.ds(src_row, 1)], dst.at[pl.ds(dst_row, 1)], sem)


ROW_DMA_UNROLL = 8


def _dispatch_kernel(pos1_ref, pos2_ref, h_ref, xs_in_hbm, xs_hbm, sem, *, tm):
    del xs_in_hbm

    def start(r, carry):
        _row_copy(h_ref, r, xs_hbm, pos1_ref[r], sem).start()
        _row_copy(h_ref, r, xs_hbm, pos2_ref[r], sem).start()
        return carry

    def wait(r, carry):
        _row_copy(h_ref, 0, xs_hbm, 0, sem).wait()
        _row_copy(h_ref, 0, xs_hbm, 0, sem).wait()
        return carry

    lax.fori_loop(0, tm, start, 0, unroll=ROW_DMA_UNROLL)
    lax.fori_loop(0, tm, wait, 0, unroll=ROW_DMA_UNROLL)


def _dispatch(h2d, pos1, pos2, n_slots, tm):
    m, d = h2d.shape
    smem = lambda: pl.BlockSpec((tm,), lambda i: (i,), memory_space=pltpu.SMEM)
    return pl.pallas_call(
        functools.partial(_dispatch_kernel, tm=tm),
        grid=(m // tm,),
        in_specs=[smem(), smem(), pl.BlockSpec((tm, d), lambda i: (i, 0)), pl.BlockSpec(memory_space=pl.ANY)],
        out_specs=pl.BlockSpec(memory_space=pl.ANY),
        out_shape=jax.ShapeDtypeStruct((n_slots, d), h2d.dtype),
        scratch_shapes=[pltpu.SemaphoreType.DMA(())],
        input_output_aliases={3: 0},
        compiler_params=_params("arbitrary"),
        name="moe_dispatch",
    )(pos1, pos2, h2d, jnp.zeros((n_slots, d), h2d.dtype))


def _expert_ffn_kernel(te_ref, nt_ref, xs_ref, wgu_ref, wd_ref, ys_ref, *, d_ff, tf):
    @pl.when(pl.program_id(0) < nt_ref[0])
    def _():
        ys_ref[...] = _swiglu(xs_ref[...].astype(BF16), wgu_ref, wd_ref, d_ff, tf)

    @pl.when(pl.program_id(0) >= nt_ref[0])
    def _():
        ys_ref[...] = jnp.zeros(ys_ref.shape, F32)


def _expert_ffn(xs, tile_expert, n_tiles, wgu_bf, wd_bf, tmx):
    n_slots, d = xs.shape
    _, d_ff, _ = wd_bf.shape
    row = lambda i, te, nt: (jnp.minimum(i, nt[0] - 1), 0)
    grid_spec = pltpu.PrefetchScalarGridSpec(
        num_scalar_prefetch=2,
        grid=(n_slots // tmx,),
        in_specs=[pl.BlockSpec((tmx, d), row),
                  pl.BlockSpec((None, d, 2 * d_ff), lambda i, te, nt: (te[i], 0, 0)),
                  pl.BlockSpec((None, d_ff, d), lambda i, te, nt: (te[i], 0, 0))],
        out_specs=pl.BlockSpec((tmx, d), lambda i, te, nt: (i, 0)),
    )
    return pl.pallas_call(
        functools.partial(_expert_ffn_kernel, d_ff=d_ff, tf=256),
        grid_spec=grid_spec,
        out_shape=jax.ShapeDtypeStruct((n_slots, d), F32),
        compiler_params=_params("arbitrary"),
        name="moe_experts",
    )(tile_expert, n_tiles, xs, wgu_bf, wd_bf)


def _combine_kernel(pos1_ref, pos2_ref, ys_hbm, x_ref, meta_ref, gf_ref, o_ref, y1_ref, y2_ref, sem, *, tm):
    def start(r, carry):
        _row_copy(ys_hbm, pos1_ref[r], y1_ref, r, sem).start()
        _row_copy(ys_hbm, pos2_ref[r], y2_ref, r, sem).start()
        return carry

    def wait(r, carry):
        _row_copy(ys_hbm, 0, y1_ref, 0, sem).wait()
        _row_copy(ys_hbm, 0, y2_ref, 0, sem).wait()
        return carry

    lax.fori_loop(0, tm, start, 0, unroll=ROW_DMA_UNROLL)
    lax.fori_loop(0, tm, wait, 0, unroll=ROW_DMA_UNROLL)
    meta = meta_ref[...]
    out = meta[:, 2:3] * y1_ref[...] + meta[:, 3:4] * y2_ref[...]
    o_ref[...] = _rmsnorm(x_ref[...] + out, gf_ref[...], RMS_EPS)


def _combine(ys, pos1, pos2, x2d, meta, gain_final, tm):
    m, d = x2d.shape
    smem = lambda: pl.BlockSpec((tm,), lambda i: (i,), memory_space=pltpu.SMEM)
    row = lambda i: (i, 0)
    return pl.pallas_call(
        functools.partial(_combine_kernel, tm=tm),
        grid=(m // tm,),
        in_specs=[smem(), smem(), pl.BlockSpec(memory_space=pl.ANY), pl.BlockSpec((tm, d), row),
                  pl.BlockSpec((tm, LANES), row), _const_spec((1, d))],
        out_specs=pl.BlockSpec((tm, d), row),
        out_shape=jax.ShapeDtypeStruct((m, d), F32),
        scratch_shapes=[pltpu.VMEM((tm, d), F32), pltpu.VMEM((tm, d), F32), pltpu.SemaphoreType.DMA(())],
        compiler_params=_params("arbitrary"),
        name="moe_combine",
    )(pos1, pos2, ys, x2d, meta, gain_final.reshape(1, d))


def _moe(x2d, norm_gain, w_router, b_router, wgu_bf, wd_bf, gain_final, tm):
    m, d = x2d.shape
    tmx = 512 if m >= 4096 else 128
    h, meta, counts = _router(x2d, norm_gain, w_router, b_router, tm)
    counts = counts[0, :N_EXPERTS].astype(jnp.int32)
    padded = (counts + tmx - 1) // tmx * tmx
    ends = jnp.cumsum(padded)
    starts = ends - padded
    n_slots = 2 * m + N_EXPERTS * tmx
    n_tiles_max = n_slots // tmx
    tile_start = jnp.arange(n_tiles_max, dtype=jnp.int32) * tmx
    tile_expert = jnp.minimum(jnp.sum((tile_start[:, None] >= ends[None, :]).astype(jnp.int32), axis=1),
                              N_EXPERTS - 1).astype(jnp.int32)
    n_tiles = (ends[-1:] // tmx).astype(jnp.int32)
    experts = jnp.arange(N_EXPERTS, dtype=jnp.int32)[None, :]
    start_of = lambda e: jnp.sum(jnp.where(e[:, None] == experts, starts[None, :], 0), axis=1)
    e1, e2 = meta[:, 0].astype(jnp.int32), meta[:, 1].astype(jnp.int32)
    pos1 = start_of(e1) + meta[:, 4].astype(jnp.int32)
    pos2 = start_of(e2) + meta[:, 5].astype(jnp.int32)
    td = min(256, m)
    xs = _dispatch(h, pos1, pos2, n_slots, td)
    ys = _expert_ffn(xs, tile_expert, n_tiles, wgu_bf, wd_bf, tmx)
    return _combine(ys, pos1, pos2, x2d, meta, gain_final, td)


def _forward(x, caches, w):
    b, t, d = x.shape
    m = b * t
    tm = min(512, m)
    x2d = x.reshape(m, d)
    lambda_init = 0.8 - 0.6 * math.exp(-0.3 * 0)
    width = DIFF_HEADS * 2 * DIFF_HEAD_DIM
    ncb = width // LANES

    if caches is None:
        pbf, ka, va, kb, vb, vt, kaug = _qkv_proj(x2d, w["norm_attn"], w["attn_w_in"], tm,
                                                  w["attn_w_vt"], t)
    else:
        pbf, ka, va, kb, vb = _qkv_proj(x2d, w["norm_attn"], w["attn_w_in"], tm)
    pbf3 = pbf.reshape(b, t, 6 * width)
    if caches is None:
        oa = _diff_attn_prompt(pbf3, kaug.reshape(b, t, 2 * width), vt, 0, w["diff_lambda"],
                               w["diff_subln"], tile=min(1024, t), lambda_init=lambda_init)
        tqb = min(256, t)
        n_parts = BAND_PAST_MAX // tqb + 1
        bias = _band_bias(w["band_rel_bias"], 0, -BAND_PAST_MAX, tqb, n_parts * tqb)
        ob = _band_attn(pbf3, pbf3, pbf3, 3 * ncb, 4 * ncb, 5 * ncb, bias,
                        tq=tqb, n_parts=n_parts, tkp=tqb)
        n_band = min(BAND_PAST_MAX, t)
        new_bk = kb.reshape(b, t, BAND_HEADS, BAND_HEAD_DIM)[:, t - n_band:]
        new_bv = vb.reshape(b, t, BAND_HEADS, BAND_HEAD_DIM)[:, t - n_band:]
    else:
        ck_a, cv_a, ck_b, cv_b = caches
        p_len, pb_len = ck_a.shape[1], ck_b.shape[1]
        cat = lambda c, lo: jnp.concatenate(
            [c.reshape(b, c.shape[1], width).astype(BF16), pbf3[:, :, lo:lo + width]], axis=1)
        k_a, v_a = cat(ck_a, width), cat(cv_a, 2 * width)
        k_b, v_b = cat(ck_b, 4 * width), cat(cv_b, 5 * width)
        oa = _diff_attn(pbf3, k_a, v_a, 0, 0, 0, w["diff_lambda"], w["diff_subln"],
                        tq=t, tk=p_len + t, q_off=p_len, lambda_init=lambda_init)
        bias = _band_bias(w["band_rel_bias"], p_len, p_len - pb_len, t, pb_len + t)
        ob = _band_attn(pbf3, k_b, v_b, 3 * ncb, 0, 0, bias, tq=t, n_parts=1, tkp=pb_len + t)
        new_bk = kb.reshape(b, t, BAND_HEADS, BAND_HEAD_DIM)
        new_bv = vb.reshape(b, t, BAND_HEADS, BAND_HEAD_DIM)
    new_dk = ka.reshape(b, t, 2 * DIFF_HEADS, DIFF_HEAD_DIM)
    new_dv = va.reshape(b, t, DIFF_HEADS, 2 * DIFF_HEAD_DIM)

    x2d = _attn_out_ffn(oa.reshape(m, width), ob.reshape(m, width), x2d, w["attn_w_out"],
                        w["norm_ffn"], w["ffn_w_gu"], w["ffn_w_down"], tm)

    seg = min(t, GMLP_CHUNK)
    x2d, v_rows = _gmlp(x2d, w["norm_gmlp"], w["gmlp_w_in"], w["gmlp_b_in"], w["gmlp_ln_g"],
                        w["gmlp_ln_b"], w["gmlp_w_s"], w["gmlp_b_s"], w["gmlp_w_out"], tm, seg)
    y = _moe(x2d, w["norm_moe"], w["moe_w_router"], w["moe_b_router"], w["moe_w_gu"], w["moe_w_down"],
             w["norm_final"], tm)
    return (y.reshape(b, t, d), new_dk[None], new_dv[None], new_bk[None], new_bv[None],
            v_rows.reshape(b, t, -1)[None])


def kernel(x_prompt, x_sample, cache_diff_k, cache_diff_v, cache_band_k, cache_band_v,
           norm_attn, attn_w_in, diff_lambda, diff_subln, band_rel_bias, attn_w_out,
           norm_ffn, ffn_w_gu, ffn_w_down,
           norm_gmlp, gmlp_w_in, gmlp_b_in, gmlp_ln_g, gmlp_ln_b, gmlp_w_s, gmlp_b_s, gmlp_w_out,
           norm_moe, moe_w_router, moe_b_router, moe_w_gu, moe_w_down, norm_final):
    w = {
        "norm_attn": norm_attn[0], "attn_w_in": attn_w_in[0].astype(BF16),
        "attn_w_vt": attn_w_in[0][:, 1024:1536].T.astype(BF16),
        "diff_lambda": diff_lambda[0], "diff_subln": diff_subln[0],
        "band_rel_bias": band_rel_bias[0], "attn_w_out": attn_w_out[0].astype(BF16),
        "norm_ffn": norm_ffn[0], "ffn_w_gu": ffn_w_gu[0].astype(BF16),
        "ffn_w_down": ffn_w_down[0].astype(BF16),
        "norm_gmlp": norm_gmlp[0], "gmlp_w_in": gmlp_w_in[0].astype(BF16),
        "gmlp_b_in": gmlp_b_in[0], "gmlp_ln_g": gmlp_ln_g[0], "gmlp_ln_b": gmlp_ln_b[0],
        "gmlp_w_s": gmlp_w_s[0], "gmlp_b_s": gmlp_b_s[0], "gmlp_w_out": gmlp_w_out[0].astype(BF16),
        "norm_moe": norm_moe[0], "moe_w_router": moe_w_router[0], "moe_b_router": moe_b_router[0],
        "moe_w_gu": moe_w_gu[0].astype(BF16), "moe_w_down": moe_w_down[0].astype(BF16),
        "norm_final": norm_final,
    }
    y_p, dk_p, dv_p, bk_p, bv_p, _ = _forward(x_prompt, None, w)
    y_s, dk_s, dv_s, bk_s, bv_s, gv_s = _forward(
        x_sample, (cache_diff_k[0], cache_diff_v[0], cache_band_k[0], cache_band_v[0]), w)
    return (y_p, y_s, dk_p, dv_p, bk_p, bv_p, dk_s, dv_s, bk_s, bv_s, gv_s)
```

```python
import functools
import math

import numpy as np
import jax
import jax.numpy as jnp
from jax import lax
from jax.experimental import pallas as pl
from jax.experimental.pallas import tpu as pltpu

CHUNK = 64
DIFF_HEADS = 4
DIFF_HEAD_DIM = 64
DIFF_SUBLN_EPS = 1e-5
BAND_HEADS = 8
BAND_HEAD_DIM = 64
BAND_PREV_CHUNKS = 8
BAND_PAST_MAX = BAND_PREV_CHUNKS * CHUNK
REL_CLIP = 256
GMLP_GROUPS = 8
GMLP_CHUNK = 128
N_EXPERTS = 8
RMS_EPS = 1e-6
LN_EPS = 1e-5
MASK_VALUE = -1e30

LANES = 128
ONES_ROWS = 16
VMEM_LIMIT_BYTES = 56 << 20

F32 = jnp.float32
BF16 = jnp.bfloat16
LOG2E = math.log2(math.e)
Q_SCALE = DIFF_HEAD_DIM ** -0.5 * LOG2E
NT_DIMS = (((1,), (1,)), ((), ()))


def _params(*sem, flags=None):
    return pltpu.CompilerParams(dimension_semantics=sem, vmem_limit_bytes=VMEM_LIMIT_BYTES, flags=flags)


def _const_spec(shape):
    nd = len(shape)
    return pl.BlockSpec(shape, lambda *_: (0,) * nd, pipeline_mode=pl.Buffered(1))


def _rmsnorm(x, g, eps):
    return (x * lax.rsqrt(jnp.mean(x * x, axis=-1, keepdims=True) + eps)) * g


def _dot(a, b):
    return jnp.dot(a, b, preferred_element_type=F32)


def _swiglu(h, wgu_ref, wd_ref, d_ff, tf):
    y = None
    for c in range(d_ff // tf):
        g = _dot(h, wgu_ref[:, c * tf:(c + 1) * tf])
        u = _dot(h, wgu_ref[:, d_ff + c * tf:d_ff + (c + 1) * tf])
        a = (g * jax.nn.sigmoid(g) * u).astype(BF16)
        part = _dot(a, wd_ref[c * tf:(c + 1) * tf, :])
        y = part if y is None else y + part
    return y


def _qkv_kernel(x_ref, g_ref, w_ref, *rest, width, with_vt, tiles_per_seq):
    if with_vt:
        wvt_ref, pbf_ref, ka_ref, va_ref, kb_ref, vb_ref, vt_ref, kaug_ref = rest
    else:
        pbf_ref, ka_ref, va_ref, kb_ref, vb_ref = rest
    tm = x_ref.shape[0]
    h = _rmsnorm(x_ref[...], g_ref[...], RMS_EPS).astype(BF16)
    f32_outs = {1: ka_ref, 2: va_ref, 4: kb_ref, 5: vb_ref}
    for c in range(6):
        r = _dot(h, w_ref[:, c * width:(c + 1) * width])
        if c in f32_outs:
            f32_outs[c][...] = r
        else:
            r = r * Q_SCALE
        pbf_ref[:, c * width:(c + 1) * width] = r.astype(BF16)
        if with_vt and c == 1:
            pos = (pl.program_id(0) % tiles_per_seq) * tm + lax.broadcasted_iota(jnp.int32, (tm, LANES), 0)
            lane = lax.broadcasted_iota(jnp.int32, (tm, LANES), 1)
            hi = (pos // LANES).astype(F32)
            lo = (pos % LANES).astype(F32)
            feats = jnp.where(lane < 3, hi, jnp.where(lane < 6, lo, 0.0)).astype(BF16)
            for hd in range(DIFF_HEADS):
                kaug_ref[:, 2 * hd * LANES:(2 * hd + 1) * LANES] = r[:, hd * LANES:(hd + 1) * LANES].astype(BF16)
                kaug_ref[:, (2 * hd + 1) * LANES:(2 * hd + 2) * LANES] = feats
    if with_vt:
        vt_ref[...] = lax.dot_general(wvt_ref[...], h, NT_DIMS, preferred_element_type=F32).astype(BF16)


def _qkv_proj(x2d, gain, w_bf, tm, wvt_bf=None, seq=None):
    m, d = x2d.shape
    width = w_bf.shape[1] // 6
    row = lambda i: (i, 0)
    with_vt = wvt_bf is not None
    in_specs = [pl.BlockSpec((tm, d), row), _const_spec((1, d)), _const_spec(w_bf.shape)]
    out_specs = [pl.BlockSpec((tm, 6 * width), row)] + [pl.BlockSpec((tm, width), row)] * 4
    out_shape = [jax.ShapeDtypeStruct((m, 6 * width), BF16)] + [jax.ShapeDtypeStruct((m, width), F32)] * 4
    args = [x2d, gain.reshape(1, d), w_bf]
    nt = 1
    if with_vt:
        nt = seq // tm
        assert seq <= 256 * LANES
        in_specs.append(_const_spec(wvt_bf.shape))
        out_specs.append(pl.BlockSpec((None, width, tm), lambda i: (i // nt, 0, i % nt)))
        out_shape.append(jax.ShapeDtypeStruct((m // seq, width, seq), BF16))
        out_specs.append(pl.BlockSpec((tm, 2 * width), row))
        out_shape.append(jax.ShapeDtypeStruct((m, 2 * width), BF16))
        args.append(wvt_bf)
    return pl.pallas_call(
        functools.partial(_qkv_kernel, width=width, with_vt=with_vt, tiles_per_seq=nt),
        grid=(m // tm,),
        in_specs=in_specs, out_specs=out_specs, out_shape=out_shape,
        compiler_params=_params("parallel"),
        name="qkv_proj",
    )(*args)


def _split_heads_rows(q, tq):
    qf = q.astype(F32)
    lane = lax.broadcasted_iota(jnp.int32, qf.shape, 1)
    lo = jnp.where(lane < DIFF_HEAD_DIM, qf, 0.0)
    hi = jnp.where(lane >= DIFF_HEAD_DIM, qf, 0.0)
    return jnp.concatenate([lo, hi], axis=0).astype(BF16)


def _diff_attn_kernel(ii_ref, jj_ref, last_ref, q_ref, k_ref, v_ref, slope_ref, lam_ref, g_ref,
                      o_ref, qz_ref, m_ref, l_ref, acc_ref, *, tq, tk, q_off, lambda_init):
    s = pl.program_id(2)
    i = ii_ref[s]
    j = jj_ref[s]

    @pl.when(j == 0)
    def _init():
        qz_ref[...] = _split_heads_rows(q_ref[...], tq)
        m_ref[...] = jnp.full(m_ref.shape, MASK_VALUE, F32)
        l_ref[...] = jnp.zeros(l_ref.shape, F32)
        acc_ref[...] = jnp.zeros(acc_ref.shape, F32)

    sc = lax.dot_general(qz_ref[...], k_ref[...], NT_DIMS, preferred_element_type=F32)
    row = lax.broadcasted_iota(jnp.int32, (2 * tq, 1), 0)
    row = jnp.where(row >= tq, row - tq, row)
    qpos = q_off + i * tq + row
    kpos = j * tk + lax.broadcasted_iota(jnp.int32, (1, tk), 1)
    dist = jnp.abs(qpos - kpos).astype(F32)
    sc = sc - slope_ref[:, 0:1] * dist
    visible = (kpos // CHUNK) <= (qpos // CHUNK)
    sc = jnp.where(visible, sc, MASK_VALUE)

    m_prev = m_ref[...]
    m_new = jnp.maximum(m_prev, jnp.max(sc, axis=1, keepdims=True))
    alpha = jnp.exp2(m_prev - m_new)
    p = jnp.exp2(sc - m_new)
    l_ref[...] = alpha * l_ref[...] + jnp.sum(p, axis=1, keepdims=True)
    acc_ref[...] = alpha * acc_ref[...] + _dot(p.astype(BF16), v_ref[...])
    m_ref[...] = m_new

    @pl.when(last_ref[s] == 1)
    def _finish():
        lp = lam_ref[...]
        lam = (jnp.exp(jnp.sum(lp[0:1] * lp[1:2], axis=1, keepdims=True))
               - jnp.exp(jnp.sum(lp[2:3] * lp[3:4], axis=1, keepdims=True)) + lambda_init)
        o_all = acc_ref[...] / l_ref[...]
        o = o_all[:tq] - lam * o_all[tq:]
        o = _rmsnorm(o, g_ref[...], DIFF_SUBLN_EPS) * (1.0 - lambda_init)
        o_ref[...] = o.astype(o_ref.dtype)


def _diff_attn(q_arr, k_arr, v_arr, q_cb, k_cb, v_cb, lam_p, subln_g, *, tq, tk, q_off, lambda_init):
    b, t_q, _ = q_arr.shape
    t_k = k_arr.shape[1]
    nq, nk = t_q // tq, t_k // tk
    pairs = [(i, j) for i in range(nq) for j in range(nk)
             if (j * tk) // CHUNK <= (q_off + i * tq + tq - 1) // CHUNK]
    ii = np.array([p[0] for p in pairs], np.int32)
    jj = np.array([p[1] for p in pairs], np.int32)
    last = np.array([1 if (n + 1 == len(pairs) or pairs[n + 1][0] != pairs[n][0]) else 0
                     for n in range(len(pairs))], np.int32)
    slopes = jnp.asarray(_alibi_slopes_log2()[:, None, None] * np.ones((1, 1, LANES), np.float32))

    grid_spec = pltpu.PrefetchScalarGridSpec(
        num_scalar_prefetch=3,
        grid=(b, DIFF_HEADS, len(pairs)),
        in_specs=[
            pl.BlockSpec((None, tq, LANES), lambda b_, h, s, ii, jj, la: (b_, ii[s], q_cb + h)),
            pl.BlockSpec((None, tk, LANES), lambda b_, h, s, ii, jj, la: (b_, jj[s], k_cb + h)),
            pl.BlockSpec((None, tk, LANES), lambda b_, h, s, ii, jj, la: (b_, jj[s], v_cb + h)),
            pl.BlockSpec((None, 1, LANES), lambda b_, h, s, ii, jj, la: (h, 0, 0)),
            pl.BlockSpec((4, DIFF_HEAD_DIM), lambda b_, h, s, ii, jj, la: (0, 0)),
            pl.BlockSpec((1, LANES), lambda b_, h, s, ii, jj, la: (0, 0)),
        ],
        out_specs=pl.BlockSpec((None, tq, LANES), lambda b_, h, s, ii, jj, la: (b_, ii[s], h)),
        scratch_shapes=[
            pltpu.VMEM((2 * tq, LANES), BF16),
            pltpu.VMEM((2 * tq, 1), F32),
            pltpu.VMEM((2 * tq, 1), F32),
            pltpu.VMEM((2 * tq, LANES), F32),
        ],
    )
    return pl.pallas_call(
        functools.partial(_diff_attn_kernel, tq=tq, tk=tk, q_off=q_off, lambda_init=lambda_init),
        grid_spec=grid_spec,
        out_shape=jax.ShapeDtypeStruct((b, t_q, DIFF_HEADS * LANES), BF16),
        compiler_params=_params("parallel", "parallel", "arbitrary"),
        name="diff_attn",
    )(jnp.asarray(ii), jnp.asarray(jj), jnp.asarray(last), q_arr, k_arr, v_arr, slopes,
      lam_p, subln_g.reshape(1, LANES))


def _alibi_slopes_log2():
    return (2.0 ** (-8.0 * np.arange(1, DIFF_HEADS + 1, dtype=np.float64) / DIFF_HEADS) * LOG2E).astype(np.float32)


def _diff_prompt_kernel(ii_ref, jj_ref, last_ref, lin_ref, q_ref, k_ref, vt_ref, dbias_ref, qfeat_ref,
                        lam_ref, g_ref, o_ref, qzt_ref, s0_ref, s1_ref, mx0_ref, mx1_ref, m_ref, acc_ref,
                        *, tq, tk, cb, lambda_init):
    s = pl.program_id(2)
    n_pairs = pl.num_programs(2) - 1
    sp = jnp.minimum(s, n_pairs - 1)
    sc = jnp.maximum(s - 1, 0)
    consuming = s > 0
    nblk = 2 * tq // cb
    blk = lambda c: slice(c * cb, (c + 1) * cb)

    @pl.when(s == 0)
    def _first_step():
        s1_ref[...] = jnp.zeros(s1_ref.shape, F32)
        mx1_ref[...] = jnp.zeros(mx1_ref.shape, F32)

    @pl.when(jj_ref[sp] == 0)
    def _new_queries():
        qf = q_ref[...].astype(F32)
        lane = lax.broadcasted_iota(jnp.int32, qf.shape, 1)
        qzt_ref[0:LANES, 0:tq] = jnp.where(lane < DIFF_HEAD_DIM, qf, 0.0).T.astype(BF16)
        qzt_ref[0:LANES, tq:2 * tq] = jnp.where(lane >= DIFF_HEAD_DIM, qf, 0.0).T.astype(BF16)
        qzt_ref[LANES:2 * LANES, :] = jnp.broadcast_to(qfeat_ref[...], (LANES, 2 * tq)).astype(BF16)

    @pl.when(jj_ref[sc] == 0)
    def _reset():
        m_ref[...] = jnp.full(m_ref.shape, MASK_VALUE, F32)
        acc_ref[...] = jnp.zeros(acc_ref.shape, F32)

    def stages(s_prod, mx_prod, s_cons, mx_cons):
        @pl.when(jnp.logical_and(consuming, lin_ref[sc] == 0))
        def _diagonal():
            for c in range(nblk):
                v = s_cons[:, blk(c)] + dbias_ref[:, (c * cb) % tq:(c * cb) % tq + cb]
                s_cons[:, blk(c)] = v
                mx_cons[:, blk(c)] = jnp.max(v, axis=0, keepdims=True)

        k = k_ref[...]
        vt = jnp.concatenate([vt_ref[...], jnp.ones((ONES_ROWS, tk), BF16)], axis=0)
        for c in range(nblk):
            v = _dot(k, qzt_ref[:, blk(c)])
            s_prod[:, blk(c)] = v
            mx_prod[:, blk(c)] = jnp.max(v, axis=0, keepdims=True)
        m_prev = [m_ref[:, blk(c)] for c in range(nblk)]
        acc_prev = [acc_ref[:, blk(c)] for c in range(nblk)]
        m_out, acc_out = [], []
        for c in range(nblk):
            m_new = jnp.maximum(m_prev[c], mx_cons[:, blk(c)])
            alpha = jnp.exp2(m_prev[c] - m_new)
            p = jnp.exp2((s_cons[:, blk(c)] - m_new).astype(BF16))
            acc_out.append(alpha * acc_prev[c] + _dot(vt, p))
            m_out.append(m_new)
        for c in range(nblk):
            m_ref[:, blk(c)] = m_out[c]
            acc_ref[:, blk(c)] = acc_out[c]

    @pl.when(s % 2 == 0)
    def _even():
        stages(s0_ref, mx0_ref, s1_ref, mx1_ref)

    @pl.when(s % 2 == 1)
    def _odd():
        stages(s1_ref, mx1_ref, s0_ref, mx0_ref)

    @pl.when(jnp.logical_and(consuming, last_ref[sc] == 1))
    def _finish():
        lp = lam_ref[...]
        lam = (jnp.exp(jnp.sum(lp[0:1] * lp[1:2], axis=1, keepdims=True))
               - jnp.exp(jnp.sum(lp[2:3] * lp[3:4], axis=1, keepdims=True)) + lambda_init)
        o_all = acc_ref[0:LANES, :] / acc_ref[LANES:LANES + 1, :]
        o = o_all[:, 0:tq] - lam * o_all[:, tq:2 * tq]
        o = o * lax.rsqrt(jnp.mean(o * o, axis=0, keepdims=True) + DIFF_SUBLN_EPS)
        o = o * g_ref[...] * (1.0 - lambda_init)
        o_ref[...] = o.T.astype(o_ref.dtype)


def _bf16_split3(x):
    import ml_dtypes
    rnd = lambda v: v.astype(ml_dtypes.bfloat16).astype(np.float32)
    x = np.asarray(x, np.float32)
    hi = rnd(x)
    mid = rnd(x - hi)
    lo = rnd(x - hi - mid)
    return hi, mid, lo


def _diff_attn_prompt(pbf3, kaug3, vt_arr, q_cb, lam_p, subln_g, *, tile, lambda_init):
    b, t, _ = pbf3.shape
    tq = tk = tile
    n = t // tile
    pairs = [(i, j) for i in range(n) for j in range(i + 1)]
    ii = np.array([p[0] for p in pairs], np.int32)
    jj = np.array([p[1] for p in pairs], np.int32)
    last = (ii == jj).astype(np.int32)
    lin = (ii != jj).astype(np.int32)
    c2 = _alibi_slopes_log2()
    qfeat = np.zeros((DIFF_HEADS, LANES, 1), np.float32)
    for r, part in enumerate(_bf16_split3(c2)):
        qfeat[:, r, 0] = part * LANES
        qfeat[:, 3 + r, 0] = part
    pos = np.arange(tile)
    d = (pos[None, :] - pos[:, None]).astype(np.float32)
    visible = (pos[:, None] // CHUNK) <= (pos[None, :] // CHUNK)
    dbias = jnp.asarray(np.where(visible[None], 2.0 * c2[:, None, None] * np.minimum(d, 0.0)[None],
                                 MASK_VALUE).astype(np.float32))
    cb = min(256, 2 * tq)
    n_pairs = len(pairs)
    prod = lambda s: jnp.minimum(s, n_pairs - 1)
    cons = lambda s: jnp.maximum(s - 1, 0)
    idx = lambda f: (lambda b_, h, s, ii, jj, la, li: f(b_, h, s, ii, jj))
    grid_spec = pltpu.PrefetchScalarGridSpec(
        num_scalar_prefetch=4,
        grid=(b, DIFF_HEADS, n_pairs + 1),
        in_specs=[
            pl.BlockSpec((None, tq, LANES), idx(lambda b_, h, s, ii, jj: (b_, ii[prod(s)], q_cb + h))),
            pl.BlockSpec((None, tk, 2 * LANES), idx(lambda b_, h, s, ii, jj: (b_, jj[prod(s)], h))),
            pl.BlockSpec((None, LANES, tk), idx(lambda b_, h, s, ii, jj: (b_, h, jj[cons(s)]))),
            pl.BlockSpec((None, tk, tq), idx(lambda b_, h, s, ii, jj: (h, 0, 0))),
            pl.BlockSpec((None, LANES, 1), idx(lambda b_, h, s, ii, jj: (h, 0, 0))),
            pl.BlockSpec((4, DIFF_HEAD_DIM), idx(lambda b_, h, s, ii, jj: (0, 0))),
            pl.BlockSpec((LANES, 1), idx(lambda b_, h, s, ii, jj: (0, 0))),
        ],
        out_specs=pl.BlockSpec((None, tq, LANES), idx(lambda b_, h, s, ii, jj: (b_, ii[cons(s)], h))),
        scratch_shapes=[
            pltpu.VMEM((2 * LANES, 2 * tq), BF16),
            pltpu.VMEM((tk, 2 * tq), F32), pltpu.VMEM((tk, 2 * tq), F32),
            pltpu.VMEM((1, 2 * tq), F32), pltpu.VMEM((1, 2 * tq), F32),
            pltpu.VMEM((1, 2 * tq), F32),
            pltpu.VMEM((LANES + ONES_ROWS, 2 * tq), F32),
        ],
    )
    return pl.pallas_call(
        functools.partial(_diff_prompt_kernel, tq=tq, tk=tk, cb=cb, lambda_init=lambda_init),
        grid_spec=grid_spec,
        out_shape=jax.ShapeDtypeStruct((b, t, DIFF_HEADS * LANES), BF16),
        compiler_params=_params("parallel", "parallel", "arbitrary"),
        name="diff_attn_prompt",
    )(jnp.asarray(ii), jnp.asarray(jj), jnp.asarray(last), jnp.asarray(lin), pbf3, kaug3, vt_arr, dbias,
      jnp.asarray(qfeat), lam_p, subln_g.reshape(LANES, 1))


def _band_attn_kernel(*refs, tq, n_parts, tkp):
    q_ref = refs[0]
    k_refs = refs[1:1 + n_parts]
    v_refs = refs[1 + n_parts:1 + 2 * n_parts]
    bias_ref = refs[1 + 2 * n_parts]
    o_ref = refs[2 + 2 * n_parts]
    i = pl.program_id(2)
    qz = _split_heads_rows(q_ref[...], tq)
    scores = []
    for m in range(n_parts):
        sc = lax.dot_general(qz, k_refs[m][...], NT_DIMS, preferred_element_type=F32)
        sc = sc + bias_ref[:, m * tkp:(m + 1) * tkp]
        if n_parts > 1:
            sc = jnp.where(i - (n_parts - 1) + m >= 0, sc, MASK_VALUE)
        scores.append(sc)
    mx = functools.reduce(jnp.maximum, [jnp.max(sc, axis=1, keepdims=True) for sc in scores])
    den = None
    num = None
    for m in range(n_parts):
        p = jnp.exp2(scores[m] - mx)
        d = jnp.sum(p, axis=1, keepdims=True)
        r = _dot(p.astype(BF16), v_refs[m][...])
        den = d if den is None else den + d
        num = r if num is None else num + r
    r = num / den
    lane = lax.broadcasted_iota(jnp.int32, (tq, LANES), 1)
    o_ref[...] = jnp.where(lane < BAND_HEAD_DIM, r[:tq], r[tq:]).astype(o_ref.dtype)


def _band_bias_kernel(line_ref, o_ref, *, q0, k0):
    tq, nk = o_ref.shape
    line = jnp.broadcast_to(line_ref[...], (tq, line_ref.shape[1]))
    rolled = pltpu.roll(line, 0, 1, stride=1, stride_axis=0)
    qc = (q0 + lax.broadcasted_iota(jnp.int32, (tq, nk), 0)) >> 6
    kc = (k0 + lax.broadcasted_iota(jnp.int32, (tq, nk), 1)) >> 6
    visible = (kc <= qc) & (qc - kc <= BAND_PREV_CHUNKS)
    o_ref[...] = jnp.where(visible, rolled[:, 0:nk], MASK_VALUE)


def _band_bias(rel_table, q0, k0, tq, nk):
    assert CHUNK == 64
    width = pl.next_power_of_2(tq + nk - 1)
    y = np.arange(width)
    c_minus_r = np.where(y < nk, y, y - width)
    rel = np.clip((q0 - k0) - c_minus_r, -(CHUNK - 1), REL_CLIP) + (CHUNK - 1)
    line = (rel_table.astype(F32) * LOG2E)[:, rel].reshape(BAND_HEADS, 1, width)
    return pl.pallas_call(
        functools.partial(_band_bias_kernel, q0=q0, k0=k0),
        grid=(BAND_HEADS,),
        in_specs=[pl.BlockSpec((None, 1, width), lambda h: (h, 0, 0))],
        out_specs=pl.BlockSpec((None, tq, nk), lambda h: (h // 2, h % 2, 0)),
        out_shape=jax.ShapeDtypeStruct((BAND_HEADS // 2, 2 * tq, nk), F32),
        compiler_params=_params("parallel"),
        name="band_bias",
    )(line)


def _band_attn(q_arr, k_arr, v_arr, q_cb, k_cb, v_cb, bias, *, tq, n_parts, tkp):
    b, t_q, _ = q_arr.shape
    n_pairs = BAND_HEADS // 2

    def kv_spec(cb, m):
        return pl.BlockSpec((None, tkp, LANES),
                            lambda p, b_, i: (b_, jnp.maximum(i - (n_parts - 1) + m, 0), cb + p))

    return pl.pallas_call(
        functools.partial(_band_attn_kernel, tq=tq, n_parts=n_parts, tkp=tkp),
        grid=(n_pairs, b, t_q // tq),
        in_specs=[pl.BlockSpec((None, tq, LANES), lambda p, b_, i: (b_, i, q_cb + p))]
        + [kv_spec(k_cb, m) for m in range(n_parts)]
        + [kv_spec(v_cb, m) for m in range(n_parts)]
        + [pl.BlockSpec((None, 2 * tq, n_parts * tkp), lambda p, b_, i: (p, 0, 0))],
        out_specs=pl.BlockSpec((None, tq, LANES), lambda p, b_, i: (b_, i, p)),
        out_shape=jax.ShapeDtypeStruct((b, t_q, n_pairs * LANES), BF16),
        compiler_params=_params("parallel", "parallel", "parallel"),
        name="band_attn",
    )(q_arr, *([k_arr] * n_parts), *([v_arr] * n_parts), bias)


def _attn_out_ffn_kernel(oa_ref, ob_ref, x_ref, wo_ref, g_ref, wgu_ref, wd_ref, o_ref, *, d_ff, tf):
    half = oa_ref.shape[1]
    x1 = x_ref[...] + _dot(oa_ref[...], wo_ref[0:half, :]) + _dot(ob_ref[...], wo_ref[half:2 * half, :])
    h = _rmsnorm(x1, g_ref[...], RMS_EPS).astype(BF16)
    o_ref[...] = x1 + _swiglu(h, wgu_ref, wd_ref, d_ff, tf)


def _attn_out_ffn(oa, ob, x2d, wo_bf, gain, wgu_bf, wd_bf, tm):
    m, d = x2d.shape
    d_ff = wd_bf.shape[0]
    row = lambda i: (i, 0)
    return pl.pallas_call(
        functools.partial(_attn_out_ffn_kernel, d_ff=d_ff, tf=256),
        grid=(m // tm,),
        in_specs=[pl.BlockSpec((tm, oa.shape[1]), row), pl.BlockSpec((tm, ob.shape[1]), row),
                  pl.BlockSpec((tm, d), row), _const_spec(wo_bf.shape), _const_spec((1, d)),
                  _const_spec(wgu_bf.shape), _const_spec(wd_bf.shape)],
        out_specs=pl.BlockSpec((tm, d), row),
        out_shape=jax.ShapeDtypeStruct((m, d), F32),
        compiler_params=_params("parallel"),
        name="attn_out_ffn",
    )(oa, ob, x2d, wo_bf, gain.reshape(1, d), wgu_bf, wd_bf)


def _gelu(z):
    return 0.5 * z * (1.0 + lax.erf(z * (2.0 ** -0.5)))


def _gmlp_kernel(x_ref, g_ref, win_ref, bin_ref, lng_ref, lnb_ref, ws_ref, bs_ref, wout_ref,
                 o_ref, v_ref, u_s, act_s, *, seg, cw):
    tm, gd = v_ref.shape
    gw = gd // GMLP_GROUPS
    x = x_ref[...]
    h = _rmsnorm(x, g_ref[...], RMS_EPS).astype(BF16)
    for c in range(gd // cw):
        lo, hi = c * cw, (c + 1) * cw
        u_s[:, lo:hi] = _gelu(_dot(h, win_ref[:, lo:hi]) + bin_ref[:, lo:hi])
        v_ref[:, lo:hi] = _gelu(_dot(h, win_ref[:, gd + lo:gd + hi]) + bin_ref[:, gd + lo:gd + hi])
    v = v_ref[...]
    mu = jnp.mean(v, axis=-1, keepdims=True)
    var = jnp.mean(jnp.square(v - mu), axis=-1, keepdims=True)
    v_ref[...] = (v - mu) * lax.rsqrt(var + LN_EPS) * lng_ref[...] + lnb_ref[...]
    r_i = lax.broadcasted_iota(jnp.int32, (seg, seg), 0)
    c_i = lax.broadcasted_iota(jnp.int32, (seg, seg), 1)
    for g in range(GMLP_GROUPS):
        w = jnp.where(r_i >= c_i, ws_ref[g], 0.0).astype(BF16)
        for n in range(tm // seg):
            rows = slice(n * seg, (n + 1) * seg)
            cols = slice(g * gw, (g + 1) * gw)
            sv = _dot(w, v_ref[rows, cols].astype(BF16)) + bs_ref[g]
            act_s[rows, cols] = (u_s[rows, cols] * sv).astype(BF16)
    o_ref[...] = x + _dot(act_s[...], wout_ref[...])


def _gmlp(x2d, gain, win_bf, b_in, ln_g, ln_b, w_s, b_s, wout_bf, tm, seg):
    m, d = x2d.shape
    gd = wout_bf.shape[0]
    row = lambda i: (i, 0)
    ws = w_s[:, :seg, :seg]
    bs = b_s[:, :seg, None]
    return pl.pallas_call(
        functools.partial(_gmlp_kernel, seg=seg, cw=512),
        grid=(m // tm,),
        in_specs=[pl.BlockSpec((tm, d), row), _const_spec((1, d)), _const_spec(win_bf.shape),
                  _const_spec((1, 2 * gd)), _const_spec((1, gd)), _const_spec((1, gd)),
                  _const_spec(ws.shape), _const_spec(bs.shape), _const_spec(wout_bf.shape)],
        out_specs=[pl.BlockSpec((tm, d), row), pl.BlockSpec((tm, gd), row)],
        out_shape=[jax.ShapeDtypeStruct((m, d), F32), jax.ShapeDtypeStruct((m, gd), F32)],
        scratch_shapes=[pltpu.VMEM((tm, gd), F32), pltpu.VMEM((tm, gd), BF16)],
        compiler_params=_params("parallel"),
        name="gmlp",
    )(x2d, gain.reshape(1, d), win_bf, b_in.reshape(1, 2 * gd), ln_g.reshape(1, gd),
      ln_b.reshape(1, gd), ws, bs, wout_bf)


def _router_kernel(x_ref, g_ref, whi_ref, wlo_ref, b_ref, h_ref, meta_ref, cnt_ref):
    h = _rmsnorm(x_ref[...], g_ref[...], RMS_EPS)
    h_hi = h.astype(BF16)
    h_lo = (h - h_hi.astype(F32)).astype(BF16)
    logits = (_dot(h_hi, whi_ref[...]) + _dot(h_hi, wlo_ref[...]) + _dot(h_lo, whi_ref[...])
              + b_ref[...])
    tm = logits.shape[0]
    lane = lax.broadcasted_iota(jnp.int32, logits.shape, 1)
    logits = jnp.where(lane < N_EXPERTS, logits, -jnp.inf)
    v1 = jnp.max(logits, axis=1, keepdims=True)
    i1 = jnp.min(jnp.where(logits == v1, lane, LANES), axis=1, keepdims=True)
    rest = jnp.where(lane == i1, -jnp.inf, logits)
    v2 = jnp.max(rest, axis=1, keepdims=True)
    i2 = jnp.min(jnp.where(rest == v2, lane, LANES), axis=1, keepdims=True)
    e2 = jnp.exp(v2 - v1)
    den = 1.0 + e2

    @pl.when(pl.program_id(0) == 0)
    def _init():
        cnt_ref[...] = jnp.zeros(cnt_ref.shape, F32)

    oh1 = lane == i1
    oh2 = lane == i2
    r_i = lax.broadcasted_iota(jnp.int32, (tm, tm), 0)
    c_i = lax.broadcasted_iota(jnp.int32, (tm, tm), 1)
    before = jnp.where(c_i < r_i, 1.0, 0.0).astype(BF16)
    cum1 = _dot(before, jnp.where(oh1, 1.0, 0.0).astype(BF16))
    cum2 = _dot(before, jnp.where(oh2, 1.0, 0.0).astype(BF16))
    n1 = jnp.sum(jnp.where(oh1, 1.0, 0.0), axis=0, keepdims=True)
    n2 = jnp.sum(jnp.where(oh2, 1.0, 0.0), axis=0, keepdims=True)
    base = cnt_ref[...]
    rank1 = jnp.sum(jnp.where(oh1, base + cum1, 0.0), axis=1, keepdims=True)
    rank2 = jnp.sum(jnp.where(oh2, base + n1 + cum2, 0.0), axis=1, keepdims=True)
    cnt_ref[...] = base + n1 + n2
    meta = jnp.where(lane == 0, i1.astype(F32), 0.0)
    meta = jnp.where(lane == 1, i2.astype(F32), meta)
    meta = jnp.where(lane == 2, 1.0 / den, meta)
    meta = jnp.where(lane == 3, e2 / den, meta)
    meta = jnp.where(lane == 4, rank1, meta)
    meta = jnp.where(lane == 5, rank2, meta)
    meta_ref[...] = meta
    h_ref[...] = h


def _router(x2d, gain, w_router, b_router, tm):
    m, d = x2d.shape
    assert 2 * m < 2 ** 24
    w_pad = jnp.pad(w_router, ((0, 0), (0, LANES - N_EXPERTS)))
    w_hi = w_pad.astype(BF16)
    w_lo = (w_pad - w_hi.astype(F32)).astype(BF16)
    b_pad = jnp.pad(b_router, (0, LANES - N_EXPERTS)).reshape(1, LANES)
    row = lambda i: (i, 0)
    return pl.pallas_call(
        _router_kernel,
        grid=(m // tm,),
        in_specs=[pl.BlockSpec((tm, d), row), _const_spec((1, d)), _const_spec((d, LANES)),
                  _const_spec((d, LANES)), _const_spec((1, LANES))],
        out_specs=[pl.BlockSpec((tm, d), row), pl.BlockSpec((tm, LANES), row),
                   pl.BlockSpec((1, LANES), lambda i: (0, 0))],
        out_shape=[jax.ShapeDtypeStruct((m, d), F32), jax.ShapeDtypeStruct((m, LANES), F32),
                   jax.ShapeDtypeStruct((1, LANES), F32)],
        compiler_params=_params("arbitrary"),
        name="moe_router",
    )(x2d, gain.reshape(1, d), w_hi, w_lo, b_pad)


def _row_copy(src_hbm, src_row, dst, dst_row, sem):
    return pltpu.make_async_copy(src_hbm.at[pl.ds(src_row, 1)], dst.at[pl.ds(dst_row, 1)], sem)


ROW_DMA_UNROLL = 8


def _dispatch_kernel(pos1_ref, pos2_ref, h_ref, xs_in_hbm, xs_hbm, sem, *, tm):
    del xs_in_hbm

    def start(r, carry):
        _row_copy(h_ref, r, xs_hbm, pos1_ref[r], sem).start()
        _row_copy(h_ref, r, xs_hbm, pos2_ref[r], sem).start()
        return carry

    def wait(r, carry):
        _row_copy(h_ref, 0, xs_hbm, 0, sem).wait()
        _row_copy(h_ref, 0, xs_hbm, 0, sem).wait()
        return carry

    lax.fori_loop(0, tm, start, 0, unroll=ROW_DMA_UNROLL)
    lax.fori_loop(0, tm, wait, 0, unroll=ROW_DMA_UNROLL)


def _dispatch(h2d, pos1, pos2, n_slots, tm):
    m, d = h2d.shape
    smem = lambda: pl.BlockSpec((tm,), lambda i: (i,), memory_space=pltpu.SMEM)
    return pl.pallas_call(
        functools.partial(_dispatch_kernel, tm=tm),
        grid=(m // tm,),
        in_specs=[smem(), smem(), pl.BlockSpec((tm, d), lambda i: (i, 0)), pl.BlockSpec(memory_space=pl.ANY)],
        out_specs=pl.BlockSpec(memory_space=pl.ANY),
        out_shape=jax.ShapeDtypeStruct((n_slots, d), h2d.dtype),
        scratch_shapes=[pltpu.SemaphoreType.DMA(())],
        input_output_aliases={3: 0},
        compiler_params=_params("arbitrary"),
        name="moe_dispatch",
    )(pos1, pos2, h2d, jnp.zeros((n_slots, d), h2d.dtype))


def _expert_ffn_kernel(te_ref, nt_ref, xs_ref, wgu_ref, wd_ref, ys_ref, *, d_ff, tf):
    @pl.when(pl.program_id(0) < nt_ref[0])
    def _():
        ys_ref[...] = _swiglu(xs_ref[...].astype(BF16), wgu_ref, wd_ref, d_ff, tf)

    @pl.when(pl.program_id(0) >= nt_ref[0])
    def _():
        ys_ref[...] = jnp.zeros(ys_ref.shape, F32)


def _expert_ffn(xs, tile_expert, n_tiles, wgu_bf, wd_bf, tmx):
    n_slots, d = xs.shape
    _, d_ff, _ = wd_bf.shape
    row = lambda i, te, nt: (jnp.minimum(i, nt[0] - 1), 0)
    grid_spec = pltpu.PrefetchScalarGridSpec(
        num_scalar_prefetch=2,
        grid=(n_slots // tmx,),
        in_specs=[pl.BlockSpec((tmx, d), row),
                  pl.BlockSpec((None, d, 2 * d_ff), lambda i, te, nt: (te[i], 0, 0)),
                  pl.BlockSpec((None, d_ff, d), lambda i, te, nt: (te[i], 0, 0))],
        out_specs=pl.BlockSpec((tmx, d), lambda i, te, nt: (i, 0)),
    )
    return pl.pallas_call(
        functools.partial(_expert_ffn_kernel, d_ff=d_ff, tf=256),
        grid_spec=grid_spec,
        out_shape=jax.ShapeDtypeStruct((n_slots, d), F32),
        compiler_params=_params("arbitrary"),
        name="moe_experts",
    )(tile_expert, n_tiles, xs, wgu_bf, wd_bf)


def _combine_kernel(pos1_ref, pos2_ref, ys_hbm, x_ref, meta_ref, gf_ref, o_ref, y1_ref, y2_ref, sem, *, tm):
    def start(r, carry):
        _row_copy(ys_hbm, pos1_ref[r], y1_ref, r, sem).start()
        _row_copy(ys_hbm, pos2_ref[r], y2_ref, r, sem).start()
        return carry

    def wait(r, carry):
        _row_copy(ys_hbm, 0, y1_ref, 0, sem).wait()
        _row_copy(ys_hbm, 0, y2_ref, 0, sem).wait()
        return carry

    lax.fori_loop(0, tm, start, 0, unroll=ROW_DMA_UNROLL)
    lax.fori_loop(0, tm, wait, 0, unroll=ROW_DMA_UNROLL)
    meta = meta_ref[...]
    out = meta[:, 2:3] * y1_ref[...] + meta[:, 3:4] * y2_ref[...]
    o_ref[...] = _rmsnorm(x_ref[...] + out, gf_ref[...], RMS_EPS)


def _combine(ys, pos1, pos2, x2d, meta, gain_final, tm):
    m, d = x2d.shape
    smem = lambda: pl.BlockSpec((tm,), lambda i: (i,), memory_space=pltpu.SMEM)
    row = lambda i: (i, 0)
    return pl.pallas_call(
        functools.partial(_combine_kernel, tm=tm),
        grid=(m // tm,),
        in_specs=[smem(), smem(), pl.BlockSpec(memory_space=pl.ANY), pl.BlockSpec((tm, d), row),
                  pl.BlockSpec((tm, LANES), row), _const_spec((1, d))],
        out_specs=pl.BlockSpec((tm, d), row),
        out_shape=jax.ShapeDtypeStruct((m, d), F32),
        scratch_shapes=[pltpu.VMEM((tm, d), F32), pltpu.VMEM((tm, d), F32), pltpu.SemaphoreType.DMA(())],
        compiler_params=_params("arbitrary"),
        name="moe_combine",
    )(pos1, pos2, ys, x2d, meta, gain_final.reshape(1, d))


def _moe(x2d, norm_gain, w_router, b_router, wgu_bf, wd_bf, gain_final, tm):
    m, d = x2d.shape
    tmx = 512 if m >= 4096 else 128
    h, meta, counts = _router(x2d, norm_gain, w_router, b_router, tm)
    counts = counts[0, :N_EXPERTS].astype(jnp.int32)
    padded = (counts + tmx - 1) // tmx * tmx
    ends = jnp.cumsum(padded)
    starts = ends - padded
    n_slots = 2 * m + N_EXPERTS * tmx
    n_tiles_max = n_slots // tmx
    tile_start = jnp.arange(n_tiles_max, dtype=jnp.int32) * tmx
    tile_expert = jnp.minimum(jnp.sum((tile_start[:, None] >= ends[None, :]).astype(jnp.int32), axis=1),
                              N_EXPERTS - 1).astype(jnp.int32)
    n_tiles = (ends[-1:] // tmx).astype(jnp.int32)
    experts = jnp.arange(N_EXPERTS, dtype=jnp.int32)[None, :]
    start_of = lambda e: jnp.sum(jnp.where(e[:, None] == experts, starts[None, :], 0), axis=1)
    e1, e2 = meta[:, 0].astype(jnp.int32), meta[:, 1].astype(jnp.int32)
    pos1 = start_of(e1) + meta[:, 4].astype(jnp.int32)
    pos2 = start_of(e2) + meta[:, 5].astype(jnp.int32)
    td = min(256, m)
    xs = _dispatch(h, pos1, pos2, n_slots, td)
    ys = _expert_ffn(xs, tile_expert, n_tiles, wgu_bf, wd_bf, tmx)
    return _combine(ys, pos1, pos2, x2d, meta, gain_final, td)


def _forward(x, caches, w):
    b, t, d = x.shape
    m = b * t
    tm = min(512, m)
    x2d = x.reshape(m, d)
    lambda_init = 0.8 - 0.6 * math.exp(-0.3 * 0)
    width = DIFF_HEADS * 2 * DIFF_HEAD_DIM
    ncb = width // LANES

    if caches is None:
        pbf, ka, va, kb, vb, vt, kaug = _qkv_proj(x2d, w["norm_attn"], w["attn_w_in"], tm,
                                                  w["attn_w_vt"], t)
    else:
        pbf, ka, va, kb, vb = _qkv_proj(x2d, w["norm_attn"], w["attn_w_in"], tm)
    pbf3 = pbf.reshape(b, t, 6 * width)
    if caches is None:
        oa = _diff_attn_prompt(pbf3, kaug.reshape(b, t, 2 * width), vt, 0, w["diff_lambda"],
                               w["diff_subln"], tile=min(1024, t), lambda_init=lambda_init)
        tqb = min(256, t)
        n_parts = BAND_PAST_MAX // tqb + 1
        bias = _band_bias(w["band_rel_bias"], 0, -BAND_PAST_MAX, tqb, n_parts * tqb)
        ob = _band_attn(pbf3, pbf3, pbf3, 3 * ncb, 4 * ncb, 5 * ncb, bias,
                        tq=tqb, n_parts=n_parts, tkp=tqb)
        n_band = min(BAND_PAST_MAX, t)
        new_bk = kb.reshape(b, t, BAND_HEADS, BAND_HEAD_DIM)[:, t - n_band:]
        new_bv = vb.reshape(b, t, BAND_HEADS, BAND_HEAD_DIM)[:, t - n_band:]
    else:
        ck_a, cv_a, ck_b, cv_b = caches
        p_len, pb_len = ck_a.shape[1], ck_b.shape[1]
        cat = lambda c, lo: jnp.concatenate(
            [c.reshape(b, c.shape[1], width).astype(BF16), pbf3[:, :, lo:lo + width]], axis=1)
        k_a, v_a = cat(ck_a, width), cat(cv_a, 2 * width)
        k_b, v_b = cat(ck_b, 4 * width), cat(cv_b, 5 * width)
        oa = _diff_attn(pbf3, k_a, v_a, 0, 0, 0, w["diff_lambda"], w["diff_subln"],
                        tq=t, tk=p_len + t, q_off=p_len, lambda_init=lambda_init)
        bias = _band_bias(w["band_rel_bias"], p_len, p_len - pb_len, t, pb_len + t)
        ob = _band_attn(pbf3, k_b, v_b, 3 * ncb, 0, 0, bias, tq=t, n_parts=1, tkp=pb_len + t)
        new_bk = kb.reshape(b, t, BAND_HEADS, BAND_HEAD_DIM)
        new_bv = vb.reshape(b, t, BAND_HEADS, BAND_HEAD_DIM)
    new_dk = ka.reshape(b, t, 2 * DIFF_HEADS, DIFF_HEAD_DIM)
    new_dv = va.reshape(b, t, DIFF_HEADS, 2 * DIFF_HEAD_DIM)

    x2d = _attn_out_ffn(oa.reshape(m, width), ob.reshape(m, width), x2d, w["attn_w_out"],
                        w["norm_ffn"], w["ffn_w_gu"], w["ffn_w_down"], tm)

    seg = min(t, GMLP_CHUNK)
    x2d, v_rows = _gmlp(x2d, w["norm_gmlp"], w["gmlp_w_in"], w["gmlp_b_in"], w["gmlp_ln_g"],
                        w["gmlp_ln_b"], w["gmlp_w_s"], w["gmlp_b_s"], w["gmlp_w_out"], tm, seg)
    y = _moe(x2d, w["norm_moe"], w["moe_w_router"], w["moe_b_router"], w["moe_w_gu"], w["moe_w_down"],
             w["norm_final"], tm)
    return (y.reshape(b, t, d), new_dk[None], new_dv[None], new_bk[None], new_bv[None],
            v_rows.reshape(b, t, -1)[None])


def kernel(x_prompt, x_sample, cache_diff_k, cache_diff_v, cache_band_k, cache_band_v,
           norm_attn, attn_w_in, diff_lambda, diff_subln, band_rel_bias, attn_w_out,
           norm_ffn, ffn_w_gu, ffn_w_down,
           norm_gmlp, gmlp_w_in, gmlp_b_in, gmlp_ln_g, gmlp_ln_b, gmlp_w_s, gmlp_b_s, gmlp_w_out,
           norm_moe, moe_w_router, moe_b_router, moe_w_gu, moe_w_down, norm_final):
    w = {
        "norm_attn": norm_attn[0], "attn_w_in": attn_w_in[0].astype(BF16),
        "attn_w_vt": attn_w_in[0][:, 1024:1536].T.astype(BF16),
        "diff_lambda": diff_lambda[0], "diff_subln": diff_subln[0],
        "band_rel_bias": band_rel_bias[0], "attn_w_out": attn_w_out[0].astype(BF16),
        "norm_ffn": norm_ffn[0], "ffn_w_gu": ffn_w_gu[0].astype(BF16),
        "ffn_w_down": ffn_w_down[0].astype(BF16),
        "norm_gmlp": norm_gmlp[0], "gmlp_w_in": gmlp_w_in[0].astype(BF16),
        "gmlp_b_in": gmlp_b_in[0], "gmlp_ln_g": gmlp_ln_g[0], "gmlp_ln_b": gmlp_ln_b[0],
        "gmlp_w_s": gmlp_w_s[0], "gmlp_b_s": gmlp_b_s[0], "gmlp_w_out": gmlp_w_out[0].astype(BF16),
        "norm_moe": norm_moe[0], "moe_w_router": moe_w_router[0], "moe_b_router": moe_b_router[0],
        "moe_w_gu": moe_w_gu[0].astype(BF16), "moe_w_down": moe_w_down[0].astype(BF16),
        "norm_final": norm_final,
    }
    y_p, dk_p, dv_p, bk_p, bv_p, _ = _forward(x_prompt, None, w)
    y_s, dk_s, dv_s, bk_s, bv_s, gv_s = _forward(
        x_sample, (cache_diff_k[0], cache_diff_v[0], cache_band_k[0], cache_band_v[0]), w)
    return (y_p, y_s, dk_p, dv_p, bk_p, bv_p, dk_s, dv_s, bk_s, bv_s, gv_s)
```

```python
import functools
import math

import numpy as np
import jax
import jax.numpy as jnp
from jax import lax
from jax.experimental import pallas as pl
from jax.experimental.pallas import tpu as pltpu

CHUNK = 64
DIFF_HEADS = 4
DIFF_HEAD_DIM = 64
DIFF_SUBLN_EPS = 1e-5
BAND_HEADS = 8
BAND_HEAD_DIM = 64
BAND_PREV_CHUNKS = 8
BAND_PAST_MAX = BAND_PREV_CHUNKS * CHUNK
REL_CLIP = 256
GMLP_GROUPS = 8
GMLP_CHUNK = 128
N_EXPERTS = 8
RMS_EPS = 1e-6
LN_EPS = 1e-5
MASK_VALUE = -1e30

LANES = 128
ONES_ROWS = 16
VMEM_LIMIT_BYTES = 56 << 20

F32 = jnp.float32
BF16 = jnp.bfloat16
LOG2E = math.log2(math.e)
Q_SCALE = DIFF_HEAD_DIM ** -0.5 * LOG2E
NT_DIMS = (((1,), (1,)), ((), ()))


def _params(*sem, flags=None):
    return pltpu.CompilerParams(dimension_semantics=sem, vmem_limit_bytes=VMEM_LIMIT_BYTES, flags=flags)


def _const_spec(shape):
    nd = len(shape)
    return pl.BlockSpec(shape, lambda *_: (0,) * nd, pipeline_mode=pl.Buffered(1))


def _rmsnorm(x, g, eps):
    return (x * lax.rsqrt(jnp.mean(x * x, axis=-1, keepdims=True) + eps)) * g


def _dot(a, b):
    return jnp.dot(a, b, preferred_element_type=F32)


def _swiglu(h, wgu_ref, wd_ref, d_ff, tf):
    y = None
    for c in range(d_ff // tf):
        g = _dot(h, wgu_ref[:, c * tf:(c + 1) * tf])
        u = _dot(h, wgu_ref[:, d_ff + c * tf:d_ff + (c + 1) * tf])
        a = (g * jax.nn.sigmoid(g) * u).astype(BF16)
        part = _dot(a, wd_ref[c * tf:(c + 1) * tf, :])
        y = part if y is None else y + part
    return y


def _qkv_kernel(x_ref, g_ref, w_ref, *rest, width, with_vt, tiles_per_seq):
    if with_vt:
        wvt_ref, pbf_ref, ka_ref, va_ref, kb_ref, vb_ref, vt_ref, kaug_ref = rest
    else:
        pbf_ref, ka_ref, va_ref, kb_ref, vb_ref = rest
    tm = x_ref.shape[0]
    h = _rmsnorm(x_ref[...], g_ref[...], RMS_EPS).astype(BF16)
    f32_outs = {1: ka_ref, 2: va_ref, 4: kb_ref, 5: vb_ref}
    for c in range(6):
        r = _dot(h, w_ref[:, c * width:(c + 1) * width])
        if c in f32_outs:
            f32_outs[c][...] = r
        else:
            r = r * Q_SCALE
        pbf_ref[:, c * width:(c + 1) * width] = r.astype(BF16)
        if with_vt and c == 1:
            pos = (pl.program_id(0) % tiles_per_seq) * tm + lax.broadcasted_iota(jnp.int32, (tm, LANES), 0)
            lane = lax.broadcasted_iota(jnp.int32, (tm, LANES), 1)
            hi = (pos // LANES).astype(F32)
            lo = (pos % LANES).astype(F32)
            feats = jnp.where(lane < 3, hi, jnp.where(lane < 6, lo, 0.0)).astype(BF16)
            for hd in range(DIFF_HEADS):
                kaug_ref[:, 2 * hd * LANES:(2 * hd + 1) * LANES] = r[:, hd * LANES:(hd + 1) * LANES].astype(BF16)
                kaug_ref[:, (2 * hd + 1) * LANES:(2 * hd + 2) * LANES] = feats
    if with_vt:
        vt_ref[...] = lax.dot_general(wvt_ref[...], h, NT_DIMS, preferred_element_type=F32).astype(BF16)


def _qkv_proj(x2d, gain, w_bf, tm, wvt_bf=None, seq=None):
    m, d = x2d.shape
    width = w_bf.shape[1] // 6
    row = lambda i: (i, 0)
    with_vt = wvt_bf is not None
    in_specs = [pl.BlockSpec((tm, d), row), _const_spec((1, d)), _const_spec(w_bf.shape)]
    out_specs = [pl.BlockSpec((tm, 6 * width), row)] + [pl.BlockSpec((tm, width), row)] * 4
    out_shape = [jax.ShapeDtypeStruct((m, 6 * width), BF16)] + [jax.ShapeDtypeStruct((m, width), F32)] * 4
    args = [x2d, gain.reshape(1, d), w_bf]
    nt = 1
    if with_vt:
        nt = seq // tm
        assert seq <= 256 * LANES
        in_specs.append(_const_spec(wvt_bf.shape))
        vt_rows = wvt_bf.shape[0]
        out_specs.append(pl.BlockSpec((None, vt_rows, tm), lambda i: (i // nt, 0, i % nt)))
        out_shape.append(jax.ShapeDtypeStruct((m // seq, vt_rows, seq), BF16))
        out_specs.append(pl.BlockSpec((tm, 2 * width), row))
        out_shape.append(jax.ShapeDtypeStruct((m, 2 * width), BF16))
        args.append(wvt_bf)
    return pl.pallas_call(
        functools.partial(_qkv_kernel, width=width, with_vt=with_vt, tiles_per_seq=nt),
        grid=(m // tm,),
        in_specs=in_specs, out_specs=out_specs, out_shape=out_shape,
        compiler_params=_params("parallel"),
        name="qkv_proj",
    )(*args)


def _split_heads_rows(q, tq):
    qf = q.astype(F32)
    lane = lax.broadcasted_iota(jnp.int32, qf.shape, 1)
    lo = jnp.where(lane < DIFF_HEAD_DIM, qf, 0.0)
    hi = jnp.where(lane >= DIFF_HEAD_DIM, qf, 0.0)
    return jnp.concatenate([lo, hi], axis=0).astype(BF16)


def _diff_attn_kernel(ii_ref, jj_ref, last_ref, q_ref, k_ref, v_ref, slope_ref, lam_ref, g_ref,
                      o_ref, qz_ref, m_ref, l_ref, acc_ref, *, tq, tk, q_off, lambda_init):
    s = pl.program_id(2)
    i = ii_ref[s]
    j = jj_ref[s]

    @pl.when(j == 0)
    def _init():
        qz_ref[...] = _split_heads_rows(q_ref[...], tq)
        m_ref[...] = jnp.full(m_ref.shape, MASK_VALUE, F32)
        l_ref[...] = jnp.zeros(l_ref.shape, F32)
        acc_ref[...] = jnp.zeros(acc_ref.shape, F32)

    sc = lax.dot_general(qz_ref[...], k_ref[...], NT_DIMS, preferred_element_type=F32)
    row = lax.broadcasted_iota(jnp.int32, (2 * tq, 1), 0)
    row = jnp.where(row >= tq, row - tq, row)
    qpos = q_off + i * tq + row
    kpos = j * tk + lax.broadcasted_iota(jnp.int32, (1, tk), 1)
    dist = jnp.abs(qpos - kpos).astype(F32)
    sc = sc - slope_ref[:, 0:1] * dist
    visible = (kpos // CHUNK) <= (qpos // CHUNK)
    sc = jnp.where(visible, sc, MASK_VALUE)

    m_prev = m_ref[...]
    m_new = jnp.maximum(m_prev, jnp.max(sc, axis=1, keepdims=True))
    alpha = jnp.exp2(m_prev - m_new)
    p = jnp.exp2(sc - m_new)
    l_ref[...] = alpha * l_ref[...] + jnp.sum(p, axis=1, keepdims=True)
    acc_ref[...] = alpha * acc_ref[...] + _dot(p.astype(BF16), v_ref[...])
    m_ref[...] = m_new

    @pl.when(last_ref[s] == 1)
    def _finish():
        lp = lam_ref[...]
        lam = (jnp.exp(jnp.sum(lp[0:1] * lp[1:2], axis=1, keepdims=True))
               - jnp.exp(jnp.sum(lp[2:3] * lp[3:4], axis=1, keepdims=True)) + lambda_init)
        o_all = acc_ref[...] / l_ref[...]
        o = o_all[:tq] - lam * o_all[tq:]
        o = _rmsnorm(o, g_ref[...], DIFF_SUBLN_EPS) * (1.0 - lambda_init)
        o_ref[...] = o.astype(o_ref.dtype)


def _diff_attn(q_arr, k_arr, v_arr, q_cb, k_cb, v_cb, lam_p, subln_g, *, tq, tk, q_off, lambda_init):
    b, t_q, _ = q_arr.shape
    t_k = k_arr.shape[1]
    nq, nk = t_q // tq, t_k // tk
    pairs = [(i, j) for i in range(nq) for j in range(nk)
             if (j * tk) // CHUNK <= (q_off + i * tq + tq - 1) // CHUNK]
    ii = np.array([p[0] for p in pairs], np.int32)
    jj = np.array([p[1] for p in pairs], np.int32)
    last = np.array([1 if (n + 1 == len(pairs) or pairs[n + 1][0] != pairs[n][0]) else 0
                     for n in range(len(pairs))], np.int32)
    slopes = jnp.asarray(_alibi_slopes_log2()[:, None, None] * np.ones((1, 1, LANES), np.float32))

    grid_spec = pltpu.PrefetchScalarGridSpec(
        num_scalar_prefetch=3,
        grid=(b, DIFF_HEADS, len(pairs)),
        in_specs=[
            pl.BlockSpec((None, tq, LANES), lambda b_, h, s, ii, jj, la: (b_, ii[s], q_cb + h)),
            pl.BlockSpec((None, tk, LANES), lambda b_, h, s, ii, jj, la: (b_, jj[s], k_cb + h)),
            pl.BlockSpec((None, tk, LANES), lambda b_, h, s, ii, jj, la: (b_, jj[s], v_cb + h)),
            pl.BlockSpec((None, 1, LANES), lambda b_, h, s, ii, jj, la: (h, 0, 0)),
            pl.BlockSpec((4, DIFF_HEAD_DIM), lambda b_, h, s, ii, jj, la: (0, 0)),
            pl.BlockSpec((1, LANES), lambda b_, h, s, ii, jj, la: (0, 0)),
        ],
        out_specs=pl.BlockSpec((None, tq, LANES), lambda b_, h, s, ii, jj, la: (b_, ii[s], h)),
        scratch_shapes=[
            pltpu.VMEM((2 * tq, LANES), BF16),
            pltpu.VMEM((2 * tq, 1), F32),
            pltpu.VMEM((2 * tq, 1), F32),
            pltpu.VMEM((2 * tq, LANES), F32),
        ],
    )
    return pl.pallas_call(
        functools.partial(_diff_attn_kernel, tq=tq, tk=tk, q_off=q_off, lambda_init=lambda_init),
        grid_spec=grid_spec,
        out_shape=jax.ShapeDtypeStruct((b, t_q, DIFF_HEADS * LANES), BF16),
        compiler_params=_params("parallel", "parallel", "arbitrary"),
        name="diff_attn",
    )(jnp.asarray(ii), jnp.asarray(jj), jnp.asarray(last), q_arr, k_arr, v_arr, slopes,
      lam_p, subln_g.reshape(1, LANES))


def _alibi_slopes_log2():
    return (2.0 ** (-8.0 * np.arange(1, DIFF_HEADS + 1, dtype=np.float64) / DIFF_HEADS) * LOG2E).astype(np.float32)


def _diff_prompt_kernel(ii_ref, jj_ref, last_ref, lin_ref, q_ref, k_ref, vt_ref, dbias_ref, qfeat_ref,
                        lam_ref, g_ref, o_ref, qzt_ref, s0_ref, s1_ref, mx0_ref, mx1_ref, m_ref, acc_ref,
                        *, tq, tk, cb, lambda_init):
    s = pl.program_id(2)
    n_pairs = pl.num_programs(2) - 1
    sp = jnp.minimum(s, n_pairs - 1)
    sc = jnp.maximum(s - 1, 0)
    consuming = s > 0
    nblk = 2 * tq // cb
    blk = lambda c: slice(c * cb, (c + 1) * cb)

    @pl.when(s == 0)
    def _first_step():
        s1_ref[...] = jnp.zeros(s1_ref.shape, F32)
        mx1_ref[...] = jnp.zeros(mx1_ref.shape, F32)

    @pl.when(jj_ref[sp] == 0)
    def _new_queries():
        qf = q_ref[...].astype(F32)
        lane = lax.broadcasted_iota(jnp.int32, qf.shape, 1)
        qzt_ref[0:LANES, 0:tq] = jnp.where(lane < DIFF_HEAD_DIM, qf, 0.0).T.astype(BF16)
        qzt_ref[0:LANES, tq:2 * tq] = jnp.where(lane >= DIFF_HEAD_DIM, qf, 0.0).T.astype(BF16)
        qzt_ref[LANES:2 * LANES, :] = jnp.broadcast_to(qfeat_ref[...], (LANES, 2 * tq)).astype(BF16)

    @pl.when(jj_ref[sc] == 0)
    def _reset():
        m_ref[...] = jnp.full(m_ref.shape, MASK_VALUE, F32)
        acc_ref[...] = jnp.zeros(acc_ref.shape, F32)

    def stages(s_prod, mx_prod, s_cons, mx_cons):
        @pl.when(jnp.logical_and(consuming, lin_ref[sc] == 0))
        def _diagonal():
            for c in range(nblk):
                v = s_cons[:, blk(c)] + dbias_ref[:, (c * cb) % tq:(c * cb) % tq + cb]
                s_cons[:, blk(c)] = v
                mx_cons[:, blk(c)] = jnp.max(v, axis=0, keepdims=True)

        k = k_ref[...]
        vt = jnp.concatenate([vt_ref[...], jnp.ones((ONES_ROWS, tk), BF16)], axis=0)
        for c in range(nblk):
            v = _dot(k, qzt_ref[:, blk(c)])
            s_prod[:, blk(c)] = v
            mx_prod[:, blk(c)] = jnp.max(v, axis=0, keepdims=True)
        m_prev = [m_ref[:, blk(c)] for c in range(nblk)]
        acc_prev = [acc_ref[:, blk(c)] for c in range(nblk)]
        m_out, acc_out = [], []
        for c in range(nblk):
            m_new = jnp.maximum(m_prev[c], mx_cons[:, blk(c)])
            alpha = jnp.exp2(m_prev[c] - m_new)
            p = jnp.exp2((s_cons[:, blk(c)] - m_new).astype(BF16))
            acc_out.append(alpha * acc_prev[c] + _dot(vt, p))
            m_out.append(m_new)
        for c in range(nblk):
            m_ref[:, blk(c)] = m_out[c]
            acc_ref[:, blk(c)] = acc_out[c]

    @pl.when(s % 2 == 0)
    def _even():
        stages(s0_ref, mx0_ref, s1_ref, mx1_ref)

    @pl.when(s % 2 == 1)
    def _odd():
        stages(s1_ref, mx1_ref, s0_ref, mx0_ref)

    @pl.when(jnp.logical_and(consuming, last_ref[sc] == 1))
    def _finish():
        lp = lam_ref[...]
        lam = (jnp.exp(jnp.sum(lp[0:1] * lp[1:2], axis=1, keepdims=True))
               - jnp.exp(jnp.sum(lp[2:3] * lp[3:4], axis=1, keepdims=True)) + lambda_init)
        o_all = acc_ref[0:LANES, :] / acc_ref[LANES:LANES + 1, :]
        o = o_all[:, 0:tq] - lam * o_all[:, tq:2 * tq]
        o = o * lax.rsqrt(jnp.mean(o * o, axis=0, keepdims=True) + DIFF_SUBLN_EPS)
        o = o * g_ref[...] * (1.0 - lambda_init)
        o_ref[...] = o.T.astype(o_ref.dtype)


def _bf16_split3(x):
    import ml_dtypes
    rnd = lambda v: v.astype(ml_dtypes.bfloat16).astype(np.float32)
    x = np.asarray(x, np.float32)
    hi = rnd(x)
    mid = rnd(x - hi)
    lo = rnd(x - hi - mid)
    return hi, mid, lo


def _diff_attn_prompt(pbf3, kaug3, vt_arr, q_cb, lam_p, subln_g, *, tile, lambda_init):
    b, t, _ = pbf3.shape
    tq = tk = tile
    n = t // tile
    pairs = [(i, j) for i in range(n) for j in range(i + 1)]
    ii = np.array([p[0] for p in pairs], np.int32)
    jj = np.array([p[1] for p in pairs], np.int32)
    last = (ii == jj).astype(np.int32)
    lin = (ii != jj).astype(np.int32)
    c2 = _alibi_slopes_log2()
    qfeat = np.zeros((DIFF_HEADS, LANES, 1), np.float32)
    for r, part in enumerate(_bf16_split3(c2)):
        qfeat[:, r, 0] = part * LANES
        qfeat[:, 3 + r, 0] = part
    pos = np.arange(tile)
    d = (pos[None, :] - pos[:, None]).astype(np.float32)
    visible = (pos[:, None] // CHUNK) <= (pos[None, :] // CHUNK)
    dbias = jnp.asarray(np.where(visible[None], 2.0 * c2[:, None, None] * np.minimum(d, 0.0)[None],
                                 MASK_VALUE).astype(np.float32))
    cb = min(256, 2 * tq)
    n_pairs = len(pairs)
    prod = lambda s: jnp.minimum(s, n_pairs - 1)
    cons = lambda s: jnp.maximum(s - 1, 0)
    idx = lambda f: (lambda b_, h, s, ii, jj, la, li: f(b_, h, s, ii, jj))
    grid_spec = pltpu.PrefetchScalarGridSpec(
        num_scalar_prefetch=4,
        grid=(b, DIFF_HEADS, n_pairs + 1),
        in_specs=[
            pl.BlockSpec((None, tq, LANES), idx(lambda b_, h, s, ii, jj: (b_, ii[prod(s)], q_cb + h))),
            pl.BlockSpec((None, tk, 2 * LANES), idx(lambda b_, h, s, ii, jj: (b_, jj[prod(s)], h))),
            pl.BlockSpec((None, LANES, tk), idx(lambda b_, h, s, ii, jj: (b_, h, jj[cons(s)]))),
            pl.BlockSpec((None, tk, tq), idx(lambda b_, h, s, ii, jj: (h, 0, 0))),
            pl.BlockSpec((None, LANES, 1), idx(lambda b_, h, s, ii, jj: (h, 0, 0))),
            pl.BlockSpec((4, DIFF_HEAD_DIM), idx(lambda b_, h, s, ii, jj: (0, 0))),
            pl.BlockSpec((LANES, 1), idx(lambda b_, h, s, ii, jj: (0, 0))),
        ],
        out_specs=pl.BlockSpec((None, tq, LANES), idx(lambda b_, h, s, ii, jj: (b_, ii[cons(s)], h))),
        scratch_shapes=[
            pltpu.VMEM((2 * LANES, 2 * tq), BF16),
            pltpu.VMEM((tk, 2 * tq), F32), pltpu.VMEM((tk, 2 * tq), F32),
            pltpu.VMEM((1, 2 * tq), F32), pltpu.VMEM((1, 2 * tq), F32),
            pltpu.VMEM((1, 2 * tq), F32),
            pltpu.VMEM((LANES + ONES_ROWS, 2 * tq), F32),
        ],
    )
    return pl.pallas_call(
        functools.partial(_diff_prompt_kernel, tq=tq, tk=tk, cb=cb, lambda_init=lambda_init),
        grid_spec=grid_spec,
        out_shape=jax.ShapeDtypeStruct((b, t, DIFF_HEADS * LANES), BF16),
        compiler_params=_params("parallel", "parallel", "arbitrary"),
        name="diff_attn_prompt",
    )(jnp.asarray(ii), jnp.asarray(jj), jnp.asarray(last), jnp.asarray(lin), pbf3, kaug3, vt_arr, dbias,
      jnp.asarray(qfeat), lam_p, subln_g.reshape(LANES, 1))


def _band_attn_kernel(*refs, tq, n_parts, tkp):
    q_ref = refs[0]
    k_refs = refs[1:1 + n_parts]
    v_refs = refs[1 + n_parts:1 + 2 * n_parts]
    bias_ref = refs[1 + 2 * n_parts]
    o_ref = refs[2 + 2 * n_parts]
    i = pl.program_id(2)
    qz = _split_heads_rows(q_ref[...], tq)
    scores = []
    for m in range(n_parts):
        sc = lax.dot_general(qz, k_refs[m][...], NT_DIMS, preferred_element_type=F32)
        sc = sc + bias_ref[:, m * tkp:(m + 1) * tkp]
        if n_parts > 1:
            sc = jnp.where(i - (n_parts - 1) + m >= 0, sc, MASK_VALUE)
        scores.append(sc)
    mx = functools.reduce(jnp.maximum, [jnp.max(sc, axis=1, keepdims=True) for sc in scores])
    den = None
    num = None
    for m in range(n_parts):
        p = jnp.exp2(scores[m] - mx)
        d = jnp.sum(p, axis=1, keepdims=True)
        r = _dot(p.astype(BF16), v_refs[m][...])
        den = d if den is None else den + d
        num = r if num is None else num + r
    r = num / den
    lane = lax.broadcasted_iota(jnp.int32, (tq, LANES), 1)
    o_ref[...] = jnp.where(lane < BAND_HEAD_DIM, r[:tq], r[tq:]).astype(o_ref.dtype)


def _band_bias_kernel(line_ref, o_ref, *, q0, k0):
    tq, nk = o_ref.shape
    line = jnp.broadcast_to(line_ref[...], (tq, line_ref.shape[1]))
    rolled = pltpu.roll(line, 0, 1, stride=1, stride_axis=0)
    qc = (q0 + lax.broadcasted_iota(jnp.int32, (tq, nk), 0)) >> 6
    kc = (k0 + lax.broadcasted_iota(jnp.int32, (tq, nk), 1)) >> 6
    visible = (kc <= qc) & (qc - kc <= BAND_PREV_CHUNKS)
    o_ref[...] = jnp.where(visible, rolled[:, 0:nk], MASK_VALUE)


def _band_bias_t_kernel(line_ref, o_ref, *, q0, k0):
    nk, tq = o_ref.shape
    line = jnp.broadcast_to(line_ref[...], (nk, line_ref.shape[1]))
    rolled = pltpu.roll(line, 0, 1, stride=1, stride_axis=0)
    kc = (k0 + lax.broadcasted_iota(jnp.int32, (nk, tq), 0)) >> 6
    qc = (q0 + lax.broadcasted_iota(jnp.int32, (nk, tq), 1)) >> 6
    visible = (kc <= qc) & (qc - kc <= BAND_PREV_CHUNKS)
    o_ref[...] = jnp.where(visible, rolled[:, 0:tq], MASK_VALUE)


def _band_bias(rel_table, q0, k0, tq, nk, key_major=False):
    assert CHUNK == 64
    width = pl.next_power_of_2(tq + nk - 1)
    y = np.arange(width)
    c_minus_r = np.where(y < nk, y, y - width)
    rel = np.clip((q0 - k0) - c_minus_r, -(CHUNK - 1), REL_CLIP) + (CHUNK - 1)
    if key_major:
        rel = rel[(-y) % width]
    line = (rel_table.astype(F32) * LOG2E)[:, rel].reshape(BAND_HEADS, 1, width)
    if key_major:
        kern, block, shape = _band_bias_t_kernel, (None, nk, tq), (BAND_HEADS // 2, nk, 2 * tq)
        index = lambda h: (h // 2, 0, h % 2)
    else:
        kern, block, shape = _band_bias_kernel, (None, tq, nk), (BAND_HEADS // 2, 2 * tq, nk)
        index = lambda h: (h // 2, h % 2, 0)
    return pl.pallas_call(
        functools.partial(kern, q0=q0, k0=k0),
        grid=(BAND_HEADS,),
        in_specs=[pl.BlockSpec((None, 1, width), lambda h: (h, 0, 0))],
        out_specs=pl.BlockSpec(block, index),
        out_shape=jax.ShapeDtypeStruct(shape, F32),
        compiler_params=_params("parallel"),
        name="band_bias",
    )(line)


def _band_prompt_kernel(*refs, tq, n_parts, tkp):
    q_ref = refs[0]
    k_refs = refs[1:1 + n_parts]
    vt_refs = refs[1 + n_parts:1 + 2 * n_parts]
    bias_ref = refs[1 + 2 * n_parts]
    o_ref = refs[2 + 2 * n_parts]
    nk = n_parts * tkp
    hd = BAND_HEAD_DIM
    i = pl.program_id(1)
    ones = jnp.ones((ONES_ROWS, nk), BF16)
    for pr in range(BAND_HEADS // 2):
        ln = slice(pr * LANES, (pr + 1) * LANES)
        qf = q_ref[:, ln].astype(F32)
        lane = lax.broadcasted_iota(jnp.int32, qf.shape, 1)
        qzt = jnp.concatenate([jnp.where(lane < hd, qf, 0.0).T, jnp.where(lane >= hd, qf, 0.0).T],
                              axis=1).astype(BF16)
        k = jnp.concatenate([r[:, ln] for r in k_refs], axis=0)
        sc = _dot(k, qzt) + bias_ref[pr]
        row = lax.broadcasted_iota(jnp.int32, sc.shape, 0)
        sc = jnp.where(row >= (n_parts - 1 - i) * tkp, sc, MASK_VALUE)
        mx = jnp.max(sc, axis=0, keepdims=True)
        p = jnp.exp2((sc - mx).astype(BF16))
        vt = jnp.concatenate([r[ln, :] for r in vt_refs] , axis=1)
        r = _dot(jnp.concatenate([vt, ones], axis=0), p)
        o = r[0:LANES, :] / r[LANES:LANES + 1, :]
        o = jnp.concatenate([o[0:hd, 0:tq], o[hd:2 * hd, tq:2 * tq]], axis=0)
        o_ref[:, ln] = o.T.astype(o_ref.dtype)


def _band_attn_prompt(pbf3, vt_arr, q_cb, k_cb, vt_rb, bias_t, *, tq, n_parts, tkp):
    b, t, _ = pbf3.shape
    width = (BAND_HEADS // 2) * LANES
    part = lambda i, m: jnp.maximum(i - (n_parts - 1) + m, 0)
    k_spec = lambda m: pl.BlockSpec((None, tkp, width), lambda b_, i: (b_, part(i, m), k_cb))
    vt_spec = lambda m: pl.BlockSpec((None, width, tkp), lambda b_, i: (b_, vt_rb, part(i, m)))
    return pl.pallas_call(
        functools.partial(_band_prompt_kernel, tq=tq, n_parts=n_parts, tkp=tkp),
        grid=(b, t // tq),
        in_specs=[pl.BlockSpec((None, tq, width), lambda b_, i: (b_, i, q_cb))]
        + [k_spec(m) for m in range(n_parts)] + [vt_spec(m) for m in range(n_parts)]
        + [_const_spec(bias_t.shape)],
        out_specs=pl.BlockSpec((None, tq, width), lambda b_, i: (b_, i, 0)),
        out_shape=jax.ShapeDtypeStruct((b, t, width), BF16),
        compiler_params=_params("parallel", "parallel"),
        name="band_attn_prompt",
    )(pbf3, *([pbf3] * n_parts), *([vt_arr] * n_parts), bias_t)


def _band_attn(q_arr, k_arr, v_arr, q_cb, k_cb, v_cb, bias, *, tq, n_parts, tkp):
    b, t_q, _ = q_arr.shape
    n_pairs = BAND_HEADS // 2

    def kv_spec(cb, m):
        return pl.BlockSpec((None, tkp, LANES),
                            lambda p, b_, i: (b_, jnp.maximum(i - (n_parts - 1) + m, 0), cb + p))

    return pl.pallas_call(
        functools.partial(_band_attn_kernel, tq=tq, n_parts=n_parts, tkp=tkp),
        grid=(n_pairs, b, t_q // tq),
        in_specs=[pl.BlockSpec((None, tq, LANES), lambda p, b_, i: (b_, i, q_cb + p))]
        + [kv_spec(k_cb, m) for m in range(n_parts)]
        + [kv_spec(v_cb, m) for m in range(n_parts)]
        + [pl.BlockSpec((None, 2 * tq, n_parts * tkp), lambda p, b_, i: (p, 0, 0))],
        out_specs=pl.BlockSpec((None, tq, LANES), lambda p, b_, i: (b_, i, p)),
        out_shape=jax.ShapeDtypeStruct((b, t_q, n_pairs * LANES), BF16),
        compiler_params=_params("parallel", "parallel", "parallel"),
        name="band_attn",
    )(q_arr, *([k_arr] * n_parts), *([v_arr] * n_parts), bias)


def _attn_out_ffn_kernel(oa_ref, ob_ref, x_ref, wo_ref, g_ref, wgu_ref, wd_ref, o_ref, *, d_ff, tf):
    half = oa_ref.shape[1]
    x1 = x_ref[...] + _dot(oa_ref[...], wo_ref[0:half, :]) + _dot(ob_ref[...], wo_ref[half:2 * half, :])
    h = _rmsnorm(x1, g_ref[...], RMS_EPS).astype(BF16)
    o_ref[...] = x1 + _swiglu(h, wgu_ref, wd_ref, d_ff, tf)


def _attn_out_ffn(oa, ob, x2d, wo_bf, gain, wgu_bf, wd_bf, tm):
    m, d = x2d.shape
    d_ff = wd_bf.shape[0]
    row = lambda i: (i, 0)
    return pl.pallas_call(
        functools.partial(_attn_out_ffn_kernel, d_ff=d_ff, tf=256),
        grid=(m // tm,),
        in_specs=[pl.BlockSpec((tm, oa.shape[1]), row), pl.BlockSpec((tm, ob.shape[1]), row),
                  pl.BlockSpec((tm, d), row), _const_spec(wo_bf.shape), _const_spec((1, d)),
                  _const_spec(wgu_bf.shape), _const_spec(wd_bf.shape)],
        out_specs=pl.BlockSpec((tm, d), row),
        out_shape=jax.ShapeDtypeStruct((m, d), F32),
        compiler_params=_params("parallel"),
        name="attn_out_ffn",
    )(oa, ob, x2d, wo_bf, gain.reshape(1, d), wgu_bf, wd_bf)


def _gelu(z):
    return 0.5 * z * (1.0 + lax.erf(z * (2.0 ** -0.5)))


def _gmlp_kernel(x_ref, g_ref, win_ref, bin_ref, lng_ref, lnb_ref, ws_ref, bs_ref, wout_ref,
                 o_ref, v_ref, u_s, act_s, *, seg, cw):
    tm, gd = v_ref.shape
    gw = gd // GMLP_GROUPS
    x = x_ref[...]
    h = _rmsnorm(x, g_ref[...], RMS_EPS).astype(BF16)
    for c in range(gd // cw):
        lo, hi = c * cw, (c + 1) * cw
        u_s[:, lo:hi] = _gelu(_dot(h, win_ref[:, lo:hi]) + bin_ref[:, lo:hi])
        v_ref[:, lo:hi] = _gelu(_dot(h, win_ref[:, gd + lo:gd + hi]) + bin_ref[:, gd + lo:gd + hi])
    v = v_ref[...]
    mu = jnp.mean(v, axis=-1, keepdims=True)
    var = jnp.mean(jnp.square(v - mu), axis=-1, keepdims=True)
    v_ref[...] = (v - mu) * lax.rsqrt(var + LN_EPS) * lng_ref[...] + lnb_ref[...]
    r_i = lax.broadcasted_iota(jnp.int32, (seg, seg), 0)
    c_i = lax.broadcasted_iota(jnp.int32, (seg, seg), 1)
    for g in range(GMLP_GROUPS):
        w = jnp.where(r_i >= c_i, ws_ref[g], 0.0).astype(BF16)
        for n in range(tm // seg):
            rows = slice(n * seg, (n + 1) * seg)
            cols = slice(g * gw, (g + 1) * gw)
            sv = _dot(w, v_ref[rows, cols].astype(BF16)) + bs_ref[g]
            act_s[rows, cols] = (u_s[rows, cols] * sv).astype(BF16)
    o_ref[...] = x + _dot(act_s[...], wout_ref[...])


def _gmlp(x2d, gain, win_bf, b_in, ln_g, ln_b, w_s, b_s, wout_bf, tm, seg, return_v):
    m, d = x2d.shape
    gd = wout_bf.shape[0]
    row = lambda i: (i, 0)
    ws = w_s[:, :seg, :seg]
    bs = b_s[:, :seg, None]
    out_specs = [pl.BlockSpec((tm, d), row)]
    out_shape = [jax.ShapeDtypeStruct((m, d), F32)]
    scratch = [pltpu.VMEM((tm, gd), F32), pltpu.VMEM((tm, gd), BF16)]
    if return_v:
        out_specs.append(pl.BlockSpec((tm, gd), row))
        out_shape.append(jax.ShapeDtypeStruct((m, gd), F32))
    else:
        scratch.insert(0, pltpu.VMEM((tm, gd), F32))
    outs = pl.pallas_call(
        functools.partial(_gmlp_kernel, seg=seg, cw=512),
        grid=(m // tm,),
        in_specs=[pl.BlockSpec((tm, d), row), _const_spec((1, d)), _const_spec(win_bf.shape),
                  _const_spec((1, 2 * gd)), _const_spec((1, gd)), _const_spec((1, gd)),
                  _const_spec(ws.shape), _const_spec(bs.shape), _const_spec(wout_bf.shape)],
        out_specs=out_specs, out_shape=out_shape, scratch_shapes=scratch,
        compiler_params=_params("parallel"),
        name="gmlp",
    )(x2d, gain.reshape(1, d), win_bf, b_in.reshape(1, 2 * gd), ln_g.reshape(1, gd),
      ln_b.reshape(1, gd), ws, bs, wout_bf)
    return (outs[0], outs[1]) if return_v else (outs[0], None)


def _router_kernel(x_ref, g_ref, whi_ref, wlo_ref, b_ref, h_ref, meta_ref, cnt_ref):
    h = _rmsnorm(x_ref[...], g_ref[...], RMS_EPS)
    h_hi = h.astype(BF16)
    h_lo = (h - h_hi.astype(F32)).astype(BF16)
    logits = (_dot(h_hi, whi_ref[...]) + _dot(h_hi, wlo_ref[...]) + _dot(h_lo, whi_ref[...])
              + b_ref[...])
    tm = logits.shape[0]
    lane = lax.broadcasted_iota(jnp.int32, logits.shape, 1)
    logits = jnp.where(lane < N_EXPERTS, logits, -jnp.inf)
    v1 = jnp.max(logits, axis=1, keepdims=True)
    i1 = jnp.min(jnp.where(logits == v1, lane, LANES), axis=1, keepdims=True)
    rest = jnp.where(lane == i1, -jnp.inf, logits)
    v2 = jnp.max(rest, axis=1, keepdims=True)
    i2 = jnp.min(jnp.where(rest == v2, lane, LANES), axis=1, keepdims=True)
    e2 = jnp.exp(v2 - v1)
    den = 1.0 + e2

    @pl.when(pl.program_id(0) == 0)
    def _init():
        cnt_ref[...] = jnp.zeros(cnt_ref.shape, F32)

    oh1 = lane == i1
    oh2 = lane == i2
    r_i = lax.broadcasted_iota(jnp.int32, (tm, tm), 0)
    c_i = lax.broadcasted_iota(jnp.int32, (tm, tm), 1)
    before = jnp.where(c_i < r_i, 1.0, 0.0).astype(BF16)
    cum1 = _dot(before, jnp.where(oh1, 1.0, 0.0).astype(BF16))
    cum2 = _dot(before, jnp.where(oh2, 1.0, 0.0).astype(BF16))
    n1 = jnp.sum(jnp.where(oh1, 1.0, 0.0), axis=0, keepdims=True)
    n2 = jnp.sum(jnp.where(oh2, 1.0, 0.0), axis=0, keepdims=True)
    base = cnt_ref[...]
    rank1 = jnp.sum(jnp.where(oh1, base + cum1, 0.0), axis=1, keepdims=True)
    rank2 = jnp.sum(jnp.where(oh2, base + n1 + cum2, 0.0), axis=1, keepdims=True)
    cnt_ref[...] = base + n1 + n2
    meta = jnp.where(lane == 0, i1.astype(F32), 0.0)
    meta = jnp.where(lane == 1, i2.astype(F32), meta)
    meta = jnp.where(lane == 2, 1.0 / den, meta)
    meta = jnp.where(lane == 3, e2 / den, meta)
    meta = jnp.where(lane == 4, rank1, meta)
    meta = jnp.where(lane == 5, rank2, meta)
    meta_ref[...] = meta
    h_ref[...] = h


def _router(x2d, gain, w_router, b_router, tm):
    m, d = x2d.shape
    assert 2 * m < 2 ** 24
    w_pad = jnp.pad(w_router, ((0, 0), (0, LANES - N_EXPERTS)))
    w_hi = w_pad.astype(BF16)
    w_lo = (w_pad - w_hi.astype(F32)).astype(BF16)
    b_pad = jnp.pad(b_router, (0, LANES - N_EXPERTS)).reshape(1, LANES)
    row = lambda i: (i, 0)
    return pl.pallas_call(
        _router_kernel,
        grid=(m // tm,),
        in_specs=[pl.BlockSpec((tm, d), row), _const_spec((1, d)), _const_spec((d, LANES)),
                  _const_spec((d, LANES)), _const_spec((1, LANES))],
        out_specs=[pl.BlockSpec((tm, d), row), pl.BlockSpec((tm, LANES), row),
                   pl.BlockSpec((1, LANES), lambda i: (0, 0))],
        out_shape=[jax.ShapeDtypeStruct((m, d), F32), jax.ShapeDtypeStruct((m, LANES), F32),
                   jax.ShapeDtypeStruct((1, LANES), F32)],
        compiler_params=_params("arbitrary"),
        name="moe_router",
    )(x2d, gain.reshape(1, d), w_hi, w_lo, b_pad)


def _row_copy(src_hbm, src_row, dst, dst_row, sem):
    return pltpu.make_async_copy(src_hbm.at[pl.ds(src_row, 1)], dst.at[pl.ds(dst_row, 1)], sem)


ROW_DMA_UNROLL = 8


def _dispatch_kernel(ends_ref, pos1_ref, pos2_ref, h_ref, xs_hbm, zero_ref, sem, zsem, *, tm, tmx):
    @pl.when(pl.program_id(0) == 0)
    def _zero_padding():
        zero_ref[...] = jnp.zeros(zero_ref.shape, zero_ref.dtype)

        def fill(row0):
            cp = pltpu.make_async_copy(zero_ref, xs_hbm.at[pl.ds(row0, tmx)], zsem)
            cp.start()
            cp.wait()

        for e in range(N_EXPERTS):
            start_e = ends_ref[e - 1] if e else 0

            @pl.when(ends_ref[e] > start_e)
            def _():
                fill(pl.multiple_of(ends_ref[e] - tmx, tmx))

        def tail(t, carry):
            fill(pl.multiple_of(t * tmx, tmx))
            return carry

        lax.fori_loop(ends_ref[N_EXPERTS - 1] // tmx, xs_hbm.shape[0] // tmx, tail, 0)

    def start(r, carry):
        _row_copy(h_ref, r, xs_hbm, pos1_ref[r], sem).start()
        _row_copy(h_ref, r, xs_hbm, pos2_ref[r], sem).start()
        return carry

    def wait(r, carry):
        _row_copy(h_ref, 0, xs_hbm, 0, sem).wait()
        _row_copy(h_ref, 0, xs_hbm, 0, sem).wait()
        return carry

    lax.fori_loop(0, tm, start, 0, unroll=ROW_DMA_UNROLL)
    lax.fori_loop(0, tm, wait, 0, unroll=ROW_DMA_UNROLL)


def _dispatch(h2d, pos1, pos2, ends, n_slots, tm, tmx):
    m, d = h2d.shape
    smem = lambda: pl.BlockSpec((tm,), lambda i, ends: (i,), memory_space=pltpu.SMEM)
    grid_spec = pltpu.PrefetchScalarGridSpec(
        num_scalar_prefetch=1,
        grid=(m // tm,),
        in_specs=[smem(), smem(), pl.BlockSpec((tm, d), lambda i, ends: (i, 0))],
        out_specs=pl.BlockSpec(memory_space=pl.ANY),
        scratch_shapes=[pltpu.VMEM((tmx, d), h2d.dtype), pltpu.SemaphoreType.DMA(()),
                        pltpu.SemaphoreType.DMA(())],
    )
    return pl.pallas_call(
        functools.partial(_dispatch_kernel, tm=tm, tmx=tmx),
        grid_spec=grid_spec,
        out_shape=jax.ShapeDtypeStruct((n_slots, d), h2d.dtype),
        compiler_params=_params("arbitrary"),
        name="moe_dispatch",
    )(ends, pos1, pos2, h2d)


def _expert_ffn_kernel(te_ref, nt_ref, xs_ref, wgu_ref, wd_ref, ys_ref, *, d_ff, tf):
    @pl.when(pl.program_id(0) < nt_ref[0])
    def _():
        ys_ref[...] = _swiglu(xs_ref[...].astype(BF16), wgu_ref, wd_ref, d_ff, tf)

    @pl.when(pl.program_id(0) >= nt_ref[0])
    def _():
        ys_ref[...] = jnp.zeros(ys_ref.shape, F32)


def _expert_ffn(xs, tile_expert, n_tiles, wgu_bf, wd_bf, tmx):
    n_slots, d = xs.shape
    _, d_ff, _ = wd_bf.shape
    row = lambda i, te, nt: (jnp.minimum(i, nt[0] - 1), 0)
    grid_spec = pltpu.PrefetchScalarGridSpec(
        num_scalar_prefetch=2,
        grid=(n_slots // tmx,),
        in_specs=[pl.BlockSpec((tmx, d), row),
                  pl.BlockSpec((None, d, 2 * d_ff), lambda i, te, nt: (te[i], 0, 0)),
                  pl.BlockSpec((None, d_ff, d), lambda i, te, nt: (te[i], 0, 0))],
        out_specs=pl.BlockSpec((tmx, d), lambda i, te, nt: (i, 0)),
    )
    return pl.pallas_call(
        functools.partial(_expert_ffn_kernel, d_ff=d_ff, tf=256),
        grid_spec=grid_spec,
        out_shape=jax.ShapeDtypeStruct((n_slots, d), F32),
        compiler_params=_params("arbitrary"),
        name="moe_experts",
    )(tile_expert, n_tiles, xs, wgu_bf, wd_bf)


def _combine_kernel(pos1_ref, pos2_ref, ys_hbm, x_ref, meta_ref, gf_ref, o_ref, y1_ref, y2_ref, sem, *, tm):
    def start(r, carry):
        _row_copy(ys_hbm, pos1_ref[r], y1_ref, r, sem).start()
        _row_copy(ys_hbm, pos2_ref[r], y2_ref, r, sem).start()
        return carry

    def wait(r, carry):
        _row_copy(ys_hbm, 0, y1_ref, 0, sem).wait()
        _row_copy(ys_hbm, 0, y2_ref, 0, sem).wait()
        return carry

    lax.fori_loop(0, tm, start, 0, unroll=ROW_DMA_UNROLL)
    lax.fori_loop(0, tm, wait, 0, unroll=ROW_DMA_UNROLL)
    meta = meta_ref[...]
    out = meta[:, 2:3] * y1_ref[...] + meta[:, 3:4] * y2_ref[...]
    o_ref[...] = _rmsnorm(x_ref[...] + out, gf_ref[...], RMS_EPS)


def _combine(ys, pos1, pos2, x2d, meta, gain_final, tm):
    m, d = x2d.shape
    smem = lambda: pl.BlockSpec((tm,), lambda i: (i,), memory_space=pltpu.SMEM)
    row = lambda i: (i, 0)
    return pl.pallas_call(
        functools.partial(_combine_kernel, tm=tm),
        grid=(m // tm,),
        in_specs=[smem(), smem(), pl.BlockSpec(memory_space=pl.ANY), pl.BlockSpec((tm, d), row),
                  pl.BlockSpec((tm, LANES), row), _const_spec((1, d))],
        out_specs=pl.BlockSpec((tm, d), row),
        out_shape=jax.ShapeDtypeStruct((m, d), F32),
        scratch_shapes=[pltpu.VMEM((tm, d), F32), pltpu.VMEM((tm, d), F32), pltpu.SemaphoreType.DMA(())],
        compiler_params=_params("arbitrary"),
        name="moe_combine",
    )(pos1, pos2, ys, x2d, meta, gain_final.reshape(1, d))


def _moe(x2d, norm_gain, w_router, b_router, wgu_bf, wd_bf, gain_final, tm):
    m, d = x2d.shape
    tmx = 512 if m >= 4096 else 128
    h, meta, counts = _router(x2d, norm_gain, w_router, b_router, tm)
    counts = counts[0, :N_EXPERTS].astype(jnp.int32)
    padded = (counts + tmx - 1) // tmx * tmx
    ends = jnp.cumsum(padded)
    starts = ends - padded
    n_slots = 2 * m + N_EXPERTS * tmx
    n_tiles_max = n_slots // tmx
    tile_start = jnp.arange(n_tiles_max, dtype=jnp.int32) * tmx
    tile_expert = jnp.minimum(jnp.sum((tile_start[:, None] >= ends[None, :]).astype(jnp.int32), axis=1),
                              N_EXPERTS - 1).astype(jnp.int32)
    n_tiles = (ends[-1:] // tmx).astype(jnp.int32)
    experts = jnp.arange(N_EXPERTS, dtype=jnp.int32)[None, :]
    start_of = lambda e: jnp.sum(jnp.where(e[:, None] == experts, starts[None, :], 0), axis=1)
    e1, e2 = meta[:, 0].astype(jnp.int32), meta[:, 1].astype(jnp.int32)
    pos1 = start_of(e1) + meta[:, 4].astype(jnp.int32)
    pos2 = start_of(e2) + meta[:, 5].astype(jnp.int32)
    td = min(256, m)
    xs = _dispatch(h, pos1, pos2, ends.astype(jnp.int32), n_slots, td, tmx)
    ys = _expert_ffn(xs, tile_expert, n_tiles, wgu_bf, wd_bf, tmx)
    return _combine(ys, pos1, pos2, x2d, meta, gain_final, td)


def _v_columns_transposed(w_in):
    width = w_in.shape[1] // 6
    return jnp.concatenate([w_in[:, 2 * width:3 * width], w_in[:, 5 * width:6 * width]], axis=1).T


def _forward(x, caches, w):
    b, t, d = x.shape
    m = b * t
    tm = min(512, m)
    x2d = x.reshape(m, d)
    lambda_init = 0.8 - 0.6 * math.exp(-0.3 * 0)
    width = DIFF_HEADS * 2 * DIFF_HEAD_DIM
    ncb = width // LANES

    if caches is None:
        pbf, ka, va, kb, vb, vt, kaug = _qkv_proj(x2d, w["norm_attn"], w["attn_w_in"], tm,
                                                  w["attn_w_vt"], t)
    else:
        pbf, ka, va, kb, vb = _qkv_proj(x2d, w["norm_attn"], w["attn_w_in"], tm)
    pbf3 = pbf.reshape(b, t, 6 * width)
    if caches is None:
        oa = _diff_attn_prompt(pbf3, kaug.reshape(b, t, 2 * width), vt, 0, w["diff_lambda"],
                               w["diff_subln"], tile=min(1024, t), lambda_init=lambda_init)
        tqb = min(256, t)
        n_parts = BAND_PAST_MAX // tqb + 1
        bias = _band_bias(w["band_rel_bias"], 0, -BAND_PAST_MAX, tqb, n_parts * tqb, key_major=True)
        ob = _band_attn_prompt(pbf3, vt, 3, 4, 1, bias, tq=tqb, n_parts=n_parts, tkp=tqb)
        n_band = min(BAND_PAST_MAX, t)
        new_bk = kb.reshape(b, t, BAND_HEADS, BAND_HEAD_DIM)[:, t - n_band:]
        new_bv = vb.reshape(b, t, BAND_HEADS, BAND_HEAD_DIM)[:, t - n_band:]
    else:
        ck_a, cv_a, ck_b, cv_b = caches
        p_len, pb_len = ck_a.shape[1], ck_b.shape[1]
        cat = lambda c, lo: jnp.concatenate(
            [c.reshape(b, c.shape[1], width).astype(BF16), pbf3[:, :, lo:lo + width]], axis=1)
        k_a, v_a = cat(ck_a, width), cat(cv_a, 2 * width)
        k_b, v_b = cat(ck_b, 4 * width), cat(cv_b, 5 * width)
        oa = _diff_attn(pbf3, k_a, v_a, 0, 0, 0, w["diff_lambda"], w["diff_subln"],
                        tq=t, tk=p_len + t, q_off=p_len, lambda_init=lambda_init)
        bias = _band_bias(w["band_rel_bias"], p_len, p_len - pb_len, t, pb_len + t)
        ob = _band_attn(pbf3, k_b, v_b, 3 * ncb, 0, 0, bias, tq=t, n_parts=1, tkp=pb_len + t)
        new_bk = kb.reshape(b, t, BAND_HEADS, BAND_HEAD_DIM)
        new_bv = vb.reshape(b, t, BAND_HEADS, BAND_HEAD_DIM)
    new_dk = ka.reshape(b, t, 2 * DIFF_HEADS, DIFF_HEAD_DIM)
    new_dv = va.reshape(b, t, DIFF_HEADS, 2 * DIFF_HEAD_DIM)

    x2d = _attn_out_ffn(oa.reshape(m, width), ob.reshape(m, width), x2d, w["attn_w_out"],
                        w["norm_ffn"], w["ffn_w_gu"], w["ffn_w_down"], tm)

    seg = min(t, GMLP_CHUNK)
    x2d, v_rows = _gmlp(x2d, w["norm_gmlp"], w["gmlp_w_in"], w["gmlp_b_in"], w["gmlp_ln_g"],
                        w["gmlp_ln_b"], w["gmlp_w_s"], w["gmlp_b_s"], w["gmlp_w_out"], tm, seg,
                        return_v=caches is not None)
    y = _moe(x2d, w["norm_moe"], w["moe_w_router"], w["moe_b_router"], w["moe_w_gu"], w["moe_w_down"],
             w["norm_final"], tm)
    new_gv = None if v_rows is None else v_rows.reshape(b, t, -1)[None]
    return (y.reshape(b, t, d), new_dk[None], new_dv[None], new_bk[None], new_bv[None], new_gv)


def kernel(x_prompt, x_sample, cache_diff_k, cache_diff_v, cache_band_k, cache_band_v,
           norm_attn, attn_w_in, diff_lambda, diff_subln, band_rel_bias, attn_w_out,
           norm_ffn, ffn_w_gu, ffn_w_down,
           norm_gmlp, gmlp_w_in, gmlp_b_in, gmlp_ln_g, gmlp_ln_b, gmlp_w_s, gmlp_b_s, gmlp_w_out,
           norm_moe, moe_w_router, moe_b_router, moe_w_gu, moe_w_down, norm_final):
    w = {
        "norm_attn": norm_attn[0], "attn_w_in": attn_w_in[0].astype(BF16),
        "attn_w_vt": _v_columns_transposed(attn_w_in[0]).astype(BF16),
        "diff_lambda": diff_lambda[0], "diff_subln": diff_subln[0],
        "band_rel_bias": band_rel_bias[0], "attn_w_out": attn_w_out[0].astype(BF16),
        "norm_ffn": norm_ffn[0], "ffn_w_gu": ffn_w_gu[0].astype(BF16),
        "ffn_w_down": ffn_w_down[0].astype(BF16),
        "norm_gmlp": norm_gmlp[0], "gmlp_w_in": gmlp_w_in[0].astype(BF16),
        "gmlp_b_in": gmlp_b_in[0], "gmlp_ln_g": gmlp_ln_g[0], "gmlp_ln_b": gmlp_ln_b[0],
        "gmlp_w_s": gmlp_w_s[0], "gmlp_b_s": gmlp_b_s[0], "gmlp_w_out": gmlp_w_out[0].astype(BF16),
        "norm_moe": norm_moe[0], "moe_w_router": moe_w_router[0], "moe_b_router": moe_b_router[0],
        "moe_w_gu": moe_w_gu[0].astype(BF16), "moe_w_down": moe_w_down[0].astype(BF16),
        "norm_final": norm_final,
    }
    y_p, dk_p, dv_p, bk_p, bv_p, _ = _forward(x_prompt, None, w)
    y_s, dk_s, dv_s, bk_s, bv_s, gv_s = _forward(
        x_sample, (cache_diff_k[0], cache_diff_v[0], cache_band_k[0], cache_band_v[0]), w)
    return (y_p, y_s, dk_p, dv_p, bk_p, bv_p, dk_s, dv_s, bk_s, bv_s, gv_s)
```

```python
import functools
import math

import numpy as np
import jax
import jax.numpy as jnp
from jax import lax
from jax.experimental import pallas as pl
from jax.experimental.pallas import tpu as pltpu

CHUNK = 64
DIFF_HEADS = 4
DIFF_HEAD_DIM = 64
DIFF_SUBLN_EPS = 1e-5
BAND_HEADS = 8
BAND_HEAD_DIM = 64
BAND_PREV_CHUNKS = 8
BAND_PAST_MAX = BAND_PREV_CHUNKS * CHUNK
REL_CLIP = 256
GMLP_GROUPS = 8
GMLP_CHUNK = 128
N_EXPERTS = 8
RMS_EPS = 1e-6
LN_EPS = 1e-5
MASK_VALUE = -1e30

LANES = 128
ONES_ROWS = 16
VMEM_LIMIT_BYTES = 56 << 20

F32 = jnp.float32
BF16 = jnp.bfloat16
LOG2E = math.log2(math.e)
Q_SCALE = DIFF_HEAD_DIM ** -0.5 * LOG2E
NT_DIMS = (((1,), (1,)), ((), ()))


def _params(*sem, flags=None):
    return pltpu.CompilerParams(dimension_semantics=sem, vmem_limit_bytes=VMEM_LIMIT_BYTES, flags=flags)


def _const_spec(shape):
    nd = len(shape)
    return pl.BlockSpec(shape, lambda *_: (0,) * nd, pipeline_mode=pl.Buffered(1))


def _rmsnorm(x, g, eps):
    return (x * lax.rsqrt(jnp.mean(x * x, axis=-1, keepdims=True) + eps)) * g


def _dot(a, b):
    return jnp.dot(a, b, preferred_element_type=F32)


def _swiglu(h, wgu_ref, wd_ref, d_ff, tf):
    y = None
    for c in range(d_ff // tf):
        g = _dot(h, wgu_ref[:, c * tf:(c + 1) * tf])
        u = _dot(h, wgu_ref[:, d_ff + c * tf:d_ff + (c + 1) * tf])
        a = (g * jax.nn.sigmoid(g) * u).astype(BF16)
        part = _dot(a, wd_ref[c * tf:(c + 1) * tf, :])
        y = part if y is None else y + part
    return y


def _qkv_kernel(x_ref, g_ref, w_ref, *rest, width, with_vt, tiles_per_seq):
    if with_vt:
        pbf_ref, ka_ref, va_ref, kb_ref, vb_ref, vt_ref, kaug_ref = rest
    else:
        pbf_ref, ka_ref, va_ref, kb_ref, vb_ref = rest
    tm = x_ref.shape[0]
    h = _rmsnorm(x_ref[...], g_ref[...], RMS_EPS).astype(BF16)
    f32_outs = {1: ka_ref, 2: va_ref, 4: kb_ref, 5: vb_ref}
    for c in range(6):
        r = _dot(h, w_ref[:, c * width:(c + 1) * width])
        if c in f32_outs:
            f32_outs[c][...] = r
        else:
            r = r * Q_SCALE
        pbf_ref[:, c * width:(c + 1) * width] = r.astype(BF16)
        if with_vt and c == 1:
            pos = (pl.program_id(0) % tiles_per_seq) * tm + lax.broadcasted_iota(jnp.int32, (tm, LANES), 0)
            lane = lax.broadcasted_iota(jnp.int32, (tm, LANES), 1)
            hi = (pos // LANES).astype(F32)
            lo = (pos % LANES).astype(F32)
            feats = jnp.where(lane < 3, hi, jnp.where(lane < 6, lo, 0.0)).astype(BF16)
            for hd in range(DIFF_HEADS):
                kaug_ref[:, 2 * hd * LANES:(2 * hd + 1) * LANES] = r[:, hd * LANES:(hd + 1) * LANES].astype(BF16)
                kaug_ref[:, (2 * hd + 1) * LANES:(2 * hd + 2) * LANES] = feats
        if with_vt and c in (2, 5):
            r0 = 0 if c == 2 else width
            vt_ref[r0:r0 + width, :] = r.T.astype(BF16)


def _qkv_proj(x2d, gain, w_bf, tm, seq=None):
    m, d = x2d.shape
    width = w_bf.shape[1] // 6
    row = lambda i: (i, 0)
    with_vt = seq is not None
    in_specs = [pl.BlockSpec((tm, d), row), _const_spec((1, d)), _const_spec(w_bf.shape)]
    out_specs = [pl.BlockSpec((tm, 6 * width), row)] + [pl.BlockSpec((tm, width), row)] * 4
    out_shape = [jax.ShapeDtypeStruct((m, 6 * width), BF16)] + [jax.ShapeDtypeStruct((m, width), F32)] * 4
    args = [x2d, gain.reshape(1, d), w_bf]
    nt = 1
    if with_vt:
        nt = seq // tm
        assert seq <= 256 * LANES
        vt_rows = 2 * width
        out_specs.append(pl.BlockSpec((None, vt_rows, tm), lambda i: (i // nt, 0, i % nt)))
        out_shape.append(jax.ShapeDtypeStruct((m // seq, vt_rows, seq), BF16))
        out_specs.append(pl.BlockSpec((tm, 2 * width), row))
        out_shape.append(jax.ShapeDtypeStruct((m, 2 * width), BF16))
    return pl.pallas_call(
        functools.partial(_qkv_kernel, width=width, with_vt=with_vt, tiles_per_seq=nt),
        grid=(m // tm,),
        in_specs=in_specs, out_specs=out_specs, out_shape=out_shape,
        compiler_params=_params("parallel"),
        name="qkv_proj",
    )(*args)


def _split_heads_rows(q, tq):
    qf = q.astype(F32)
    lane = lax.broadcasted_iota(jnp.int32, qf.shape, 1)
    lo = jnp.where(lane < DIFF_HEAD_DIM, qf, 0.0)
    hi = jnp.where(lane >= DIFF_HEAD_DIM, qf, 0.0)
    return jnp.concatenate([lo, hi], axis=0).astype(BF16)


def _diff_attn_kernel(ii_ref, jj_ref, last_ref, q_ref, k_ref, v_ref, slope_ref, lam_ref, g_ref,
                      o_ref, qz_ref, m_ref, l_ref, acc_ref, *, tq, tk, q_off, lambda_init):
    s = pl.program_id(2)
    i = ii_ref[s]
    j = jj_ref[s]

    @pl.when(j == 0)
    def _init():
        qz_ref[...] = _split_heads_rows(q_ref[...], tq)
        m_ref[...] = jnp.full(m_ref.shape, MASK_VALUE, F32)
        l_ref[...] = jnp.zeros(l_ref.shape, F32)
        acc_ref[...] = jnp.zeros(acc_ref.shape, F32)

    sc = lax.dot_general(qz_ref[...], k_ref[...], NT_DIMS, preferred_element_type=F32)
    row = lax.broadcasted_iota(jnp.int32, (2 * tq, 1), 0)
    row = jnp.where(row >= tq, row - tq, row)
    qpos = q_off + i * tq + row
    kpos = j * tk + lax.broadcasted_iota(jnp.int32, (1, tk), 1)
    dist = jnp.abs(qpos - kpos).astype(F32)
    sc = sc - slope_ref[:, 0:1] * dist
    visible = (kpos // CHUNK) <= (qpos // CHUNK)
    sc = jnp.where(visible, sc, MASK_VALUE)

    m_prev = m_ref[...]
    m_new = jnp.maximum(m_prev, jnp.max(sc, axis=1, keepdims=True))
    alpha = jnp.exp2(m_prev - m_new)
    p = jnp.exp2(sc - m_new)
    l_ref[...] = alpha * l_ref[...] + jnp.sum(p, axis=1, keepdims=True)
    acc_ref[...] = alpha * acc_ref[...] + _dot(p.astype(BF16), v_ref[...])
    m_ref[...] = m_new

    @pl.when(last_ref[s] == 1)
    def _finish():
        lp = lam_ref[...]
        lam = (jnp.exp(jnp.sum(lp[0:1] * lp[1:2], axis=1, keepdims=True))
               - jnp.exp(jnp.sum(lp[2:3] * lp[3:4], axis=1, keepdims=True)) + lambda_init)
        o_all = acc_ref[...] / l_ref[...]
        o = o_all[:tq] - lam * o_all[tq:]
        o = _rmsnorm(o, g_ref[...], DIFF_SUBLN_EPS) * (1.0 - lambda_init)
        o_ref[...] = o.astype(o_ref.dtype)


def _diff_attn(q_arr, k_arr, v_arr, q_cb, k_cb, v_cb, lam_p, subln_g, *, tq, tk, q_off, lambda_init):
    b, t_q, _ = q_arr.shape
    t_k = k_arr.shape[1]
    nq, nk = t_q // tq, t_k // tk
    pairs = [(i, j) for i in range(nq) for j in range(nk)
             if (j * tk) // CHUNK <= (q_off + i * tq + tq - 1) // CHUNK]
    ii = np.array([p[0] for p in pairs], np.int32)
    jj = np.array([p[1] for p in pairs], np.int32)
    last = np.array([1 if (n + 1 == len(pairs) or pairs[n + 1][0] != pairs[n][0]) else 0
                     for n in range(len(pairs))], np.int32)
    slopes = jnp.asarray(_alibi_slopes_log2()[:, None, None] * np.ones((1, 1, LANES), np.float32))

    grid_spec = pltpu.PrefetchScalarGridSpec(
        num_scalar_prefetch=3,
        grid=(b, DIFF_HEADS, len(pairs)),
        in_specs=[
            pl.BlockSpec((None, tq, LANES), lambda b_, h, s, ii, jj, la: (b_, ii[s], q_cb + h)),
            pl.BlockSpec((None, tk, LANES), lambda b_, h, s, ii, jj, la: (b_, jj[s], k_cb + h)),
            pl.BlockSpec((None, tk, LANES), lambda b_, h, s, ii, jj, la: (b_, jj[s], v_cb + h)),
            pl.BlockSpec((None, 1, LANES), lambda b_, h, s, ii, jj, la: (h, 0, 0)),
            pl.BlockSpec((4, DIFF_HEAD_DIM), lambda b_, h, s, ii, jj, la: (0, 0)),
            pl.BlockSpec((1, LANES), lambda b_, h, s, ii, jj, la: (0, 0)),
        ],
        out_specs=pl.BlockSpec((None, tq, LANES), lambda b_, h, s, ii, jj, la: (b_, ii[s], h)),
        scratch_shapes=[
            pltpu.VMEM((2 * tq, LANES), BF16),
            pltpu.VMEM((2 * tq, 1), F32),
            pltpu.VMEM((2 * tq, 1), F32),
            pltpu.VMEM((2 * tq, LANES), F32),
        ],
    )
    return pl.pallas_call(
        functools.partial(_diff_attn_kernel, tq=tq, tk=tk, q_off=q_off, lambda_init=lambda_init),
        grid_spec=grid_spec,
        out_shape=jax.ShapeDtypeStruct((b, t_q, DIFF_HEADS * LANES), BF16),
        compiler_params=_params("parallel", "parallel", "arbitrary"),
        name="diff_attn",
    )(jnp.asarray(ii), jnp.asarray(jj), jnp.asarray(last), q_arr, k_arr, v_arr, slopes,
      lam_p, subln_g.reshape(1, LANES))


def _alibi_slopes_log2():
    return (2.0 ** (-8.0 * np.arange(1, DIFF_HEADS + 1, dtype=np.float64) / DIFF_HEADS) * LOG2E).astype(np.float32)


def _diff_prompt_kernel(ii_ref, jj_ref, last_ref, lin_ref, q_ref, k_ref, vt_ref, dbias_ref, qfeat_ref,
                        lam_ref, g_ref, o_ref, qzt_ref, s0_ref, s1_ref, mx0_ref, mx1_ref, m_ref, acc_ref,
                        *, tq, tk, cb, lambda_init):
    s = pl.program_id(2)
    n_pairs = pl.num_programs(2) - 1
    sp = jnp.minimum(s, n_pairs - 1)
    sc = jnp.maximum(s - 1, 0)
    consuming = s > 0
    nblk = 2 * tq // cb
    blk = lambda c: slice(c * cb, (c + 1) * cb)

    @pl.when(s == 0)
    def _first_step():
        s1_ref[...] = jnp.zeros(s1_ref.shape, F32)
        mx1_ref[...] = jnp.zeros(mx1_ref.shape, F32)

    @pl.when(jj_ref[sp] == 0)
    def _new_queries():
        qf = q_ref[...].astype(F32)
        lane = lax.broadcasted_iota(jnp.int32, qf.shape, 1)
        qzt_ref[0:LANES, 0:tq] = jnp.where(lane < DIFF_HEAD_DIM, qf, 0.0).T.astype(BF16)
        qzt_ref[0:LANES, tq:2 * tq] = jnp.where(lane >= DIFF_HEAD_DIM, qf, 0.0).T.astype(BF16)
        qzt_ref[LANES:2 * LANES, :] = jnp.broadcast_to(qfeat_ref[...], (LANES, 2 * tq)).astype(BF16)

    @pl.when(jj_ref[sc] == 0)
    def _reset():
        m_ref[...] = jnp.full(m_ref.shape, MASK_VALUE, F32)
        acc_ref[...] = jnp.zeros(acc_ref.shape, F32)

    def stages(s_prod, mx_prod, s_cons, mx_cons):
        @pl.when(jnp.logical_and(consuming, lin_ref[sc] == 0))
        def _diagonal():
            for c in range(nblk):
                v = s_cons[:, blk(c)] + dbias_ref[:, (c * cb) % tq:(c * cb) % tq + cb]
                s_cons[:, blk(c)] = v
                mx_cons[:, blk(c)] = jnp.max(v, axis=0, keepdims=True)

        k = k_ref[...]
        vt = jnp.concatenate([vt_ref[...], jnp.ones((ONES_ROWS, tk), BF16)], axis=0)
        for c in range(nblk):
            v = _dot(k, qzt_ref[:, blk(c)])
            s_prod[:, blk(c)] = v
            mx_prod[:, blk(c)] = jnp.max(v, axis=0, keepdims=True)
        m_prev = [m_ref[:, blk(c)] for c in range(nblk)]
        acc_prev = [acc_ref[:, blk(c)] for c in range(nblk)]
        m_out, acc_out = [], []
        for c in range(nblk):
            m_new = jnp.maximum(m_prev[c], mx_cons[:, blk(c)])
            alpha = jnp.exp2(m_prev[c] - m_new)
            p = jnp.exp2((s_cons[:, blk(c)] - m_new).astype(BF16))
            acc_out.append(alpha * acc_prev[c] + _dot(vt, p))
            m_out.append(m_new)
        for c in range(nblk):
            m_ref[:, blk(c)] = m_out[c]
            acc_ref[:, blk(c)] = acc_out[c]

    @pl.when(s % 2 == 0)
    def _even():
        stages(s0_ref, mx0_ref, s1_ref, mx1_ref)

    @pl.when(s % 2 == 1)
    def _odd():
        stages(s1_ref, mx1_ref, s0_ref, mx0_ref)

    @pl.when(jnp.logical_and(consuming, last_ref[sc] == 1))
    def _finish():
        lp = lam_ref[...]
        lam = (jnp.exp(jnp.sum(lp[0:1] * lp[1:2], axis=1, keepdims=True))
               - jnp.exp(jnp.sum(lp[2:3] * lp[3:4], axis=1, keepdims=True)) + lambda_init)
        o_all = acc_ref[0:LANES, :] / acc_ref[LANES:LANES + 1, :]
        o = o_all[:, 0:tq] - lam * o_all[:, tq:2 * tq]
        o = o * lax.rsqrt(jnp.mean(o * o, axis=0, keepdims=True) + DIFF_SUBLN_EPS)
        o = o * g_ref[...] * (1.0 - lambda_init)
        o_ref[...] = o.T.astype(o_ref.dtype)


def _bf16_split3(x):
    import ml_dtypes
    rnd = lambda v: v.astype(ml_dtypes.bfloat16).astype(np.float32)
    x = np.asarray(x, np.float32)
    hi = rnd(x)
    mid = rnd(x - hi)
    lo = rnd(x - hi - mid)
    return hi, mid, lo


def _diff_attn_prompt(pbf3, kaug3, vt_arr, q_cb, lam_p, subln_g, *, tile, lambda_init):
    b, t, _ = pbf3.shape
    tq = tk = tile
    n = t // tile
    pairs = [(i, j) for i in range(n) for j in range(i + 1)]
    ii = np.array([p[0] for p in pairs], np.int32)
    jj = np.array([p[1] for p in pairs], np.int32)
    last = (ii == jj).astype(np.int32)
    lin = (ii != jj).astype(np.int32)
    c2 = _alibi_slopes_log2()
    qfeat = np.zeros((DIFF_HEADS, LANES, 1), np.float32)
    for r, part in enumerate(_bf16_split3(c2)):
        qfeat[:, r, 0] = part * LANES
        qfeat[:, 3 + r, 0] = part
    pos = np.arange(tile)
    d = (pos[None, :] - pos[:, None]).astype(np.float32)
    visible = (pos[:, None] // CHUNK) <= (pos[None, :] // CHUNK)
    dbias = jnp.asarray(np.where(visible[None], 2.0 * c2[:, None, None] * np.minimum(d, 0.0)[None],
                                 MASK_VALUE).astype(np.float32))
    cb = min(256, 2 * tq)
    n_pairs = len(pairs)
    prod = lambda s: jnp.minimum(s, n_pairs - 1)
    cons = lambda s: jnp.maximum(s - 1, 0)
    idx = lambda f: (lambda b_, h, s, ii, jj, la, li: f(b_, h, s, ii, jj))
    grid_spec = pltpu.PrefetchScalarGridSpec(
        num_scalar_prefetch=4,
        grid=(b, DIFF_HEADS, n_pairs + 1),
        in_specs=[
            pl.BlockSpec((None, tq, LANES), idx(lambda b_, h, s, ii, jj: (b_, ii[prod(s)], q_cb + h))),
            pl.BlockSpec((None, tk, 2 * LANES), idx(lambda b_, h, s, ii, jj: (b_, jj[prod(s)], h))),
            pl.BlockSpec((None, LANES, tk), idx(lambda b_, h, s, ii, jj: (b_, h, jj[cons(s)]))),
            pl.BlockSpec((None, tk, tq), idx(lambda b_, h, s, ii, jj: (h, 0, 0))),
            pl.BlockSpec((None, LANES, 1), idx(lambda b_, h, s, ii, jj: (h, 0, 0))),
            pl.BlockSpec((4, DIFF_HEAD_DIM), idx(lambda b_, h, s, ii, jj: (0, 0))),
            pl.BlockSpec((LANES, 1), idx(lambda b_, h, s, ii, jj: (0, 0))),
        ],
        out_specs=pl.BlockSpec((None, tq, LANES), idx(lambda b_, h, s, ii, jj: (b_, ii[cons(s)], h))),
        scratch_shapes=[
            pltpu.VMEM((2 * LANES, 2 * tq), BF16),
            pltpu.VMEM((tk, 2 * tq), F32), pltpu.VMEM((tk, 2 * tq), F32),
            pltpu.VMEM((1, 2 * tq), F32), pltpu.VMEM((1, 2 * tq), F32),
            pltpu.VMEM((1, 2 * tq), F32),
            pltpu.VMEM((LANES + ONES_ROWS, 2 * tq), F32),
        ],
    )
    return pl.pallas_call(
        functools.partial(_diff_prompt_kernel, tq=tq, tk=tk, cb=cb, lambda_init=lambda_init),
        grid_spec=grid_spec,
        out_shape=jax.ShapeDtypeStruct((b, t, DIFF_HEADS * LANES), BF16),
        compiler_params=_params("parallel", "parallel", "arbitrary"),
        name="diff_attn_prompt",
    )(jnp.asarray(ii), jnp.asarray(jj), jnp.asarray(last), jnp.asarray(lin), pbf3, kaug3, vt_arr, dbias,
      jnp.asarray(qfeat), lam_p, subln_g.reshape(LANES, 1))


def _band_attn_kernel(*refs, tq, n_parts, tkp):
    q_ref = refs[0]
    k_refs = refs[1:1 + n_parts]
    v_refs = refs[1 + n_parts:1 + 2 * n_parts]
    bias_ref = refs[1 + 2 * n_parts]
    o_ref = refs[2 + 2 * n_parts]
    i = pl.program_id(2)
    qz = _split_heads_rows(q_ref[...], tq)
    scores = []
    for m in range(n_parts):
        sc = lax.dot_general(qz, k_refs[m][...], NT_DIMS, preferred_element_type=F32)
        sc = sc + bias_ref[:, m * tkp:(m + 1) * tkp]
        if n_parts > 1:
            sc = jnp.where(i - (n_parts - 1) + m >= 0, sc, MASK_VALUE)
        scores.append(sc)
    mx = functools.reduce(jnp.maximum, [jnp.max(sc, axis=1, keepdims=True) for sc in scores])
    den = None
    num = None
    for m in range(n_parts):
        p = jnp.exp2(scores[m] - mx)
        d = jnp.sum(p, axis=1, keepdims=True)
        r = _dot(p.astype(BF16), v_refs[m][...])
        den = d if den is None else den + d
        num = r if num is None else num + r
    r = num / den
    lane = lax.broadcasted_iota(jnp.int32, (tq, LANES), 1)
    o_ref[...] = jnp.where(lane < BAND_HEAD_DIM, r[:tq], r[tq:]).astype(o_ref.dtype)


def _band_bias_kernel(line_ref, o_ref, *, q0, k0):
    tq, nk = o_ref.shape
    line = jnp.broadcast_to(line_ref[...], (tq, line_ref.shape[1]))
    rolled = pltpu.roll(line, 0, 1, stride=1, stride_axis=0)
    qc = (q0 + lax.broadcasted_iota(jnp.int32, (tq, nk), 0)) >> 6
    kc = (k0 + lax.broadcasted_iota(jnp.int32, (tq, nk), 1)) >> 6
    visible = (kc <= qc) & (qc - kc <= BAND_PREV_CHUNKS)
    o_ref[...] = jnp.where(visible, rolled[:, 0:nk], MASK_VALUE)


def _band_bias_t_kernel(line_ref, o_ref, *, q0, k0):
    nk, tq = o_ref.shape
    line = jnp.broadcast_to(line_ref[...], (nk, line_ref.shape[1]))
    rolled = pltpu.roll(line, 0, 1, stride=1, stride_axis=0)
    kc = (k0 + lax.broadcasted_iota(jnp.int32, (nk, tq), 0)) >> 6
    qc = (q0 + lax.broadcasted_iota(jnp.int32, (nk, tq), 1)) >> 6
    visible = (kc <= qc) & (qc - kc <= BAND_PREV_CHUNKS)
    o_ref[...] = jnp.where(visible, rolled[:, 0:tq], MASK_VALUE)


def _band_bias(rel_table, q0, k0, tq, nk, key_major=False):
    assert CHUNK == 64
    width = pl.next_power_of_2(tq + nk - 1)
    y = np.arange(width)
    c_minus_r = np.where(y < nk, y, y - width)
    rel = np.clip((q0 - k0) - c_minus_r, -(CHUNK - 1), REL_CLIP) + (CHUNK - 1)
    if key_major:
        rel = rel[(-y) % width]
    line = (rel_table.astype(F32) * LOG2E)[:, rel].reshape(BAND_HEADS, 1, width)
    if key_major:
        kern, block, shape = _band_bias_t_kernel, (None, nk, tq), (BAND_HEADS // 2, nk, 2 * tq)
        index = lambda h: (h // 2, 0, h % 2)
    else:
        kern, block, shape = _band_bias_kernel, (None, tq, nk), (BAND_HEADS // 2, 2 * tq, nk)
        index = lambda h: (h // 2, h % 2, 0)
    return pl.pallas_call(
        functools.partial(kern, q0=q0, k0=k0),
        grid=(BAND_HEADS,),
        in_specs=[pl.BlockSpec((None, 1, width), lambda h: (h, 0, 0))],
        out_specs=pl.BlockSpec(block, index),
        out_shape=jax.ShapeDtypeStruct(shape, F32),
        compiler_params=_params("parallel"),
        name="band_bias",
    )(line)


def _band_prompt_kernel(*refs, tq, n_parts, tkp):
    q_ref = refs[0]
    k_refs = refs[1:1 + n_parts]
    vt_refs = refs[1 + n_parts:1 + 2 * n_parts]
    bias_ref = refs[1 + 2 * n_parts]
    o_ref = refs[2 + 2 * n_parts]
    nk = n_parts * tkp
    hd = BAND_HEAD_DIM
    i = pl.program_id(1)
    ones = jnp.ones((ONES_ROWS, nk), BF16)
    for pr in range(BAND_HEADS // 2):
        ln = slice(pr * LANES, (pr + 1) * LANES)
        qf = q_ref[:, ln].astype(F32)
        lane = lax.broadcasted_iota(jnp.int32, qf.shape, 1)
        qzt = jnp.concatenate([jnp.where(lane < hd, qf, 0.0).T, jnp.where(lane >= hd, qf, 0.0).T],
                              axis=1).astype(BF16)
        k = jnp.concatenate([r[:, ln] for r in k_refs], axis=0)
        sc = _dot(k, qzt) + bias_ref[pr]
        row = lax.broadcasted_iota(jnp.int32, sc.shape, 0)
        sc = jnp.where(row >= (n_parts - 1 - i) * tkp, sc, MASK_VALUE)
        mx = jnp.max(sc, axis=0, keepdims=True)
        p = jnp.exp2((sc - mx).astype(BF16))
        vt = jnp.concatenate([r[ln, :] for r in vt_refs] , axis=1)
        r = _dot(jnp.concatenate([vt, ones], axis=0), p)
        o = r[0:LANES, :] / r[LANES:LANES + 1, :]
        o = jnp.concatenate([o[0:hd, 0:tq], o[hd:2 * hd, tq:2 * tq]], axis=0)
        o_ref[:, ln] = o.T.astype(o_ref.dtype)


def _band_attn_prompt(pbf3, vt_arr, q_cb, k_cb, vt_rb, bias_t, *, tq, n_parts, tkp):
    b, t, _ = pbf3.shape
    width = (BAND_HEADS // 2) * LANES
    part = lambda i, m: jnp.maximum(i - (n_parts - 1) + m, 0)
    k_spec = lambda m: pl.BlockSpec((None, tkp, width), lambda b_, i: (b_, part(i, m), k_cb))
    vt_spec = lambda m: pl.BlockSpec((None, width, tkp), lambda b_, i: (b_, vt_rb, part(i, m)))
    return pl.pallas_call(
        functools.partial(_band_prompt_kernel, tq=tq, n_parts=n_parts, tkp=tkp),
        grid=(b, t // tq),
        in_specs=[pl.BlockSpec((None, tq, width), lambda b_, i: (b_, i, q_cb))]
        + [k_spec(m) for m in range(n_parts)] + [vt_spec(m) for m in range(n_parts)]
        + [_const_spec(bias_t.shape)],
        out_specs=pl.BlockSpec((None, tq, width), lambda b_, i: (b_, i, 0)),
        out_shape=jax.ShapeDtypeStruct((b, t, width), BF16),
        compiler_params=_params("parallel", "parallel"),
        name="band_attn_prompt",
    )(pbf3, *([pbf3] * n_parts), *([vt_arr] * n_parts), bias_t)


def _band_attn(q_arr, k_arr, v_arr, q_cb, k_cb, v_cb, bias, *, tq, n_parts, tkp):
    b, t_q, _ = q_arr.shape
    n_pairs = BAND_HEADS // 2

    def kv_spec(cb, m):
        return pl.BlockSpec((None, tkp, LANES),
                            lambda p, b_, i: (b_, jnp.maximum(i - (n_parts - 1) + m, 0), cb + p))

    return pl.pallas_call(
        functools.partial(_band_attn_kernel, tq=tq, n_parts=n_parts, tkp=tkp),
        grid=(n_pairs, b, t_q // tq),
        in_specs=[pl.BlockSpec((None, tq, LANES), lambda p, b_, i: (b_, i, q_cb + p))]
        + [kv_spec(k_cb, m) for m in range(n_parts)]
        + [kv_spec(v_cb, m) for m in range(n_parts)]
        + [pl.BlockSpec((None, 2 * tq, n_parts * tkp), lambda p, b_, i: (p, 0, 0))],
        out_specs=pl.BlockSpec((None, tq, LANES), lambda p, b_, i: (b_, i, p)),
        out_shape=jax.ShapeDtypeStruct((b, t_q, n_pairs * LANES), BF16),
        compiler_params=_params("parallel", "parallel", "parallel"),
        name="band_attn",
    )(q_arr, *([k_arr] * n_parts), *([v_arr] * n_parts), bias)


def _attn_out_ffn_kernel(oa_ref, ob_ref, x_ref, wo_ref, g_ref, wgu_ref, wd_ref, o_ref, *, d_ff, tf):
    half = oa_ref.shape[1]
    x1 = x_ref[...] + _dot(oa_ref[...], wo_ref[0:half, :]) + _dot(ob_ref[...], wo_ref[half:2 * half, :])
    h = _rmsnorm(x1, g_ref[...], RMS_EPS).astype(BF16)
    o_ref[...] = x1 + _swiglu(h, wgu_ref, wd_ref, d_ff, tf)


def _attn_out_ffn(oa, ob, x2d, wo_bf, gain, wgu_bf, wd_bf, tm):
    m, d = x2d.shape
    d_ff = wd_bf.shape[0]
    row = lambda i: (i, 0)
    return pl.pallas_call(
        functools.partial(_attn_out_ffn_kernel, d_ff=d_ff, tf=256),
        grid=(m // tm,),
        in_specs=[pl.BlockSpec((tm, oa.shape[1]), row), pl.BlockSpec((tm, ob.shape[1]), row),
                  pl.BlockSpec((tm, d), row), _const_spec(wo_bf.shape), _const_spec((1, d)),
                  _const_spec(wgu_bf.shape), _const_spec(wd_bf.shape)],
        out_specs=pl.BlockSpec((tm, d), row),
        out_shape=jax.ShapeDtypeStruct((m, d), F32),
        compiler_params=_params("parallel"),
        name="attn_out_ffn",
    )(oa, ob, x2d, wo_bf, gain.reshape(1, d), wgu_bf, wd_bf)


def _gelu(z):
    return 0.5 * z * (1.0 + lax.erf(z * (2.0 ** -0.5)))


def _gmlp_kernel(x_ref, g_ref, win_ref, bin_ref, lng_ref, lnb_ref, ws_ref, bs_ref, wout_ref,
                 o_ref, v_ref, u_s, act_s, *, seg, cw):
    tm, gd = v_ref.shape
    gw = gd // GMLP_GROUPS
    x = x_ref[...]
    h = _rmsnorm(x, g_ref[...], RMS_EPS).astype(BF16)
    for c in range(gd // cw):
        lo, hi = c * cw, (c + 1) * cw
        u_s[:, lo:hi] = _gelu(_dot(h, win_ref[:, lo:hi]) + bin_ref[:, lo:hi])
        v_ref[:, lo:hi] = _gelu(_dot(h, win_ref[:, gd + lo:gd + hi]) + bin_ref[:, gd + lo:gd + hi])
    v = v_ref[...]
    mu = jnp.mean(v, axis=-1, keepdims=True)
    var = jnp.mean(jnp.square(v - mu), axis=-1, keepdims=True)
    v_ref[...] = (v - mu) * lax.rsqrt(var + LN_EPS) * lng_ref[...] + lnb_ref[...]
    r_i = lax.broadcasted_iota(jnp.int32, (seg, seg), 0)
    c_i = lax.broadcasted_iota(jnp.int32, (seg, seg), 1)
    for g in range(GMLP_GROUPS):
        w = jnp.where(r_i >= c_i, ws_ref[g], 0.0).astype(BF16)
        for n in range(tm // seg):
            rows = slice(n * seg, (n + 1) * seg)
            cols = slice(g * gw, (g + 1) * gw)
            sv = _dot(w, v_ref[rows, cols].astype(BF16)) + bs_ref[g]
            act_s[rows, cols] = (u_s[rows, cols] * sv).astype(BF16)
    o_ref[...] = x + _dot(act_s[...], wout_ref[...])


def _gmlp(x2d, gain, win_bf, b_in, ln_g, ln_b, w_s, b_s, wout_bf, tm, seg, return_v):
    m, d = x2d.shape
    gd = wout_bf.shape[0]
    row = lambda i: (i, 0)
    ws = w_s[:, :seg, :seg]
    bs = b_s[:, :seg, None]
    out_specs = [pl.BlockSpec((tm, d), row)]
    out_shape = [jax.ShapeDtypeStruct((m, d), F32)]
    scratch = [pltpu.VMEM((tm, gd), F32), pltpu.VMEM((tm, gd), BF16)]
    if return_v:
        out_specs.append(pl.BlockSpec((tm, gd), row))
        out_shape.append(jax.ShapeDtypeStruct((m, gd), F32))
    else:
        scratch.insert(0, pltpu.VMEM((tm, gd), F32))
    outs = pl.pallas_call(
        functools.partial(_gmlp_kernel, seg=seg, cw=512),
        grid=(m // tm,),
        in_specs=[pl.BlockSpec((tm, d), row), _const_spec((1, d)), _const_spec(win_bf.shape),
                  _const_spec((1, 2 * gd)), _const_spec((1, gd)), _const_spec((1, gd)),
                  _const_spec(ws.shape), _const_spec(bs.shape), _const_spec(wout_bf.shape)],
        out_specs=out_specs, out_shape=out_shape, scratch_shapes=scratch,
        compiler_params=_params("parallel"),
        name="gmlp",
    )(x2d, gain.reshape(1, d), win_bf, b_in.reshape(1, 2 * gd), ln_g.reshape(1, gd),
      ln_b.reshape(1, gd), ws, bs, wout_bf)
    return (outs[0], outs[1]) if return_v else (outs[0], None)


def _router_kernel(x_ref, g_ref, whi_ref, wlo_ref, b_ref, meta_ref, cnt_ref):
    h = _rmsnorm(x_ref[...], g_ref[...], RMS_EPS)
    h_hi = h.astype(BF16)
    h_lo = (h - h_hi.astype(F32)).astype(BF16)
    logits = (_dot(h_hi, whi_ref[...]) + _dot(h_hi, wlo_ref[...]) + _dot(h_lo, whi_ref[...])
              + b_ref[...])
    tm = logits.shape[0]
    lane = lax.broadcasted_iota(jnp.int32, logits.shape, 1)
    logits = jnp.where(lane < N_EXPERTS, logits, -jnp.inf)
    v1 = jnp.max(logits, axis=1, keepdims=True)
    i1 = jnp.min(jnp.where(logits == v1, lane, LANES), axis=1, keepdims=True)
    rest = jnp.where(lane == i1, -jnp.inf, logits)
    v2 = jnp.max(rest, axis=1, keepdims=True)
    i2 = jnp.min(jnp.where(rest == v2, lane, LANES), axis=1, keepdims=True)
    e2 = jnp.exp(v2 - v1)
    den = 1.0 + e2

    @pl.when(pl.program_id(0) == 0)
    def _init():
        cnt_ref[...] = jnp.zeros(cnt_ref.shape, F32)

    oh1 = lane == i1
    oh2 = lane == i2
    r_i = lax.broadcasted_iota(jnp.int32, (tm, tm), 0)
    c_i = lax.broadcasted_iota(jnp.int32, (tm, tm), 1)
    before = jnp.where(c_i < r_i, 1.0, 0.0).astype(BF16)
    cum1 = _dot(before, jnp.where(oh1, 1.0, 0.0).astype(BF16))
    cum2 = _dot(before, jnp.where(oh2, 1.0, 0.0).astype(BF16))
    n1 = jnp.sum(jnp.where(oh1, 1.0, 0.0), axis=0, keepdims=True)
    n2 = jnp.sum(jnp.where(oh2, 1.0, 0.0), axis=0, keepdims=True)
    base = cnt_ref[...]
    rank1 = jnp.sum(jnp.where(oh1, base + cum1, 0.0), axis=1, keepdims=True)
    rank2 = jnp.sum(jnp.where(oh2, base + n1 + cum2, 0.0), axis=1, keepdims=True)
    cnt_ref[...] = base + n1 + n2
    meta = jnp.where(lane == 0, i1.astype(F32), 0.0)
    meta = jnp.where(lane == 1, i2.astype(F32), meta)
    meta = jnp.where(lane == 2, 1.0 / den, meta)
    meta = jnp.where(lane == 3, e2 / den, meta)
    meta = jnp.where(lane == 4, rank1, meta)
    meta = jnp.where(lane == 5, rank2, meta)
    meta_ref[...] = meta


def _router(x2d, gain, w_router, b_router, tm):
    m, d = x2d.shape
    assert 2 * m < 2 ** 24
    w_pad = jnp.pad(w_router, ((0, 0), (0, LANES - N_EXPERTS)))
    w_hi = w_pad.astype(BF16)
    w_lo = (w_pad - w_hi.astype(F32)).astype(BF16)
    b_pad = jnp.pad(b_router, (0, LANES - N_EXPERTS)).reshape(1, LANES)
    row = lambda i: (i, 0)
    return pl.pallas_call(
        _router_kernel,
        grid=(m // tm,),
        in_specs=[pl.BlockSpec((tm, d), row), _const_spec((1, d)), _const_spec((d, LANES)),
                  _const_spec((d, LANES)), _const_spec((1, LANES))],
        out_specs=[pl.BlockSpec((tm, LANES), row), pl.BlockSpec((1, LANES), lambda i: (0, 0))],
        out_shape=[jax.ShapeDtypeStruct((m, LANES), F32), jax.ShapeDtypeStruct((1, LANES), F32)],
        compiler_params=_params("arbitrary"),
        name="moe_router",
    )(x2d, gain.reshape(1, d), w_hi, w_lo, b_pad)


def _row_copy(src_hbm, src_row, dst, dst_row, sem):
    return pltpu.make_async_copy(src_hbm.at[pl.ds(src_row, 1)], dst.at[pl.ds(dst_row, 1)], sem)


ROW_DMA_UNROLL = 8


def _dispatch_kernel(ends_ref, pos1_ref, pos2_ref, x_ref, g_ref, xs_hbm, h_ref, zero_ref, sem, zsem,
                     *, tm, tmx):
    @pl.when(pl.program_id(0) == 0)
    def _zero_padding():
        zero_ref[...] = jnp.zeros(zero_ref.shape, zero_ref.dtype)

        def fill(row0):
            cp = pltpu.make_async_copy(zero_ref, xs_hbm.at[pl.ds(row0, tmx)], zsem)
            cp.start()
            cp.wait()

        for e in range(N_EXPERTS):
            start_e = ends_ref[e - 1] if e else 0

            @pl.when(ends_ref[e] > start_e)
            def _():
                fill(pl.multiple_of(ends_ref[e] - tmx, tmx))

        def tail(t, carry):
            fill(pl.multiple_of(t * tmx, tmx))
            return carry

        lax.fori_loop(ends_ref[N_EXPERTS - 1] // tmx, xs_hbm.shape[0] // tmx, tail, 0)

    i = pl.program_id(0)
    slot = i % 2
    h_ref.at[slot][...] = _rmsnorm(x_ref[...], g_ref[...], RMS_EPS)

    def start(r, carry):
        _row_copy(h_ref.at[slot], r, xs_hbm, pos1_ref[r], sem.at[slot]).start()
        _row_copy(h_ref.at[slot], r, xs_hbm, pos2_ref[r], sem.at[slot]).start()
        return carry

    def wait_slot(sl):
        def wait(r, carry):
            _row_copy(h_ref.at[sl], 0, xs_hbm, 0, sem.at[sl]).wait()
            _row_copy(h_ref.at[sl], 0, xs_hbm, 0, sem.at[sl]).wait()
            return carry
        lax.fori_loop(0, tm, wait, 0, unroll=ROW_DMA_UNROLL)

    lax.fori_loop(0, tm, start, 0, unroll=ROW_DMA_UNROLL)

    @pl.when(i > 0)
    def _previous():
        wait_slot(1 - slot)

    @pl.when(i == pl.num_programs(0) - 1)
    def _own():
        wait_slot(slot)


def _dispatch(x2d, gain, pos1, pos2, ends, n_slots, tm, tmx):
    m, d = x2d.shape
    smem = lambda: pl.BlockSpec((tm,), lambda i, ends: (i,), memory_space=pltpu.SMEM)
    grid_spec = pltpu.PrefetchScalarGridSpec(
        num_scalar_prefetch=1,
        grid=(m // tm,),
        in_specs=[smem(), smem(), pl.BlockSpec((tm, d), lambda i, ends: (i, 0)),
                  pl.BlockSpec((1, d), lambda i, ends: (0, 0))],
        out_specs=pl.BlockSpec(memory_space=pl.ANY),
        scratch_shapes=[pltpu.VMEM((2, tm, d), F32), pltpu.VMEM((tmx, d), F32),
                        pltpu.SemaphoreType.DMA((2,)), pltpu.SemaphoreType.DMA(())],
    )
    return pl.pallas_call(
        functools.partial(_dispatch_kernel, tm=tm, tmx=tmx),
        grid_spec=grid_spec,
        out_shape=jax.ShapeDtypeStruct((n_slots, d), F32),
        compiler_params=_params("arbitrary"),
        name="moe_dispatch",
    )(ends, pos1, pos2, x2d, gain.reshape(1, d))


def _expert_ffn_kernel(te_ref, nt_ref, xs_ref, wgu_ref, wd_ref, ys_ref, *, d_ff, tf):
    @pl.when(pl.program_id(0) < nt_ref[0])
    def _():
        ys_ref[...] = _swiglu(xs_ref[...].astype(BF16), wgu_ref, wd_ref, d_ff, tf)

    @pl.when(pl.program_id(0) >= nt_ref[0])
    def _():
        ys_ref[...] = jnp.zeros(ys_ref.shape, F32)


def _expert_ffn(xs, tile_expert, n_tiles, wgu_bf, wd_bf, tmx):
    n_slots, d = xs.shape
    _, d_ff, _ = wd_bf.shape
    row = lambda i, te, nt: (jnp.minimum(i, nt[0] - 1), 0)
    grid_spec = pltpu.PrefetchScalarGridSpec(
        num_scalar_prefetch=2,
        grid=(n_slots // tmx,),
        in_specs=[pl.BlockSpec((tmx, d), row),
                  pl.BlockSpec((None, d, 2 * d_ff), lambda i, te, nt: (te[i], 0, 0)),
                  pl.BlockSpec((None, d_ff, d), lambda i, te, nt: (te[i], 0, 0))],
        out_specs=pl.BlockSpec((tmx, d), lambda i, te, nt: (i, 0)),
    )
    return pl.pallas_call(
        functools.partial(_expert_ffn_kernel, d_ff=d_ff, tf=256),
        grid_spec=grid_spec,
        out_shape=jax.ShapeDtypeStruct((n_slots, d), F32),
        compiler_params=_params("arbitrary"),
        name="moe_experts",
    )(tile_expert, n_tiles, xs, wgu_bf, wd_bf)


def _combine_kernel(pos1_ref, pos2_ref, nxt1_ref, nxt2_ref, ys_hbm, x_ref, meta_ref, gf_ref, o_ref,
                    y1_ref, y2_ref, sem, *, tm):
    i = pl.program_id(0)
    slot = i % 2

    def gather(p1_ref, p2_ref, sl):
        def start(r, carry):
            _row_copy(ys_hbm, p1_ref[r], y1_ref.at[sl], r, sem.at[sl]).start()
            _row_copy(ys_hbm, p2_ref[r], y2_ref.at[sl], r, sem.at[sl]).start()
            return carry
        lax.fori_loop(0, tm, start, 0, unroll=ROW_DMA_UNROLL)

    @pl.when(i == 0)
    def _first():
        gather(pos1_ref, pos2_ref, 0)

    @pl.when(i + 1 < pl.num_programs(0))
    def _next():
        gather(nxt1_ref, nxt2_ref, 1 - slot)

    def wait(r, carry):
        _row_copy(ys_hbm, 0, y1_ref.at[slot], 0, sem.at[slot]).wait()
        _row_copy(ys_hbm, 0, y2_ref.at[slot], 0, sem.at[slot]).wait()
        return carry

    lax.fori_loop(0, tm, wait, 0, unroll=ROW_DMA_UNROLL)
    meta = meta_ref[...]
    out = meta[:, 2:3] * y1_ref[slot] + meta[:, 3:4] * y2_ref[slot]
    o_ref[...] = _rmsnorm(x_ref[...] + out, gf_ref[...], RMS_EPS)


def _combine(ys, pos1, pos2, x2d, meta, gain_final, tm):
    m, d = x2d.shape
    last = m // tm - 1
    smem = lambda off: pl.BlockSpec((tm,), lambda i: (jnp.minimum(i + off, last),), memory_space=pltpu.SMEM)
    row = lambda i: (i, 0)
    return pl.pallas_call(
        functools.partial(_combine_kernel, tm=tm),
        grid=(m // tm,),
        in_specs=[smem(0), smem(0), smem(1), smem(1), pl.BlockSpec(memory_space=pl.ANY),
                  pl.BlockSpec((tm, d), row), pl.BlockSpec((tm, LANES), row), _const_spec((1, d))],
        out_specs=pl.BlockSpec((tm, d), row),
        out_shape=jax.ShapeDtypeStruct((m, d), F32),
        scratch_shapes=[pltpu.VMEM((2, tm, d), F32), pltpu.VMEM((2, tm, d), F32),
                        pltpu.SemaphoreType.DMA((2,))],
        compiler_params=_params("arbitrary"),
        name="moe_combine",
    )(pos1, pos2, pos1, pos2, ys, x2d, meta, gain_final.reshape(1, d))


def _moe(x2d, norm_gain, w_router, b_router, wgu_bf, wd_bf, gain_final, tm):
    m, d = x2d.shape
    tmx = 512 if m >= 4096 else 128
    meta, counts = _router(x2d, norm_gain, w_router, b_router, tm)
    counts = counts[0, :N_EXPERTS].astype(jnp.int32)
    padded = (counts + tmx - 1) // tmx * tmx
    ends = jnp.cumsum(padded)
    starts = ends - padded
    n_slots = 2 * m + N_EXPERTS * tmx
    n_tiles_max = n_slots // tmx
    tile_start = jnp.arange(n_tiles_max, dtype=jnp.int32) * tmx
    tile_expert = jnp.minimum(jnp.sum((tile_start[:, None] >= ends[None, :]).astype(jnp.int32), axis=1),
                              N_EXPERTS - 1).astype(jnp.int32)
    n_tiles = (ends[-1:] // tmx).astype(jnp.int32)
    experts = jnp.arange(N_EXPERTS, dtype=jnp.int32)[None, :]
    start_of = lambda e: jnp.sum(jnp.where(e[:, None] == experts, starts[None, :], 0), axis=1)
    e1, e2 = meta[:, 0].astype(jnp.int32), meta[:, 1].astype(jnp.int32)
    pos1 = start_of(e1) + meta[:, 4].astype(jnp.int32)
    pos2 = start_of(e2) + meta[:, 5].astype(jnp.int32)
    td = min(512, m)
    xs = _dispatch(x2d, norm_gain, pos1, pos2, ends.astype(jnp.int32), n_slots, td, tmx)
    ys = _expert_ffn(xs, tile_expert, n_tiles, wgu_bf, wd_bf, tmx)
    return _combine(ys, pos1, pos2, x2d, meta, gain_final, td)


def _forward(x, caches, w):
    b, t, d = x.shape
    m = b * t
    tm = min(512, m)
    x2d = x.reshape(m, d)
    lambda_init = 0.8 - 0.6 * math.exp(-0.3 * 0)
    width = DIFF_HEADS * 2 * DIFF_HEAD_DIM
    ncb = width // LANES

    if caches is None:
        pbf, ka, va, kb, vb, vt, kaug = _qkv_proj(x2d, w["norm_attn"], w["attn_w_in"], tm, seq=t)
    else:
        pbf, ka, va, kb, vb = _qkv_proj(x2d, w["norm_attn"], w["attn_w_in"], tm)
    pbf3 = pbf.reshape(b, t, 6 * width)
    if caches is None:
        oa = _diff_attn_prompt(pbf3, kaug.reshape(b, t, 2 * width), vt, 0, w["diff_lambda"],
                               w["diff_subln"], tile=min(1024, t), lambda_init=lambda_init)
        tqb = min(256, t)
        n_parts = BAND_PAST_MAX // tqb + 1
        bias = _band_bias(w["band_rel_bias"], 0, -BAND_PAST_MAX, tqb, n_parts * tqb, key_major=True)
        ob = _band_attn_prompt(pbf3, vt, 3, 4, 1, bias, tq=tqb, n_parts=n_parts, tkp=tqb)
        n_band = min(BAND_PAST_MAX, t)
        new_bk = kb.reshape(b, t, BAND_HEADS, BAND_HEAD_DIM)[:, t - n_band:]
        new_bv = vb.reshape(b, t, BAND_HEADS, BAND_HEAD_DIM)[:, t - n_band:]
    else:
        ck_a, cv_a, ck_b, cv_b = caches
        p_len, pb_len = ck_a.shape[1], ck_b.shape[1]
        cat = lambda c, lo: jnp.concatenate(
            [c.reshape(b, c.shape[1], width).astype(BF16), pbf3[:, :, lo:lo + width]], axis=1)
        k_a, v_a = cat(ck_a, width), cat(cv_a, 2 * width)
        k_b, v_b = cat(ck_b, 4 * width), cat(cv_b, 5 * width)
        oa = _diff_attn(pbf3, k_a, v_a, 0, 0, 0, w["diff_lambda"], w["diff_subln"],
                        tq=t, tk=p_len + t, q_off=p_len, lambda_init=lambda_init)
        bias = _band_bias(w["band_rel_bias"], p_len, p_len - pb_len, t, pb_len + t)
        ob = _band_attn(pbf3, k_b, v_b, 3 * ncb, 0, 0, bias, tq=t, n_parts=1, tkp=pb_len + t)
        new_bk = kb.reshape(b, t, BAND_HEADS, BAND_HEAD_DIM)
        new_bv = vb.reshape(b, t, BAND_HEADS, BAND_HEAD_DIM)
    new_dk = ka.reshape(b, t, 2 * DIFF_HEADS, DIFF_HEAD_DIM)
    new_dv = va.reshape(b, t, DIFF_HEADS, 2 * DIFF_HEAD_DIM)

    x2d = _attn_out_ffn(oa.reshape(m, width), ob.reshape(m, width), x2d, w["attn_w_out"],
                        w["norm_ffn"], w["ffn_w_gu"], w["ffn_w_down"], tm)

    seg = min(t, GMLP_CHUNK)
    x2d, v_rows = _gmlp(x2d, w["norm_gmlp"], w["gmlp_w_in"], w["gmlp_b_in"], w["gmlp_ln_g"],
                        w["gmlp_ln_b"], w["gmlp_w_s"], w["gmlp_b_s"], w["gmlp_w_out"], tm, seg,
                        return_v=caches is not None)
    y = _moe(x2d, w["norm_moe"], w["moe_w_router"], w["moe_b_router"], w["moe_w_gu"], w["moe_w_down"],
             w["norm_final"], tm)
    new_gv = None if v_rows is None else v_rows.reshape(b, t, -1)[None]
    return (y.reshape(b, t, d), new_dk[None], new_dv[None], new_bk[None], new_bv[None], new_gv)


def kernel(x_prompt, x_sample, cache_diff_k, cache_diff_v, cache_band_k, cache_band_v,
           norm_attn, attn_w_in, diff_lambda, diff_subln, band_rel_bias, attn_w_out,
           norm_ffn, ffn_w_gu, ffn_w_down,
           norm_gmlp, gmlp_w_in, gmlp_b_in, gmlp_ln_g, gmlp_ln_b, gmlp_w_s, gmlp_b_s, gmlp_w_out,
           norm_moe, moe_w_router, moe_b_router, moe_w_gu, moe_w_down, norm_final):
    w = {
        "norm_attn": norm_attn[0], "attn_w_in": attn_w_in[0].astype(BF16),
        "diff_lambda": diff_lambda[0], "diff_subln": diff_subln[0],
        "band_rel_bias": band_rel_bias[0], "attn_w_out": attn_w_out[0].astype(BF16),
        "norm_ffn": norm_ffn[0], "ffn_w_gu": ffn_w_gu[0].astype(BF16),
        "ffn_w_down": ffn_w_down[0].astype(BF16),
        "norm_gmlp": norm_gmlp[0], "gmlp_w_in": gmlp_w_in[0].astype(BF16),
        "gmlp_b_in": gmlp_b_in[0], "gmlp_ln_g": gmlp_ln_g[0], "gmlp_ln_b": gmlp_ln_b[0],
        "gmlp_w_s": gmlp_w_s[0], "gmlp_b_s": gmlp_b_s[0], "gmlp_w_out": gmlp_w_out[0].astype(BF16),
        "norm_moe": norm_moe[0], "moe_w_router": moe_w_router[0], "moe_b_router": moe_b_router[0],
        "moe_w_gu": moe_w_gu[0].astype(BF16), "moe_w_down": moe_w_down[0].astype(BF16),
        "norm_final": norm_final,
    }
    y_p, dk_p, dv_p, bk_p, bv_p, _ = _forward(x_prompt, None, w)
    y_s, dk_s, dv_s, bk_s, bv_s, gv_s = _forward(
        x_sample, (cache_diff_k[0], cache_diff_v[0], cache_band_k[0], cache_band_v[0]), w)
    return (y_p, y_s, dk_p, dv_p, bk_p, bv_p, dk_s, dv_s, bk_s, bv_s, gv_s)
```

```python
import functools
import math

import numpy as np
import jax
import jax.numpy as jnp
from jax import lax
from jax.experimental import pallas as pl
from jax.experimental.pallas import tpu as pltpu

CHUNK = 64
DIFF_HEADS = 4
DIFF_HEAD_DIM = 64
DIFF_SUBLN_EPS = 1e-5
BAND_HEADS = 8
BAND_HEAD_DIM = 64
BAND_PREV_CHUNKS = 8
BAND_PAST_MAX = BAND_PREV_CHUNKS * CHUNK
REL_CLIP = 256
GMLP_GROUPS = 8
GMLP_CHUNK = 128
N_EXPERTS = 8
RMS_EPS = 1e-6
LN_EPS = 1e-5
MASK_VALUE = -1e30

LANES = 128
ONES_ROWS = 16
VMEM_LIMIT_BYTES = 56 << 20

F32 = jnp.float32
BF16 = jnp.bfloat16
LOG2E = math.log2(math.e)
BF16_EXP2_ZERO = 140.0
Q_SCALE = DIFF_HEAD_DIM ** -0.5 * LOG2E
NT_DIMS = (((1,), (1,)), ((), ()))


def _params(*sem, flags=None):
    return pltpu.CompilerParams(dimension_semantics=sem, vmem_limit_bytes=VMEM_LIMIT_BYTES, flags=flags)


def _const_spec(shape):
    nd = len(shape)
    return pl.BlockSpec(shape, lambda *_: (0,) * nd, pipeline_mode=pl.Buffered(1))


def _rmsnorm(x, g, eps):
    return (x * lax.rsqrt(jnp.mean(x * x, axis=-1, keepdims=True) + eps)) * g


def _dot(a, b):
    return jnp.dot(a, b, preferred_element_type=F32)


def _swiglu(h, wgu_ref, wd_ref, d_ff, tf):
    y = None
    for c in range(d_ff // tf):
        g = _dot(h, wgu_ref[:, c * tf:(c + 1) * tf])
        u = _dot(h, wgu_ref[:, d_ff + c * tf:d_ff + (c + 1) * tf])
        a = (g * jax.nn.sigmoid(g) * u).astype(BF16)
        part = _dot(a, wd_ref[c * tf:(c + 1) * tf, :])
        y = part if y is None else y + part
    return y


def _qkv_kernel(x_ref, g_ref, w_ref, *rest, width, with_vt, tiles_per_seq):
    if with_vt:
        pbf_ref, ka_ref, va_ref, kb_ref, vb_ref, vt_ref, kaug_ref = rest
    else:
        pbf_ref, ka_ref, va_ref, kb_ref, vb_ref = rest
    tm = x_ref.shape[0]
    h = _rmsnorm(x_ref[...], g_ref[...], RMS_EPS).astype(BF16)
    f32_outs = {1: ka_ref, 2: va_ref, 4: kb_ref, 5: vb_ref}
    for c in range(6):
        r = _dot(h, w_ref[:, c * width:(c + 1) * width])
        if c in f32_outs:
            f32_outs[c][...] = r
        else:
            r = r * Q_SCALE
        pbf_ref[:, c * width:(c + 1) * width] = r.astype(BF16)
        if with_vt and c == 1:
            pos = (pl.program_id(0) % tiles_per_seq) * tm + lax.broadcasted_iota(jnp.int32, (tm, LANES), 0)
            lane = lax.broadcasted_iota(jnp.int32, (tm, LANES), 1)
            hi = (pos // LANES).astype(F32)
            lo = (pos % LANES).astype(F32)
            feats = jnp.where(lane < 3, hi, jnp.where(lane < 6, lo, 0.0)).astype(BF16)
            for hd in range(DIFF_HEADS):
                kaug_ref[:, 2 * hd * LANES:(2 * hd + 1) * LANES] = r[:, hd * LANES:(hd + 1) * LANES].astype(BF16)
                kaug_ref[:, (2 * hd + 1) * LANES:(2 * hd + 2) * LANES] = feats
        if with_vt and c in (2, 5):
            r0 = 0 if c == 2 else width
            vt_ref[r0:r0 + width, :] = r.T.astype(BF16)


def _qkv_proj(x2d, gain, w_bf, tm, seq=None):
    m, d = x2d.shape
    width = w_bf.shape[1] // 6
    row = lambda i: (i, 0)
    with_vt = seq is not None
    in_specs = [pl.BlockSpec((tm, d), row), _const_spec((1, d)), _const_spec(w_bf.shape)]
    out_specs = [pl.BlockSpec((tm, 6 * width), row)] + [pl.BlockSpec((tm, width), row)] * 4
    out_shape = [jax.ShapeDtypeStruct((m, 6 * width), BF16)] + [jax.ShapeDtypeStruct((m, width), F32)] * 4
    args = [x2d, gain.reshape(1, d), w_bf]
    nt = 1
    if with_vt:
        nt = seq // tm
        assert seq <= 256 * LANES
        vt_rows = 2 * width
        out_specs.append(pl.BlockSpec((None, vt_rows, tm), lambda i: (i // nt, 0, i % nt)))
        out_shape.append(jax.ShapeDtypeStruct((m // seq, vt_rows, seq), BF16))
        out_specs.append(pl.BlockSpec((tm, 2 * width), row))
        out_shape.append(jax.ShapeDtypeStruct((m, 2 * width), BF16))
    return pl.pallas_call(
        functools.partial(_qkv_kernel, width=width, with_vt=with_vt, tiles_per_seq=nt),
        grid=(m // tm,),
        in_specs=in_specs, out_specs=out_specs, out_shape=out_shape,
        compiler_params=_params("parallel"),
        name="qkv_proj",
    )(*args)


def _split_heads_rows(q, tq):
    qf = q.astype(F32)
    lane = lax.broadcasted_iota(jnp.int32, qf.shape, 1)
    lo = jnp.where(lane < DIFF_HEAD_DIM, qf, 0.0)
    hi = jnp.where(lane >= DIFF_HEAD_DIM, qf, 0.0)
    return jnp.concatenate([lo, hi], axis=0).astype(BF16)


def _diff_attn_kernel(ii_ref, jj_ref, last_ref, q_ref, k_ref, v_ref, slope_ref, lam_ref, g_ref,
                      o_ref, qz_ref, m_ref, l_ref, acc_ref, *, tq, tk, q_off, lambda_init):
    s = pl.program_id(2)
    i = ii_ref[s]
    j = jj_ref[s]

    @pl.when(j == 0)
    def _init():
        qz_ref[...] = _split_heads_rows(q_ref[...], tq)
        m_ref[...] = jnp.full(m_ref.shape, MASK_VALUE, F32)
        l_ref[...] = jnp.zeros(l_ref.shape, F32)
        acc_ref[...] = jnp.zeros(acc_ref.shape, F32)

    sc = lax.dot_general(qz_ref[...], k_ref[...], NT_DIMS, preferred_element_type=F32)
    row = lax.broadcasted_iota(jnp.int32, (2 * tq, 1), 0)
    row = jnp.where(row >= tq, row - tq, row)
    qpos = q_off + i * tq + row
    kpos = j * tk + lax.broadcasted_iota(jnp.int32, (1, tk), 1)
    dist = jnp.abs(qpos - kpos).astype(F32)
    sc = sc - slope_ref[:, 0:1] * dist
    visible = (kpos // CHUNK) <= (qpos // CHUNK)
    sc = jnp.where(visible, sc, MASK_VALUE)

    m_prev = m_ref[...]
    m_new = jnp.maximum(m_prev, jnp.max(sc, axis=1, keepdims=True))
    alpha = jnp.exp2(m_prev - m_new)
    p = jnp.exp2(sc - m_new)
    l_ref[...] = alpha * l_ref[...] + jnp.sum(p, axis=1, keepdims=True)
    acc_ref[...] = alpha * acc_ref[...] + _dot(p.astype(BF16), v_ref[...])
    m_ref[...] = m_new

    @pl.when(last_ref[s] == 1)
    def _finish():
        lp = lam_ref[...]
        lam = (jnp.exp(jnp.sum(lp[0:1] * lp[1:2], axis=1, keepdims=True))
               - jnp.exp(jnp.sum(lp[2:3] * lp[3:4], axis=1, keepdims=True)) + lambda_init)
        o_all = acc_ref[...] / l_ref[...]
        o = o_all[:tq] - lam * o_all[tq:]
        o = _rmsnorm(o, g_ref[...], DIFF_SUBLN_EPS) * (1.0 - lambda_init)
        o_ref[...] = o.astype(o_ref.dtype)


def _diff_attn(q_arr, k_arr, v_arr, q_cb, k_cb, v_cb, lam_p, subln_g, *, tq, tk, q_off, lambda_init):
    b, t_q, _ = q_arr.shape
    t_k = k_arr.shape[1]
    nq, nk = t_q // tq, t_k // tk
    pairs = [(i, j) for i in range(nq) for j in range(nk)
             if (j * tk) // CHUNK <= (q_off + i * tq + tq - 1) // CHUNK]
    ii = np.array([p[0] for p in pairs], np.int32)
    jj = np.array([p[1] for p in pairs], np.int32)
    last = np.array([1 if (n + 1 == len(pairs) or pairs[n + 1][0] != pairs[n][0]) else 0
                     for n in range(len(pairs))], np.int32)
    slopes = jnp.asarray(_alibi_slopes_log2()[:, None, None] * np.ones((1, 1, LANES), np.float32))

    grid_spec = pltpu.PrefetchScalarGridSpec(
        num_scalar_prefetch=3,
        grid=(b, DIFF_HEADS, len(pairs)),
        in_specs=[
            pl.BlockSpec((None, tq, LANES), lambda b_, h, s, ii, jj, la: (b_, ii[s], q_cb + h)),
            pl.BlockSpec((None, tk, LANES), lambda b_, h, s, ii, jj, la: (b_, jj[s], k_cb + h)),
            pl.BlockSpec((None, tk, LANES), lambda b_, h, s, ii, jj, la: (b_, jj[s], v_cb + h)),
            pl.BlockSpec((None, 1, LANES), lambda b_, h, s, ii, jj, la: (h, 0, 0)),
            pl.BlockSpec((4, DIFF_HEAD_DIM), lambda b_, h, s, ii, jj, la: (0, 0)),
            pl.BlockSpec((1, LANES), lambda b_, h, s, ii, jj, la: (0, 0)),
        ],
        out_specs=pl.BlockSpec((None, tq, LANES), lambda b_, h, s, ii, jj, la: (b_, ii[s], h)),
        scratch_shapes=[
            pltpu.VMEM((2 * tq, LANES), BF16),
            pltpu.VMEM((2 * tq, 1), F32),
            pltpu.VMEM((2 * tq, 1), F32),
            pltpu.VMEM((2 * tq, LANES), F32),
        ],
    )
    return pl.pallas_call(
        functools.partial(_diff_attn_kernel, tq=tq, tk=tk, q_off=q_off, lambda_init=lambda_init),
        grid_spec=grid_spec,
        out_shape=jax.ShapeDtypeStruct((b, t_q, DIFF_HEADS * LANES), BF16),
        compiler_params=_params("parallel", "parallel", "arbitrary"),
        name="diff_attn",
    )(jnp.asarray(ii), jnp.asarray(jj), jnp.asarray(last), q_arr, k_arr, v_arr, slopes,
      lam_p, subln_g.reshape(1, LANES))


def _alibi_slopes_log2():
    return (2.0 ** (-8.0 * np.arange(1, DIFF_HEADS + 1, dtype=np.float64) / DIFF_HEADS) * LOG2E).astype(np.float32)


def _diff_prompt_kernel(ii_ref, jj_ref, last_ref, lin_ref, q_ref, k_ref, vt_ref, dbias_ref, qfeat_ref,
                        lam_ref, g_ref, o_ref, qzt_ref, s0_ref, s1_ref, mx0_ref, mx1_ref, m_ref, acc_ref,
                        *, tq, tk, cb, lambda_init):
    s = pl.program_id(2)
    n_pairs = pl.num_programs(2) - 1
    sp = jnp.minimum(s, n_pairs - 1)
    sc = jnp.maximum(s - 1, 0)
    consuming = s > 0
    nblk = 2 * tq // cb
    blk = lambda c: slice(c * cb, (c + 1) * cb)

    @pl.when(s == 0)
    def _first_step():
        s1_ref[...] = jnp.zeros(s1_ref.shape, F32)
        mx1_ref[...] = jnp.zeros(mx1_ref.shape, F32)

    @pl.when(jj_ref[sp] == 0)
    def _new_queries():
        qf = q_ref[...].astype(F32)
        lane = lax.broadcasted_iota(jnp.int32, qf.shape, 1)
        qzt_ref[0:LANES, 0:tq] = jnp.where(lane < DIFF_HEAD_DIM, qf, 0.0).T.astype(BF16)
        qzt_ref[0:LANES, tq:2 * tq] = jnp.where(lane >= DIFF_HEAD_DIM, qf, 0.0).T.astype(BF16)
        qzt_ref[LANES:2 * LANES, :] = jnp.broadcast_to(qfeat_ref[...], (LANES, 2 * tq)).astype(BF16)

    @pl.when(jj_ref[sc] == 0)
    def _reset():
        m_ref[...] = jnp.full(m_ref.shape, MASK_VALUE, F32)
        acc_ref[...] = jnp.zeros(acc_ref.shape, F32)

    def stages(s_prod, mx_prod, s_cons, mx_cons):
        @pl.when(jnp.logical_and(consuming, lin_ref[sc] == 0))
        def _diagonal():
            for c in range(nblk):
                v = s_cons[:, blk(c)] + dbias_ref[:, (c * cb) % tq:(c * cb) % tq + cb]
                s_cons[:, blk(c)] = v
                mx_cons[:, blk(c)] = jnp.max(v, axis=0, keepdims=True)

        def produce():
            k = k_ref[...]
            for c in range(nblk):
                v = _dot(k, qzt_ref[:, blk(c)])
                s_prod[:, blk(c)] = v
                mx_prod[:, blk(c)] = jnp.max(v, axis=0, keepdims=True)

        def consume():
            vt = jnp.concatenate([vt_ref[...], jnp.ones((ONES_ROWS, tk), BF16)], axis=0)
            m_prev = [m_ref[:, blk(c)] for c in range(nblk)]
            acc_prev = [acc_ref[:, blk(c)] for c in range(nblk)]
            m_out, acc_out = [], []
            for c in range(nblk):
                m_new = jnp.maximum(m_prev[c], mx_cons[:, blk(c)])
                alpha = jnp.exp2(m_prev[c] - m_new)
                p = jnp.exp2((s_cons[:, blk(c)] - m_new).astype(BF16))
                acc_out.append(alpha * acc_prev[c] + _dot(vt, p))
                m_out.append(m_new)
            for c in range(nblk):
                m_ref[:, blk(c)] = m_out[c]
                acc_ref[:, blk(c)] = acc_out[c]

        negligible = jnp.max(mx_cons[...] - m_ref[...]) < -BF16_EXP2_ZERO

        @pl.when(negligible)
        def _produce_only():
            produce()

        @pl.when(jnp.logical_not(negligible))
        def _both():
            produce()
            consume()

    @pl.when(s % 2 == 0)
    def _even():
        stages(s0_ref, mx0_ref, s1_ref, mx1_ref)

    @pl.when(s % 2 == 1)
    def _odd():
        stages(s1_ref, mx1_ref, s0_ref, mx0_ref)

    @pl.when(jnp.logical_and(consuming, last_ref[sc] == 1))
    def _finish():
        lp = lam_ref[...]
        lam = (jnp.exp(jnp.sum(lp[0:1] * lp[1:2], axis=1, keepdims=True))
               - jnp.exp(jnp.sum(lp[2:3] * lp[3:4], axis=1, keepdims=True)) + lambda_init)
        o_all = acc_ref[0:LANES, :] / acc_ref[LANES:LANES + 1, :]
        o = o_all[:, 0:tq] - lam * o_all[:, tq:2 * tq]
        o = o * lax.rsqrt(jnp.mean(o * o, axis=0, keepdims=True) + DIFF_SUBLN_EPS)
        o = o * g_ref[...] * (1.0 - lambda_init)
        o_ref[...] = o.T.astype(o_ref.dtype)


def _bf16_split3(x):
    import ml_dtypes
    rnd = lambda v: v.astype(ml_dtypes.bfloat16).astype(np.float32)
    x = np.asarray(x, np.float32)
    hi = rnd(x)
    mid = rnd(x - hi)
    lo = rnd(x - hi - mid)
    return hi, mid, lo


def _diff_attn_prompt(pbf3, kaug3, vt_arr, q_cb, lam_p, subln_g, *, tile, lambda_init):
    b, t, _ = pbf3.shape
    tq = tk = tile
    n = t // tile
    pairs = [(i, j) for i in range(n) for j in range(i + 1)]
    ii = np.array([p[0] for p in pairs], np.int32)
    jj = np.array([p[1] for p in pairs], np.int32)
    last = (ii == jj).astype(np.int32)
    lin = (ii != jj).astype(np.int32)
    c2 = _alibi_slopes_log2()
    qfeat = np.zeros((DIFF_HEADS, LANES, 1), np.float32)
    for r, part in enumerate(_bf16_split3(c2)):
        qfeat[:, r, 0] = part * LANES
        qfeat[:, 3 + r, 0] = part
    pos = np.arange(tile)
    d = (pos[None, :] - pos[:, None]).astype(np.float32)
    visible = (pos[:, None] // CHUNK) <= (pos[None, :] // CHUNK)
    dbias = jnp.asarray(np.where(visible[None], 2.0 * c2[:, None, None] * np.minimum(d, 0.0)[None],
                                 MASK_VALUE).astype(np.float32))
    cb = min(256, 2 * tq)
    n_pairs = len(pairs)
    prod = lambda s: jnp.minimum(s, n_pairs - 1)
    cons = lambda s: jnp.maximum(s - 1, 0)
    idx = lambda f: (lambda b_, h, s, ii, jj, la, li: f(b_, h, s, ii, jj))
    grid_spec = pltpu.PrefetchScalarGridSpec(
        num_scalar_prefetch=4,
        grid=(b, DIFF_HEADS, n_pairs + 1),
        in_specs=[
            pl.BlockSpec((None, tq, LANES), idx(lambda b_, h, s, ii, jj: (b_, ii[prod(s)], q_cb + h))),
            pl.BlockSpec((None, tk, 2 * LANES), idx(lambda b_, h, s, ii, jj: (b_, jj[prod(s)], h))),
            pl.BlockSpec((None, LANES, tk), idx(lambda b_, h, s, ii, jj: (b_, h, jj[cons(s)]))),
            pl.BlockSpec((None, tk, tq), idx(lambda b_, h, s, ii, jj: (h, 0, 0))),
            pl.BlockSpec((None, LANES, 1), idx(lambda b_, h, s, ii, jj: (h, 0, 0))),
            pl.BlockSpec((4, DIFF_HEAD_DIM), idx(lambda b_, h, s, ii, jj: (0, 0))),
            pl.BlockSpec((LANES, 1), idx(lambda b_, h, s, ii, jj: (0, 0))),
        ],
        out_specs=pl.BlockSpec((None, tq, LANES), idx(lambda b_, h, s, ii, jj: (b_, ii[cons(s)], h))),
        scratch_shapes=[
            pltpu.VMEM((2 * LANES, 2 * tq), BF16),
            pltpu.VMEM((tk, 2 * tq), F32), pltpu.VMEM((tk, 2 * tq), F32),
            pltpu.VMEM((1, 2 * tq), F32), pltpu.VMEM((1, 2 * tq), F32),
            pltpu.VMEM((1, 2 * tq), F32),
            pltpu.VMEM((LANES + ONES_ROWS, 2 * tq), F32),
        ],
    )
    return pl.pallas_call(
        functools.partial(_diff_prompt_kernel, tq=tq, tk=tk, cb=cb, lambda_init=lambda_init),
        grid_spec=grid_spec,
        out_shape=jax.ShapeDtypeStruct((b, t, DIFF_HEADS * LANES), BF16),
        compiler_params=_params("parallel", "parallel", "arbitrary"),
        name="diff_attn_prompt",
    )(jnp.asarray(ii), jnp.asarray(jj), jnp.asarray(last), jnp.asarray(lin), pbf3, kaug3, vt_arr, dbias,
      jnp.asarray(qfeat), lam_p, subln_g.reshape(LANES, 1))


def _band_attn_kernel(*refs, tq, n_parts, tkp):
    q_ref = refs[0]
    k_refs = refs[1:1 + n_parts]
    v_refs = refs[1 + n_parts:1 + 2 * n_parts]
    bias_ref = refs[1 + 2 * n_parts]
    o_ref = refs[2 + 2 * n_parts]
    i = pl.program_id(2)
    qz = _split_heads_rows(q_ref[...], tq)
    scores = []
    for m in range(n_parts):
        sc = lax.dot_general(qz, k_refs[m][...], NT_DIMS, preferred_element_type=F32)
        sc = sc + bias_ref[:, m * tkp:(m + 1) * tkp]
        if n_parts > 1:
            sc = jnp.where(i - (n_parts - 1) + m >= 0, sc, MASK_VALUE)
        scores.append(sc)
    mx = functools.reduce(jnp.maximum, [jnp.max(sc, axis=1, keepdims=True) for sc in scores])
    den = None
    num = None
    for m in range(n_parts):
        p = jnp.exp2(scores[m] - mx)
        d = jnp.sum(p, axis=1, keepdims=True)
        r = _dot(p.astype(BF16), v_refs[m][...])
        den = d if den is None else den + d
        num = r if num is None else num + r
    r = num / den
    lane = lax.broadcasted_iota(jnp.int32, (tq, LANES), 1)
    o_ref[...] = jnp.where(lane < BAND_HEAD_DIM, r[:tq], r[tq:]).astype(o_ref.dtype)


def _band_bias_kernel(line_ref, o_ref, *, q0, k0):
    tq, nk = o_ref.shape
    line = jnp.broadcast_to(line_ref[...], (tq, line_ref.shape[1]))
    rolled = pltpu.roll(line, 0, 1, stride=1, stride_axis=0)
    qc = (q0 + lax.broadcasted_iota(jnp.int32, (tq, nk), 0)) >> 6
    kc = (k0 + lax.broadcasted_iota(jnp.int32, (tq, nk), 1)) >> 6
    visible = (kc <= qc) & (qc - kc <= BAND_PREV_CHUNKS)
    o_ref[...] = jnp.where(visible, rolled[:, 0:nk], MASK_VALUE)


def _band_bias_t_kernel(line_ref, o_ref, *, q0, k0):
    nk, tq = o_ref.shape
    line = jnp.broadcast_to(line_ref[...], (nk, line_ref.shape[1]))
    rolled = pltpu.roll(line, 0, 1, stride=1, stride_axis=0)
    kc = (k0 + lax.broadcasted_iota(jnp.int32, (nk, tq), 0)) >> 6
    qc = (q0 + lax.broadcasted_iota(jnp.int32, (nk, tq), 1)) >> 6
    visible = (kc <= qc) & (qc - kc <= BAND_PREV_CHUNKS)
    o_ref[...] = jnp.where(visible, rolled[:, 0:tq], MASK_VALUE)


def _band_bias(rel_table, q0, k0, tq, nk, key_major=False):
    assert CHUNK == 64
    width = pl.next_power_of_2(tq + nk - 1)
    y = np.arange(width)
    c_minus_r = np.where(y < nk, y, y - width)
    rel = np.clip((q0 - k0) - c_minus_r, -(CHUNK - 1), REL_CLIP) + (CHUNK - 1)
    if key_major:
        rel = rel[(-y) % width]
    line = (rel_table.astype(F32) * LOG2E)[:, rel].reshape(BAND_HEADS, 1, width)
    if key_major:
        kern, block, shape = _band_bias_t_kernel, (None, nk, tq), (BAND_HEADS // 2, nk, 2 * tq)
        index = lambda h: (h // 2, 0, h % 2)
    else:
        kern, block, shape = _band_bias_kernel, (None, tq, nk), (BAND_HEADS // 2, 2 * tq, nk)
        index = lambda h: (h // 2, h % 2, 0)
    return pl.pallas_call(
        functools.partial(kern, q0=q0, k0=k0),
        grid=(BAND_HEADS,),
        in_specs=[pl.BlockSpec((None, 1, width), lambda h: (h, 0, 0))],
        out_specs=pl.BlockSpec(block, index),
        out_shape=jax.ShapeDtypeStruct(shape, F32),
        compiler_params=_params("parallel"),
        name="band_bias",
    )(line)


def _band_prompt_kernel(*refs, tq, n_parts, tkp):
    q_ref = refs[0]
    k_refs = refs[1:1 + n_parts]
    vt_refs = refs[1 + n_parts:1 + 2 * n_parts]
    bias_ref = refs[1 + 2 * n_parts]
    o_ref = refs[2 + 2 * n_parts]
    nk = n_parts * tkp
    hd = BAND_HEAD_DIM
    i = pl.program_id(1)
    ones = jnp.ones((ONES_ROWS, nk), BF16)
    for pr in range(BAND_HEADS // 2):
        ln = slice(pr * LANES, (pr + 1) * LANES)
        qf = q_ref[:, ln].astype(F32)
        lane = lax.broadcasted_iota(jnp.int32, qf.shape, 1)
        qzt = jnp.concatenate([jnp.where(lane < hd, qf, 0.0).T, jnp.where(lane >= hd, qf, 0.0).T],
                              axis=1).astype(BF16)
        k = jnp.concatenate([r[:, ln] for r in k_refs], axis=0)
        sc = _dot(k, qzt) + bias_ref[pr]
        row = lax.broadcasted_iota(jnp.int32, sc.shape, 0)
        sc = jnp.where(row >= (n_parts - 1 - i) * tkp, sc, MASK_VALUE)
        mx = jnp.max(sc, axis=0, keepdims=True)
        p = jnp.exp2((sc - mx).astype(BF16))
        vt = jnp.concatenate([r[ln, :] for r in vt_refs] , axis=1)
        r = _dot(jnp.concatenate([vt, ones], axis=0), p)
        o = r[0:LANES, :] / r[LANES:LANES + 1, :]
        o = jnp.concatenate([o[0:hd, 0:tq], o[hd:2 * hd, tq:2 * tq]], axis=0)
        o_ref[:, ln] = o.T.astype(o_ref.dtype)


def _band_attn_prompt(pbf3, vt_arr, q_cb, k_cb, vt_rb, bias_t, *, tq, n_parts, tkp):
    b, t, _ = pbf3.shape
    width = (BAND_HEADS // 2) * LANES
    part = lambda i, m: jnp.maximum(i - (n_parts - 1) + m, 0)
    k_spec = lambda m: pl.BlockSpec((None, tkp, width), lambda b_, i: (b_, part(i, m), k_cb))
    vt_spec = lambda m: pl.BlockSpec((None, width, tkp), lambda b_, i: (b_, vt_rb, part(i, m)))
    return pl.pallas_call(
        functools.partial(_band_prompt_kernel, tq=tq, n_parts=n_parts, tkp=tkp),
        grid=(b, t // tq),
        in_specs=[pl.BlockSpec((None, tq, width), lambda b_, i: (b_, i, q_cb))]
        + [k_spec(m) for m in range(n_parts)] + [vt_spec(m) for m in range(n_parts)]
        + [_const_spec(bias_t.shape)],
        out_specs=pl.BlockSpec((None, tq, width), lambda b_, i: (b_, i, 0)),
        out_shape=jax.ShapeDtypeStruct((b, t, width), BF16),
        compiler_params=_params("parallel", "parallel"),
        name="band_attn_prompt",
    )(pbf3, *([pbf3] * n_parts), *([vt_arr] * n_parts), bias_t)


def _band_attn(q_arr, k_arr, v_arr, q_cb, k_cb, v_cb, bias, *, tq, n_parts, tkp):
    b, t_q, _ = q_arr.shape
    n_pairs = BAND_HEADS // 2

    def kv_spec(cb, m):
        return pl.BlockSpec((None, tkp, LANES),
                            lambda p, b_, i: (b_, jnp.maximum(i - (n_parts - 1) + m, 0), cb + p))

    return pl.pallas_call(
        functools.partial(_band_attn_kernel, tq=tq, n_parts=n_parts, tkp=tkp),
        grid=(n_pairs, b, t_q // tq),
        in_specs=[pl.BlockSpec((None, tq, LANES), lambda p, b_, i: (b_, i, q_cb + p))]
        + [kv_spec(k_cb, m) for m in range(n_parts)]
        + [kv_spec(v_cb, m) for m in range(n_parts)]
        + [pl.BlockSpec((None, 2 * tq, n_parts * tkp), lambda p, b_, i: (p, 0, 0))],
        out_specs=pl.BlockSpec((None, tq, LANES), lambda p, b_, i: (b_, i, p)),
        out_shape=jax.ShapeDtypeStruct((b, t_q, n_pairs * LANES), BF16),
        compiler_params=_params("parallel", "parallel", "parallel"),
        name="band_attn",
    )(q_arr, *([k_arr] * n_parts), *([v_arr] * n_parts), bias)


def _attn_out_ffn_kernel(oa_ref, ob_ref, x_ref, wo_ref, g_ref, wgu_ref, wd_ref, o_ref, *, d_ff, tf):
    half = oa_ref.shape[1]
    x1 = x_ref[...] + _dot(oa_ref[...], wo_ref[0:half, :]) + _dot(ob_ref[...], wo_ref[half:2 * half, :])
    h = _rmsnorm(x1, g_ref[...], RMS_EPS).astype(BF16)
    o_ref[...] = x1 + _swiglu(h, wgu_ref, wd_ref, d_ff, tf)


def _attn_out_ffn(oa, ob, x2d, wo_bf, gain, wgu_bf, wd_bf, tm):
    m, d = x2d.shape
    d_ff = wd_bf.shape[0]
    row = lambda i: (i, 0)
    return pl.pallas_call(
        functools.partial(_attn_out_ffn_kernel, d_ff=d_ff, tf=256),
        grid=(m // tm,),
        in_specs=[pl.BlockSpec((tm, oa.shape[1]), row), pl.BlockSpec((tm, ob.shape[1]), row),
                  pl.BlockSpec((tm, d), row), _const_spec(wo_bf.shape), _const_spec((1, d)),
                  _const_spec(wgu_bf.shape), _const_spec(wd_bf.shape)],
        out_specs=pl.BlockSpec((tm, d), row),
        out_shape=jax.ShapeDtypeStruct((m, d), F32),
        compiler_params=_params("parallel"),
        name="attn_out_ffn",
    )(oa, ob, x2d, wo_bf, gain.reshape(1, d), wgu_bf, wd_bf)


def _gelu(z):
    return 0.5 * z * (1.0 + lax.erf(z * (2.0 ** -0.5)))


def _gmlp_kernel(x_ref, g_ref, win_ref, bin_ref, lng_ref, lnb_ref, ws_ref, bs_ref, wout_ref,
                 o_ref, v_ref, u_s, act_s, *, seg, cw):
    tm, gd = v_ref.shape
    gw = gd // GMLP_GROUPS
    x = x_ref[...]
    h = _rmsnorm(x, g_ref[...], RMS_EPS).astype(BF16)
    for c in range(gd // cw):
        lo, hi = c * cw, (c + 1) * cw
        u_s[:, lo:hi] = _gelu(_dot(h, win_ref[:, lo:hi]) + bin_ref[:, lo:hi])
        v_ref[:, lo:hi] = _gelu(_dot(h, win_ref[:, gd + lo:gd + hi]) + bin_ref[:, gd + lo:gd + hi])
    v = v_ref[...]
    mu = jnp.mean(v, axis=-1, keepdims=True)
    var = jnp.mean(jnp.square(v - mu), axis=-1, keepdims=True)
    v_ref[...] = (v - mu) * lax.rsqrt(var + LN_EPS) * lng_ref[...] + lnb_ref[...]
    r_i = lax.broadcasted_iota(jnp.int32, (seg, seg), 0)
    c_i = lax.broadcasted_iota(jnp.int32, (seg, seg), 1)
    for g in range(GMLP_GROUPS):
        w = jnp.where(r_i >= c_i, ws_ref[g], 0.0).astype(BF16)
        for n in range(tm // seg):
            rows = slice(n * seg, (n + 1) * seg)
            cols = slice(g * gw, (g + 1) * gw)
            sv = _dot(w, v_ref[rows, cols].astype(BF16)) + bs_ref[g]
            act_s[rows, cols] = (u_s[rows, cols] * sv).astype(BF16)
    o_ref[...] = x + _dot(act_s[...], wout_ref[...])


def _gmlp(x2d, gain, win_bf, b_in, ln_g, ln_b, w_s, b_s, wout_bf, tm, seg, return_v):
    m, d = x2d.shape
    gd = wout_bf.shape[0]
    row = lambda i: (i, 0)
    ws = w_s[:, :seg, :seg]
    bs = b_s[:, :seg, None]
    out_specs = [pl.BlockSpec((tm, d), row)]
    out_shape = [jax.ShapeDtypeStruct((m, d), F32)]
    scratch = [pltpu.VMEM((tm, gd), F32), pltpu.VMEM((tm, gd), BF16)]
    if return_v:
        out_specs.append(pl.BlockSpec((tm, gd), row))
        out_shape.append(jax.ShapeDtypeStruct((m, gd), F32))
    else:
        scratch.insert(0, pltpu.VMEM((tm, gd), F32))
    outs = pl.pallas_call(
        functools.partial(_gmlp_kernel, seg=seg, cw=512),
        grid=(m // tm,),
        in_specs=[pl.BlockSpec((tm, d), row), _const_spec((1, d)), _const_spec(win_bf.shape),
                  _const_spec((1, 2 * gd)), _const_spec((1, gd)), _const_spec((1, gd)),
                  _const_spec(ws.shape), _const_spec(bs.shape), _const_spec(wout_bf.shape)],
        out_specs=out_specs, out_shape=out_shape, scratch_shapes=scratch,
        compiler_params=_params("parallel"),
        name="gmlp",
    )(x2d, gain.reshape(1, d), win_bf, b_in.reshape(1, 2 * gd), ln_g.reshape(1, gd),
      ln_b.reshape(1, gd), ws, bs, wout_bf)
    return (outs[0], outs[1]) if return_v else (outs[0], None)


def _router_kernel(x_ref, g_ref, whi_ref, wlo_ref, b_ref, meta_ref, cnt_ref):
    h = _rmsnorm(x_ref[...], g_ref[...], RMS_EPS)
    h_hi = h.astype(BF16)
    h_lo = (h - h_hi.astype(F32)).astype(BF16)
    logits = (_dot(h_hi, whi_ref[...]) + _dot(h_hi, wlo_ref[...]) + _dot(h_lo, whi_ref[...])
              + b_ref[...])
    tm = logits.shape[0]
    lane = lax.broadcasted_iota(jnp.int32, logits.shape, 1)
    logits = jnp.where(lane < N_EXPERTS, logits, -jnp.inf)
    v1 = jnp.max(logits, axis=1, keepdims=True)
    i1 = jnp.min(jnp.where(logits == v1, lane, LANES), axis=1, keepdims=True)
    rest = jnp.where(lane == i1, -jnp.inf, logits)
    v2 = jnp.max(rest, axis=1, keepdims=True)
    i2 = jnp.min(jnp.where(rest == v2, lane, LANES), axis=1, keepdims=True)
    e2 = jnp.exp(v2 - v1)
    den = 1.0 + e2

    @pl.when(pl.program_id(0) == 0)
    def _init():
        cnt_ref[...] = jnp.zeros(cnt_ref.shape, F32)

    oh1 = lane == i1
    oh2 = lane == i2
    r_i = lax.broadcasted_iota(jnp.int32, (tm, tm), 0)
    c_i = lax.broadcasted_iota(jnp.int32, (tm, tm), 1)
    before = jnp.where(c_i < r_i, 1.0, 0.0).astype(BF16)
    cum1 = _dot(before, jnp.where(oh1, 1.0, 0.0).astype(BF16))
    cum2 = _dot(before, jnp.where(oh2, 1.0, 0.0).astype(BF16))
    n1 = jnp.sum(jnp.where(oh1, 1.0, 0.0), axis=0, keepdims=True)
    n2 = jnp.sum(jnp.where(oh2, 1.0, 0.0), axis=0, keepdims=True)
    base = cnt_ref[...]
    rank1 = jnp.sum(jnp.where(oh1, base + cum1, 0.0), axis=1, keepdims=True)
    rank2 = jnp.sum(jnp.where(oh2, base + n1 + cum2, 0.0), axis=1, keepdims=True)
    cnt_ref[...] = base + n1 + n2
    meta = jnp.where(lane == 0, i1.astype(F32), 0.0)
    meta = jnp.where(lane == 1, i2.astype(F32), meta)
    meta = jnp.where(lane == 2, 1.0 / den, meta)
    meta = jnp.where(lane == 3, e2 / den, meta)
    meta = jnp.where(lane == 4, rank1, meta)
    meta = jnp.where(lane == 5, rank2, meta)
    meta_ref[...] = meta


def _router(x2d, gain, w_router, b_router, tm):
    m, d = x2d.shape
    assert 2 * m < 2 ** 24
    w_pad = jnp.pad(w_router, ((0, 0), (0, LANES - N_EXPERTS)))
    w_hi = w_pad.astype(BF16)
    w_lo = (w_pad - w_hi.astype(F32)).astype(BF16)
    b_pad = jnp.pad(b_router, (0, LANES - N_EXPERTS)).reshape(1, LANES)
    row = lambda i: (i, 0)
    return pl.pallas_call(
        _router_kernel,
        grid=(m // tm,),
        in_specs=[pl.BlockSpec((tm, d), row), _const_spec((1, d)), _const_spec((d, LANES)),
                  _const_spec((d, LANES)), _const_spec((1, LANES))],
        out_specs=[pl.BlockSpec((tm, LANES), row), pl.BlockSpec((1, LANES), lambda i: (0, 0))],
        out_shape=[jax.ShapeDtypeStruct((m, LANES), F32), jax.ShapeDtypeStruct((1, LANES), F32)],
        compiler_params=_params("arbitrary"),
        name="moe_router",
    )(x2d, gain.reshape(1, d), w_hi, w_lo, b_pad)


def _row_copy(src_hbm, src_row, dst, dst_row, sem):
    return pltpu.make_async_copy(src_hbm.at[pl.ds(src_row, 1)], dst.at[pl.ds(dst_row, 1)], sem)


ROW_DMA_UNROLL = 8


def _dispatch_kernel(ends_ref, pos1_ref, pos2_ref, x_ref, g_ref, xs_hbm, h_ref, zero_ref, sem, zsem,
                     *, tm, tmx):
    @pl.when(pl.program_id(0) == 0)
    def _zero_padding():
        zero_ref[...] = jnp.zeros(zero_ref.shape, zero_ref.dtype)

        def fill(row0):
            cp = pltpu.make_async_copy(zero_ref, xs_hbm.at[pl.ds(row0, tmx)], zsem)
            cp.start()
            cp.wait()

        for e in range(N_EXPERTS):
            start_e = ends_ref[e - 1] if e else 0

            @pl.when(ends_ref[e] > start_e)
            def _():
                fill(pl.multiple_of(ends_ref[e] - tmx, tmx))

        def tail(t, carry):
            fill(pl.multiple_of(t * tmx, tmx))
            return carry

        lax.fori_loop(ends_ref[N_EXPERTS - 1] // tmx, xs_hbm.shape[0] // tmx, tail, 0)

    i = pl.program_id(0)
    slot = i % 2
    h_ref.at[slot][...] = _rmsnorm(x_ref[...], g_ref[...], RMS_EPS)

    def start(r, carry):
        _row_copy(h_ref.at[slot], r, xs_hbm, pos1_ref[r], sem.at[slot]).start()
        _row_copy(h_ref.at[slot], r, xs_hbm, pos2_ref[r], sem.at[slot]).start()
        return carry

    def wait_slot(sl):
        def wait(r, carry):
            _row_copy(h_ref.at[sl], 0, xs_hbm, 0, sem.at[sl]).wait()
            _row_copy(h_ref.at[sl], 0, xs_hbm, 0, sem.at[sl]).wait()
            return carry
        lax.fori_loop(0, tm, wait, 0, unroll=ROW_DMA_UNROLL)

    lax.fori_loop(0, tm, start, 0, unroll=ROW_DMA_UNROLL)

    @pl.when(i > 0)
    def _previous():
        wait_slot(1 - slot)

    @pl.when(i == pl.num_programs(0) - 1)
    def _own():
        wait_slot(slot)


def _dispatch(x2d, gain, pos1, pos2, ends, n_slots, tm, tmx):
    m, d = x2d.shape
    smem = lambda: pl.BlockSpec((tm,), lambda i, ends: (i,), memory_space=pltpu.SMEM)
    grid_spec = pltpu.PrefetchScalarGridSpec(
        num_scalar_prefetch=1,
        grid=(m // tm,),
        in_specs=[smem(), smem(), pl.BlockSpec((tm, d), lambda i, ends: (i, 0)),
                  pl.BlockSpec((1, d), lambda i, ends: (0, 0))],
        out_specs=pl.BlockSpec(memory_space=pl.ANY),
        scratch_shapes=[pltpu.VMEM((2, tm, d), F32), pltpu.VMEM((tmx, d), F32),
                        pltpu.SemaphoreType.DMA((2,)), pltpu.SemaphoreType.DMA(())],
    )
    return pl.pallas_call(
        functools.partial(_dispatch_kernel, tm=tm, tmx=tmx),
        grid_spec=grid_spec,
        out_shape=jax.ShapeDtypeStruct((n_slots, d), F32),
        compiler_params=_params("arbitrary"),
        name="moe_dispatch",
    )(ends, pos1, pos2, x2d, gain.reshape(1, d))


def _expert_ffn_kernel(te_ref, nt_ref, xs_ref, wgu_ref, wd_ref, ys_ref, *, d_ff, tf):
    @pl.when(pl.program_id(0) < nt_ref[0])
    def _():
        ys_ref[...] = _swiglu(xs_ref[...].astype(BF16), wgu_ref, wd_ref, d_ff, tf)

    @pl.when(pl.program_id(0) >= nt_ref[0])
    def _():
        ys_ref[...] = jnp.zeros(ys_ref.shape, F32)


def _expert_ffn(xs, tile_expert, n_tiles, wgu_bf, wd_bf, tmx):
    n_slots, d = xs.shape
    _, d_ff, _ = wd_bf.shape
    row = lambda i, te, nt: (jnp.minimum(i, nt[0] - 1), 0)
    grid_spec = pltpu.PrefetchScalarGridSpec(
        num_scalar_prefetch=2,
        grid=(n_slots // tmx,),
        in_specs=[pl.BlockSpec((tmx, d), row),
                  pl.BlockSpec((None, d, 2 * d_ff), lambda i, te, nt: (te[i], 0, 0)),
                  pl.BlockSpec((None, d_ff, d), lambda i, te, nt: (te[i], 0, 0))],
        out_specs=pl.BlockSpec((tmx, d), lambda i, te, nt: (i, 0)),
    )
    return pl.pallas_call(
        functools.partial(_expert_ffn_kernel, d_ff=d_ff, tf=256),
        grid_spec=grid_spec,
        out_shape=jax.ShapeDtypeStruct((n_slots, d), F32),
        compiler_params=_params("arbitrary"),
        name="moe_experts",
    )(tile_expert, n_tiles, xs, wgu_bf, wd_bf)


def _combine_kernel(pos1_ref, pos2_ref, nxt1_ref, nxt2_ref, ys_hbm, x_ref, meta_ref, gf_ref, o_ref,
                    y1_ref, y2_ref, sem, *, tm):
    i = pl.program_id(0)
    slot = i % 2

    def gather(p1_ref, p2_ref, sl):
        def start(r, carry):
            _row_copy(ys_hbm, p1_ref[r], y1_ref.at[sl], r, sem.at[sl]).start()
            _row_copy(ys_hbm, p2_ref[r], y2_ref.at[sl], r, sem.at[sl]).start()
            return carry
        lax.fori_loop(0, tm, start, 0, unroll=ROW_DMA_UNROLL)

    @pl.when(i == 0)
    def _first():
        gather(pos1_ref, pos2_ref, 0)

    @pl.when(i + 1 < pl.num_programs(0))
    def _next():
        gather(nxt1_ref, nxt2_ref, 1 - slot)

    def wait(r, carry):
        _row_copy(ys_hbm, 0, y1_ref.at[slot], 0, sem.at[slot]).wait()
        _row_copy(ys_hbm, 0, y2_ref.at[slot], 0, sem.at[slot]).wait()
        return carry

    lax.fori_loop(0, tm, wait, 0, unroll=ROW_DMA_UNROLL)
    meta = meta_ref[...]
    out = meta[:, 2:3] * y1_ref[slot] + meta[:, 3:4] * y2_ref[slot]
    o_ref[...] = _rmsnorm(x_ref[...] + out, gf_ref[...], RMS_EPS)


def _combine(ys, pos1, pos2, x2d, meta, gain_final, tm):
    m, d = x2d.shape
    last = m // tm - 1
    smem = lambda off: pl.BlockSpec((tm,), lambda i: (jnp.minimum(i + off, last),), memory_space=pltpu.SMEM)
    row = lambda i: (i, 0)
    return pl.pallas_call(
        functools.partial(_combine_kernel, tm=tm),
        grid=(m // tm,),
        in_specs=[smem(0), smem(0), smem(1), smem(1), pl.BlockSpec(memory_space=pl.ANY),
                  pl.BlockSpec((tm, d), row), pl.BlockSpec((tm, LANES), row), _const_spec((1, d))],
        out_specs=pl.BlockSpec((tm, d), row),
        out_shape=jax.ShapeDtypeStruct((m, d), F32),
        scratch_shapes=[pltpu.VMEM((2, tm, d), F32), pltpu.VMEM((2, tm, d), F32),
                        pltpu.SemaphoreType.DMA((2,))],
        compiler_params=_params("arbitrary"),
        name="moe_combine",
    )(pos1, pos2, pos1, pos2, ys, x2d, meta, gain_final.reshape(1, d))


def _moe(x2d, norm_gain, w_router, b_router, wgu_bf, wd_bf, gain_final, tm):
    m, d = x2d.shape
    tmx = 512 if m >= 4096 else 128
    meta, counts = _router(x2d, norm_gain, w_router, b_router, tm)
    counts = counts[0, :N_EXPERTS].astype(jnp.int32)
    padded = (counts + tmx - 1) // tmx * tmx
    ends = jnp.cumsum(padded)
    starts = ends - padded
    n_slots = 2 * m + N_EXPERTS * tmx
    n_tiles_max = n_slots // tmx
    tile_start = jnp.arange(n_tiles_max, dtype=jnp.int32) * tmx
    tile_expert = jnp.minimum(jnp.sum((tile_start[:, None] >= ends[None, :]).astype(jnp.int32), axis=1),
                              N_EXPERTS - 1).astype(jnp.int32)
    n_tiles = (ends[-1:] // tmx).astype(jnp.int32)
    experts = jnp.arange(N_EXPERTS, dtype=jnp.int32)[None, :]
    start_of = lambda e: jnp.sum(jnp.where(e[:, None] == experts, starts[None, :], 0), axis=1)
    e1, e2 = meta[:, 0].astype(jnp.int32), meta[:, 1].astype(jnp.int32)
    pos1 = start_of(e1) + meta[:, 4].astype(jnp.int32)
    pos2 = start_of(e2) + meta[:, 5].astype(jnp.int32)
    td = min(512, m)
    xs = _dispatch(x2d, norm_gain, pos1, pos2, ends.astype(jnp.int32), n_slots, td, tmx)
    ys = _expert_ffn(xs, tile_expert, n_tiles, wgu_bf, wd_bf, tmx)
    return _combine(ys, pos1, pos2, x2d, meta, gain_final, td)


def _forward(x, caches, w):
    b, t, d = x.shape
    m = b * t
    tm = min(512, m)
    x2d = x.reshape(m, d)
    lambda_init = 0.8 - 0.6 * math.exp(-0.3 * 0)
    width = DIFF_HEADS * 2 * DIFF_HEAD_DIM
    ncb = width // LANES

    if caches is None:
        pbf, ka, va, kb, vb, vt, kaug = _qkv_proj(x2d, w["norm_attn"], w["attn_w_in"], tm, seq=t)
    else:
        pbf, ka, va, kb, vb = _qkv_proj(x2d, w["norm_attn"], w["attn_w_in"], tm)
    pbf3 = pbf.reshape(b, t, 6 * width)
    if caches is None:
        oa = _diff_attn_prompt(pbf3, kaug.reshape(b, t, 2 * width), vt, 0, w["diff_lambda"],
                               w["diff_subln"], tile=min(1024, t), lambda_init=lambda_init)
        tqb = min(256, t)
        n_parts = BAND_PAST_MAX // tqb + 1
        bias = _band_bias(w["band_rel_bias"], 0, -BAND_PAST_MAX, tqb, n_parts * tqb, key_major=True)
        ob = _band_attn_prompt(pbf3, vt, 3, 4, 1, bias, tq=tqb, n_parts=n_parts, tkp=tqb)
        n_band = min(BAND_PAST_MAX, t)
        tail = lambda a: a.reshape(b, t, width)[:, t - n_band:].reshape(b, n_band, BAND_HEADS, BAND_HEAD_DIM)
        new_bk, new_bv = tail(kb), tail(vb)
    else:
        ck_a, cv_a, ck_b, cv_b = caches
        p_len, pb_len = ck_a.shape[1], ck_b.shape[1]
        cat = lambda c, lo: jnp.concatenate(
            [c.reshape(b, c.shape[1], width).astype(BF16), pbf3[:, :, lo:lo + width]], axis=1)
        k_a, v_a = cat(ck_a, width), cat(cv_a, 2 * width)
        k_b, v_b = cat(ck_b, 4 * width), cat(cv_b, 5 * width)
        oa = _diff_attn(pbf3, k_a, v_a, 0, 0, 0, w["diff_lambda"], w["diff_subln"],
                        tq=t, tk=p_len + t, q_off=p_len, lambda_init=lambda_init)
        bias = _band_bias(w["band_rel_bias"], p_len, p_len - pb_len, t, pb_len + t)
        ob = _band_attn(pbf3, k_b, v_b, 3 * ncb, 0, 0, bias, tq=t, n_parts=1, tkp=pb_len + t)
        new_bk = kb.reshape(b, t, BAND_HEADS, BAND_HEAD_DIM)
        new_bv = vb.reshape(b, t, BAND_HEADS, BAND_HEAD_DIM)
    new_dk = ka.reshape(b, t, 2 * DIFF_HEADS, DIFF_HEAD_DIM)
    new_dv = va.reshape(b, t, DIFF_HEADS, 2 * DIFF_HEAD_DIM)

    x2d = _attn_out_ffn(oa.reshape(m, width), ob.reshape(m, width), x2d, w["attn_w_out"],
                        w["norm_ffn"], w["ffn_w_gu"], w["ffn_w_down"], tm)

    seg = min(t, GMLP_CHUNK)
    x2d, v_rows = _gmlp(x2d, w["norm_gmlp"], w["gmlp_w_in"], w["gmlp_b_in"], w["gmlp_ln_g"],
                        w["gmlp_ln_b"], w["gmlp_w_s"], w["gmlp_b_s"], w["gmlp_w_out"], tm, seg,
                        return_v=caches is not None)
    y = _moe(x2d, w["norm_moe"], w["moe_w_router"], w["moe_b_router"], w["moe_w_gu"], w["moe_w_down"],
             w["norm_final"], tm)
    new_gv = None if v_rows is None else v_rows.reshape(b, t, -1)[None]
    return (y.reshape(b, t, d), new_dk[None], new_dv[None], new_bk[None], new_bv[None], new_gv)


def kernel(x_prompt, x_sample, cache_diff_k, cache_diff_v, cache_band_k, cache_band_v,
           norm_attn, attn_w_in, diff_lambda, diff_subln, band_rel_bias, attn_w_out,
           norm_ffn, ffn_w_gu, ffn_w_down,
           norm_gmlp, gmlp_w_in, gmlp_b_in, gmlp_ln_g, gmlp_ln_b, gmlp_w_s, gmlp_b_s, gmlp_w_out,
           norm_moe, moe_w_router, moe_b_router, moe_w_gu, moe_w_down, norm_final):
    w = {
        "norm_attn": norm_attn[0], "attn_w_in": attn_w_in[0].astype(BF16),
        "diff_lambda": diff_lambda[0], "diff_subln": diff_subln[0],
        "band_rel_bias": band_rel_bias[0], "attn_w_out": attn_w_out[0].astype(BF16),
        "norm_ffn": norm_ffn[0], "ffn_w_gu": ffn_w_gu[0].astype(BF16),
        "ffn_w_down": ffn_w_down[0].astype(BF16),
        "norm_gmlp": norm_gmlp[0], "gmlp_w_in": gmlp_w_in[0].astype(BF16),
        "gmlp_b_in": gmlp_b_in[0], "gmlp_ln_g": gmlp_ln_g[0], "gmlp_ln_b": gmlp_ln_b[0],
        "gmlp_w_s": gmlp_w_s[0], "gmlp_b_s": gmlp_b_s[0], "gmlp_w_out": gmlp_w_out[0].astype(BF16),
        "norm_moe": norm_moe[0], "moe_w_router": moe_w_router[0], "moe_b_router": moe_b_router[0],
        "moe_w_gu": moe_w_gu[0].astype(BF16), "moe_w_down": moe_w_down[0].astype(BF16),
        "norm_final": norm_final,
    }
    y_p, dk_p, dv_p, bk_p, bv_p, _ = _forward(x_prompt, None, w)
    y_s, dk_s, dv_s, bk_s, bv_s, gv_s = _forward(
        x_sample, (cache_diff_k[0], cache_diff_v[0], cache_band_k[0], cache_band_v[0]), w)
    return (y_p, y_s, dk_p, dv_p, bk_p, bv_p, dk_s, dv_s, bk_s, bv_s, gv_s)
```

```python
import functools
import math

import numpy as np
import jax
import jax.numpy as jnp
from jax import lax
from jax.experimental import pallas as pl
from jax.experimental.pallas import tpu as pltpu

CHUNK = 64
DIFF_HEADS = 4
DIFF_HEAD_DIM = 64
DIFF_SUBLN_EPS = 1e-5
BAND_HEADS = 8
BAND_HEAD_DIM = 64
BAND_PREV_CHUNKS = 8
BAND_PAST_MAX = BAND_PREV_CHUNKS * CHUNK
REL_CLIP = 256
GMLP_GROUPS = 8
GMLP_CHUNK = 128
N_EXPERTS = 8
RMS_EPS = 1e-6
LN_EPS = 1e-5
MASK_VALUE = -1e30

LANES = 128
ONES_ROWS = 16
VMEM_LIMIT_BYTES = 56 << 20

F32 = jnp.float32
BF16 = jnp.bfloat16
LOG2E = math.log2(math.e)
BF16_EXP2_ZERO = 140.0
Q_SCALE = DIFF_HEAD_DIM ** -0.5 * LOG2E
NT_DIMS = (((1,), (1,)), ((), ()))


def _params(*sem, flags=None):
    return pltpu.CompilerParams(dimension_semantics=sem, vmem_limit_bytes=VMEM_LIMIT_BYTES, flags=flags)


def _const_spec(shape):
    nd = len(shape)
    return pl.BlockSpec(shape, lambda *_: (0,) * nd, pipeline_mode=pl.Buffered(1))


def _rmsnorm(x, g, eps):
    return (x * lax.rsqrt(jnp.mean(x * x, axis=-1, keepdims=True) + eps)) * g


def _dot(a, b):
    return jnp.dot(a, b, preferred_element_type=F32)


def _swiglu(h, wgu_ref, wd_ref, d_ff, tf):
    y = None
    for c in range(d_ff // tf):
        g = _dot(h, wgu_ref[:, c * tf:(c + 1) * tf])
        u = _dot(h, wgu_ref[:, d_ff + c * tf:d_ff + (c + 1) * tf])
        a = (g * jax.nn.sigmoid(g) * u).astype(BF16)
        part = _dot(a, wd_ref[c * tf:(c + 1) * tf, :])
        y = part if y is None else y + part
    return y


def _qkv_kernel(x_ref, g_ref, w_ref, *rest, width, with_vt, tiles_per_seq):
    if with_vt:
        pbf_ref, ka_ref, va_ref, kb_ref, vb_ref, vt_ref, kaug_ref = rest
    else:
        pbf_ref, ka_ref, va_ref, kb_ref, vb_ref = rest
    tm = x_ref.shape[0]
    h = _rmsnorm(x_ref[...], g_ref[...], RMS_EPS).astype(BF16)
    f32_outs = {1: ka_ref, 2: va_ref, 4: kb_ref, 5: vb_ref}
    for c in range(6):
        r = _dot(h, w_ref[:, c * width:(c + 1) * width])
        if c in f32_outs:
            f32_outs[c][...] = r
        else:
            r = r * Q_SCALE
        pbf_ref[:, c * width:(c + 1) * width] = r.astype(BF16)
        if with_vt and c == 1:
            pos = (pl.program_id(0) % tiles_per_seq) * tm + lax.broadcasted_iota(jnp.int32, (tm, LANES), 0)
            lane = lax.broadcasted_iota(jnp.int32, (tm, LANES), 1)
            hi = (pos // LANES).astype(F32)
            lo = (pos % LANES).astype(F32)
            feats = jnp.where(lane < 3, hi, jnp.where(lane < 6, lo, 0.0)).astype(BF16)
            for hd in range(DIFF_HEADS):
                kaug_ref[:, 2 * hd * LANES:(2 * hd + 1) * LANES] = r[:, hd * LANES:(hd + 1) * LANES].astype(BF16)
                kaug_ref[:, (2 * hd + 1) * LANES:(2 * hd + 2) * LANES] = feats
        if with_vt and c in (2, 5):
            r0 = 0 if c == 2 else width
            vt_ref[r0:r0 + width, :] = r.T.astype(BF16)


def _qkv_proj(x2d, gain, w_bf, tm, seq=None):
    m, d = x2d.shape
    width = w_bf.shape[1] // 6
    row = lambda i: (i, 0)
    with_vt = seq is not None
    in_specs = [pl.BlockSpec((tm, d), row), _const_spec((1, d)), _const_spec(w_bf.shape)]
    out_specs = [pl.BlockSpec((tm, 6 * width), row)] + [pl.BlockSpec((tm, width), row)] * 4
    out_shape = [jax.ShapeDtypeStruct((m, 6 * width), BF16)] + [jax.ShapeDtypeStruct((m, width), F32)] * 4
    args = [x2d, gain.reshape(1, d), w_bf]
    nt = 1
    if with_vt:
        nt = seq // tm
        assert seq <= 256 * LANES
        vt_rows = 2 * width
        out_specs.append(pl.BlockSpec((None, vt_rows, tm), lambda i: (i // nt, 0, i % nt)))
        out_shape.append(jax.ShapeDtypeStruct((m // seq, vt_rows, seq), BF16))
        out_specs.append(pl.BlockSpec((tm, 2 * width), row))
        out_shape.append(jax.ShapeDtypeStruct((m, 2 * width), BF16))
    return pl.pallas_call(
        functools.partial(_qkv_kernel, width=width, with_vt=with_vt, tiles_per_seq=nt),
        grid=(m // tm,),
        in_specs=in_specs, out_specs=out_specs, out_shape=out_shape,
        compiler_params=_params("parallel"),
        name="qkv_proj",
    )(*args)


def _split_heads_rows(q, tq):
    qf = q.astype(F32)
    lane = lax.broadcasted_iota(jnp.int32, qf.shape, 1)
    lo = jnp.where(lane < DIFF_HEAD_DIM, qf, 0.0)
    hi = jnp.where(lane >= DIFF_HEAD_DIM, qf, 0.0)
    return jnp.concatenate([lo, hi], axis=0).astype(BF16)


def _diff_attn_kernel(ii_ref, jj_ref, last_ref, q_ref, k_ref, v_ref, slope_ref, lam_ref, g_ref,
                      o_ref, qz_ref, m_ref, l_ref, acc_ref, *, tq, tk, q_off, lambda_init):
    s = pl.program_id(2)
    i = ii_ref[s]
    j = jj_ref[s]

    @pl.when(j == 0)
    def _init():
        qz_ref[...] = _split_heads_rows(q_ref[...], tq)
        m_ref[...] = jnp.full(m_ref.shape, MASK_VALUE, F32)
        l_ref[...] = jnp.zeros(l_ref.shape, F32)
        acc_ref[...] = jnp.zeros(acc_ref.shape, F32)

    sc = lax.dot_general(qz_ref[...], k_ref[...], NT_DIMS, preferred_element_type=F32)
    row = lax.broadcasted_iota(jnp.int32, (2 * tq, 1), 0)
    row = jnp.where(row >= tq, row - tq, row)
    qpos = q_off + i * tq + row
    kpos = j * tk + lax.broadcasted_iota(jnp.int32, (1, tk), 1)
    dist = jnp.abs(qpos - kpos).astype(F32)
    sc = sc - slope_ref[:, 0:1] * dist
    visible = (kpos // CHUNK) <= (qpos // CHUNK)
    sc = jnp.where(visible, sc, MASK_VALUE)

    m_prev = m_ref[...]
    m_new = jnp.maximum(m_prev, jnp.max(sc, axis=1, keepdims=True))
    alpha = jnp.exp2(m_prev - m_new)
    p = jnp.exp2(sc - m_new)
    l_ref[...] = alpha * l_ref[...] + jnp.sum(p, axis=1, keepdims=True)
    acc_ref[...] = alpha * acc_ref[...] + _dot(p.astype(BF16), v_ref[...])
    m_ref[...] = m_new

    @pl.when(last_ref[s] == 1)
    def _finish():
        lp = lam_ref[...]
        lam = (jnp.exp(jnp.sum(lp[0:1] * lp[1:2], axis=1, keepdims=True))
               - jnp.exp(jnp.sum(lp[2:3] * lp[3:4], axis=1, keepdims=True)) + lambda_init)
        o_all = acc_ref[...] / l_ref[...]
        o = o_all[:tq] - lam * o_all[tq:]
        o = _rmsnorm(o, g_ref[...], DIFF_SUBLN_EPS) * (1.0 - lambda_init)
        o_ref[...] = o.astype(o_ref.dtype)


def _diff_attn(q_arr, k_arr, v_arr, q_cb, k_cb, v_cb, lam_p, subln_g, *, tq, tk, q_off, lambda_init):
    b, t_q, _ = q_arr.shape
    t_k = k_arr.shape[1]
    nq, nk = t_q // tq, t_k // tk
    pairs = [(i, j) for i in range(nq) for j in range(nk)
             if (j * tk) // CHUNK <= (q_off + i * tq + tq - 1) // CHUNK]
    ii = np.array([p[0] for p in pairs], np.int32)
    jj = np.array([p[1] for p in pairs], np.int32)
    last = np.array([1 if (n + 1 == len(pairs) or pairs[n + 1][0] != pairs[n][0]) else 0
                     for n in range(len(pairs))], np.int32)
    slopes = jnp.asarray(_alibi_slopes_log2()[:, None, None] * np.ones((1, 1, LANES), np.float32))

    grid_spec = pltpu.PrefetchScalarGridSpec(
        num_scalar_prefetch=3,
        grid=(b, DIFF_HEADS, len(pairs)),
        in_specs=[
            pl.BlockSpec((None, tq, LANES), lambda b_, h, s, ii, jj, la: (b_, ii[s], q_cb + h)),
            pl.BlockSpec((None, tk, LANES), lambda b_, h, s, ii, jj, la: (b_, jj[s], k_cb + h)),
            pl.BlockSpec((None, tk, LANES), lambda b_, h, s, ii, jj, la: (b_, jj[s], v_cb + h)),
            pl.BlockSpec((None, 1, LANES), lambda b_, h, s, ii, jj, la: (h, 0, 0)),
            pl.BlockSpec((4, DIFF_HEAD_DIM), lambda b_, h, s, ii, jj, la: (0, 0)),
            pl.BlockSpec((1, LANES), lambda b_, h, s, ii, jj, la: (0, 0)),
        ],
        out_specs=pl.BlockSpec((None, tq, LANES), lambda b_, h, s, ii, jj, la: (b_, ii[s], h)),
        scratch_shapes=[
            pltpu.VMEM((2 * tq, LANES), BF16),
            pltpu.VMEM((2 * tq, 1), F32),
            pltpu.VMEM((2 * tq, 1), F32),
            pltpu.VMEM((2 * tq, LANES), F32),
        ],
    )
    return pl.pallas_call(
        functools.partial(_diff_attn_kernel, tq=tq, tk=tk, q_off=q_off, lambda_init=lambda_init),
        grid_spec=grid_spec,
        out_shape=jax.ShapeDtypeStruct((b, t_q, DIFF_HEADS * LANES), BF16),
        compiler_params=_params("parallel", "parallel", "arbitrary"),
        name="diff_attn",
    )(jnp.asarray(ii), jnp.asarray(jj), jnp.asarray(last), q_arr, k_arr, v_arr, slopes,
      lam_p, subln_g.reshape(1, LANES))


def _alibi_slopes_log2():
    return (2.0 ** (-8.0 * np.arange(1, DIFF_HEADS + 1, dtype=np.float64) / DIFF_HEADS) * LOG2E).astype(np.float32)


def _diff_prompt_kernel(ii_ref, jj_ref, last_ref, lin_ref, q_ref, k_ref, vt_ref, dbias_ref, qfeat_ref,
                        lam_ref, g_ref, o_ref, qzt_ref, s0_ref, s1_ref, mx0_ref, mx1_ref, m_ref, acc_ref,
                        *, tq, tk, cb, lambda_init):
    s = pl.program_id(2)
    n_pairs = pl.num_programs(2) - 1
    sp = jnp.minimum(s, n_pairs - 1)
    sc = jnp.maximum(s - 1, 0)
    consuming = s > 0
    nblk = 2 * tq // cb
    blk = lambda c: slice(c * cb, (c + 1) * cb)

    @pl.when(s == 0)
    def _first_step():
        s1_ref[...] = jnp.zeros(s1_ref.shape, F32)
        mx1_ref[...] = jnp.zeros(mx1_ref.shape, F32)

    @pl.when(jj_ref[sp] == ii_ref[sp])
    def _new_queries():
        qf = q_ref[...].astype(F32)
        lane = lax.broadcasted_iota(jnp.int32, qf.shape, 1)
        qzt_ref[0:LANES, 0:tq] = jnp.where(lane < DIFF_HEAD_DIM, qf, 0.0).T.astype(BF16)
        qzt_ref[0:LANES, tq:2 * tq] = jnp.where(lane >= DIFF_HEAD_DIM, qf, 0.0).T.astype(BF16)
        qzt_ref[LANES:2 * LANES, :] = jnp.broadcast_to(qfeat_ref[...], (LANES, 2 * tq)).astype(BF16)

    @pl.when(jj_ref[sc] == ii_ref[sc])
    def _reset():
        m_ref[...] = jnp.full(m_ref.shape, MASK_VALUE, F32)
        acc_ref[...] = jnp.zeros(acc_ref.shape, F32)

    def stages(s_prod, mx_prod, s_cons, mx_cons):
        @pl.when(jnp.logical_and(consuming, lin_ref[sc] == 0))
        def _diagonal():
            for c in range(nblk):
                v = s_cons[:, blk(c)] + dbias_ref[:, (c * cb) % tq:(c * cb) % tq + cb]
                s_cons[:, blk(c)] = v
                mx_cons[:, blk(c)] = jnp.max(v, axis=0, keepdims=True)

        def produce():
            k = k_ref[...]
            for c in range(nblk):
                v = _dot(k, qzt_ref[:, blk(c)])
                s_prod[:, blk(c)] = v
                mx_prod[:, blk(c)] = jnp.max(v, axis=0, keepdims=True)

        def consume():
            vt = jnp.concatenate([vt_ref[...], jnp.ones((ONES_ROWS, tk), BF16)], axis=0)
            m_prev = [m_ref[:, blk(c)] for c in range(nblk)]
            acc_prev = [acc_ref[:, blk(c)] for c in range(nblk)]
            m_out, acc_out = [], []
            for c in range(nblk):
                m_new = jnp.maximum(m_prev[c], mx_cons[:, blk(c)])
                alpha = jnp.exp2(m_prev[c] - m_new)
                p = jnp.exp2((s_cons[:, blk(c)] - m_new).astype(BF16))
                acc_out.append(alpha * acc_prev[c] + _dot(vt, p))
                m_out.append(m_new)
            for c in range(nblk):
                m_ref[:, blk(c)] = m_out[c]
                acc_ref[:, blk(c)] = acc_out[c]

        negligible = jnp.max(mx_cons[...] - m_ref[...]) < -BF16_EXP2_ZERO

        @pl.when(negligible)
        def _produce_only():
            produce()

        @pl.when(jnp.logical_not(negligible))
        def _both():
            produce()
            consume()

    @pl.when(s % 2 == 0)
    def _even():
        stages(s0_ref, mx0_ref, s1_ref, mx1_ref)

    @pl.when(s % 2 == 1)
    def _odd():
        stages(s1_ref, mx1_ref, s0_ref, mx0_ref)

    @pl.when(jnp.logical_and(consuming, last_ref[sc] == 1))
    def _finish():
        lp = lam_ref[...]
        lam = (jnp.exp(jnp.sum(lp[0:1] * lp[1:2], axis=1, keepdims=True))
               - jnp.exp(jnp.sum(lp[2:3] * lp[3:4], axis=1, keepdims=True)) + lambda_init)
        o_all = acc_ref[0:LANES, :] / acc_ref[LANES:LANES + 1, :]
        o = o_all[:, 0:tq] - lam * o_all[:, tq:2 * tq]
        o = o * lax.rsqrt(jnp.mean(o * o, axis=0, keepdims=True) + DIFF_SUBLN_EPS)
        o = o * g_ref[...] * (1.0 - lambda_init)
        o_ref[...] = o.T.astype(o_ref.dtype)


def _bf16_split3(x):
    import ml_dtypes
    rnd = lambda v: v.astype(ml_dtypes.bfloat16).astype(np.float32)
    x = np.asarray(x, np.float32)
    hi = rnd(x)
    mid = rnd(x - hi)
    lo = rnd(x - hi - mid)
    return hi, mid, lo


def _diff_attn_prompt(pbf3, kaug3, vt_arr, q_cb, lam_p, subln_g, *, tile, lambda_init):
    b, t, _ = pbf3.shape
    tq = tk = tile
    n = t // tile
    pairs = [(i, j) for i in range(n) for j in range(i, -1, -1)]
    ii = np.array([p[0] for p in pairs], np.int32)
    jj = np.array([p[1] for p in pairs], np.int32)
    last = (jj == 0).astype(np.int32)
    lin = (ii != jj).astype(np.int32)
    c2 = _alibi_slopes_log2()
    qfeat = np.zeros((DIFF_HEADS, LANES, 1), np.float32)
    for r, part in enumerate(_bf16_split3(c2)):
        qfeat[:, r, 0] = part * LANES
        qfeat[:, 3 + r, 0] = part
    pos = np.arange(tile)
    d = (pos[None, :] - pos[:, None]).astype(np.float32)
    visible = (pos[:, None] // CHUNK) <= (pos[None, :] // CHUNK)
    dbias = jnp.asarray(np.where(visible[None], 2.0 * c2[:, None, None] * np.minimum(d, 0.0)[None],
                                 MASK_VALUE).astype(np.float32))
    cb = min(256, 2 * tq)
    n_pairs = len(pairs)
    prod = lambda s: jnp.minimum(s, n_pairs - 1)
    cons = lambda s: jnp.maximum(s - 1, 0)
    idx = lambda f: (lambda b_, h, s, ii, jj, la, li: f(b_, h, s, ii, jj))
    grid_spec = pltpu.PrefetchScalarGridSpec(
        num_scalar_prefetch=4,
        grid=(b, DIFF_HEADS, n_pairs + 1),
        in_specs=[
            pl.BlockSpec((None, tq, LANES), idx(lambda b_, h, s, ii, jj: (b_, ii[prod(s)], q_cb + h))),
            pl.BlockSpec((None, tk, 2 * LANES), idx(lambda b_, h, s, ii, jj: (b_, jj[prod(s)], h))),
            pl.BlockSpec((None, LANES, tk), idx(lambda b_, h, s, ii, jj: (b_, h, jj[cons(s)]))),
            pl.BlockSpec((None, tk, tq), idx(lambda b_, h, s, ii, jj: (h, 0, 0))),
            pl.BlockSpec((None, LANES, 1), idx(lambda b_, h, s, ii, jj: (h, 0, 0))),
            pl.BlockSpec((4, DIFF_HEAD_DIM), idx(lambda b_, h, s, ii, jj: (0, 0))),
            pl.BlockSpec((LANES, 1), idx(lambda b_, h, s, ii, jj: (0, 0))),
        ],
        out_specs=pl.BlockSpec((None, tq, LANES), idx(lambda b_, h, s, ii, jj: (b_, ii[cons(s)], h))),
        scratch_shapes=[
            pltpu.VMEM((2 * LANES, 2 * tq), BF16),
            pltpu.VMEM((tk, 2 * tq), F32), pltpu.VMEM((tk, 2 * tq), F32),
            pltpu.VMEM((1, 2 * tq), F32), pltpu.VMEM((1, 2 * tq), F32),
            pltpu.VMEM((1, 2 * tq), F32),
            pltpu.VMEM((LANES + ONES_ROWS, 2 * tq), F32),
        ],
    )
    return pl.pallas_call(
        functools.partial(_diff_prompt_kernel, tq=tq, tk=tk, cb=cb, lambda_init=lambda_init),
        grid_spec=grid_spec,
        out_shape=jax.ShapeDtypeStruct((b, t, DIFF_HEADS * LANES), BF16),
        compiler_params=_params("parallel", "parallel", "arbitrary"),
        name="diff_attn_prompt",
    )(jnp.asarray(ii), jnp.asarray(jj), jnp.asarray(last), jnp.asarray(lin), pbf3, kaug3, vt_arr, dbias,
      jnp.asarray(qfeat), lam_p, subln_g.reshape(LANES, 1))


def _band_attn_kernel(*refs, tq, n_parts, tkp):
    q_ref = refs[0]
    k_refs = refs[1:1 + n_parts]
    v_refs = refs[1 + n_parts:1 + 2 * n_parts]
    bias_ref = refs[1 + 2 * n_parts]
    o_ref = refs[2 + 2 * n_parts]
    i = pl.program_id(2)
    qz = _split_heads_rows(q_ref[...], tq)
    scores = []
    for m in range(n_parts):
        sc = lax.dot_general(qz, k_refs[m][...], NT_DIMS, preferred_element_type=F32)
        sc = sc + bias_ref[:, m * tkp:(m + 1) * tkp]
        if n_parts > 1:
            sc = jnp.where(i - (n_parts - 1) + m >= 0, sc, MASK_VALUE)
        scores.append(sc)
    mx = functools.reduce(jnp.maximum, [jnp.max(sc, axis=1, keepdims=True) for sc in scores])
    den = None
    num = None
    for m in range(n_parts):
        p = jnp.exp2(scores[m] - mx)
        d = jnp.sum(p, axis=1, keepdims=True)
        r = _dot(p.astype(BF16), v_refs[m][...])
        den = d if den is None else den + d
        num = r if num is None else num + r
    r = num / den
    lane = lax.broadcasted_iota(jnp.int32, (tq, LANES), 1)
    o_ref[...] = jnp.where(lane < BAND_HEAD_DIM, r[:tq], r[tq:]).astype(o_ref.dtype)


def _band_bias_kernel(line_ref, o_ref, *, q0, k0):
    tq, nk = o_ref.shape
    line = jnp.broadcast_to(line_ref[...], (tq, line_ref.shape[1]))
    rolled = pltpu.roll(line, 0, 1, stride=1, stride_axis=0)
    qc = (q0 + lax.broadcasted_iota(jnp.int32, (tq, nk), 0)) >> 6
    kc = (k0 + lax.broadcasted_iota(jnp.int32, (tq, nk), 1)) >> 6
    visible = (kc <= qc) & (qc - kc <= BAND_PREV_CHUNKS)
    o_ref[...] = jnp.where(visible, rolled[:, 0:nk], MASK_VALUE)


def _band_bias_t_kernel(line_ref, o_ref, *, q0, k0):
    nk, tq = o_ref.shape
    line = jnp.broadcast_to(line_ref[...], (nk, line_ref.shape[1]))
    rolled = pltpu.roll(line, 0, 1, stride=1, stride_axis=0)
    kc = (k0 + lax.broadcasted_iota(jnp.int32, (nk, tq), 0)) >> 6
    qc = (q0 + lax.broadcasted_iota(jnp.int32, (nk, tq), 1)) >> 6
    visible = (kc <= qc) & (qc - kc <= BAND_PREV_CHUNKS)
    o_ref[...] = jnp.where(visible, rolled[:, 0:tq], MASK_VALUE)


def _band_bias(rel_table, q0, k0, tq, nk, key_major=False):
    assert CHUNK == 64
    width = pl.next_power_of_2(tq + nk - 1)
    y = np.arange(width)
    c_minus_r = np.where(y < nk, y, y - width)
    rel = np.clip((q0 - k0) - c_minus_r, -(CHUNK - 1), REL_CLIP) + (CHUNK - 1)
    if key_major:
        rel = rel[(-y) % width]
    line = (rel_table.astype(F32) * LOG2E)[:, rel].reshape(BAND_HEADS, 1, width)
    if key_major:
        kern, block, shape = _band_bias_t_kernel, (None, nk, tq), (BAND_HEADS // 2, nk, 2 * tq)
        index = lambda h: (h // 2, 0, h % 2)
    else:
        kern, block, shape = _band_bias_kernel, (None, tq, nk), (BAND_HEADS // 2, 2 * tq, nk)
        index = lambda h: (h // 2, h % 2, 0)
    return pl.pallas_call(
        functools.partial(kern, q0=q0, k0=k0),
        grid=(BAND_HEADS,),
        in_specs=[pl.BlockSpec((None, 1, width), lambda h: (h, 0, 0))],
        out_specs=pl.BlockSpec(block, index),
        out_shape=jax.ShapeDtypeStruct(shape, F32),
        compiler_params=_params("parallel"),
        name="band_bias",
    )(line)


def _band_prompt_kernel(*refs, tq, n_parts, tkp):
    q_ref = refs[0]
    k_refs = refs[1:1 + n_parts]
    vt_refs = refs[1 + n_parts:1 + 2 * n_parts]
    bias_ref = refs[1 + 2 * n_parts]
    o_ref = refs[2 + 2 * n_parts]
    nk = n_parts * tkp
    hd = BAND_HEAD_DIM
    i = pl.program_id(1)
    ones = jnp.ones((ONES_ROWS, nk), BF16)
    for pr in range(BAND_HEADS // 2):
        ln = slice(pr * LANES, (pr + 1) * LANES)
        qf = q_ref[:, ln].astype(F32)
        lane = lax.broadcasted_iota(jnp.int32, qf.shape, 1)
        qzt = jnp.concatenate([jnp.where(lane < hd, qf, 0.0).T, jnp.where(lane >= hd, qf, 0.0).T],
                              axis=1).astype(BF16)
        k = jnp.concatenate([r[:, ln] for r in k_refs], axis=0)
        sc = _dot(k, qzt) + bias_ref[pr]
        row = lax.broadcasted_iota(jnp.int32, sc.shape, 0)
        sc = jnp.where(row >= (n_parts - 1 - i) * tkp, sc, MASK_VALUE)
        mx = jnp.max(sc, axis=0, keepdims=True)
        p = jnp.exp2((sc - mx).astype(BF16))
        vt = jnp.concatenate([r[ln, :] for r in vt_refs] , axis=1)
        r = _dot(jnp.concatenate([vt, ones], axis=0), p)
        o = r[0:LANES, :] / r[LANES:LANES + 1, :]
        o = jnp.concatenate([o[0:hd, 0:tq], o[hd:2 * hd, tq:2 * tq]], axis=0)
        o_ref[:, ln] = o.T.astype(o_ref.dtype)


def _band_attn_prompt(pbf3, vt_arr, q_cb, k_cb, vt_rb, bias_t, *, tq, n_parts, tkp):
    b, t, _ = pbf3.shape
    width = (BAND_HEADS // 2) * LANES
    part = lambda i, m: jnp.maximum(i - (n_parts - 1) + m, 0)
    k_spec = lambda m: pl.BlockSpec((None, tkp, width), lambda b_, i: (b_, part(i, m), k_cb))
    vt_spec = lambda m: pl.BlockSpec((None, width, tkp), lambda b_, i: (b_, vt_rb, part(i, m)))
    return pl.pallas_call(
        functools.partial(_band_prompt_kernel, tq=tq, n_parts=n_parts, tkp=tkp),
        grid=(b, t // tq),
        in_specs=[pl.BlockSpec((None, tq, width), lambda b_, i: (b_, i, q_cb))]
        + [k_spec(m) for m in range(n_parts)] + [vt_spec(m) for m in range(n_parts)]
        + [_const_spec(bias_t.shape)],
        out_specs=pl.BlockSpec((None, tq, width), lambda b_, i: (b_, i, 0)),
        out_shape=jax.ShapeDtypeStruct((b, t, width), BF16),
        compiler_params=_params("parallel", "parallel"),
        name="band_attn_prompt",
    )(pbf3, *([pbf3] * n_parts), *([vt_arr] * n_parts), bias_t)


def _band_attn(q_arr, k_arr, v_arr, q_cb, k_cb, v_cb, bias, *, tq, n_parts, tkp):
    b, t_q, _ = q_arr.shape
    n_pairs = BAND_HEADS // 2

    def kv_spec(cb, m):
        return pl.BlockSpec((None, tkp, LANES),
                            lambda p, b_, i: (b_, jnp.maximum(i - (n_parts - 1) + m, 0), cb + p))

    return pl.pallas_call(
        functools.partial(_band_attn_kernel, tq=tq, n_parts=n_parts, tkp=tkp),
        grid=(n_pairs, b, t_q // tq),
        in_specs=[pl.BlockSpec((None, tq, LANES), lambda p, b_, i: (b_, i, q_cb + p))]
        + [kv_spec(k_cb, m) for m in range(n_parts)]
        + [kv_spec(v_cb, m) for m in range(n_parts)]
        + [pl.BlockSpec((None, 2 * tq, n_parts * tkp), lambda p, b_, i: (p, 0, 0))],
        out_specs=pl.BlockSpec((None, tq, LANES), lambda p, b_, i: (b_, i, p)),
        out_shape=jax.ShapeDtypeStruct((b, t_q, n_pairs * LANES), BF16),
        compiler_params=_params("parallel", "parallel", "parallel"),
        name="band_attn",
    )(q_arr, *([k_arr] * n_parts), *([v_arr] * n_parts), bias)


def _attn_out_ffn_kernel(oa_ref, ob_ref, x_ref, wo_ref, g_ref, wgu_ref, wd_ref, o_ref, *, d_ff, tf):
    half = oa_ref.shape[1]
    x1 = x_ref[...] + _dot(oa_ref[...], wo_ref[0:half, :]) + _dot(ob_ref[...], wo_ref[half:2 * half, :])
    h = _rmsnorm(x1, g_ref[...], RMS_EPS).astype(BF16)
    o_ref[...] = x1 + _swiglu(h, wgu_ref, wd_ref, d_ff, tf)


def _attn_out_ffn(oa, ob, x2d, wo_bf, gain, wgu_bf, wd_bf, tm):
    m, d = x2d.shape
    d_ff = wd_bf.shape[0]
    row = lambda i: (i, 0)
    return pl.pallas_call(
        functools.partial(_attn_out_ffn_kernel, d_ff=d_ff, tf=256),
        grid=(m // tm,),
        in_specs=[pl.BlockSpec((tm, oa.shape[1]), row), pl.BlockSpec((tm, ob.shape[1]), row),
                  pl.BlockSpec((tm, d), row), _const_spec(wo_bf.shape), _const_spec((1, d)),
                  _const_spec(wgu_bf.shape), _const_spec(wd_bf.shape)],
        out_specs=pl.BlockSpec((tm, d), row),
        out_shape=jax.ShapeDtypeStruct((m, d), F32),
        compiler_params=_params("parallel"),
        name="attn_out_ffn",
    )(oa, ob, x2d, wo_bf, gain.reshape(1, d), wgu_bf, wd_bf)


def _gelu(z):
    return 0.5 * z * (1.0 + lax.erf(z * (2.0 ** -0.5)))


def _gmlp_kernel(x_ref, g_ref, win_ref, bin_ref, lng_ref, lnb_ref, ws_ref, bs_ref, wout_ref,
                 o_ref, v_ref, u_s, act_s, *, seg, cw):
    tm, gd = v_ref.shape
    gw = gd // GMLP_GROUPS
    x = x_ref[...]
    h = _rmsnorm(x, g_ref[...], RMS_EPS).astype(BF16)
    for c in range(gd // cw):
        lo, hi = c * cw, (c + 1) * cw
        u_s[:, lo:hi] = _gelu(_dot(h, win_ref[:, lo:hi]) + bin_ref[:, lo:hi])
        v_ref[:, lo:hi] = _gelu(_dot(h, win_ref[:, gd + lo:gd + hi]) + bin_ref[:, gd + lo:gd + hi])
    v = v_ref[...]
    mu = jnp.mean(v, axis=-1, keepdims=True)
    var = jnp.mean(jnp.square(v - mu), axis=-1, keepdims=True)
    v_ref[...] = (v - mu) * lax.rsqrt(var + LN_EPS) * lng_ref[...] + lnb_ref[...]
    r_i = lax.broadcasted_iota(jnp.int32, (seg, seg), 0)
    c_i = lax.broadcasted_iota(jnp.int32, (seg, seg), 1)
    for g in range(GMLP_GROUPS):
        w = jnp.where(r_i >= c_i, ws_ref[g], 0.0).astype(BF16)
        for n in range(tm // seg):
            rows = slice(n * seg, (n + 1) * seg)
            cols = slice(g * gw, (g + 1) * gw)
            sv = _dot(w, v_ref[rows, cols].astype(BF16)) + bs_ref[g]
            act_s[rows, cols] = (u_s[rows, cols] * sv).astype(BF16)
    o_ref[...] = x + _dot(act_s[...], wout_ref[...])


def _gmlp(x2d, gain, win_bf, b_in, ln_g, ln_b, w_s, b_s, wout_bf, tm, seg, return_v):
    m, d = x2d.shape
    gd = wout_bf.shape[0]
    row = lambda i: (i, 0)
    ws = w_s[:, :seg, :seg]
    bs = b_s[:, :seg, None]
    out_specs = [pl.BlockSpec((tm, d), row)]
    out_shape = [jax.ShapeDtypeStruct((m, d), F32)]
    scratch = [pltpu.VMEM((tm, gd), F32), pltpu.VMEM((tm, gd), BF16)]
    if return_v:
        out_specs.append(pl.BlockSpec((tm, gd), row))
        out_shape.append(jax.ShapeDtypeStruct((m, gd), F32))
    else:
        scratch.insert(0, pltpu.VMEM((tm, gd), F32))
    outs = pl.pallas_call(
        functools.partial(_gmlp_kernel, seg=seg, cw=512),
        grid=(m // tm,),
        in_specs=[pl.BlockSpec((tm, d), row), _const_spec((1, d)), _const_spec(win_bf.shape),
                  _const_spec((1, 2 * gd)), _const_spec((1, gd)), _const_spec((1, gd)),
                  _const_spec(ws.shape), _const_spec(bs.shape), _const_spec(wout_bf.shape)],
        out_specs=out_specs, out_shape=out_shape, scratch_shapes=scratch,
        compiler_params=_params("parallel"),
        name="gmlp",
    )(x2d, gain.reshape(1, d), win_bf, b_in.reshape(1, 2 * gd), ln_g.reshape(1, gd),
      ln_b.reshape(1, gd), ws, bs, wout_bf)
    return (outs[0], outs[1]) if return_v else (outs[0], None)


def _router_kernel(x_ref, g_ref, whi_ref, wlo_ref, b_ref, meta_ref, cnt_ref):
    h = _rmsnorm(x_ref[...], g_ref[...], RMS_EPS)
    h_hi = h.astype(BF16)
    h_lo = (h - h_hi.astype(F32)).astype(BF16)
    logits = (_dot(h_hi, whi_ref[...]) + _dot(h_hi, wlo_ref[...]) + _dot(h_lo, whi_ref[...])
              + b_ref[...])
    tm = logits.shape[0]
    lane = lax.broadcasted_iota(jnp.int32, logits.shape, 1)
    logits = jnp.where(lane < N_EXPERTS, logits, -jnp.inf)
    v1 = jnp.max(logits, axis=1, keepdims=True)
    i1 = jnp.min(jnp.where(logits == v1, lane, LANES), axis=1, keepdims=True)
    rest = jnp.where(lane == i1, -jnp.inf, logits)
    v2 = jnp.max(rest, axis=1, keepdims=True)
    i2 = jnp.min(jnp.where(rest == v2, lane, LANES), axis=1, keepdims=True)
    e2 = jnp.exp(v2 - v1)
    den = 1.0 + e2

    @pl.when(pl.program_id(0) == 0)
    def _init():
        cnt_ref[...] = jnp.zeros(cnt_ref.shape, F32)

    oh1 = lane == i1
    oh2 = lane == i2
    r_i = lax.broadcasted_iota(jnp.int32, (tm, tm), 0)
    c_i = lax.broadcasted_iota(jnp.int32, (tm, tm), 1)
    before = jnp.where(c_i < r_i, 1.0, 0.0).astype(BF16)
    cum1 = _dot(before, jnp.where(oh1, 1.0, 0.0).astype(BF16))
    cum2 = _dot(before, jnp.where(oh2, 1.0, 0.0).astype(BF16))
    n1 = jnp.sum(jnp.where(oh1, 1.0, 0.0), axis=0, keepdims=True)
    n2 = jnp.sum(jnp.where(oh2, 1.0, 0.0), axis=0, keepdims=True)
    base = cnt_ref[...]
    rank1 = jnp.sum(jnp.where(oh1, base + cum1, 0.0), axis=1, keepdims=True)
    rank2 = jnp.sum(jnp.where(oh2, base + n1 + cum2, 0.0), axis=1, keepdims=True)
    cnt_ref[...] = base + n1 + n2
    meta = jnp.where(lane == 0, i1.astype(F32), 0.0)
    meta = jnp.where(lane == 1, i2.astype(F32), meta)
    meta = jnp.where(lane == 2, 1.0 / den, meta)
    meta = jnp.where(lane == 3, e2 / den, meta)
    meta = jnp.where(lane == 4, rank1, meta)
    meta = jnp.where(lane == 5, rank2, meta)
    meta_ref[...] = meta


def _router(x2d, gain, w_router, b_router, tm):
    m, d = x2d.shape
    assert 2 * m < 2 ** 24
    w_pad = jnp.pad(w_router, ((0, 0), (0, LANES - N_EXPERTS)))
    w_hi = w_pad.astype(BF16)
    w_lo = (w_pad - w_hi.astype(F32)).astype(BF16)
    b_pad = jnp.pad(b_router, (0, LANES - N_EXPERTS)).reshape(1, LANES)
    row = lambda i: (i, 0)
    return pl.pallas_call(
        _router_kernel,
        grid=(m // tm,),
        in_specs=[pl.BlockSpec((tm, d), row), _const_spec((1, d)), _const_spec((d, LANES)),
                  _const_spec((d, LANES)), _const_spec((1, LANES))],
        out_specs=[pl.BlockSpec((tm, LANES), row), pl.BlockSpec((1, LANES), lambda i: (0, 0))],
        out_shape=[jax.ShapeDtypeStruct((m, LANES), F32), jax.ShapeDtypeStruct((1, LANES), F32)],
        compiler_params=_params("arbitrary"),
        name="moe_router",
    )(x2d, gain.reshape(1, d), w_hi, w_lo, b_pad)


def _row_copy(src_hbm, src_row, dst, dst_row, sem):
    return pltpu.make_async_copy(src_hbm.at[pl.ds(src_row, 1)], dst.at[pl.ds(dst_row, 1)], sem)


ROW_DMA_UNROLL = 8


def _dispatch_kernel(ends_ref, pos1_ref, pos2_ref, x_ref, g_ref, xs_hbm, h_ref, zero_ref, sem, zsem,
                     *, tm, tmx):
    @pl.when(pl.program_id(0) == 0)
    def _zero_padding():
        zero_ref[...] = jnp.zeros(zero_ref.shape, zero_ref.dtype)

        def fill(row0):
            cp = pltpu.make_async_copy(zero_ref, xs_hbm.at[pl.ds(row0, tmx)], zsem)
            cp.start()
            cp.wait()

        for e in range(N_EXPERTS):
            start_e = ends_ref[e - 1] if e else 0

            @pl.when(ends_ref[e] > start_e)
            def _():
                fill(pl.multiple_of(ends_ref[e] - tmx, tmx))

        def tail(t, carry):
            fill(pl.multiple_of(t * tmx, tmx))
            return carry

        lax.fori_loop(ends_ref[N_EXPERTS - 1] // tmx, xs_hbm.shape[0] // tmx, tail, 0)

    i = pl.program_id(0)
    slot = i % 2
    h_ref.at[slot][...] = _rmsnorm(x_ref[...], g_ref[...], RMS_EPS)

    def start(r, carry):
        _row_copy(h_ref.at[slot], r, xs_hbm, pos1_ref[r], sem.at[slot]).start()
        _row_copy(h_ref.at[slot], r, xs_hbm, pos2_ref[r], sem.at[slot]).start()
        return carry

    def wait_slot(sl):
        def wait(r, carry):
            _row_copy(h_ref.at[sl], 0, xs_hbm, 0, sem.at[sl]).wait()
            _row_copy(h_ref.at[sl], 0, xs_hbm, 0, sem.at[sl]).wait()
            return carry
        lax.fori_loop(0, tm, wait, 0, unroll=ROW_DMA_UNROLL)

    lax.fori_loop(0, tm, start, 0, unroll=ROW_DMA_UNROLL)

    @pl.when(i > 0)
    def _previous():
        wait_slot(1 - slot)

    @pl.when(i == pl.num_programs(0) - 1)
    def _own():
        wait_slot(slot)


def _dispatch(x2d, gain, pos1, pos2, ends, n_slots, tm, tmx):
    m, d = x2d.shape
    smem = lambda: pl.BlockSpec((tm,), lambda i, ends: (i,), memory_space=pltpu.SMEM)
    grid_spec = pltpu.PrefetchScalarGridSpec(
        num_scalar_prefetch=1,
        grid=(m // tm,),
        in_specs=[smem(), smem(), pl.BlockSpec((tm, d), lambda i, ends: (i, 0)),
                  pl.BlockSpec((1, d), lambda i, ends: (0, 0))],
        out_specs=pl.BlockSpec(memory_space=pl.ANY),
        scratch_shapes=[pltpu.VMEM((2, tm, d), F32), pltpu.VMEM((tmx, d), F32),
                        pltpu.SemaphoreType.DMA((2,)), pltpu.SemaphoreType.DMA(())],
    )
    return pl.pallas_call(
        functools.partial(_dispatch_kernel, tm=tm, tmx=tmx),
        grid_spec=grid_spec,
        out_shape=jax.ShapeDtypeStruct((n_slots, d), F32),
        compiler_params=_params("arbitrary"),
        name="moe_dispatch",
    )(ends, pos1, pos2, x2d, gain.reshape(1, d))


def _expert_ffn_kernel(te_ref, nt_ref, xs_ref, wgu_ref, wd_ref, ys_ref, *, d_ff, tf):
    @pl.when(pl.program_id(0) < nt_ref[0])
    def _():
        ys_ref[...] = _swiglu(xs_ref[...].astype(BF16), wgu_ref, wd_ref, d_ff, tf)

    @pl.when(pl.program_id(0) >= nt_ref[0])
    def _():
        ys_ref[...] = jnp.zeros(ys_ref.shape, F32)


def _expert_ffn(xs, tile_expert, n_tiles, wgu_bf, wd_bf, tmx):
    n_slots, d = xs.shape
    _, d_ff, _ = wd_bf.shape
    row = lambda i, te, nt: (jnp.minimum(i, nt[0] - 1), 0)
    grid_spec = pltpu.PrefetchScalarGridSpec(
        num_scalar_prefetch=2,
        grid=(n_slots // tmx,),
        in_specs=[pl.BlockSpec((tmx, d), row),
                  pl.BlockSpec((None, d, 2 * d_ff), lambda i, te, nt: (te[i], 0, 0)),
                  pl.BlockSpec((None, d_ff, d), lambda i, te, nt: (te[i], 0, 0))],
        out_specs=pl.BlockSpec((tmx, d), lambda i, te, nt: (i, 0)),
    )
    return pl.pallas_call(
        functools.partial(_expert_ffn_kernel, d_ff=d_ff, tf=256),
        grid_spec=grid_spec,
        out_shape=jax.ShapeDtypeStruct((n_slots, d), F32),
        compiler_params=_params("arbitrary"),
        name="moe_experts",
    )(tile_expert, n_tiles, xs, wgu_bf, wd_bf)


def _combine_kernel(pos1_ref, pos2_ref, nxt1_ref, nxt2_ref, ys_hbm, x_ref, meta_ref, gf_ref, o_ref,
                    y1_ref, y2_ref, sem, *, tm):
    i = pl.program_id(0)
    slot = i % 2

    def gather(p1_ref, p2_ref, sl):
        def start(r, carry):
            _row_copy(ys_hbm, p1_ref[r], y1_ref.at[sl], r, sem.at[sl]).start()
            _row_copy(ys_hbm, p2_ref[r], y2_ref.at[sl], r, sem.at[sl]).start()
            return carry
        lax.fori_loop(0, tm, start, 0, unroll=ROW_DMA_UNROLL)

    @pl.when(i == 0)
    def _first():
        gather(pos1_ref, pos2_ref, 0)

    @pl.when(i + 1 < pl.num_programs(0))
    def _next():
        gather(nxt1_ref, nxt2_ref, 1 - slot)

    def wait(r, carry):
        _row_copy(ys_hbm, 0, y1_ref.at[slot], 0, sem.at[slot]).wait()
        _row_copy(ys_hbm, 0, y2_ref.at[slot], 0, sem.at[slot]).wait()
        return carry

    lax.fori_loop(0, tm, wait, 0, unroll=ROW_DMA_UNROLL)
    meta = meta_ref[...]
    out = meta[:, 2:3] * y1_ref[slot] + meta[:, 3:4] * y2_ref[slot]
    o_ref[...] = _rmsnorm(x_ref[...] + out, gf_ref[...], RMS_EPS)


def _combine(ys, pos1, pos2, x2d, meta, gain_final, tm):
    m, d = x2d.shape
    last = m // tm - 1
    smem = lambda off: pl.BlockSpec((tm,), lambda i: (jnp.minimum(i + off, last),), memory_space=pltpu.SMEM)
    row = lambda i: (i, 0)
    return pl.pallas_call(
        functools.partial(_combine_kernel, tm=tm),
        grid=(m // tm,),
        in_specs=[smem(0), smem(0), smem(1), smem(1), pl.BlockSpec(memory_space=pl.ANY),
                  pl.BlockSpec((tm, d), row), pl.BlockSpec((tm, LANES), row), _const_spec((1, d))],
        out_specs=pl.BlockSpec((tm, d), row),
        out_shape=jax.ShapeDtypeStruct((m, d), F32),
        scratch_shapes=[pltpu.VMEM((2, tm, d), F32), pltpu.VMEM((2, tm, d), F32),
                        pltpu.SemaphoreType.DMA((2,))],
        compiler_params=_params("arbitrary"),
        name="moe_combine",
    )(pos1, pos2, pos1, pos2, ys, x2d, meta, gain_final.reshape(1, d))


def _moe(x2d, norm_gain, w_router, b_router, wgu_bf, wd_bf, gain_final, tm):
    m, d = x2d.shape
    tmx = 512 if m >= 4096 else 128
    meta, counts = _router(x2d, norm_gain, w_router, b_router, tm)
    counts = counts[0, :N_EXPERTS].astype(jnp.int32)
    padded = (counts + tmx - 1) // tmx * tmx
    ends = jnp.cumsum(padded)
    starts = ends - padded
    n_slots = 2 * m + N_EXPERTS * tmx
    n_tiles_max = n_slots // tmx
    tile_start = jnp.arange(n_tiles_max, dtype=jnp.int32) * tmx
    tile_expert = jnp.minimum(jnp.sum((tile_start[:, None] >= ends[None, :]).astype(jnp.int32), axis=1),
                              N_EXPERTS - 1).astype(jnp.int32)
    n_tiles = (ends[-1:] // tmx).astype(jnp.int32)
    experts = jnp.arange(N_EXPERTS, dtype=jnp.int32)[None, :]
    start_of = lambda e: jnp.sum(jnp.where(e[:, None] == experts, starts[None, :], 0), axis=1)
    e1, e2 = meta[:, 0].astype(jnp.int32), meta[:, 1].astype(jnp.int32)
    pos1 = start_of(e1) + meta[:, 4].astype(jnp.int32)
    pos2 = start_of(e2) + meta[:, 5].astype(jnp.int32)
    td = min(512, m)
    xs = _dispatch(x2d, norm_gain, pos1, pos2, ends.astype(jnp.int32), n_slots, td, tmx)
    ys = _expert_ffn(xs, tile_expert, n_tiles, wgu_bf, wd_bf, tmx)
    return _combine(ys, pos1, pos2, x2d, meta, gain_final, td)


def _forward(x, caches, w):
    b, t, d = x.shape
    m = b * t
    tm = min(512, m)
    x2d = x.reshape(m, d)
    lambda_init = 0.8 - 0.6 * math.exp(-0.3 * 0)
    width = DIFF_HEADS * 2 * DIFF_HEAD_DIM
    ncb = width // LANES

    if caches is None:
        pbf, ka, va, kb, vb, vt, kaug = _qkv_proj(x2d, w["norm_attn"], w["attn_w_in"], tm, seq=t)
    else:
        pbf, ka, va, kb, vb = _qkv_proj(x2d, w["norm_attn"], w["attn_w_in"], tm)
    pbf3 = pbf.reshape(b, t, 6 * width)
    if caches is None:
        oa = _diff_attn_prompt(pbf3, kaug.reshape(b, t, 2 * width), vt, 0, w["diff_lambda"],
                               w["diff_subln"], tile=min(1024, t), lambda_init=lambda_init)
        tqb = min(256, t)
        n_parts = BAND_PAST_MAX // tqb + 1
        bias = _band_bias(w["band_rel_bias"], 0, -BAND_PAST_MAX, tqb, n_parts * tqb, key_major=True)
        ob = _band_attn_prompt(pbf3, vt, 3, 4, 1, bias, tq=tqb, n_parts=n_parts, tkp=tqb)
        n_band = min(BAND_PAST_MAX, t)
        tail = lambda a: a.reshape(b, t, width)[:, t - n_band:].reshape(b, n_band, BAND_HEADS, BAND_HEAD_DIM)
        new_bk, new_bv = tail(kb), tail(vb)
    else:
        ck_a, cv_a, ck_b, cv_b = caches
        p_len, pb_len = ck_a.shape[1], ck_b.shape[1]
        cat = lambda c, lo: jnp.concatenate(
            [c.reshape(b, c.shape[1], width).astype(BF16), pbf3[:, :, lo:lo + width]], axis=1)
        k_a, v_a = cat(ck_a, width), cat(cv_a, 2 * width)
        k_b, v_b = cat(ck_b, 4 * width), cat(cv_b, 5 * width)
        oa = _diff_attn(pbf3, k_a, v_a, 0, 0, 0, w["diff_lambda"], w["diff_subln"],
                        tq=t, tk=p_len + t, q_off=p_len, lambda_init=lambda_init)
        bias = _band_bias(w["band_rel_bias"], p_len, p_len - pb_len, t, pb_len + t)
        ob = _band_attn(pbf3, k_b, v_b, 3 * ncb, 0, 0, bias, tq=t, n_parts=1, tkp=pb_len + t)
        new_bk = kb.reshape(b, t, BAND_HEADS, BAND_HEAD_DIM)
        new_bv = vb.reshape(b, t, BAND_HEADS, BAND_HEAD_DIM)
    new_dk = ka.reshape(b, t, 2 * DIFF_HEADS, DIFF_HEAD_DIM)
    new_dv = va.reshape(b, t, DIFF_HEADS, 2 * DIFF_HEAD_DIM)

    x2d = _attn_out_ffn(oa.reshape(m, width), ob.reshape(m, width), x2d, w["attn_w_out"],
                        w["norm_ffn"], w["ffn_w_gu"], w["ffn_w_down"], tm)

    seg = min(t, GMLP_CHUNK)
    x2d, v_rows = _gmlp(x2d, w["norm_gmlp"], w["gmlp_w_in"], w["gmlp_b_in"], w["gmlp_ln_g"],
                        w["gmlp_ln_b"], w["gmlp_w_s"], w["gmlp_b_s"], w["gmlp_w_out"], tm, seg,
                        return_v=caches is not None)
    y = _moe(x2d, w["norm_moe"], w["moe_w_router"], w["moe_b_router"], w["moe_w_gu"], w["moe_w_down"],
             w["norm_final"], tm)
    new_gv = None if v_rows is None else v_rows.reshape(b, t, -1)[None]
    return (y.reshape(b, t, d), new_dk[None], new_dv[None], new_bk[None], new_bv[None], new_gv)


def kernel(x_prompt, x_sample, cache_diff_k, cache_diff_v, cache_band_k, cache_band_v,
           norm_attn, attn_w_in, diff_lambda, diff_subln, band_rel_bias, attn_w_out,
           norm_ffn, ffn_w_gu, ffn_w_down,
           norm_gmlp, gmlp_w_in, gmlp_b_in, gmlp_ln_g, gmlp_ln_b, gmlp_w_s, gmlp_b_s, gmlp_w_out,
           norm_moe, moe_w_router, moe_b_router, moe_w_gu, moe_w_down, norm_final):
    w = {
        "norm_attn": norm_attn[0], "attn_w_in": attn_w_in[0].astype(BF16),
        "diff_lambda": diff_lambda[0], "diff_subln": diff_subln[0],
        "band_rel_bias": band_rel_bias[0], "attn_w_out": attn_w_out[0].astype(BF16),
        "norm_ffn": norm_ffn[0], "ffn_w_gu": ffn_w_gu[0].astype(BF16),
        "ffn_w_down": ffn_w_down[0].astype(BF16),
        "norm_gmlp": norm_gmlp[0], "gmlp_w_in": gmlp_w_in[0].astype(BF16),
        "gmlp_b_in": gmlp_b_in[0], "gmlp_ln_g": gmlp_ln_g[0], "gmlp_ln_b": gmlp_ln_b[0],
        "gmlp_w_s": gmlp_w_s[0], "gmlp_b_s": gmlp_b_s[0], "gmlp_w_out": gmlp_w_out[0].astype(BF16),
        "norm_moe": norm_moe[0], "moe_w_router": moe_w_router[0], "moe_b_router": moe_b_router[0],
        "moe_w_gu": moe_w_gu[0].astype(BF16), "moe_w_down": moe_w_down[0].astype(BF16),
        "norm_final": norm_final,
    }
    y_p, dk_p, dv_p, bk_p, bv_p, _ = _forward(x_prompt, None, w)
    y_s, dk_s, dv_s, bk_s, bv_s, gv_s = _forward(
        x_sample, (cache_diff_k[0], cache_diff_v[0], cache_band_k[0], cache_band_v[0]), w)
    return (y_p, y_s, dk_p, dv_p, bk_p, bv_p, dk_s, dv_s, bk_s, bv_s, gv_s)
```

```python
import functools
import math

import numpy as np
import jax
import jax.numpy as jnp
from jax import lax
from jax.experimental import pallas as pl
from jax.experimental.pallas import tpu as pltpu

CHUNK = 64
DIFF_HEADS = 4
DIFF_HEAD_DIM = 64
DIFF_SUBLN_EPS = 1e-5
BAND_HEADS = 8
BAND_HEAD_DIM = 64
BAND_PREV_CHUNKS = 8
BAND_PAST_MAX = BAND_PREV_CHUNKS * CHUNK
REL_CLIP = 256
GMLP_GROUPS = 8
GMLP_CHUNK = 128
N_EXPERTS = 8
RMS_EPS = 1e-6
LN_EPS = 1e-5
MASK_VALUE = -1e30

LANES = 128
ONES_ROWS = 16
VMEM_LIMIT_BYTES = 56 << 20

F32 = jnp.float32
BF16 = jnp.bfloat16
LOG2E = math.log2(math.e)
BF16_EXP2_ZERO = 140.0
KNORM_LANE = 6
ROUND_UP = 1.0 + 2.0 ** -6
Q_SCALE = DIFF_HEAD_DIM ** -0.5 * LOG2E
NT_DIMS = (((1,), (1,)), ((), ()))


def _params(*sem, flags=None):
    return pltpu.CompilerParams(dimension_semantics=sem, vmem_limit_bytes=VMEM_LIMIT_BYTES, flags=flags)


def _const_spec(shape):
    nd = len(shape)
    return pl.BlockSpec(shape, lambda *_: (0,) * nd, pipeline_mode=pl.Buffered(1))


def _rmsnorm(x, g, eps):
    return (x * lax.rsqrt(jnp.mean(x * x, axis=-1, keepdims=True) + eps)) * g


def _dot(a, b):
    return jnp.dot(a, b, preferred_element_type=F32)


def _swiglu(h, wgu_ref, wd_ref, d_ff, tf):
    y = None
    for c in range(d_ff // tf):
        g = _dot(h, wgu_ref[:, c * tf:(c + 1) * tf])
        u = _dot(h, wgu_ref[:, d_ff + c * tf:d_ff + (c + 1) * tf])
        a = (g * jax.nn.sigmoid(g) * u).astype(BF16)
        part = _dot(a, wd_ref[c * tf:(c + 1) * tf, :])
        y = part if y is None else y + part
    return y


def _qkv_kernel(x_ref, g_ref, w_ref, *rest, width, with_vt, tiles_per_seq):
    if with_vt:
        pbf_ref, ka_ref, va_ref, kb_ref, vb_ref, vt_ref, kaug_ref = rest
    else:
        pbf_ref, ka_ref, va_ref, kb_ref, vb_ref = rest
    tm = x_ref.shape[0]
    h = _rmsnorm(x_ref[...], g_ref[...], RMS_EPS).astype(BF16)
    f32_outs = {1: ka_ref, 2: va_ref, 4: kb_ref, 5: vb_ref}
    for c in range(6):
        r = _dot(h, w_ref[:, c * width:(c + 1) * width])
        if c in f32_outs:
            f32_outs[c][...] = r
        else:
            r = r * Q_SCALE
        pbf_ref[:, c * width:(c + 1) * width] = r.astype(BF16)
        if with_vt and c == 1:
            pos = (pl.program_id(0) % tiles_per_seq) * tm + lax.broadcasted_iota(jnp.int32, (tm, LANES), 0)
            lane = lax.broadcasted_iota(jnp.int32, (tm, LANES), 1)
            hi = (pos // LANES).astype(F32)
            lo = (pos % LANES).astype(F32)
            feats = jnp.where(lane < 3, hi, jnp.where(lane < 6, lo, 0.0))
            for hd in range(DIFF_HEADS):
                k_bf = r[:, hd * LANES:(hd + 1) * LANES].astype(BF16)
                kaug_ref[:, 2 * hd * LANES:(2 * hd + 1) * LANES] = k_bf
                k2 = jnp.square(k_bf.astype(F32))
                n1 = jnp.sqrt(jnp.sum(jnp.where(lane < DIFF_HEAD_DIM, k2, 0.0), axis=1, keepdims=True))
                n2 = jnp.sqrt(jnp.sum(jnp.where(lane >= DIFF_HEAD_DIM, k2, 0.0), axis=1, keepdims=True))
                f = jnp.where(lane == KNORM_LANE, n1 * ROUND_UP, jnp.where(lane == KNORM_LANE + 1, n2 * ROUND_UP, feats))
                kaug_ref[:, (2 * hd + 1) * LANES:(2 * hd + 2) * LANES] = f.astype(BF16)
        if with_vt and c in (2, 5):
            r0 = 0 if c == 2 else width
            vt_ref[r0:r0 + width, :] = r.T.astype(BF16)


def _qkv_proj(x2d, gain, w_bf, tm, seq=None):
    m, d = x2d.shape
    width = w_bf.shape[1] // 6
    row = lambda i: (i, 0)
    with_vt = seq is not None
    in_specs = [pl.BlockSpec((tm, d), row), _const_spec((1, d)), _const_spec(w_bf.shape)]
    out_specs = [pl.BlockSpec((tm, 6 * width), row)] + [pl.BlockSpec((tm, width), row)] * 4
    out_shape = [jax.ShapeDtypeStruct((m, 6 * width), BF16)] + [jax.ShapeDtypeStruct((m, width), F32)] * 4
    args = [x2d, gain.reshape(1, d), w_bf]
    nt = 1
    if with_vt:
        nt = seq // tm
        assert seq <= 256 * LANES
        vt_rows = 2 * width
        out_specs.append(pl.BlockSpec((None, vt_rows, tm), lambda i: (i // nt, 0, i % nt)))
        out_shape.append(jax.ShapeDtypeStruct((m // seq, vt_rows, seq), BF16))
        out_specs.append(pl.BlockSpec((tm, 2 * width), row))
        out_shape.append(jax.ShapeDtypeStruct((m, 2 * width), BF16))
    return pl.pallas_call(
        functools.partial(_qkv_kernel, width=width, with_vt=with_vt, tiles_per_seq=nt),
        grid=(m // tm,),
        in_specs=in_specs, out_specs=out_specs, out_shape=out_shape,
        compiler_params=_params("parallel"),
        name="qkv_proj",
    )(*args)


def _split_heads_rows(q, tq):
    qf = q.astype(F32)
    lane = lax.broadcasted_iota(jnp.int32, qf.shape, 1)
    lo = jnp.where(lane < DIFF_HEAD_DIM, qf, 0.0)
    hi = jnp.where(lane >= DIFF_HEAD_DIM, qf, 0.0)
    return jnp.concatenate([lo, hi], axis=0).astype(BF16)


def _diff_attn_kernel(ii_ref, jj_ref, last_ref, q_ref, k_ref, v_ref, slope_ref, lam_ref, g_ref,
                      o_ref, qz_ref, m_ref, l_ref, acc_ref, *, tq, tk, q_off, lambda_init):
    s = pl.program_id(2)
    i = ii_ref[s]
    j = jj_ref[s]

    @pl.when(j == 0)
    def _init():
        qz_ref[...] = _split_heads_rows(q_ref[...], tq)
        m_ref[...] = jnp.full(m_ref.shape, MASK_VALUE, F32)
        l_ref[...] = jnp.zeros(l_ref.shape, F32)
        acc_ref[...] = jnp.zeros(acc_ref.shape, F32)

    sc = lax.dot_general(qz_ref[...], k_ref[...], NT_DIMS, preferred_element_type=F32)
    row = lax.broadcasted_iota(jnp.int32, (2 * tq, 1), 0)
    row = jnp.where(row >= tq, row - tq, row)
    qpos = q_off + i * tq + row
    kpos = j * tk + lax.broadcasted_iota(jnp.int32, (1, tk), 1)
    dist = jnp.abs(qpos - kpos).astype(F32)
    sc = sc - slope_ref[:, 0:1] * dist
    visible = (kpos // CHUNK) <= (qpos // CHUNK)
    sc = jnp.where(visible, sc, MASK_VALUE)

    m_prev = m_ref[...]
    m_new = jnp.maximum(m_prev, jnp.max(sc, axis=1, keepdims=True))
    alpha = jnp.exp2(m_prev - m_new)
    p = jnp.exp2(sc - m_new)
    l_ref[...] = alpha * l_ref[...] + jnp.sum(p, axis=1, keepdims=True)
    acc_ref[...] = alpha * acc_ref[...] + _dot(p.astype(BF16), v_ref[...])
    m_ref[...] = m_new

    @pl.when(last_ref[s] == 1)
    def _finish():
        lp = lam_ref[...]
        lam = (jnp.exp(jnp.sum(lp[0:1] * lp[1:2], axis=1, keepdims=True))
               - jnp.exp(jnp.sum(lp[2:3] * lp[3:4], axis=1, keepdims=True)) + lambda_init)
        o_all = acc_ref[...] / l_ref[...]
        o = o_all[:tq] - lam * o_all[tq:]
        o = _rmsnorm(o, g_ref[...], DIFF_SUBLN_EPS) * (1.0 - lambda_init)
        o_ref[...] = o.astype(o_ref.dtype)


def _diff_attn(q_arr, k_arr, v_arr, q_cb, k_cb, v_cb, lam_p, subln_g, *, tq, tk, q_off, lambda_init):
    b, t_q, _ = q_arr.shape
    t_k = k_arr.shape[1]
    nq, nk = t_q // tq, t_k // tk
    pairs = [(i, j) for i in range(nq) for j in range(nk)
             if (j * tk) // CHUNK <= (q_off + i * tq + tq - 1) // CHUNK]
    ii = np.array([p[0] for p in pairs], np.int32)
    jj = np.array([p[1] for p in pairs], np.int32)
    last = np.array([1 if (n + 1 == len(pairs) or pairs[n + 1][0] != pairs[n][0]) else 0
                     for n in range(len(pairs))], np.int32)
    slopes = jnp.asarray(_alibi_slopes_log2()[:, None, None] * np.ones((1, 1, LANES), np.float32))

    grid_spec = pltpu.PrefetchScalarGridSpec(
        num_scalar_prefetch=3,
        grid=(b, DIFF_HEADS, len(pairs)),
        in_specs=[
            pl.BlockSpec((None, tq, LANES), lambda b_, h, s, ii, jj, la: (b_, ii[s], q_cb + h)),
            pl.BlockSpec((None, tk, LANES), lambda b_, h, s, ii, jj, la: (b_, jj[s], k_cb + h)),
            pl.BlockSpec((None, tk, LANES), lambda b_, h, s, ii, jj, la: (b_, jj[s], v_cb + h)),
            pl.BlockSpec((None, 1, LANES), lambda b_, h, s, ii, jj, la: (h, 0, 0)),
            pl.BlockSpec((4, DIFF_HEAD_DIM), lambda b_, h, s, ii, jj, la: (0, 0)),
            pl.BlockSpec((1, LANES), lambda b_, h, s, ii, jj, la: (0, 0)),
        ],
        out_specs=pl.BlockSpec((None, tq, LANES), lambda b_, h, s, ii, jj, la: (b_, ii[s], h)),
        scratch_shapes=[
            pltpu.VMEM((2 * tq, LANES), BF16),
            pltpu.VMEM((2 * tq, 1), F32),
            pltpu.VMEM((2 * tq, 1), F32),
            pltpu.VMEM((2 * tq, LANES), F32),
        ],
    )
    return pl.pallas_call(
        functools.partial(_diff_attn_kernel, tq=tq, tk=tk, q_off=q_off, lambda_init=lambda_init),
        grid_spec=grid_spec,
        out_shape=jax.ShapeDtypeStruct((b, t_q, DIFF_HEADS * LANES), BF16),
        compiler_params=_params("parallel", "parallel", "arbitrary"),
        name="diff_attn",
    )(jnp.asarray(ii), jnp.asarray(jj), jnp.asarray(last), q_arr, k_arr, v_arr, slopes,
      lam_p, subln_g.reshape(1, LANES))


def _alibi_slopes_log2():
    return (2.0 ** (-8.0 * np.arange(1, DIFF_HEADS + 1, dtype=np.float64) / DIFF_HEADS) * LOG2E).astype(np.float32)


def _diff_prompt_kernel(ii_ref, jj_ref, last_ref, lin_ref, q_ref, k_ref, vt_ref, dbias_ref, qfeat_ref,
                        lam_ref, g_ref, o_ref, qzt_ref, qn_ref, s0_ref, s1_ref, mx0_ref, mx1_ref, m_ref,
                        acc_ref, *, tq, tk, cb, lambda_init):
    s = pl.program_id(2)
    n_pairs = pl.num_programs(2) - 1
    sp = jnp.minimum(s, n_pairs - 1)
    sc = jnp.maximum(s - 1, 0)
    consuming = s > 0
    nblk = 2 * tq // cb
    blk = lambda c: slice(c * cb, (c + 1) * cb)

    @pl.when(s == 0)
    def _first_step():
        s1_ref[...] = jnp.zeros(s1_ref.shape, F32)
        mx1_ref[...] = jnp.zeros(mx1_ref.shape, F32)

    @pl.when(jj_ref[sp] == ii_ref[sp])
    def _new_queries():
        qf = q_ref[...].astype(F32)
        lane = lax.broadcasted_iota(jnp.int32, qf.shape, 1)
        qzt_ref[0:LANES, 0:tq] = jnp.where(lane < DIFF_HEAD_DIM, qf, 0.0).T.astype(BF16)
        qzt_ref[0:LANES, tq:2 * tq] = jnp.where(lane >= DIFF_HEAD_DIM, qf, 0.0).T.astype(BF16)
        qzt_ref[LANES:2 * LANES, :] = jnp.broadcast_to(qfeat_ref[...], (LANES, 2 * tq)).astype(BF16)
        qn_ref[...] = jnp.sqrt(jnp.sum(jnp.square(qzt_ref[0:LANES, :].astype(F32)), axis=0, keepdims=True))

    @pl.when(jj_ref[sc] == ii_ref[sc])
    def _reset():
        m_ref[...] = jnp.full(m_ref.shape, MASK_VALUE, F32)
        acc_ref[...] = jnp.zeros(acc_ref.shape, F32)

    kfeat_max = jnp.max(k_ref[:, LANES:2 * LANES].astype(F32), axis=0, keepdims=True)
    knorm = jnp.concatenate([jnp.broadcast_to(kfeat_max[:, KNORM_LANE:KNORM_LANE + 1], (1, tq)),
                             jnp.broadcast_to(kfeat_max[:, KNORM_LANE + 1:KNORM_LANE + 2], (1, tq))], axis=1)
    slope = jnp.sum(qfeat_ref[3:6, :], axis=0, keepdims=True)
    kpos_max = ((jj_ref[sp] + 1) * tk - 1).astype(F32)
    bound = qn_ref[...] * knorm * ROUND_UP + slope * kpos_max
    out_of_reach = jnp.max(bound - m_ref[...]) < -BF16_EXP2_ZERO
    skip_produce = jnp.logical_or(s == n_pairs, jnp.logical_and(lin_ref[sp] == 1, out_of_reach))

    def stages(s_prod, mx_prod, s_cons, mx_cons):
        @pl.when(jnp.logical_and(consuming, lin_ref[sc] == 0))
        def _diagonal():
            for c in range(nblk):
                v = s_cons[:, blk(c)] + dbias_ref[:, (c * cb) % tq:(c * cb) % tq + cb]
                s_cons[:, blk(c)] = v
                mx_cons[:, blk(c)] = jnp.max(v, axis=0, keepdims=True)

        def produce():
            k = k_ref[...]
            for c in range(nblk):
                v = _dot(k, qzt_ref[:, blk(c)])
                s_prod[:, blk(c)] = v
                mx_prod[:, blk(c)] = jnp.max(v, axis=0, keepdims=True)

        def consume():
            vt = jnp.concatenate([vt_ref[...], jnp.ones((ONES_ROWS, tk), BF16)], axis=0)
            m_prev = [m_ref[:, blk(c)] for c in range(nblk)]
            acc_prev = [acc_ref[:, blk(c)] for c in range(nblk)]
            m_out, acc_out = [], []
            for c in range(nblk):
                m_new = jnp.maximum(m_prev[c], mx_cons[:, blk(c)])
                alpha = jnp.exp2(m_prev[c] - m_new)
                p = jnp.exp2((s_cons[:, blk(c)] - m_new).astype(BF16))
                acc_out.append(alpha * acc_prev[c] + _dot(vt, p))
                m_out.append(m_new)
            for c in range(nblk):
                m_ref[:, blk(c)] = m_out[c]
                acc_ref[:, blk(c)] = acc_out[c]

        negligible = jnp.max(mx_cons[...] - m_ref[...]) < -BF16_EXP2_ZERO
        do_produce = jnp.logical_not(skip_produce)
        do_consume = jnp.logical_not(negligible)

        @pl.when(skip_produce)
        def _mark_skipped():
            mx_prod[...] = jnp.full(mx_prod.shape, MASK_VALUE, F32)

        @pl.when(jnp.logical_and(skip_produce, do_consume))
        def _consume_only():
            consume()

        @pl.when(jnp.logical_and(do_produce, negligible))
        def _produce_only():
            produce()

        @pl.when(jnp.logical_and(do_produce, do_consume))
        def _both():
            produce()
            consume()

    @pl.when(s % 2 == 0)
    def _even():
        stages(s0_ref, mx0_ref, s1_ref, mx1_ref)

    @pl.when(s % 2 == 1)
    def _odd():
        stages(s1_ref, mx1_ref, s0_ref, mx0_ref)

    @pl.when(jnp.logical_and(consuming, last_ref[sc] == 1))
    def _finish():
        lp = lam_ref[...]
        lam = (jnp.exp(jnp.sum(lp[0:1] * lp[1:2], axis=1, keepdims=True))
               - jnp.exp(jnp.sum(lp[2:3] * lp[3:4], axis=1, keepdims=True)) + lambda_init)
        o_all = acc_ref[0:LANES, :] / acc_ref[LANES:LANES + 1, :]
        o = o_all[:, 0:tq] - lam * o_all[:, tq:2 * tq]
        o = o * lax.rsqrt(jnp.mean(o * o, axis=0, keepdims=True) + DIFF_SUBLN_EPS)
        o = o * g_ref[...] * (1.0 - lambda_init)
        o_ref[...] = o.T.astype(o_ref.dtype)


def _bf16_split3(x):
    import ml_dtypes
    rnd = lambda v: v.astype(ml_dtypes.bfloat16).astype(np.float32)
    x = np.asarray(x, np.float32)
    hi = rnd(x)
    mid = rnd(x - hi)
    lo = rnd(x - hi - mid)
    return hi, mid, lo


def _diff_attn_prompt(pbf3, kaug3, vt_arr, q_cb, lam_p, subln_g, *, tile, lambda_init):
    b, t, _ = pbf3.shape
    tq = tk = tile
    n = t // tile
    pairs = [(i, j) for i in range(n) for j in range(i, -1, -1)]
    ii = np.array([p[0] for p in pairs], np.int32)
    jj = np.array([p[1] for p in pairs], np.int32)
    last = (jj == 0).astype(np.int32)
    lin = (ii != jj).astype(np.int32)
    c2 = _alibi_slopes_log2()
    qfeat = np.zeros((DIFF_HEADS, LANES, 1), np.float32)
    for r, part in enumerate(_bf16_split3(c2)):
        qfeat[:, r, 0] = part * LANES
        qfeat[:, 3 + r, 0] = part
    pos = np.arange(tile)
    d = (pos[None, :] - pos[:, None]).astype(np.float32)
    visible = (pos[:, None] // CHUNK) <= (pos[None, :] // CHUNK)
    dbias = jnp.asarray(np.where(visible[None], 2.0 * c2[:, None, None] * np.minimum(d, 0.0)[None],
                                 MASK_VALUE).astype(np.float32))
    cb = min(256, 2 * tq)
    n_pairs = len(pairs)
    prod = lambda s: jnp.minimum(s, n_pairs - 1)
    cons = lambda s: jnp.maximum(s - 1, 0)
    idx = lambda f: (lambda b_, h, s, ii, jj, la, li: f(b_, h, s, ii, jj))
    grid_spec = pltpu.PrefetchScalarGridSpec(
        num_scalar_prefetch=4,
        grid=(b, DIFF_HEADS, n_pairs + 1),
        in_specs=[
            pl.BlockSpec((None, tq, LANES), idx(lambda b_, h, s, ii, jj: (b_, ii[prod(s)], q_cb + h))),
            pl.BlockSpec((None, tk, 2 * LANES), idx(lambda b_, h, s, ii, jj: (b_, jj[prod(s)], h))),
            pl.BlockSpec((None, LANES, tk), idx(lambda b_, h, s, ii, jj: (b_, h, jj[cons(s)]))),
            pl.BlockSpec((None, tk, tq), idx(lambda b_, h, s, ii, jj: (h, 0, 0))),
            pl.BlockSpec((None, LANES, 1), idx(lambda b_, h, s, ii, jj: (h, 0, 0))),
            pl.BlockSpec((4, DIFF_HEAD_DIM), idx(lambda b_, h, s, ii, jj: (0, 0))),
            pl.BlockSpec((LANES, 1), idx(lambda b_, h, s, ii, jj: (0, 0))),
        ],
        out_specs=pl.BlockSpec((None, tq, LANES), idx(lambda b_, h, s, ii, jj: (b_, ii[cons(s)], h))),
        scratch_shapes=[
            pltpu.VMEM((2 * LANES, 2 * tq), BF16),
            pltpu.VMEM((1, 2 * tq), F32),
            pltpu.VMEM((tk, 2 * tq), F32), pltpu.VMEM((tk, 2 * tq), F32),
            pltpu.VMEM((1, 2 * tq), F32), pltpu.VMEM((1, 2 * tq), F32),
            pltpu.VMEM((1, 2 * tq), F32),
            pltpu.VMEM((LANES + ONES_ROWS, 2 * tq), F32),
        ],
    )
    return pl.pallas_call(
        functools.partial(_diff_prompt_kernel, tq=tq, tk=tk, cb=cb, lambda_init=lambda_init),
        grid_spec=grid_spec,
        out_shape=jax.ShapeDtypeStruct((b, t, DIFF_HEADS * LANES), BF16),
        compiler_params=_params("parallel", "parallel", "arbitrary"),
        name="diff_attn_prompt",
    )(jnp.asarray(ii), jnp.asarray(jj), jnp.asarray(last), jnp.asarray(lin), pbf3, kaug3, vt_arr, dbias,
      jnp.asarray(qfeat), lam_p, subln_g.reshape(LANES, 1))


def _band_attn_kernel(*refs, tq, n_parts, tkp):
    q_ref = refs[0]
    k_refs = refs[1:1 + n_parts]
    v_refs = refs[1 + n_parts:1 + 2 * n_parts]
    bias_ref = refs[1 + 2 * n_parts]
    o_ref = refs[2 + 2 * n_parts]
    i = pl.program_id(2)
    qz = _split_heads_rows(q_ref[...], tq)
    scores = []
    for m in range(n_parts):
        sc = lax.dot_general(qz, k_refs[m][...], NT_DIMS, preferred_element_type=F32)
        sc = sc + bias_ref[:, m * tkp:(m + 1) * tkp]
        if n_parts > 1:
            sc = jnp.where(i - (n_parts - 1) + m >= 0, sc, MASK_VALUE)
        scores.append(sc)
    mx = functools.reduce(jnp.maximum, [jnp.max(sc, axis=1, keepdims=True) for sc in scores])
    den = None
    num = None
    for m in range(n_parts):
        p = jnp.exp2(scores[m] - mx)
        d = jnp.sum(p, axis=1, keepdims=True)
        r = _dot(p.astype(BF16), v_refs[m][...])
        den = d if den is None else den + d
        num = r if num is None else num + r
    r = num / den
    lane = lax.broadcasted_iota(jnp.int32, (tq, LANES), 1)
    o_ref[...] = jnp.where(lane < BAND_HEAD_DIM, r[:tq], r[tq:]).astype(o_ref.dtype)


def _band_bias_kernel(line_ref, o_ref, *, q0, k0):
    tq, nk = o_ref.shape
    line = jnp.broadcast_to(line_ref[...], (tq, line_ref.shape[1]))
    rolled = pltpu.roll(line, 0, 1, stride=1, stride_axis=0)
    qc = (q0 + lax.broadcasted_iota(jnp.int32, (tq, nk), 0)) >> 6
    kc = (k0 + lax.broadcasted_iota(jnp.int32, (tq, nk), 1)) >> 6
    visible = (kc <= qc) & (qc - kc <= BAND_PREV_CHUNKS)
    o_ref[...] = jnp.where(visible, rolled[:, 0:nk], MASK_VALUE)


def _band_bias_t_kernel(line_ref, o_ref, *, q0, k0):
    nk, tq = o_ref.shape
    line = jnp.broadcast_to(line_ref[...], (nk, line_ref.shape[1]))
    rolled = pltpu.roll(line, 0, 1, stride=1, stride_axis=0)
    kc = (k0 + lax.broadcasted_iota(jnp.int32, (nk, tq), 0)) >> 6
    qc = (q0 + lax.broadcasted_iota(jnp.int32, (nk, tq), 1)) >> 6
    visible = (kc <= qc) & (qc - kc <= BAND_PREV_CHUNKS)
    o_ref[...] = jnp.where(visible, rolled[:, 0:tq], MASK_VALUE)


def _band_bias(rel_table, q0, k0, tq, nk, key_major=False):
    assert CHUNK == 64
    width = pl.next_power_of_2(tq + nk - 1)
    y = np.arange(width)
    c_minus_r = np.where(y < nk, y, y - width)
    rel = np.clip((q0 - k0) - c_minus_r, -(CHUNK - 1), REL_CLIP) + (CHUNK - 1)
    if key_major:
        rel = rel[(-y) % width]
    line = (rel_table.astype(F32) * LOG2E)[:, rel].reshape(BAND_HEADS, 1, width)
    if key_major:
        kern, block, shape = _band_bias_t_kernel, (None, nk, tq), (BAND_HEADS // 2, nk, 2 * tq)
        index = lambda h: (h // 2, 0, h % 2)
    else:
        kern, block, shape = _band_bias_kernel, (None, tq, nk), (BAND_HEADS // 2, 2 * tq, nk)
        index = lambda h: (h // 2, h % 2, 0)
    return pl.pallas_call(
        functools.partial(kern, q0=q0, k0=k0),
        grid=(BAND_HEADS,),
        in_specs=[pl.BlockSpec((None, 1, width), lambda h: (h, 0, 0))],
        out_specs=pl.BlockSpec(block, index),
        out_shape=jax.ShapeDtypeStruct(shape, F32),
        compiler_params=_params("parallel"),
        name="band_bias",
    )(line)


def _band_prompt_kernel(*refs, tq, n_parts, tkp):
    q_ref = refs[0]
    k_refs = refs[1:1 + n_parts]
    vt_refs = refs[1 + n_parts:1 + 2 * n_parts]
    bias_ref = refs[1 + 2 * n_parts]
    o_ref = refs[2 + 2 * n_parts]
    nk = n_parts * tkp
    hd = BAND_HEAD_DIM
    i = pl.program_id(1)
    ones = jnp.ones((ONES_ROWS, nk), BF16)
    for pr in range(BAND_HEADS // 2):
        ln = slice(pr * LANES, (pr + 1) * LANES)
        qf = q_ref[:, ln].astype(F32)
        lane = lax.broadcasted_iota(jnp.int32, qf.shape, 1)
        qzt = jnp.concatenate([jnp.where(lane < hd, qf, 0.0).T, jnp.where(lane >= hd, qf, 0.0).T],
                              axis=1).astype(BF16)
        k = jnp.concatenate([r[:, ln] for r in k_refs], axis=0)
        sc = _dot(k, qzt) + bias_ref[pr]
        row = lax.broadcasted_iota(jnp.int32, sc.shape, 0)
        sc = jnp.where(row >= (n_parts - 1 - i) * tkp, sc, MASK_VALUE)
        mx = jnp.max(sc, axis=0, keepdims=True)
        p = jnp.exp2((sc - mx).astype(BF16))
        vt = jnp.concatenate([r[ln, :] for r in vt_refs] , axis=1)
        r = _dot(jnp.concatenate([vt, ones], axis=0), p)
        o = r[0:LANES, :] / r[LANES:LANES + 1, :]
        o = jnp.concatenate([o[0:hd, 0:tq], o[hd:2 * hd, tq:2 * tq]], axis=0)
        o_ref[:, ln] = o.T.astype(o_ref.dtype)


def _band_attn_prompt(pbf3, vt_arr, q_cb, k_cb, vt_rb, bias_t, *, tq, n_parts, tkp):
    b, t, _ = pbf3.shape
    width = (BAND_HEADS // 2) * LANES
    part = lambda i, m: jnp.maximum(i - (n_parts - 1) + m, 0)
    k_spec = lambda m: pl.BlockSpec((None, tkp, width), lambda b_, i: (b_, part(i, m), k_cb))
    vt_spec = lambda m: pl.BlockSpec((None, width, tkp), lambda b_, i: (b_, vt_rb, part(i, m)))
    return pl.pallas_call(
        functools.partial(_band_prompt_kernel, tq=tq, n_parts=n_parts, tkp=tkp),
        grid=(b, t // tq),
        in_specs=[pl.BlockSpec((None, tq, width), lambda b_, i: (b_, i, q_cb))]
        + [k_spec(m) for m in range(n_parts)] + [vt_spec(m) for m in range(n_parts)]
        + [_const_spec(bias_t.shape)],
        out_specs=pl.BlockSpec((None, tq, width), lambda b_, i: (b_, i, 0)),
        out_shape=jax.ShapeDtypeStruct((b, t, width), BF16),
        compiler_params=_params("parallel", "parallel"),
        name="band_attn_prompt",
    )(pbf3, *([pbf3] * n_parts), *([vt_arr] * n_parts), bias_t)


def _band_attn(q_arr, k_arr, v_arr, q_cb, k_cb, v_cb, bias, *, tq, n_parts, tkp):
    b, t_q, _ = q_arr.shape
    n_pairs = BAND_HEADS // 2

    def kv_spec(cb, m):
        return pl.BlockSpec((None, tkp, LANES),
                            lambda p, b_, i: (b_, jnp.maximum(i - (n_parts - 1) + m, 0), cb + p))

    return pl.pallas_call(
        functools.partial(_band_attn_kernel, tq=tq, n_parts=n_parts, tkp=tkp),
        grid=(n_pairs, b, t_q // tq),
        in_specs=[pl.BlockSpec((None, tq, LANES), lambda p, b_, i: (b_, i, q_cb + p))]
        + [kv_spec(k_cb, m) for m in range(n_parts)]
        + [kv_spec(v_cb, m) for m in range(n_parts)]
        + [pl.BlockSpec((None, 2 * tq, n_parts * tkp), lambda p, b_, i: (p, 0, 0))],
        out_specs=pl.BlockSpec((None, tq, LANES), lambda p, b_, i: (b_, i, p)),
        out_shape=jax.ShapeDtypeStruct((b, t_q, n_pairs * LANES), BF16),
        compiler_params=_params("parallel", "parallel", "parallel"),
        name="band_attn",
    )(q_arr, *([k_arr] * n_parts), *([v_arr] * n_parts), bias)


def _attn_out_ffn_kernel(oa_ref, ob_ref, x_ref, wo_ref, g_ref, wgu_ref, wd_ref, o_ref, *, d_ff, tf):
    half = oa_ref.shape[1]
    x1 = x_ref[...] + _dot(oa_ref[...], wo_ref[0:half, :]) + _dot(ob_ref[...], wo_ref[half:2 * half, :])
    h = _rmsnorm(x1, g_ref[...], RMS_EPS).astype(BF16)
    o_ref[...] = x1 + _swiglu(h, wgu_ref, wd_ref, d_ff, tf)


def _attn_out_ffn(oa, ob, x2d, wo_bf, gain, wgu_bf, wd_bf, tm):
    m, d = x2d.shape
    d_ff = wd_bf.shape[0]
    row = lambda i: (i, 0)
    return pl.pallas_call(
        functools.partial(_attn_out_ffn_kernel, d_ff=d_ff, tf=256),
        grid=(m // tm,),
        in_specs=[pl.BlockSpec((tm, oa.shape[1]), row), pl.BlockSpec((tm, ob.shape[1]), row),
                  pl.BlockSpec((tm, d), row), _const_spec(wo_bf.shape), _const_spec((1, d)),
                  _const_spec(wgu_bf.shape), _const_spec(wd_bf.shape)],
        out_specs=pl.BlockSpec((tm, d), row),
        out_shape=jax.ShapeDtypeStruct((m, d), F32),
        compiler_params=_params("parallel"),
        name="attn_out_ffn",
    )(oa, ob, x2d, wo_bf, gain.reshape(1, d), wgu_bf, wd_bf)


def _gelu(z):
    return 0.5 * z * (1.0 + lax.erf(z * (2.0 ** -0.5)))


def _gmlp_kernel(x_ref, g_ref, win_ref, bin_ref, lng_ref, lnb_ref, ws_ref, bs_ref, wout_ref,
                 o_ref, v_ref, u_s, act_s, *, seg, cw):
    tm, gd = v_ref.shape
    gw = gd // GMLP_GROUPS
    x = x_ref[...]
    h = _rmsnorm(x, g_ref[...], RMS_EPS).astype(BF16)
    for c in range(gd // cw):
        lo, hi = c * cw, (c + 1) * cw
        u_s[:, lo:hi] = _gelu(_dot(h, win_ref[:, lo:hi]) + bin_ref[:, lo:hi])
        v_ref[:, lo:hi] = _gelu(_dot(h, win_ref[:, gd + lo:gd + hi]) + bin_ref[:, gd + lo:gd + hi])
    v = v_ref[...]
    mu = jnp.mean(v, axis=-1, keepdims=True)
    var = jnp.mean(jnp.square(v - mu), axis=-1, keepdims=True)
    v_ref[...] = (v - mu) * lax.rsqrt(var + LN_EPS) * lng_ref[...] + lnb_ref[...]
    r_i = lax.broadcasted_iota(jnp.int32, (seg, seg), 0)
    c_i = lax.broadcasted_iota(jnp.int32, (seg, seg), 1)
    for g in range(GMLP_GROUPS):
        w = jnp.where(r_i >= c_i, ws_ref[g], 0.0).astype(BF16)
        for n in range(tm // seg):
            rows = slice(n * seg, (n + 1) * seg)
            cols = slice(g * gw, (g + 1) * gw)
            sv = _dot(w, v_ref[rows, cols].astype(BF16)) + bs_ref[g]
            act_s[rows, cols] = (u_s[rows, cols] * sv).astype(BF16)
    o_ref[...] = x + _dot(act_s[...], wout_ref[...])


def _gmlp(x2d, gain, win_bf, b_in, ln_g, ln_b, w_s, b_s, wout_bf, tm, seg, return_v):
    m, d = x2d.shape
    gd = wout_bf.shape[0]
    row = lambda i: (i, 0)
    ws = w_s[:, :seg, :seg]
    bs = b_s[:, :seg, None]
    out_specs = [pl.BlockSpec((tm, d), row)]
    out_shape = [jax.ShapeDtypeStruct((m, d), F32)]
    scratch = [pltpu.VMEM((tm, gd), F32), pltpu.VMEM((tm, gd), BF16)]
    if return_v:
        out_specs.append(pl.BlockSpec((tm, gd), row))
        out_shape.append(jax.ShapeDtypeStruct((m, gd), F32))
    else:
        scratch.insert(0, pltpu.VMEM((tm, gd), F32))
    outs = pl.pallas_call(
        functools.partial(_gmlp_kernel, seg=seg, cw=512),
        grid=(m // tm,),
        in_specs=[pl.BlockSpec((tm, d), row), _const_spec((1, d)), _const_spec(win_bf.shape),
                  _const_spec((1, 2 * gd)), _const_spec((1, gd)), _const_spec((1, gd)),
                  _const_spec(ws.shape), _const_spec(bs.shape), _const_spec(wout_bf.shape)],
        out_specs=out_specs, out_shape=out_shape, scratch_shapes=scratch,
        compiler_params=_params("parallel"),
        name="gmlp",
    )(x2d, gain.reshape(1, d), win_bf, b_in.reshape(1, 2 * gd), ln_g.reshape(1, gd),
      ln_b.reshape(1, gd), ws, bs, wout_bf)
    return (outs[0], outs[1]) if return_v else (outs[0], None)


def _router_kernel(x_ref, g_ref, whi_ref, wlo_ref, b_ref, meta_ref, cnt_ref):
    h = _rmsnorm(x_ref[...], g_ref[...], RMS_EPS)
    h_hi = h.astype(BF16)
    h_lo = (h - h_hi.astype(F32)).astype(BF16)
    logits = (_dot(h_hi, whi_ref[...]) + _dot(h_hi, wlo_ref[...]) + _dot(h_lo, whi_ref[...])
              + b_ref[...])
    tm = logits.shape[0]
    lane = lax.broadcasted_iota(jnp.int32, logits.shape, 1)
    logits = jnp.where(lane < N_EXPERTS, logits, -jnp.inf)
    v1 = jnp.max(logits, axis=1, keepdims=True)
    i1 = jnp.min(jnp.where(logits == v1, lane, LANES), axis=1, keepdims=True)
    rest = jnp.where(lane == i1, -jnp.inf, logits)
    v2 = jnp.max(rest, axis=1, keepdims=True)
    i2 = jnp.min(jnp.where(rest == v2, lane, LANES), axis=1, keepdims=True)
    e2 = jnp.exp(v2 - v1)
    den = 1.0 + e2

    @pl.when(pl.program_id(0) == 0)
    def _init():
        cnt_ref[...] = jnp.zeros(cnt_ref.shape, F32)

    oh1 = lane == i1
    oh2 = lane == i2
    r_i = lax.broadcasted_iota(jnp.int32, (tm, tm), 0)
    c_i = lax.broadcasted_iota(jnp.int32, (tm, tm), 1)
    before = jnp.where(c_i < r_i, 1.0, 0.0).astype(BF16)
    cum1 = _dot(before, jnp.where(oh1, 1.0, 0.0).astype(BF16))
    cum2 = _dot(before, jnp.where(oh2, 1.0, 0.0).astype(BF16))
    n1 = jnp.sum(jnp.where(oh1, 1.0, 0.0), axis=0, keepdims=True)
    n2 = jnp.sum(jnp.where(oh2, 1.0, 0.0), axis=0, keepdims=True)
    base = cnt_ref[...]
    rank1 = jnp.sum(jnp.where(oh1, base + cum1, 0.0), axis=1, keepdims=True)
    rank2 = jnp.sum(jnp.where(oh2, base + n1 + cum2, 0.0), axis=1, keepdims=True)
    cnt_ref[...] = base + n1 + n2
    meta = jnp.where(lane == 0, i1.astype(F32), 0.0)
    meta = jnp.where(lane == 1, i2.astype(F32), meta)
    meta = jnp.where(lane == 2, 1.0 / den, meta)
    meta = jnp.where(lane == 3, e2 / den, meta)
    meta = jnp.where(lane == 4, rank1, meta)
    meta = jnp.where(lane == 5, rank2, meta)
    meta_ref[...] = meta


def _router(x2d, gain, w_router, b_router, tm):
    m, d = x2d.shape
    assert 2 * m < 2 ** 24
    w_pad = jnp.pad(w_router, ((0, 0), (0, LANES - N_EXPERTS)))
    w_hi = w_pad.astype(BF16)
    w_lo = (w_pad - w_hi.astype(F32)).astype(BF16)
    b_pad = jnp.pad(b_router, (0, LANES - N_EXPERTS)).reshape(1, LANES)
    row = lambda i: (i, 0)
    return pl.pallas_call(
        _router_kernel,
        grid=(m // tm,),
        in_specs=[pl.BlockSpec((tm, d), row), _const_spec((1, d)), _const_spec((d, LANES)),
                  _const_spec((d, LANES)), _const_spec((1, LANES))],
        out_specs=[pl.BlockSpec((tm, LANES), row), pl.BlockSpec((1, LANES), lambda i: (0, 0))],
        out_shape=[jax.ShapeDtypeStruct((m, LANES), F32), jax.ShapeDtypeStruct((1, LANES), F32)],
        compiler_params=_params("arbitrary"),
        name="moe_router",
    )(x2d, gain.reshape(1, d), w_hi, w_lo, b_pad)


def _row_copy(src_hbm, src_row, dst, dst_row, sem):
    return pltpu.make_async_copy(src_hbm.at[pl.ds(src_row, 1)], dst.at[pl.ds(dst_row, 1)], sem)


ROW_DMA_UNROLL = 8


def _dispatch_kernel(ends_ref, pos1_ref, pos2_ref, x_ref, g_ref, xs_hbm, h_ref, zero_ref, sem, zsem,
                     *, tm, tmx):
    @pl.when(pl.program_id(0) == 0)
    def _zero_padding():
        zero_ref[...] = jnp.zeros(zero_ref.shape, zero_ref.dtype)

        def fill(row0):
            cp = pltpu.make_async_copy(zero_ref, xs_hbm.at[pl.ds(row0, tmx)], zsem)
            cp.start()
            cp.wait()

        for e in range(N_EXPERTS):
            start_e = ends_ref[e - 1] if e else 0

            @pl.when(ends_ref[e] > start_e)
            def _():
                fill(pl.multiple_of(ends_ref[e] - tmx, tmx))

        def tail(t, carry):
            fill(pl.multiple_of(t * tmx, tmx))
            return carry

        lax.fori_loop(ends_ref[N_EXPERTS - 1] // tmx, xs_hbm.shape[0] // tmx, tail, 0)

    i = pl.program_id(0)
    slot = i % 2
    h_ref.at[slot][...] = _rmsnorm(x_ref[...], g_ref[...], RMS_EPS)

    def start(r, carry):
        _row_copy(h_ref.at[slot], r, xs_hbm, pos1_ref[r], sem.at[slot]).start()
        _row_copy(h_ref.at[slot], r, xs_hbm, pos2_ref[r], sem.at[slot]).start()
        return carry

    def wait_slot(sl):
        def wait(r, carry):
            _row_copy(h_ref.at[sl], 0, xs_hbm, 0, sem.at[sl]).wait()
            _row_copy(h_ref.at[sl], 0, xs_hbm, 0, sem.at[sl]).wait()
            return carry
        lax.fori_loop(0, tm, wait, 0, unroll=ROW_DMA_UNROLL)

    lax.fori_loop(0, tm, start, 0, unroll=ROW_DMA_UNROLL)

    @pl.when(i > 0)
    def _previous():
        wait_slot(1 - slot)

    @pl.when(i == pl.num_programs(0) - 1)
    def _own():
        wait_slot(slot)


def _dispatch(x2d, gain, pos1, pos2, ends, n_slots, tm, tmx):
    m, d = x2d.shape
    smem = lambda: pl.BlockSpec((tm,), lambda i, ends: (i,), memory_space=pltpu.SMEM)
    grid_spec = pltpu.PrefetchScalarGridSpec(
        num_scalar_prefetch=1,
        grid=(m // tm,),
        in_specs=[smem(), smem(), pl.BlockSpec((tm, d), lambda i, ends: (i, 0)),
                  pl.BlockSpec((1, d), lambda i, ends: (0, 0))],
        out_specs=pl.BlockSpec(memory_space=pl.ANY),
        scratch_shapes=[pltpu.VMEM((2, tm, d), F32), pltpu.VMEM((tmx, d), F32),
                        pltpu.SemaphoreType.DMA((2,)), pltpu.SemaphoreType.DMA(())],
    )
    return pl.pallas_call(
        functools.partial(_dispatch_kernel, tm=tm, tmx=tmx),
        grid_spec=grid_spec,
        out_shape=jax.ShapeDtypeStruct((n_slots, d), F32),
        compiler_params=_params("arbitrary"),
        name="moe_dispatch",
    )(ends, pos1, pos2, x2d, gain.reshape(1, d))


def _expert_ffn_kernel(te_ref, nt_ref, xs_ref, wgu_ref, wd_ref, ys_ref, *, d_ff, tf):
    @pl.when(pl.program_id(0) < nt_ref[0])
    def _():
        ys_ref[...] = _swiglu(xs_ref[...].astype(BF16), wgu_ref, wd_ref, d_ff, tf)

    @pl.when(pl.program_id(0) >= nt_ref[0])
    def _():
        ys_ref[...] = jnp.zeros(ys_ref.shape, F32)


def _expert_ffn(xs, tile_expert, n_tiles, wgu_bf, wd_bf, tmx):
    n_slots, d = xs.shape
    _, d_ff, _ = wd_bf.shape
    row = lambda i, te, nt: (jnp.minimum(i, nt[0] - 1), 0)
    grid_spec = pltpu.PrefetchScalarGridSpec(
        num_scalar_prefetch=2,
        grid=(n_slots // tmx,),
        in_specs=[pl.BlockSpec((tmx, d), row),
                  pl.BlockSpec((None, d, 2 * d_ff), lambda i, te, nt: (te[i], 0, 0)),
                  pl.BlockSpec((None, d_ff, d), lambda i, te, nt: (te[i], 0, 0))],
        out_specs=pl.BlockSpec((tmx, d), lambda i, te, nt: (i, 0)),
    )
    return pl.pallas_call(
        functools.partial(_expert_ffn_kernel, d_ff=d_ff, tf=256),
        grid_spec=grid_spec,
        out_shape=jax.ShapeDtypeStruct((n_slots, d), F32),
        compiler_params=_params("arbitrary"),
        name="moe_experts",
    )(tile_expert, n_tiles, xs, wgu_bf, wd_bf)


def _combine_kernel(pos1_ref, pos2_ref, nxt1_ref, nxt2_ref, ys_hbm, x_ref, meta_ref, gf_ref, o_ref,
                    y1_ref, y2_ref, sem, *, tm):
    i = pl.program_id(0)
    slot = i % 2

    def gather(p1_ref, p2_ref, sl):
        def start(r, carry):
            _row_copy(ys_hbm, p1_ref[r], y1_ref.at[sl], r, sem.at[sl]).start()
            _row_copy(ys_hbm, p2_ref[r], y2_ref.at[sl], r, sem.at[sl]).start()
            return carry
        lax.fori_loop(0, tm, start, 0, unroll=ROW_DMA_UNROLL)

    @pl.when(i == 0)
    def _first():
        gather(pos1_ref, pos2_ref, 0)

    @pl.when(i + 1 < pl.num_programs(0))
    def _next():
        gather(nxt1_ref, nxt2_ref, 1 - slot)

    def wait(r, carry):
        _row_copy(ys_hbm, 0, y1_ref.at[slot], 0, sem.at[slot]).wait()
        _row_copy(ys_hbm, 0, y2_ref.at[slot], 0, sem.at[slot]).wait()
        return carry

    lax.fori_loop(0, tm, wait, 0, unroll=ROW_DMA_UNROLL)
    meta = meta_ref[...]
    out = meta[:, 2:3] * y1_ref[slot] + meta[:, 3:4] * y2_ref[slot]
    o_ref[...] = _rmsnorm(x_ref[...] + out, gf_ref[...], RMS_EPS)


def _combine(ys, pos1, pos2, x2d, meta, gain_final, tm):
    m, d = x2d.shape
    last = m // tm - 1
    smem = lambda off: pl.BlockSpec((tm,), lambda i: (jnp.minimum(i + off, last),), memory_space=pltpu.SMEM)
    row = lambda i: (i, 0)
    return pl.pallas_call(
        functools.partial(_combine_kernel, tm=tm),
        grid=(m // tm,),
        in_specs=[smem(0), smem(0), smem(1), smem(1), pl.BlockSpec(memory_space=pl.ANY),
                  pl.BlockSpec((tm, d), row), pl.BlockSpec((tm, LANES), row), _const_spec((1, d))],
        out_specs=pl.BlockSpec((tm, d), row),
        out_shape=jax.ShapeDtypeStruct((m, d), F32),
        scratch_shapes=[pltpu.VMEM((2, tm, d), F32), pltpu.VMEM((2, tm, d), F32),
                        pltpu.SemaphoreType.DMA((2,))],
        compiler_params=_params("arbitrary"),
        name="moe_combine",
    )(pos1, pos2, pos1, pos2, ys, x2d, meta, gain_final.reshape(1, d))


def _moe(x2d, norm_gain, w_router, b_router, wgu_bf, wd_bf, gain_final, tm):
    m, d = x2d.shape
    tmx = 512 if m >= 4096 else 128
    meta, counts = _router(x2d, norm_gain, w_router, b_router, tm)
    counts = counts[0, :N_EXPERTS].astype(jnp.int32)
    padded = (counts + tmx - 1) // tmx * tmx
    ends = jnp.cumsum(padded)
    starts = ends - padded
    n_slots = 2 * m + N_EXPERTS * tmx
    n_tiles_max = n_slots // tmx
    tile_start = jnp.arange(n_tiles_max, dtype=jnp.int32) * tmx
    tile_expert = jnp.minimum(jnp.sum((tile_start[:, None] >= ends[None, :]).astype(jnp.int32), axis=1),
                              N_EXPERTS - 1).astype(jnp.int32)
    n_tiles = (ends[-1:] // tmx).astype(jnp.int32)
    experts = jnp.arange(N_EXPERTS, dtype=jnp.int32)[None, :]
    start_of = lambda e: jnp.sum(jnp.where(e[:, None] == experts, starts[None, :], 0), axis=1)
    e1, e2 = meta[:, 0].astype(jnp.int32), meta[:, 1].astype(jnp.int32)
    pos1 = start_of(e1) + meta[:, 4].astype(jnp.int32)
    pos2 = start_of(e2) + meta[:, 5].astype(jnp.int32)
    td = min(512, m)
    xs = _dispatch(x2d, norm_gain, pos1, pos2, ends.astype(jnp.int32), n_slots, td, tmx)
    ys = _expert_ffn(xs, tile_expert, n_tiles, wgu_bf, wd_bf, tmx)
    return _combine(ys, pos1, pos2, x2d, meta, gain_final, td)


def _forward(x, caches, w):
    b, t, d = x.shape
    m = b * t
    tm = min(512, m)
    x2d = x.reshape(m, d)
    lambda_init = 0.8 - 0.6 * math.exp(-0.3 * 0)
    width = DIFF_HEADS * 2 * DIFF_HEAD_DIM
    ncb = width // LANES

    if caches is None:
        pbf, ka, va, kb, vb, vt, kaug = _qkv_proj(x2d, w["norm_attn"], w["attn_w_in"], tm, seq=t)
    else:
        pbf, ka, va, kb, vb = _qkv_proj(x2d, w["norm_attn"], w["attn_w_in"], tm)
    pbf3 = pbf.reshape(b, t, 6 * width)
    if caches is None:
        oa = _diff_attn_prompt(pbf3, kaug.reshape(b, t, 2 * width), vt, 0, w["diff_lambda"],
                               w["diff_subln"], tile=min(1024, t), lambda_init=lambda_init)
        tqb = min(256, t)
        n_parts = BAND_PAST_MAX // tqb + 1
        bias = _band_bias(w["band_rel_bias"], 0, -BAND_PAST_MAX, tqb, n_parts * tqb, key_major=True)
        ob = _band_attn_prompt(pbf3, vt, 3, 4, 1, bias, tq=tqb, n_parts=n_parts, tkp=tqb)
        n_band = min(BAND_PAST_MAX, t)
        tail = lambda a: a.reshape(b, t, width)[:, t - n_band:].reshape(b, n_band, BAND_HEADS, BAND_HEAD_DIM)
        new_bk, new_bv = tail(kb), tail(vb)
    else:
        ck_a, cv_a, ck_b, cv_b = caches
        p_len, pb_len = ck_a.shape[1], ck_b.shape[1]
        cat = lambda c, lo: jnp.concatenate(
            [c.reshape(b, c.shape[1], width).astype(BF16), pbf3[:, :, lo:lo + width]], axis=1)
        k_a, v_a = cat(ck_a, width), cat(cv_a, 2 * width)
        k_b, v_b = cat(ck_b, 4 * width), cat(cv_b, 5 * width)
        oa = _diff_attn(pbf3, k_a, v_a, 0, 0, 0, w["diff_lambda"], w["diff_subln"],
                        tq=t, tk=p_len + t, q_off=p_len, lambda_init=lambda_init)
        bias = _band_bias(w["band_rel_bias"], p_len, p_len - pb_len, t, pb_len + t)
        ob = _band_attn(pbf3, k_b, v_b, 3 * ncb, 0, 0, bias, tq=t, n_parts=1, tkp=pb_len + t)
        new_bk = kb.reshape(b, t, BAND_HEADS, BAND_HEAD_DIM)
        new_bv = vb.reshape(b, t, BAND_HEADS, BAND_HEAD_DIM)
    new_dk = ka.reshape(b, t, 2 * DIFF_HEADS, DIFF_HEAD_DIM)
    new_dv = va.reshape(b, t, DIFF_HEADS, 2 * DIFF_HEAD_DIM)

    x2d = _attn_out_ffn(oa.reshape(m, width), ob.reshape(m, width), x2d, w["attn_w_out"],
                        w["norm_ffn"], w["ffn_w_gu"], w["ffn_w_down"], tm)

    seg = min(t, GMLP_CHUNK)
    x2d, v_rows = _gmlp(x2d, w["norm_gmlp"], w["gmlp_w_in"], w["gmlp_b_in"], w["gmlp_ln_g"],
                        w["gmlp_ln_b"], w["gmlp_w_s"], w["gmlp_b_s"], w["gmlp_w_out"], tm, seg,
                        return_v=caches is not None)
    y = _moe(x2d, w["norm_moe"], w["moe_w_router"], w["moe_b_router"], w["moe_w_gu"], w["moe_w_down"],
             w["norm_final"], tm)
    new_gv = None if v_rows is None else v_rows.reshape(b, t, -1)[None]
    return (y.reshape(b, t, d), new_dk[None], new_dv[None], new_bk[None], new_bv[None], new_gv)


def kernel(x_prompt, x_sample, cache_diff_k, cache_diff_v, cache_band_k, cache_band_v,
           norm_attn, attn_w_in, diff_lambda, diff_subln, band_rel_bias, attn_w_out,
           norm_ffn, ffn_w_gu, ffn_w_down,
           norm_gmlp, gmlp_w_in, gmlp_b_in, gmlp_ln_g, gmlp_ln_b, gmlp_w_s, gmlp_b_s, gmlp_w_out,
           norm_moe, moe_w_router, moe_b_router, moe_w_gu, moe_w_down, norm_final):
    w = {
        "norm_attn": norm_attn[0], "attn_w_in": attn_w_in[0].astype(BF16),
        "diff_lambda": diff_lambda[0], "diff_subln": diff_subln[0],
        "band_rel_bias": band_rel_bias[0], "attn_w_out": attn_w_out[0].astype(BF16),
        "norm_ffn": norm_ffn[0], "ffn_w_gu": ffn_w_gu[0].astype(BF16),
        "ffn_w_down": ffn_w_down[0].astype(BF16),
        "norm_gmlp": norm_gmlp[0], "gmlp_w_in": gmlp_w_in[0].astype(BF16),
        "gmlp_b_in": gmlp_b_in[0], "gmlp_ln_g": gmlp_ln_g[0], "gmlp_ln_b": gmlp_ln_b[0],
        "gmlp_w_s": gmlp_w_s[0], "gmlp_b_s": gmlp_b_s[0], "gmlp_w_out": gmlp_w_out[0].astype(BF16),
        "norm_moe": norm_moe[0], "moe_w_router": moe_w_router[0], "moe_b_router": moe_b_router[0],
        "moe_w_gu": moe_w_gu[0].astype(BF16), "moe_w_down": moe_w_down[0].astype(BF16),
        "norm_final": norm_final,
    }
    y_p, dk_p, dv_p, bk_p, bv_p, _ = _forward(x_prompt, None, w)
    y_s, dk_s, dv_s, bk_s, bv_s, gv_s = _forward(
        x_sample, (cache_diff_k[0], cache_diff_v[0], cache_band_k[0], cache_band_v[0]), w)
    return (y_p, y_s, dk_p, dv_p, bk_p, bv_p, dk_s, dv_s, bk_s, bv_s, gv_s)
```

```python
import functools
import math

import numpy as np
import jax
import jax.numpy as jnp
from jax import lax
from jax.experimental import pallas as pl
from jax.experimental.pallas import tpu as pltpu

CHUNK = 64
DIFF_HEADS = 4
DIFF_HEAD_DIM = 64
DIFF_SUBLN_EPS = 1e-5
BAND_HEADS = 8
BAND_HEAD_DIM = 64
BAND_PREV_CHUNKS = 8
BAND_PAST_MAX = BAND_PREV_CHUNKS * CHUNK
REL_CLIP = 256
GMLP_GROUPS = 8
GMLP_CHUNK = 128
N_EXPERTS = 8
RMS_EPS = 1e-6
LN_EPS = 1e-5
MASK_VALUE = -1e30

LANES = 128
ONES_ROWS = 16
VMEM_LIMIT_BYTES = 56 << 20

F32 = jnp.float32
BF16 = jnp.bfloat16
LOG2E = math.log2(math.e)
BF16_EXP2_ZERO = 140.0
KNORM_LANE = 6
ROUND_UP = 1.0 + 2.0 ** -6
Q_SCALE = DIFF_HEAD_DIM ** -0.5 * LOG2E
NT_DIMS = (((1,), (1,)), ((), ()))


def _params(*sem, flags=None):
    return pltpu.CompilerParams(dimension_semantics=sem, vmem_limit_bytes=VMEM_LIMIT_BYTES, flags=flags)


def _const_spec(shape):
    nd = len(shape)
    return pl.BlockSpec(shape, lambda *_: (0,) * nd, pipeline_mode=pl.Buffered(1))


def _rmsnorm(x, g, eps):
    return (x * lax.rsqrt(jnp.mean(x * x, axis=-1, keepdims=True) + eps)) * g


def _dot(a, b):
    return jnp.dot(a, b, preferred_element_type=F32)


def _swiglu(h, wgu_ref, wd_ref, d_ff, tf):
    y = None
    for c in range(d_ff // tf):
        g = _dot(h, wgu_ref[:, c * tf:(c + 1) * tf])
        u = _dot(h, wgu_ref[:, d_ff + c * tf:d_ff + (c + 1) * tf])
        a = (g * jax.nn.sigmoid(g) * u).astype(BF16)
        part = _dot(a, wd_ref[c * tf:(c + 1) * tf, :])
        y = part if y is None else y + part
    return y


def _qkv_kernel(x_ref, g_ref, w_ref, *rest, width, with_vt, tiles_per_seq):
    if with_vt:
        pbf_ref, ka_ref, va_ref, kb_ref, vb_ref, vt_ref, kaug_ref = rest
    else:
        pbf_ref, ka_ref, va_ref, kb_ref, vb_ref = rest
    tm = x_ref.shape[0]
    h = _rmsnorm(x_ref[...], g_ref[...], RMS_EPS).astype(BF16)
    f32_outs = {1: ka_ref, 2: va_ref, 4: kb_ref, 5: vb_ref}
    for c in range(6):
        r = _dot(h, w_ref[:, c * width:(c + 1) * width])
        if c in f32_outs:
            f32_outs[c][...] = r
        else:
            r = r * Q_SCALE
        pbf_ref[:, c * width:(c + 1) * width] = r.astype(BF16)
        if with_vt and c == 1:
            pos = (pl.program_id(0) % tiles_per_seq) * tm + lax.broadcasted_iota(jnp.int32, (tm, LANES), 0)
            lane = lax.broadcasted_iota(jnp.int32, (tm, LANES), 1)
            hi = (pos // LANES).astype(F32)
            lo = (pos % LANES).astype(F32)
            feats = jnp.where(lane < 3, hi, jnp.where(lane < 6, lo, 0.0))
            for hd in range(DIFF_HEADS):
                k_bf = r[:, hd * LANES:(hd + 1) * LANES].astype(BF16)
                kaug_ref[:, 2 * hd * LANES:(2 * hd + 1) * LANES] = k_bf
                k2 = jnp.square(k_bf.astype(F32))
                n1 = jnp.sqrt(jnp.sum(jnp.where(lane < DIFF_HEAD_DIM, k2, 0.0), axis=1, keepdims=True))
                n2 = jnp.sqrt(jnp.sum(jnp.where(lane >= DIFF_HEAD_DIM, k2, 0.0), axis=1, keepdims=True))
                f = jnp.where(lane == KNORM_LANE, n1 * ROUND_UP, jnp.where(lane == KNORM_LANE + 1, n2 * ROUND_UP, feats))
                kaug_ref[:, (2 * hd + 1) * LANES:(2 * hd + 2) * LANES] = f.astype(BF16)
        if with_vt and c in (2, 5):
            r0 = 0 if c == 2 else width
            vt_ref[r0:r0 + width, :] = r.T.astype(BF16)


def _qkv_proj(x2d, gain, w_bf, tm, seq=None):
    m, d = x2d.shape
    width = w_bf.shape[1] // 6
    row = lambda i: (i, 0)
    with_vt = seq is not None
    in_specs = [pl.BlockSpec((tm, d), row), _const_spec((1, d)), _const_spec(w_bf.shape)]
    out_specs = [pl.BlockSpec((tm, 6 * width), row)] + [pl.BlockSpec((tm, width), row)] * 4
    out_shape = [jax.ShapeDtypeStruct((m, 6 * width), BF16)] + [jax.ShapeDtypeStruct((m, width), F32)] * 4
    args = [x2d, gain.reshape(1, d), w_bf]
    nt = 1
    if with_vt:
        nt = seq // tm
        assert seq <= 256 * LANES
        vt_rows = 2 * width
        out_specs.append(pl.BlockSpec((None, vt_rows, tm), lambda i: (i // nt, 0, i % nt)))
        out_shape.append(jax.ShapeDtypeStruct((m // seq, vt_rows, seq), BF16))
        out_specs.append(pl.BlockSpec((tm, 2 * width), row))
        out_shape.append(jax.ShapeDtypeStruct((m, 2 * width), BF16))
    return pl.pallas_call(
        functools.partial(_qkv_kernel, width=width, with_vt=with_vt, tiles_per_seq=nt),
        grid=(m // tm,),
        in_specs=in_specs, out_specs=out_specs, out_shape=out_shape,
        compiler_params=_params("parallel"),
        name="qkv_proj",
    )(*args)


def _split_heads_rows(q, tq):
    qf = q.astype(F32)
    lane = lax.broadcasted_iota(jnp.int32, qf.shape, 1)
    lo = jnp.where(lane < DIFF_HEAD_DIM, qf, 0.0)
    hi = jnp.where(lane >= DIFF_HEAD_DIM, qf, 0.0)
    return jnp.concatenate([lo, hi], axis=0).astype(BF16)


def _diff_attn_kernel(ii_ref, jj_ref, last_ref, q_ref, k_ref, v_ref, slope_ref, lam_ref, g_ref,
                      o_ref, qz_ref, m_ref, l_ref, acc_ref, *, tq, tk, q_off, lambda_init):
    s = pl.program_id(2)
    i = ii_ref[s]
    j = jj_ref[s]

    @pl.when(j == 0)
    def _init():
        qz_ref[...] = _split_heads_rows(q_ref[...], tq)
        m_ref[...] = jnp.full(m_ref.shape, MASK_VALUE, F32)
        l_ref[...] = jnp.zeros(l_ref.shape, F32)
        acc_ref[...] = jnp.zeros(acc_ref.shape, F32)

    sc = lax.dot_general(qz_ref[...], k_ref[...], NT_DIMS, preferred_element_type=F32)
    row = lax.broadcasted_iota(jnp.int32, (2 * tq, 1), 0)
    row = jnp.where(row >= tq, row - tq, row)
    qpos = q_off + i * tq + row
    kpos = j * tk + lax.broadcasted_iota(jnp.int32, (1, tk), 1)
    dist = jnp.abs(qpos - kpos).astype(F32)
    sc = sc - slope_ref[:, 0:1] * dist
    visible = (kpos // CHUNK) <= (qpos // CHUNK)
    sc = jnp.where(visible, sc, MASK_VALUE)

    m_prev = m_ref[...]
    m_new = jnp.maximum(m_prev, jnp.max(sc, axis=1, keepdims=True))
    alpha = jnp.exp2(m_prev - m_new)
    p = jnp.exp2(sc - m_new)
    l_ref[...] = alpha * l_ref[...] + jnp.sum(p, axis=1, keepdims=True)
    acc_ref[...] = alpha * acc_ref[...] + _dot(p.astype(BF16), v_ref[...])
    m_ref[...] = m_new

    @pl.when(last_ref[s] == 1)
    def _finish():
        lp = lam_ref[...]
        lam = (jnp.exp(jnp.sum(lp[0:1] * lp[1:2], axis=1, keepdims=True))
               - jnp.exp(jnp.sum(lp[2:3] * lp[3:4], axis=1, keepdims=True)) + lambda_init)
        o_all = acc_ref[...] / l_ref[...]
        o = o_all[:tq] - lam * o_all[tq:]
        o = _rmsnorm(o, g_ref[...], DIFF_SUBLN_EPS) * (1.0 - lambda_init)
        o_ref[...] = o.astype(o_ref.dtype)


def _diff_attn(q_arr, k_arr, v_arr, q_cb, k_cb, v_cb, lam_p, subln_g, *, tq, tk, q_off, lambda_init):
    b, t_q, _ = q_arr.shape
    t_k = k_arr.shape[1]
    nq, nk = t_q // tq, t_k // tk
    pairs = [(i, j) for i in range(nq) for j in range(nk)
             if (j * tk) // CHUNK <= (q_off + i * tq + tq - 1) // CHUNK]
    ii = np.array([p[0] for p in pairs], np.int32)
    jj = np.array([p[1] for p in pairs], np.int32)
    last = np.array([1 if (n + 1 == len(pairs) or pairs[n + 1][0] != pairs[n][0]) else 0
                     for n in range(len(pairs))], np.int32)
    slopes = jnp.asarray(_alibi_slopes_log2()[:, None, None] * np.ones((1, 1, LANES), np.float32))

    grid_spec = pltpu.PrefetchScalarGridSpec(
        num_scalar_prefetch=3,
        grid=(b, DIFF_HEADS, len(pairs)),
        in_specs=[
            pl.BlockSpec((None, tq, LANES), lambda b_, h, s, ii, jj, la: (b_, ii[s], q_cb + h)),
            pl.BlockSpec((None, tk, LANES), lambda b_, h, s, ii, jj, la: (b_, jj[s], k_cb + h)),
            pl.BlockSpec((None, tk, LANES), lambda b_, h, s, ii, jj, la: (b_, jj[s], v_cb + h)),
            pl.BlockSpec((None, 1, LANES), lambda b_, h, s, ii, jj, la: (h, 0, 0)),
            pl.BlockSpec((4, DIFF_HEAD_DIM), lambda b_, h, s, ii, jj, la: (0, 0)),
            pl.BlockSpec((1, LANES), lambda b_, h, s, ii, jj, la: (0, 0)),
        ],
        out_specs=pl.BlockSpec((None, tq, LANES), lambda b_, h, s, ii, jj, la: (b_, ii[s], h)),
        scratch_shapes=[
            pltpu.VMEM((2 * tq, LANES), BF16),
            pltpu.VMEM((2 * tq, 1), F32),
            pltpu.VMEM((2 * tq, 1), F32),
            pltpu.VMEM((2 * tq, LANES), F32),
        ],
    )
    return pl.pallas_call(
        functools.partial(_diff_attn_kernel, tq=tq, tk=tk, q_off=q_off, lambda_init=lambda_init),
        grid_spec=grid_spec,
        out_shape=jax.ShapeDtypeStruct((b, t_q, DIFF_HEADS * LANES), BF16),
        compiler_params=_params("parallel", "parallel", "arbitrary"),
        name="diff_attn",
    )(jnp.asarray(ii), jnp.asarray(jj), jnp.asarray(last), q_arr, k_arr, v_arr, slopes,
      lam_p, subln_g.reshape(1, LANES))


def _alibi_slopes_log2():
    return (2.0 ** (-8.0 * np.arange(1, DIFF_HEADS + 1, dtype=np.float64) / DIFF_HEADS) * LOG2E).astype(np.float32)


def _diff_prompt_kernel(ii_ref, jj_ref, last_ref, lin_ref, q_ref, k_ref, vt_ref, dbias_ref, qfeat_ref,
                        lam_ref, g_ref, o_ref, qzt_ref, qn_ref, s0_ref, s1_ref, mx0_ref, mx1_ref, m_ref,
                        acc_ref, *, tq, tk, cb, lambda_init):
    s = pl.program_id(2)
    n_pairs = pl.num_programs(2) - 1
    sp = jnp.minimum(s, n_pairs - 1)
    sc = jnp.maximum(s - 1, 0)
    consuming = s > 0
    nblk = 2 * tq // cb
    blk = lambda c: slice(c * cb, (c + 1) * cb)

    @pl.when(s == 0)
    def _first_step():
        s1_ref[...] = jnp.zeros(s1_ref.shape, F32)
        mx1_ref[...] = jnp.zeros(mx1_ref.shape, F32)

    @pl.when(jj_ref[sp] == ii_ref[sp])
    def _new_queries():
        qf = q_ref[...].astype(F32)
        lane = lax.broadcasted_iota(jnp.int32, qf.shape, 1)
        qzt_ref[0:LANES, 0:tq] = jnp.where(lane < DIFF_HEAD_DIM, qf, 0.0).T.astype(BF16)
        qzt_ref[0:LANES, tq:2 * tq] = jnp.where(lane >= DIFF_HEAD_DIM, qf, 0.0).T.astype(BF16)
        qzt_ref[LANES:2 * LANES, :] = jnp.broadcast_to(qfeat_ref[...], (LANES, 2 * tq)).astype(BF16)
        qn_ref[...] = jnp.sqrt(jnp.sum(jnp.square(qzt_ref[0:LANES, :].astype(F32)), axis=0, keepdims=True))

    @pl.when(jj_ref[sc] == ii_ref[sc])
    def _reset():
        m_ref[...] = jnp.full(m_ref.shape, MASK_VALUE, F32)
        acc_ref[...] = jnp.zeros(acc_ref.shape, F32)

    kfeat_max = jnp.max(k_ref[:, LANES:2 * LANES].astype(F32), axis=0, keepdims=True)
    knorm = jnp.concatenate([jnp.broadcast_to(kfeat_max[:, KNORM_LANE:KNORM_LANE + 1], (1, tq)),
                             jnp.broadcast_to(kfeat_max[:, KNORM_LANE + 1:KNORM_LANE + 2], (1, tq))], axis=1)
    slope = jnp.sum(qfeat_ref[3:6, :], axis=0, keepdims=True)
    kpos_max = ((jj_ref[sp] + 1) * tk - 1).astype(F32)
    bound = qn_ref[...] * knorm * ROUND_UP + slope * kpos_max
    out_of_reach = jnp.max(bound - m_ref[...]) < -BF16_EXP2_ZERO
    skip_produce = jnp.logical_or(s == n_pairs, jnp.logical_and(lin_ref[sp] == 1, out_of_reach))

    def stages(s_prod, mx_prod, s_cons, mx_cons):
        @pl.when(jnp.logical_and(consuming, lin_ref[sc] == 0))
        def _diagonal():
            for c in range(nblk):
                v = s_cons[:, blk(c)] + dbias_ref[:, (c * cb) % tq:(c * cb) % tq + cb]
                s_cons[:, blk(c)] = v
                mx_cons[:, blk(c)] = jnp.max(v, axis=0, keepdims=True)

        def produce():
            k = k_ref[...]
            for c in range(nblk):
                v = _dot(k, qzt_ref[:, blk(c)])
                s_prod[:, blk(c)] = v
                mx_prod[:, blk(c)] = jnp.max(v, axis=0, keepdims=True)

        def consume():
            vt = jnp.concatenate([vt_ref[...], jnp.ones((ONES_ROWS, tk), BF16)], axis=0)
            m_prev = [m_ref[:, blk(c)] for c in range(nblk)]
            acc_prev = [acc_ref[:, blk(c)] for c in range(nblk)]
            m_out, acc_out = [], []
            for c in range(nblk):
                m_new = jnp.maximum(m_prev[c], mx_cons[:, blk(c)])
                alpha = jnp.exp2(m_prev[c] - m_new)
                p = jnp.exp2((s_cons[:, blk(c)] - m_new).astype(BF16))
                acc_out.append(alpha * acc_prev[c] + _dot(vt, p))
                m_out.append(m_new)
            for c in range(nblk):
                m_ref[:, blk(c)] = m_out[c]
                acc_ref[:, blk(c)] = acc_out[c]

        negligible = jnp.max(mx_cons[...] - m_ref[...]) < -BF16_EXP2_ZERO
        idle = jnp.logical_and(skip_produce, negligible)

        @pl.when(idle)
        def _mark_skipped():
            mx_prod[...] = jnp.full(mx_prod.shape, MASK_VALUE, F32)

        @pl.when(jnp.logical_and(jnp.logical_not(idle), negligible))
        def _produce_only():
            produce()

        @pl.when(jnp.logical_not(negligible))
        def _both():
            produce()
            consume()

    @pl.when(s % 2 == 0)
    def _even():
        stages(s0_ref, mx0_ref, s1_ref, mx1_ref)

    @pl.when(s % 2 == 1)
    def _odd():
        stages(s1_ref, mx1_ref, s0_ref, mx0_ref)

    @pl.when(jnp.logical_and(consuming, last_ref[sc] == 1))
    def _finish():
        lp = lam_ref[...]
        lam = (jnp.exp(jnp.sum(lp[0:1] * lp[1:2], axis=1, keepdims=True))
               - jnp.exp(jnp.sum(lp[2:3] * lp[3:4], axis=1, keepdims=True)) + lambda_init)
        o_all = acc_ref[0:LANES, :] / acc_ref[LANES:LANES + 1, :]
        o = o_all[:, 0:tq] - lam * o_all[:, tq:2 * tq]
        o = o * lax.rsqrt(jnp.mean(o * o, axis=0, keepdims=True) + DIFF_SUBLN_EPS)
        o = o * g_ref[...] * (1.0 - lambda_init)
        o_ref[...] = o.T.astype(o_ref.dtype)


def _bf16_split3(x):
    import ml_dtypes
    rnd = lambda v: v.astype(ml_dtypes.bfloat16).astype(np.float32)
    x = np.asarray(x, np.float32)
    hi = rnd(x)
    mid = rnd(x - hi)
    lo = rnd(x - hi - mid)
    return hi, mid, lo


def _diff_attn_prompt(pbf3, kaug3, vt_arr, q_cb, lam_p, subln_g, *, tile, lambda_init):
    b, t, _ = pbf3.shape
    tq = tk = tile
    n = t // tile
    pairs = [(i, j) for i in range(n) for j in range(i, -1, -1)]
    ii = np.array([p[0] for p in pairs], np.int32)
    jj = np.array([p[1] for p in pairs], np.int32)
    last = (jj == 0).astype(np.int32)
    lin = (ii != jj).astype(np.int32)
    c2 = _alibi_slopes_log2()
    qfeat = np.zeros((DIFF_HEADS, LANES, 1), np.float32)
    for r, part in enumerate(_bf16_split3(c2)):
        qfeat[:, r, 0] = part * LANES
        qfeat[:, 3 + r, 0] = part
    pos = np.arange(tile)
    d = (pos[None, :] - pos[:, None]).astype(np.float32)
    visible = (pos[:, None] // CHUNK) <= (pos[None, :] // CHUNK)
    dbias = jnp.asarray(np.where(visible[None], 2.0 * c2[:, None, None] * np.minimum(d, 0.0)[None],
                                 MASK_VALUE).astype(np.float32))
    cb = min(256, 2 * tq)
    n_pairs = len(pairs)
    prod = lambda s: jnp.minimum(s, n_pairs - 1)
    cons = lambda s: jnp.maximum(s - 1, 0)
    idx = lambda f: (lambda b_, h, s, ii, jj, la, li: f(b_, h, s, ii, jj))
    grid_spec = pltpu.PrefetchScalarGridSpec(
        num_scalar_prefetch=4,
        grid=(b, DIFF_HEADS, n_pairs + 1),
        in_specs=[
            pl.BlockSpec((None, tq, LANES), idx(lambda b_, h, s, ii, jj: (b_, ii[prod(s)], q_cb + h))),
            pl.BlockSpec((None, tk, 2 * LANES), idx(lambda b_, h, s, ii, jj: (b_, jj[prod(s)], h))),
            pl.BlockSpec((None, LANES, tk), idx(lambda b_, h, s, ii, jj: (b_, h, jj[cons(s)]))),
            pl.BlockSpec((None, tk, tq), idx(lambda b_, h, s, ii, jj: (h, 0, 0))),
            pl.BlockSpec((None, LANES, 1), idx(lambda b_, h, s, ii, jj: (h, 0, 0))),
            pl.BlockSpec((4, DIFF_HEAD_DIM), idx(lambda b_, h, s, ii, jj: (0, 0))),
            pl.BlockSpec((LANES, 1), idx(lambda b_, h, s, ii, jj: (0, 0))),
        ],
        out_specs=pl.BlockSpec((None, tq, LANES), idx(lambda b_, h, s, ii, jj: (b_, ii[cons(s)], h))),
        scratch_shapes=[
            pltpu.VMEM((2 * LANES, 2 * tq), BF16),
            pltpu.VMEM((1, 2 * tq), F32),
            pltpu.VMEM((tk, 2 * tq), F32), pltpu.VMEM((tk, 2 * tq), F32),
            pltpu.VMEM((1, 2 * tq), F32), pltpu.VMEM((1, 2 * tq), F32),
            pltpu.VMEM((1, 2 * tq), F32),
            pltpu.VMEM((LANES + ONES_ROWS, 2 * tq), F32),
        ],
    )
    return pl.pallas_call(
        functools.partial(_diff_prompt_kernel, tq=tq, tk=tk, cb=cb, lambda_init=lambda_init),
        grid_spec=grid_spec,
        out_shape=jax.ShapeDtypeStruct((b, t, DIFF_HEADS * LANES), BF16),
        compiler_params=_params("parallel", "parallel", "arbitrary"),
        name="diff_attn_prompt",
    )(jnp.asarray(ii), jnp.asarray(jj), jnp.asarray(last), jnp.asarray(lin), pbf3, kaug3, vt_arr, dbias,
      jnp.asarray(qfeat), lam_p, subln_g.reshape(LANES, 1))


def _band_attn_kernel(*refs, tq, n_parts, tkp):
    q_ref = refs[0]
    k_refs = refs[1:1 + n_parts]
    v_refs = refs[1 + n_parts:1 + 2 * n_parts]
    bias_ref = refs[1 + 2 * n_parts]
    o_ref = refs[2 + 2 * n_parts]
    i = pl.program_id(2)
    qz = _split_heads_rows(q_ref[...], tq)
    scores = []
    for m in range(n_parts):
        sc = lax.dot_general(qz, k_refs[m][...], NT_DIMS, preferred_element_type=F32)
        sc = sc + bias_ref[:, m * tkp:(m + 1) * tkp]
        if n_parts > 1:
            sc = jnp.where(i - (n_parts - 1) + m >= 0, sc, MASK_VALUE)
        scores.append(sc)
    mx = functools.reduce(jnp.maximum, [jnp.max(sc, axis=1, keepdims=True) for sc in scores])
    den = None
    num = None
    for m in range(n_parts):
        p = jnp.exp2(scores[m] - mx)
        d = jnp.sum(p, axis=1, keepdims=True)
        r = _dot(p.astype(BF16), v_refs[m][...])
        den = d if den is None else den + d
        num = r if num is None else num + r
    r = num / den
    lane = lax.broadcasted_iota(jnp.int32, (tq, LANES), 1)
    o_ref[...] = jnp.where(lane < BAND_HEAD_DIM, r[:tq], r[tq:]).astype(o_ref.dtype)


def _band_bias_kernel(line_ref, o_ref, *, q0, k0):
    tq, nk = o_ref.shape
    line = jnp.broadcast_to(line_ref[...], (tq, line_ref.shape[1]))
    rolled = pltpu.roll(line, 0, 1, stride=1, stride_axis=0)
    qc = (q0 + lax.broadcasted_iota(jnp.int32, (tq, nk), 0)) >> 6
    kc = (k0 + lax.broadcasted_iota(jnp.int32, (tq, nk), 1)) >> 6
    visible = (kc <= qc) & (qc - kc <= BAND_PREV_CHUNKS)
    o_ref[...] = jnp.where(visible, rolled[:, 0:nk], MASK_VALUE)


def _band_bias_t_kernel(line_ref, o_ref, *, q0, k0):
    nk, tq = o_ref.shape
    line = jnp.broadcast_to(line_ref[...], (nk, line_ref.shape[1]))
    rolled = pltpu.roll(line, 0, 1, stride=1, stride_axis=0)
    kc = (k0 + lax.broadcasted_iota(jnp.int32, (nk, tq), 0)) >> 6
    qc = (q0 + lax.broadcasted_iota(jnp.int32, (nk, tq), 1)) >> 6
    visible = (kc <= qc) & (qc - kc <= BAND_PREV_CHUNKS)
    o_ref[...] = jnp.where(visible, rolled[:, 0:tq], MASK_VALUE)


def _band_bias(rel_table, q0, k0, tq, nk, key_major=False):
    assert CHUNK == 64
    width = pl.next_power_of_2(tq + nk - 1)
    y = np.arange(width)
    c_minus_r = np.where(y < nk, y, y - width)
    rel = np.clip((q0 - k0) - c_minus_r, -(CHUNK - 1), REL_CLIP) + (CHUNK - 1)
    if key_major:
        rel = rel[(-y) % width]
    line = (rel_table.astype(F32) * LOG2E)[:, rel].reshape(BAND_HEADS, 1, width)
    if key_major:
        kern, block, shape = _band_bias_t_kernel, (None, nk, tq), (BAND_HEADS // 2, nk, 2 * tq)
        index = lambda h: (h // 2, 0, h % 2)
    else:
        kern, block, shape = _band_bias_kernel, (None, tq, nk), (BAND_HEADS // 2, 2 * tq, nk)
        index = lambda h: (h // 2, h % 2, 0)
    return pl.pallas_call(
        functools.partial(kern, q0=q0, k0=k0),
        grid=(BAND_HEADS,),
        in_specs=[pl.BlockSpec((None, 1, width), lambda h: (h, 0, 0))],
        out_specs=pl.BlockSpec(block, index),
        out_shape=jax.ShapeDtypeStruct(shape, F32),
        compiler_params=_params("parallel"),
        name="band_bias",
    )(line)


def _band_prompt_kernel(*refs, tq, n_parts, tkp):
    q_ref = refs[0]
    k_refs = refs[1:1 + n_parts]
    vt_refs = refs[1 + n_parts:1 + 2 * n_parts]
    bias_ref = refs[1 + 2 * n_parts]
    o_ref = refs[2 + 2 * n_parts]
    nk = n_parts * tkp
    hd = BAND_HEAD_DIM
    i = pl.program_id(1)
    ones = jnp.ones((ONES_ROWS, nk), BF16)
    for pr in range(BAND_HEADS // 2):
        ln = slice(pr * LANES, (pr + 1) * LANES)
        qf = q_ref[:, ln].astype(F32)
        lane = lax.broadcasted_iota(jnp.int32, qf.shape, 1)
        qzt = jnp.concatenate([jnp.where(lane < hd, qf, 0.0).T, jnp.where(lane >= hd, qf, 0.0).T],
                              axis=1).astype(BF16)
        k = jnp.concatenate([r[:, ln] for r in k_refs], axis=0)
        sc = _dot(k, qzt) + bias_ref[pr]
        row = lax.broadcasted_iota(jnp.int32, sc.shape, 0)
        sc = jnp.where(row >= (n_parts - 1 - i) * tkp, sc, MASK_VALUE)
        mx = jnp.max(sc, axis=0, keepdims=True)
        p = jnp.exp2((sc - mx).astype(BF16))
        vt = jnp.concatenate([r[ln, :] for r in vt_refs] , axis=1)
        r = _dot(jnp.concatenate([vt, ones], axis=0), p)
        o = r[0:LANES, :] / r[LANES:LANES + 1, :]
        o = jnp.concatenate([o[0:hd, 0:tq], o[hd:2 * hd, tq:2 * tq]], axis=0)
        o_ref[:, ln] = o.T.astype(o_ref.dtype)


def _band_attn_prompt(pbf3, vt_arr, q_cb, k_cb, vt_rb, bias_t, *, tq, n_parts, tkp):
    b, t, _ = pbf3.shape
    width = (BAND_HEADS // 2) * LANES
    part = lambda i, m: jnp.maximum(i - (n_parts - 1) + m, 0)
    k_spec = lambda m: pl.BlockSpec((None, tkp, width), lambda b_, i: (b_, part(i, m), k_cb))
    vt_spec = lambda m: pl.BlockSpec((None, width, tkp), lambda b_, i: (b_, vt_rb, part(i, m)))
    return pl.pallas_call(
        functools.partial(_band_prompt_kernel, tq=tq, n_parts=n_parts, tkp=tkp),
        grid=(b, t // tq),
        in_specs=[pl.BlockSpec((None, tq, width), lambda b_, i: (b_, i, q_cb))]
        + [k_spec(m) for m in range(n_parts)] + [vt_spec(m) for m in range(n_parts)]
        + [_const_spec(bias_t.shape)],
        out_specs=pl.BlockSpec((None, tq, width), lambda b_, i: (b_, i, 0)),
        out_shape=jax.ShapeDtypeStruct((b, t, width), BF16),
        compiler_params=_params("parallel", "parallel"),
        name="band_attn_prompt",
    )(pbf3, *([pbf3] * n_parts), *([vt_arr] * n_parts), bias_t)


def _band_attn(q_arr, k_arr, v_arr, q_cb, k_cb, v_cb, bias, *, tq, n_parts, tkp):
    b, t_q, _ = q_arr.shape
    n_pairs = BAND_HEADS // 2

    def kv_spec(cb, m):
        return pl.BlockSpec((None, tkp, LANES),
                            lambda p, b_, i: (b_, jnp.maximum(i - (n_parts - 1) + m, 0), cb + p))

    return pl.pallas_call(
        functools.partial(_band_attn_kernel, tq=tq, n_parts=n_parts, tkp=tkp),
        grid=(n_pairs, b, t_q // tq),
        in_specs=[pl.BlockSpec((None, tq, LANES), lambda p, b_, i: (b_, i, q_cb + p))]
        + [kv_spec(k_cb, m) for m in range(n_parts)]
        + [kv_spec(v_cb, m) for m in range(n_parts)]
        + [pl.BlockSpec((None, 2 * tq, n_parts * tkp), lambda p, b_, i: (p, 0, 0))],
        out_specs=pl.BlockSpec((None, tq, LANES), lambda p, b_, i: (b_, i, p)),
        out_shape=jax.ShapeDtypeStruct((b, t_q, n_pairs * LANES), BF16),
        compiler_params=_params("parallel", "parallel", "parallel"),
        name="band_attn",
    )(q_arr, *([k_arr] * n_parts), *([v_arr] * n_parts), bias)


def _attn_out_ffn_kernel(oa_ref, ob_ref, x_ref, wo_ref, g_ref, wgu_ref, wd_ref, o_ref, *, d_ff, tf):
    half = oa_ref.shape[1]
    x1 = x_ref[...] + _dot(oa_ref[...], wo_ref[0:half, :]) + _dot(ob_ref[...], wo_ref[half:2 * half, :])
    h = _rmsnorm(x1, g_ref[...], RMS_EPS).astype(BF16)
    o_ref[...] = x1 + _swiglu(h, wgu_ref, wd_ref, d_ff, tf)


def _attn_out_ffn(oa, ob, x2d, wo_bf, gain, wgu_bf, wd_bf, tm):
    m, d = x2d.shape
    d_ff = wd_bf.shape[0]
    row = lambda i: (i, 0)
    return pl.pallas_call(
        functools.partial(_attn_out_ffn_kernel, d_ff=d_ff, tf=256),
        grid=(m // tm,),
        in_specs=[pl.BlockSpec((tm, oa.shape[1]), row), pl.BlockSpec((tm, ob.shape[1]), row),
                  pl.BlockSpec((tm, d), row), _const_spec(wo_bf.shape), _const_spec((1, d)),
                  _const_spec(wgu_bf.shape), _const_spec(wd_bf.shape)],
        out_specs=pl.BlockSpec((tm, d), row),
        out_shape=jax.ShapeDtypeStruct((m, d), F32),
        compiler_params=_params("parallel"),
        name="attn_out_ffn",
    )(oa, ob, x2d, wo_bf, gain.reshape(1, d), wgu_bf, wd_bf)


def _gelu(z):
    return 0.5 * z * (1.0 + lax.erf(z * (2.0 ** -0.5)))


def _gmlp_kernel(x_ref, g_ref, win_ref, bin_ref, lng_ref, lnb_ref, ws_ref, bs_ref, wout_ref,
                 o_ref, v_ref, u_s, act_s, *, seg, cw):
    tm, gd = v_ref.shape
    gw = gd // GMLP_GROUPS
    x = x_ref[...]
    h = _rmsnorm(x, g_ref[...], RMS_EPS).astype(BF16)
    for c in range(gd // cw):
        lo, hi = c * cw, (c + 1) * cw
        u_s[:, lo:hi] = _gelu(_dot(h, win_ref[:, lo:hi]) + bin_ref[:, lo:hi])
        v_ref[:, lo:hi] = _gelu(_dot(h, win_ref[:, gd + lo:gd + hi]) + bin_ref[:, gd + lo:gd + hi])
    v = v_ref[...]
    mu = jnp.mean(v, axis=-1, keepdims=True)
    var = jnp.mean(jnp.square(v - mu), axis=-1, keepdims=True)
    v_ref[...] = (v - mu) * lax.rsqrt(var + LN_EPS) * lng_ref[...] + lnb_ref[...]
    r_i = lax.broadcasted_iota(jnp.int32, (seg, seg), 0)
    c_i = lax.broadcasted_iota(jnp.int32, (seg, seg), 1)
    for g in range(GMLP_GROUPS):
        w = jnp.where(r_i >= c_i, ws_ref[g], 0.0).astype(BF16)
        for n in range(tm // seg):
            rows = slice(n * seg, (n + 1) * seg)
            cols = slice(g * gw, (g + 1) * gw)
            sv = _dot(w, v_ref[rows, cols].astype(BF16)) + bs_ref[g]
            act_s[rows, cols] = (u_s[rows, cols] * sv).astype(BF16)
    o_ref[...] = x + _dot(act_s[...], wout_ref[...])


def _gmlp(x2d, gain, win_bf, b_in, ln_g, ln_b, w_s, b_s, wout_bf, tm, seg, return_v):
    m, d = x2d.shape
    gd = wout_bf.shape[0]
    row = lambda i: (i, 0)
    ws = w_s[:, :seg, :seg]
    bs = b_s[:, :seg, None]
    out_specs = [pl.BlockSpec((tm, d), row)]
    out_shape = [jax.ShapeDtypeStruct((m, d), F32)]
    scratch = [pltpu.VMEM((tm, gd), F32), pltpu.VMEM((tm, gd), BF16)]
    if return_v:
        out_specs.append(pl.BlockSpec((tm, gd), row))
        out_shape.append(jax.ShapeDtypeStruct((m, gd), F32))
    else:
        scratch.insert(0, pltpu.VMEM((tm, gd), F32))
    outs = pl.pallas_call(
        functools.partial(_gmlp_kernel, seg=seg, cw=512),
        grid=(m // tm,),
        in_specs=[pl.BlockSpec((tm, d), row), _const_spec((1, d)), _const_spec(win_bf.shape),
                  _const_spec((1, 2 * gd)), _const_spec((1, gd)), _const_spec((1, gd)),
                  _const_spec(ws.shape), _const_spec(bs.shape), _const_spec(wout_bf.shape)],
        out_specs=out_specs, out_shape=out_shape, scratch_shapes=scratch,
        compiler_params=_params("parallel"),
        name="gmlp",
    )(x2d, gain.reshape(1, d), win_bf, b_in.reshape(1, 2 * gd), ln_g.reshape(1, gd),
      ln_b.reshape(1, gd), ws, bs, wout_bf)
    return (outs[0], outs[1]) if return_v else (outs[0], None)


def _router_kernel(x_ref, g_ref, whi_ref, wlo_ref, b_ref, meta_ref, cnt_ref):
    h = _rmsnorm(x_ref[...], g_ref[...], RMS_EPS)
    h_hi = h.astype(BF16)
    h_lo = (h - h_hi.astype(F32)).astype(BF16)
    logits = (_dot(h_hi, whi_ref[...]) + _dot(h_hi, wlo_ref[...]) + _dot(h_lo, whi_ref[...])
              + b_ref[...])
    tm = logits.shape[0]
    lane = lax.broadcasted_iota(jnp.int32, logits.shape, 1)
    logits = jnp.where(lane < N_EXPERTS, logits, -jnp.inf)
    v1 = jnp.max(logits, axis=1, keepdims=True)
    i1 = jnp.min(jnp.where(logits == v1, lane, LANES), axis=1, keepdims=True)
    rest = jnp.where(lane == i1, -jnp.inf, logits)
    v2 = jnp.max(rest, axis=1, keepdims=True)
    i2 = jnp.min(jnp.where(rest == v2, lane, LANES), axis=1, keepdims=True)
    e2 = jnp.exp(v2 - v1)
    den = 1.0 + e2

    @pl.when(pl.program_id(0) == 0)
    def _init():
        cnt_ref[...] = jnp.zeros(cnt_ref.shape, F32)

    oh1 = lane == i1
    oh2 = lane == i2
    r_i = lax.broadcasted_iota(jnp.int32, (tm, tm), 0)
    c_i = lax.broadcasted_iota(jnp.int32, (tm, tm), 1)
    before = jnp.where(c_i < r_i, 1.0, 0.0).astype(BF16)
    cum1 = _dot(before, jnp.where(oh1, 1.0, 0.0).astype(BF16))
    cum2 = _dot(before, jnp.where(oh2, 1.0, 0.0).astype(BF16))
    n1 = jnp.sum(jnp.where(oh1, 1.0, 0.0), axis=0, keepdims=True)
    n2 = jnp.sum(jnp.where(oh2, 1.0, 0.0), axis=0, keepdims=True)
    base = cnt_ref[...]
    rank1 = jnp.sum(jnp.where(oh1, base + cum1, 0.0), axis=1, keepdims=True)
    rank2 = jnp.sum(jnp.where(oh2, base + n1 + cum2, 0.0), axis=1, keepdims=True)
    cnt_ref[...] = base + n1 + n2
    meta = jnp.where(lane == 0, i1.astype(F32), 0.0)
    meta = jnp.where(lane == 1, i2.astype(F32), meta)
    meta = jnp.where(lane == 2, 1.0 / den, meta)
    meta = jnp.where(lane == 3, e2 / den, meta)
    meta = jnp.where(lane == 4, rank1, meta)
    meta = jnp.where(lane == 5, rank2, meta)
    meta_ref[...] = meta


def _router(x2d, gain, w_router, b_router, tm):
    m, d = x2d.shape
    assert 2 * m < 2 ** 24
    w_pad = jnp.pad(w_router, ((0, 0), (0, LANES - N_EXPERTS)))
    w_hi = w_pad.astype(BF16)
    w_lo = (w_pad - w_hi.astype(F32)).astype(BF16)
    b_pad = jnp.pad(b_router, (0, LANES - N_EXPERTS)).reshape(1, LANES)
    row = lambda i: (i, 0)
    return pl.pallas_call(
        _router_kernel,
        grid=(m // tm,),
        in_specs=[pl.BlockSpec((tm, d), row), _const_spec((1, d)), _const_spec((d, LANES)),
                  _const_spec((d, LANES)), _const_spec((1, LANES))],
        out_specs=[pl.BlockSpec((tm, LANES), row), pl.BlockSpec((1, LANES), lambda i: (0, 0))],
        out_shape=[jax.ShapeDtypeStruct((m, LANES), F32), jax.ShapeDtypeStruct((1, LANES), F32)],
        compiler_params=_params("arbitrary"),
        name="moe_router",
    )(x2d, gain.reshape(1, d), w_hi, w_lo, b_pad)


def _row_copy(src_hbm, src_row, dst, dst_row, sem):
    return pltpu.make_async_copy(src_hbm.at[pl.ds(src_row, 1)], dst.at[pl.ds(dst_row, 1)], sem)


ROW_DMA_UNROLL = 8


def _dispatch_kernel(ends_ref, pos1_ref, pos2_ref, x_ref, g_ref, xs_hbm, h_ref, zero_ref, sem, zsem,
                     *, tm, tmx):
    @pl.when(pl.program_id(0) == 0)
    def _zero_padding():
        zero_ref[...] = jnp.zeros(zero_ref.shape, zero_ref.dtype)

        def fill(row0):
            cp = pltpu.make_async_copy(zero_ref, xs_hbm.at[pl.ds(row0, tmx)], zsem)
            cp.start()
            cp.wait()

        for e in range(N_EXPERTS):
            start_e = ends_ref[e - 1] if e else 0

            @pl.when(ends_ref[e] > start_e)
            def _():
                fill(pl.multiple_of(ends_ref[e] - tmx, tmx))

        def tail(t, carry):
            fill(pl.multiple_of(t * tmx, tmx))
            return carry

        lax.fori_loop(ends_ref[N_EXPERTS - 1] // tmx, xs_hbm.shape[0] // tmx, tail, 0)

    i = pl.program_id(0)
    slot = i % 2
    h_ref.at[slot][...] = _rmsnorm(x_ref[...], g_ref[...], RMS_EPS)

    def start(r, carry):
        _row_copy(h_ref.at[slot], r, xs_hbm, pos1_ref[r], sem.at[slot]).start()
        _row_copy(h_ref.at[slot], r, xs_hbm, pos2_ref[r], sem.at[slot]).start()
        return carry

    def wait_slot(sl):
        def wait(r, carry):
            _row_copy(h_ref.at[sl], 0, xs_hbm, 0, sem.at[sl]).wait()
            _row_copy(h_ref.at[sl], 0, xs_hbm, 0, sem.at[sl]).wait()
            return carry
        lax.fori_loop(0, tm, wait, 0, unroll=ROW_DMA_UNROLL)

    lax.fori_loop(0, tm, start, 0, unroll=ROW_DMA_UNROLL)

    @pl.when(i > 0)
    def _previous():
        wait_slot(1 - slot)

    @pl.when(i == pl.num_programs(0) - 1)
    def _own():
        wait_slot(slot)


def _dispatch(x2d, gain, pos1, pos2, ends, n_slots, tm, tmx):
    m, d = x2d.shape
    smem = lambda: pl.BlockSpec((tm,), lambda i, ends: (i,), memory_space=pltpu.SMEM)
    grid_spec = pltpu.PrefetchScalarGridSpec(
        num_scalar_prefetch=1,
        grid=(m // tm,),
        in_specs=[smem(), smem(), pl.BlockSpec((tm, d), lambda i, ends: (i, 0)),
                  pl.BlockSpec((1, d), lambda i, ends: (0, 0))],
        out_specs=pl.BlockSpec(memory_space=pl.ANY),
        scratch_shapes=[pltpu.VMEM((2, tm, d), F32), pltpu.VMEM((tmx, d), F32),
                        pltpu.SemaphoreType.DMA((2,)), pltpu.SemaphoreType.DMA(())],
    )
    return pl.pallas_call(
        functools.partial(_dispatch_kernel, tm=tm, tmx=tmx),
        grid_spec=grid_spec,
        out_shape=jax.ShapeDtypeStruct((n_slots, d), F32),
        compiler_params=_params("arbitrary"),
        name="moe_dispatch",
    )(ends, pos1, pos2, x2d, gain.reshape(1, d))


def _expert_ffn_kernel(te_ref, nt_ref, xs_ref, wgu_ref, wd_ref, ys_ref, *, d_ff, tf):
    @pl.when(pl.program_id(0) < nt_ref[0])
    def _():
        ys_ref[...] = _swiglu(xs_ref[...].astype(BF16), wgu_ref, wd_ref, d_ff, tf)

    @pl.when(pl.program_id(0) >= nt_ref[0])
    def _():
        ys_ref[...] = jnp.zeros(ys_ref.shape, F32)


def _expert_ffn(xs, tile_expert, n_tiles, wgu_bf, wd_bf, tmx):
    n_slots, d = xs.shape
    _, d_ff, _ = wd_bf.shape
    row = lambda i, te, nt: (jnp.minimum(i, nt[0] - 1), 0)
    grid_spec = pltpu.PrefetchScalarGridSpec(
        num_scalar_prefetch=2,
        grid=(n_slots // tmx,),
        in_specs=[pl.BlockSpec((tmx, d), row),
                  pl.BlockSpec((None, d, 2 * d_ff), lambda i, te, nt: (te[i], 0, 0)),
                  pl.BlockSpec((None, d_ff, d), lambda i, te, nt: (te[i], 0, 0))],
        out_specs=pl.BlockSpec((tmx, d), lambda i, te, nt: (i, 0)),
    )
    return pl.pallas_call(
        functools.partial(_expert_ffn_kernel, d_ff=d_ff, tf=256),
        grid_spec=grid_spec,
        out_shape=jax.ShapeDtypeStruct((n_slots, d), F32),
        compiler_params=_params("arbitrary"),
        name="moe_experts",
    )(tile_expert, n_tiles, xs, wgu_bf, wd_bf)


def _combine_kernel(pos1_ref, pos2_ref, nxt1_ref, nxt2_ref, ys_hbm, x_ref, meta_ref, gf_ref, o_ref,
                    y1_ref, y2_ref, sem, *, tm):
    i = pl.program_id(0)
    slot = i % 2

    def gather(p1_ref, p2_ref, sl):
        def start(r, carry):
            _row_copy(ys_hbm, p1_ref[r], y1_ref.at[sl], r, sem.at[sl]).start()
            _row_copy(ys_hbm, p2_ref[r], y2_ref.at[sl], r, sem.at[sl]).start()
            return carry
        lax.fori_loop(0, tm, start, 0, unroll=ROW_DMA_UNROLL)

    @pl.when(i == 0)
    def _first():
        gather(pos1_ref, pos2_ref, 0)

    @pl.when(i + 1 < pl.num_programs(0))
    def _next():
        gather(nxt1_ref, nxt2_ref, 1 - slot)

    def wait(r, carry):
        _row_copy(ys_hbm, 0, y1_ref.at[slot], 0, sem.at[slot]).wait()
        _row_copy(ys_hbm, 0, y2_ref.at[slot], 0, sem.at[slot]).wait()
        return carry

    lax.fori_loop(0, tm, wait, 0, unroll=ROW_DMA_UNROLL)
    meta = meta_ref[...]
    out = meta[:, 2:3] * y1_ref[slot] + meta[:, 3:4] * y2_ref[slot]
    o_ref[...] = _rmsnorm(x_ref[...] + out, gf_ref[...], RMS_EPS)


def _combine(ys, pos1, pos2, x2d, meta, gain_final, tm):
    m, d = x2d.shape
    last = m // tm - 1
    smem = lambda off: pl.BlockSpec((tm,), lambda i: (jnp.minimum(i + off, last),), memory_space=pltpu.SMEM)
    row = lambda i: (i, 0)
    return pl.pallas_call(
        functools.partial(_combine_kernel, tm=tm),
        grid=(m // tm,),
        in_specs=[smem(0), smem(0), smem(1), smem(1), pl.BlockSpec(memory_space=pl.ANY),
                  pl.BlockSpec((tm, d), row), pl.BlockSpec((tm, LANES), row), _const_spec((1, d))],
        out_specs=pl.BlockSpec((tm, d), row),
        out_shape=jax.ShapeDtypeStruct((m, d), F32),
        scratch_shapes=[pltpu.VMEM((2, tm, d), F32), pltpu.VMEM((2, tm, d), F32),
                        pltpu.SemaphoreType.DMA((2,))],
        compiler_params=_params("arbitrary"),
        name="moe_combine",
    )(pos1, pos2, pos1, pos2, ys, x2d, meta, gain_final.reshape(1, d))


def _moe(x2d, norm_gain, w_router, b_router, wgu_bf, wd_bf, gain_final, tm):
    m, d = x2d.shape
    tmx = 512 if m >= 4096 else 128
    meta, counts = _router(x2d, norm_gain, w_router, b_router, tm)
    counts = counts[0, :N_EXPERTS].astype(jnp.int32)
    padded = (counts + tmx - 1) // tmx * tmx
    ends = jnp.cumsum(padded)
    starts = ends - padded
    n_slots = 2 * m + N_EXPERTS * tmx
    n_tiles_max = n_slots // tmx
    tile_start = jnp.arange(n_tiles_max, dtype=jnp.int32) * tmx
    tile_expert = jnp.minimum(jnp.sum((tile_start[:, None] >= ends[None, :]).astype(jnp.int32), axis=1),
                              N_EXPERTS - 1).astype(jnp.int32)
    n_tiles = (ends[-1:] // tmx).astype(jnp.int32)
    experts = jnp.arange(N_EXPERTS, dtype=jnp.int32)[None, :]
    start_of = lambda e: jnp.sum(jnp.where(e[:, None] == experts, starts[None, :], 0), axis=1)
    e1, e2 = meta[:, 0].astype(jnp.int32), meta[:, 1].astype(jnp.int32)
    pos1 = start_of(e1) + meta[:, 4].astype(jnp.int32)
    pos2 = start_of(e2) + meta[:, 5].astype(jnp.int32)
    td = min(512, m)
    xs = _dispatch(x2d, norm_gain, pos1, pos2, ends.astype(jnp.int32), n_slots, td, tmx)
    ys = _expert_ffn(xs, tile_expert, n_tiles, wgu_bf, wd_bf, tmx)
    return _combine(ys, pos1, pos2, x2d, meta, gain_final, td)


def _forward(x, caches, w):
    b, t, d = x.shape
    m = b * t
    tm = min(512, m)
    x2d = x.reshape(m, d)
    lambda_init = 0.8 - 0.6 * math.exp(-0.3 * 0)
    width = DIFF_HEADS * 2 * DIFF_HEAD_DIM
    ncb = width // LANES

    if caches is None:
        pbf, ka, va, kb, vb, vt, kaug = _qkv_proj(x2d, w["norm_attn"], w["attn_w_in"], tm, seq=t)
    else:
        pbf, ka, va, kb, vb = _qkv_proj(x2d, w["norm_attn"], w["attn_w_in"], tm)
    pbf3 = pbf.reshape(b, t, 6 * width)
    if caches is None:
        oa = _diff_attn_prompt(pbf3, kaug.reshape(b, t, 2 * width), vt, 0, w["diff_lambda"],
                               w["diff_subln"], tile=min(1024, t), lambda_init=lambda_init)
        tqb = min(256, t)
        n_parts = BAND_PAST_MAX // tqb + 1
        bias = _band_bias(w["band_rel_bias"], 0, -BAND_PAST_MAX, tqb, n_parts * tqb, key_major=True)
        ob = _band_attn_prompt(pbf3, vt, 3, 4, 1, bias, tq=tqb, n_parts=n_parts, tkp=tqb)
        n_band = min(BAND_PAST_MAX, t)
        tail = lambda a: a.reshape(b, t, width)[:, t - n_band:].reshape(b, n_band, BAND_HEADS, BAND_HEAD_DIM)
        new_bk, new_bv = tail(kb), tail(vb)
    else:
        ck_a, cv_a, ck_b, cv_b = caches
        p_len, pb_len = ck_a.shape[1], ck_b.shape[1]
        cat = lambda c, lo: jnp.concatenate(
            [c.reshape(b, c.shape[1], width).astype(BF16), pbf3[:, :, lo:lo + width]], axis=1)
        k_a, v_a = cat(ck_a, width), cat(cv_a, 2 * width)
        k_b, v_b = cat(ck_b, 4 * width), cat(cv_b, 5 * width)
        oa = _diff_attn(pbf3, k_a, v_a, 0, 0, 0, w["diff_lambda"], w["diff_subln"],
                        tq=t, tk=p_len + t, q_off=p_len, lambda_init=lambda_init)
        bias = _band_bias(w["band_rel_bias"], p_len, p_len - pb_len, t, pb_len + t)
        ob = _band_attn(pbf3, k_b, v_b, 3 * ncb, 0, 0, bias, tq=t, n_parts=1, tkp=pb_len + t)
        new_bk = kb.reshape(b, t, BAND_HEADS, BAND_HEAD_DIM)
        new_bv = vb.reshape(b, t, BAND_HEADS, BAND_HEAD_DIM)
    new_dk = ka.reshape(b, t, 2 * DIFF_HEADS, DIFF_HEAD_DIM)
    new_dv = va.reshape(b, t, DIFF_HEADS, 2 * DIFF_HEAD_DIM)

    x2d = _attn_out_ffn(oa.reshape(m, width), ob.reshape(m, width), x2d, w["attn_w_out"],
                        w["norm_ffn"], w["ffn_w_gu"], w["ffn_w_down"], tm)

    seg = min(t, GMLP_CHUNK)
    x2d, v_rows = _gmlp(x2d, w["norm_gmlp"], w["gmlp_w_in"], w["gmlp_b_in"], w["gmlp_ln_g"],
                        w["gmlp_ln_b"], w["gmlp_w_s"], w["gmlp_b_s"], w["gmlp_w_out"], tm, seg,
                        return_v=caches is not None)
    y = _moe(x2d, w["norm_moe"], w["moe_w_router"], w["moe_b_router"], w["moe_w_gu"], w["moe_w_down"],
             w["norm_final"], tm)
    new_gv = None if v_rows is None else v_rows.reshape(b, t, -1)[None]
    return (y.reshape(b, t, d), new_dk[None], new_dv[None], new_bk[None], new_bv[None], new_gv)


def kernel(x_prompt, x_sample, cache_diff_k, cache_diff_v, cache_band_k, cache_band_v,
           norm_attn, attn_w_in, diff_lambda, diff_subln, band_rel_bias, attn_w_out,
           norm_ffn, ffn_w_gu, ffn_w_down,
           norm_gmlp, gmlp_w_in, gmlp_b_in, gmlp_ln_g, gmlp_ln_b, gmlp_w_s, gmlp_b_s, gmlp_w_out,
           norm_moe, moe_w_router, moe_b_router, moe_w_gu, moe_w_down, norm_final):
    w = {
        "norm_attn": norm_attn[0], "attn_w_in": attn_w_in[0].astype(BF16),
        "diff_lambda": diff_lambda[0], "diff_subln": diff_subln[0],
        "band_rel_bias": band_rel_bias[0], "attn_w_out": attn_w_out[0].astype(BF16),
        "norm_ffn": norm_ffn[0], "ffn_w_gu": ffn_w_gu[0].astype(BF16),
        "ffn_w_down": ffn_w_down[0].astype(BF16),
        "norm_gmlp": norm_gmlp[0], "gmlp_w_in": gmlp_w_in[0].astype(BF16),
        "gmlp_b_in": gmlp_b_in[0], "gmlp_ln_g": gmlp_ln_g[0], "gmlp_ln_b": gmlp_ln_b[0],
        "gmlp_w_s": gmlp_w_s[0], "gmlp_b_s": gmlp_b_s[0], "gmlp_w_out": gmlp_w_out[0].astype(BF16),
        "norm_moe": norm_moe[0], "moe_w_router": moe_w_router[0], "moe_b_router": moe_b_router[0],
        "moe_w_gu": moe_w_gu[0].astype(BF16), "moe_w_down": moe_w_down[0].astype(BF16),
        "norm_final": norm_final,
    }
    y_p, dk_p, dv_p, bk_p, bv_p, _ = _forward(x_prompt, None, w)
    y_s, dk_s, dv_s, bk_s, bv_s, gv_s = _forward(
        x_sample, (cache_diff_k[0], cache_diff_v[0], cache_band_k[0], cache_band_v[0]), w)
    return (y_p, y_s, dk_p, dv_p, bk_p, bv_p, dk_s, dv_s, bk_s, bv_s, gv_s)
```

```python
import functools
import math

import numpy as np
import jax
import jax.numpy as jnp
from jax import lax
from jax.experimental import pallas as pl
from jax.experimental.pallas import tpu as pltpu

CHUNK = 64
DIFF_HEADS = 4
DIFF_HEAD_DIM = 64
DIFF_SUBLN_EPS = 1e-5
BAND_HEADS = 8
BAND_HEAD_DIM = 64
BAND_PREV_CHUNKS = 8
BAND_PAST_MAX = BAND_PREV_CHUNKS * CHUNK
REL_CLIP = 256
GMLP_GROUPS = 8
GMLP_CHUNK = 128
N_EXPERTS = 8
RMS_EPS = 1e-6
LN_EPS = 1e-5
MASK_VALUE = -1e30

LANES = 128
ONES_ROWS = 16
VMEM_LIMIT_BYTES = 56 << 20

F32 = jnp.float32
BF16 = jnp.bfloat16
LOG2E = math.log2(math.e)
BF16_EXP2_ZERO = 140.0
KNORM_LANE = 6
ROUND_UP = 1.0 + 2.0 ** -6
Q_SCALE = DIFF_HEAD_DIM ** -0.5 * LOG2E
NT_DIMS = (((1,), (1,)), ((), ()))


def _params(*sem, flags=None):
    return pltpu.CompilerParams(dimension_semantics=sem, vmem_limit_bytes=VMEM_LIMIT_BYTES, flags=flags)


def _const_spec(shape):
    nd = len(shape)
    return pl.BlockSpec(shape, lambda *_: (0,) * nd, pipeline_mode=pl.Buffered(1))


def _rmsnorm(x, g, eps):
    return (x * lax.rsqrt(jnp.mean(x * x, axis=-1, keepdims=True) + eps)) * g


def _dot(a, b):
    return jnp.dot(a, b, preferred_element_type=F32)


def _swiglu(h, wgu_ref, wd_ref, d_ff, tf):
    y = None
    for c in range(d_ff // tf):
        g = _dot(h, wgu_ref[:, c * tf:(c + 1) * tf])
        u = _dot(h, wgu_ref[:, d_ff + c * tf:d_ff + (c + 1) * tf])
        a = (g * jax.nn.sigmoid(g) * u).astype(BF16)
        part = _dot(a, wd_ref[c * tf:(c + 1) * tf, :])
        y = part if y is None else y + part
    return y


def _qkv_kernel(x_ref, g_ref, w_ref, *rest, width, with_vt, tiles_per_seq):
    if with_vt:
        pbf_ref, ka_ref, va_ref, kb_ref, vb_ref, vt_ref, kaug_ref = rest
    else:
        pbf_ref, ka_ref, va_ref, kb_ref, vb_ref = rest
    tm = x_ref.shape[0]
    h = _rmsnorm(x_ref[...], g_ref[...], RMS_EPS).astype(BF16)
    f32_outs = {1: ka_ref, 2: va_ref, 4: kb_ref, 5: vb_ref}
    for c in range(6):
        r = _dot(h, w_ref[:, c * width:(c + 1) * width])
        if c in f32_outs:
            f32_outs[c][...] = r
        else:
            r = r * Q_SCALE
        pbf_ref[:, c * width:(c + 1) * width] = r.astype(BF16)
        if with_vt and c == 1:
            pos = (pl.program_id(0) % tiles_per_seq) * tm + lax.broadcasted_iota(jnp.int32, (tm, LANES), 0)
            lane = lax.broadcasted_iota(jnp.int32, (tm, LANES), 1)
            hi = (pos // LANES).astype(F32)
            lo = (pos % LANES).astype(F32)
            feats = jnp.where(lane < 3, hi, jnp.where(lane < 6, lo, 0.0))
            for hd in range(DIFF_HEADS):
                k_bf = r[:, hd * LANES:(hd + 1) * LANES].astype(BF16)
                kaug_ref[:, 2 * hd * LANES:(2 * hd + 1) * LANES] = k_bf
                k2 = jnp.square(k_bf.astype(F32))
                n1 = jnp.sqrt(jnp.sum(jnp.where(lane < DIFF_HEAD_DIM, k2, 0.0), axis=1, keepdims=True))
                n2 = jnp.sqrt(jnp.sum(jnp.where(lane >= DIFF_HEAD_DIM, k2, 0.0), axis=1, keepdims=True))
                f = jnp.where(lane == KNORM_LANE, n1 * ROUND_UP, jnp.where(lane == KNORM_LANE + 1, n2 * ROUND_UP, feats))
                kaug_ref[:, (2 * hd + 1) * LANES:(2 * hd + 2) * LANES] = f.astype(BF16)
        if with_vt and c in (2, 5):
            r0 = 0 if c == 2 else width
            vt_ref[r0:r0 + width, :] = r.T.astype(BF16)


def _qkv_proj(x2d, gain, w_bf, tm, seq=None):
    m, d = x2d.shape
    width = w_bf.shape[1] // 6
    row = lambda i: (i, 0)
    with_vt = seq is not None
    in_specs = [pl.BlockSpec((tm, d), row), _const_spec((1, d)), _const_spec(w_bf.shape)]
    out_specs = [pl.BlockSpec((tm, 6 * width), row)] + [pl.BlockSpec((tm, width), row)] * 4
    out_shape = [jax.ShapeDtypeStruct((m, 6 * width), BF16)] + [jax.ShapeDtypeStruct((m, width), F32)] * 4
    args = [x2d, gain.reshape(1, d), w_bf]
    nt = 1
    if with_vt:
        nt = seq // tm
        assert seq <= 256 * LANES
        vt_rows = 2 * width
        out_specs.append(pl.BlockSpec((None, vt_rows, tm), lambda i: (i // nt, 0, i % nt)))
        out_shape.append(jax.ShapeDtypeStruct((m // seq, vt_rows, seq), BF16))
        out_specs.append(pl.BlockSpec((tm, 2 * width), row))
        out_shape.append(jax.ShapeDtypeStruct((m, 2 * width), BF16))
    return pl.pallas_call(
        functools.partial(_qkv_kernel, width=width, with_vt=with_vt, tiles_per_seq=nt),
        grid=(m // tm,),
        in_specs=in_specs, out_specs=out_specs, out_shape=out_shape,
        compiler_params=_params("parallel"),
        name="qkv_proj",
    )(*args)


def _split_heads_rows(q, tq):
    qf = q.astype(F32)
    lane = lax.broadcasted_iota(jnp.int32, qf.shape, 1)
    lo = jnp.where(lane < DIFF_HEAD_DIM, qf, 0.0)
    hi = jnp.where(lane >= DIFF_HEAD_DIM, qf, 0.0)
    return jnp.concatenate([lo, hi], axis=0).astype(BF16)


def _diff_attn_kernel(ii_ref, jj_ref, last_ref, q_ref, k_ref, v_ref, slope_ref, lam_ref, g_ref,
                      o_ref, qz_ref, m_ref, l_ref, acc_ref, *, tq, tk, q_off, lambda_init):
    s = pl.program_id(2)
    i = ii_ref[s]
    j = jj_ref[s]

    @pl.when(j == 0)
    def _init():
        qz_ref[...] = _split_heads_rows(q_ref[...], tq)
        m_ref[...] = jnp.full(m_ref.shape, MASK_VALUE, F32)
        l_ref[...] = jnp.zeros(l_ref.shape, F32)
        acc_ref[...] = jnp.zeros(acc_ref.shape, F32)

    sc = lax.dot_general(qz_ref[...], k_ref[...], NT_DIMS, preferred_element_type=F32)
    row = lax.broadcasted_iota(jnp.int32, (2 * tq, 1), 0)
    row = jnp.where(row >= tq, row - tq, row)
    qpos = q_off + i * tq + row
    kpos = j * tk + lax.broadcasted_iota(jnp.int32, (1, tk), 1)
    dist = jnp.abs(qpos - kpos).astype(F32)
    sc = sc - slope_ref[:, 0:1] * dist
    visible = (kpos // CHUNK) <= (qpos // CHUNK)
    sc = jnp.where(visible, sc, MASK_VALUE)

    m_prev = m_ref[...]
    m_new = jnp.maximum(m_prev, jnp.max(sc, axis=1, keepdims=True))
    alpha = jnp.exp2(m_prev - m_new)
    p = jnp.exp2(sc - m_new)
    l_ref[...] = alpha * l_ref[...] + jnp.sum(p, axis=1, keepdims=True)
    acc_ref[...] = alpha * acc_ref[...] + _dot(p.astype(BF16), v_ref[...])
    m_ref[...] = m_new

    @pl.when(last_ref[s] == 1)
    def _finish():
        lp = lam_ref[...]
        lam = (jnp.exp(jnp.sum(lp[0:1] * lp[1:2], axis=1, keepdims=True))
               - jnp.exp(jnp.sum(lp[2:3] * lp[3:4], axis=1, keepdims=True)) + lambda_init)
        o_all = acc_ref[...] / l_ref[...]
        o = o_all[:tq] - lam * o_all[tq:]
        o = _rmsnorm(o, g_ref[...], DIFF_SUBLN_EPS) * (1.0 - lambda_init)
        o_ref[...] = o.astype(o_ref.dtype)


def _diff_attn(q_arr, k_arr, v_arr, q_cb, k_cb, v_cb, lam_p, subln_g, *, tq, tk, q_off, lambda_init):
    b, t_q, _ = q_arr.shape
    t_k = k_arr.shape[1]
    nq, nk = t_q // tq, t_k // tk
    pairs = [(i, j) for i in range(nq) for j in range(nk)
             if (j * tk) // CHUNK <= (q_off + i * tq + tq - 1) // CHUNK]
    ii = np.array([p[0] for p in pairs], np.int32)
    jj = np.array([p[1] for p in pairs], np.int32)
    last = np.array([1 if (n + 1 == len(pairs) or pairs[n + 1][0] != pairs[n][0]) else 0
                     for n in range(len(pairs))], np.int32)
    slopes = jnp.asarray(_alibi_slopes_log2()[:, None, None] * np.ones((1, 1, LANES), np.float32))

    grid_spec = pltpu.PrefetchScalarGridSpec(
        num_scalar_prefetch=3,
        grid=(b, DIFF_HEADS, len(pairs)),
        in_specs=[
            pl.BlockSpec((None, tq, LANES), lambda b_, h, s, ii, jj, la: (b_, ii[s], q_cb + h)),
            pl.BlockSpec((None, tk, LANES), lambda b_, h, s, ii, jj, la: (b_, jj[s], k_cb + h)),
            pl.BlockSpec((None, tk, LANES), lambda b_, h, s, ii, jj, la: (b_, jj[s], v_cb + h)),
            pl.BlockSpec((None, 1, LANES), lambda b_, h, s, ii, jj, la: (h, 0, 0)),
            pl.BlockSpec((4, DIFF_HEAD_DIM), lambda b_, h, s, ii, jj, la: (0, 0)),
            pl.BlockSpec((1, LANES), lambda b_, h, s, ii, jj, la: (0, 0)),
        ],
        out_specs=pl.BlockSpec((None, tq, LANES), lambda b_, h, s, ii, jj, la: (b_, ii[s], h)),
        scratch_shapes=[
            pltpu.VMEM((2 * tq, LANES), BF16),
            pltpu.VMEM((2 * tq, 1), F32),
            pltpu.VMEM((2 * tq, 1), F32),
            pltpu.VMEM((2 * tq, LANES), F32),
        ],
    )
    return pl.pallas_call(
        functools.partial(_diff_attn_kernel, tq=tq, tk=tk, q_off=q_off, lambda_init=lambda_init),
        grid_spec=grid_spec,
        out_shape=jax.ShapeDtypeStruct((b, t_q, DIFF_HEADS * LANES), BF16),
        compiler_params=_params("parallel", "parallel", "arbitrary"),
        name="diff_attn",
    )(jnp.asarray(ii), jnp.asarray(jj), jnp.asarray(last), q_arr, k_arr, v_arr, slopes,
      lam_p, subln_g.reshape(1, LANES))


def _alibi_slopes_log2():
    return (2.0 ** (-8.0 * np.arange(1, DIFF_HEADS + 1, dtype=np.float64) / DIFF_HEADS) * LOG2E).astype(np.float32)


def _diff_prompt_kernel(ii_ref, jj_ref, last_ref, lin_ref, q_ref, k_ref, vt_ref, dbias_ref, qfeat_ref,
                        lam_ref, g_ref, o_ref, qzt_ref, qn_ref, s0_ref, s1_ref, mx0_ref, mx1_ref, m_ref,
                        acc_ref, *, tq, tk, cb, lambda_init):
    s = pl.program_id(2)
    n_pairs = pl.num_programs(2) - 1
    sp = jnp.minimum(s, n_pairs - 1)
    sc = jnp.maximum(s - 1, 0)
    consuming = s > 0
    nblk = 2 * tq // cb
    blk = lambda c: slice(c * cb, (c + 1) * cb)

    @pl.when(s == 0)
    def _first_step():
        s1_ref[...] = jnp.zeros(s1_ref.shape, F32)
        mx1_ref[...] = jnp.zeros(mx1_ref.shape, F32)

    @pl.when(jj_ref[sp] == ii_ref[sp])
    def _new_queries():
        qf = q_ref[...].astype(F32)
        lane = lax.broadcasted_iota(jnp.int32, qf.shape, 1)
        qzt_ref[0:LANES, 0:tq] = jnp.where(lane < DIFF_HEAD_DIM, qf, 0.0).T.astype(BF16)
        qzt_ref[0:LANES, tq:2 * tq] = jnp.where(lane >= DIFF_HEAD_DIM, qf, 0.0).T.astype(BF16)
        qzt_ref[LANES:2 * LANES, :] = jnp.broadcast_to(qfeat_ref[...], (LANES, 2 * tq)).astype(BF16)
        qn_ref[...] = jnp.sqrt(jnp.sum(jnp.square(qzt_ref[0:LANES, :].astype(F32)), axis=0, keepdims=True))

    @pl.when(jj_ref[sc] == ii_ref[sc])
    def _reset():
        m_ref[...] = jnp.full(m_ref.shape, MASK_VALUE, F32)
        acc_ref[...] = jnp.zeros(acc_ref.shape, F32)

    kfeat_max = jnp.max(k_ref[:, LANES:2 * LANES].astype(F32), axis=0, keepdims=True)
    knorm = jnp.concatenate([jnp.broadcast_to(kfeat_max[:, KNORM_LANE:KNORM_LANE + 1], (1, tq)),
                             jnp.broadcast_to(kfeat_max[:, KNORM_LANE + 1:KNORM_LANE + 2], (1, tq))], axis=1)
    slope = jnp.sum(qfeat_ref[3:6, :], axis=0, keepdims=True)
    kpos_max = ((jj_ref[sp] + 1) * tk - 1).astype(F32)
    bound = qn_ref[...] * knorm * ROUND_UP + slope * kpos_max
    out_of_reach = jnp.max(bound - m_ref[...]) < -BF16_EXP2_ZERO
    skip_produce = jnp.logical_or(s == n_pairs, jnp.logical_and(lin_ref[sp] == 1, out_of_reach))

    def stages(s_prod, mx_prod, s_cons, mx_cons):
        @pl.when(jnp.logical_and(consuming, lin_ref[sc] == 0))
        def _diagonal():
            for c in range(nblk):
                v = s_cons[:, blk(c)] + dbias_ref[:, (c * cb) % tq:(c * cb) % tq + cb]
                s_cons[:, blk(c)] = v
                mx_cons[:, blk(c)] = jnp.max(v, axis=0, keepdims=True)

        def produce():
            k = k_ref[...]
            for c in range(nblk):
                v = _dot(k, qzt_ref[:, blk(c)])
                s_prod[:, blk(c)] = v
                mx_prod[:, blk(c)] = jnp.max(v, axis=0, keepdims=True)

        def consume():
            vt = jnp.concatenate([vt_ref[...], jnp.ones((ONES_ROWS, tk), BF16)], axis=0)
            m_prev = [m_ref[:, blk(c)] for c in range(nblk)]
            acc_prev = [acc_ref[:, blk(c)] for c in range(nblk)]
            m_out, acc_out = [], []
            for c in range(nblk):
                m_new = jnp.maximum(m_prev[c], mx_cons[:, blk(c)])
                alpha = jnp.exp2(m_prev[c] - m_new)
                p = jnp.exp2((s_cons[:, blk(c)] - m_new).astype(BF16))
                acc_out.append(alpha * acc_prev[c] + _dot(vt, p))
                m_out.append(m_new)
            for c in range(nblk):
                m_ref[:, blk(c)] = m_out[c]
                acc_ref[:, blk(c)] = acc_out[c]

        negligible = jnp.max(mx_cons[...] - m_ref[...]) < -BF16_EXP2_ZERO
        do_produce = jnp.logical_not(skip_produce)
        do_consume = jnp.logical_not(negligible)

        @pl.when(skip_produce)
        def _mark_skipped():
            mx_prod[...] = jnp.full(mx_prod.shape, MASK_VALUE, F32)

        @pl.when(jnp.logical_and(skip_produce, do_consume))
        def _consume_only():
            consume()

        @pl.when(jnp.logical_and(do_produce, negligible))
        def _produce_only():
            produce()

        @pl.when(jnp.logical_and(do_produce, do_consume))
        def _both():
            produce()
            consume()

    @pl.when(s % 2 == 0)
    def _even():
        stages(s0_ref, mx0_ref, s1_ref, mx1_ref)

    @pl.when(s % 2 == 1)
    def _odd():
        stages(s1_ref, mx1_ref, s0_ref, mx0_ref)

    @pl.when(jnp.logical_and(consuming, last_ref[sc] == 1))
    def _finish():
        lp = lam_ref[...]
        lam = (jnp.exp(jnp.sum(lp[0:1] * lp[1:2], axis=1, keepdims=True))
               - jnp.exp(jnp.sum(lp[2:3] * lp[3:4], axis=1, keepdims=True)) + lambda_init)
        o_all = acc_ref[0:LANES, :] / acc_ref[LANES:LANES + 1, :]
        o = o_all[:, 0:tq] - lam * o_all[:, tq:2 * tq]
        o = o * lax.rsqrt(jnp.mean(o * o, axis=0, keepdims=True) + DIFF_SUBLN_EPS)
        o = o * g_ref[...] * (1.0 - lambda_init)
        o_ref[...] = o.T.astype(o_ref.dtype)


def _bf16_split3(x):
    import ml_dtypes
    rnd = lambda v: v.astype(ml_dtypes.bfloat16).astype(np.float32)
    x = np.asarray(x, np.float32)
    hi = rnd(x)
    mid = rnd(x - hi)
    lo = rnd(x - hi - mid)
    return hi, mid, lo


def _diff_attn_prompt(pbf3, kaug3, vt_arr, q_cb, lam_p, subln_g, *, tile, lambda_init):
    b, t, _ = pbf3.shape
    tq = tk = tile
    n = t // tile
    pairs = [(i, j) for i in range(n) for j in range(i, -1, -1)]
    ii = np.array([p[0] for p in pairs], np.int32)
    jj = np.array([p[1] for p in pairs], np.int32)
    last = (jj == 0).astype(np.int32)
    lin = (ii != jj).astype(np.int32)
    c2 = _alibi_slopes_log2()
    qfeat = np.zeros((DIFF_HEADS, LANES, 1), np.float32)
    for r, part in enumerate(_bf16_split3(c2)):
        qfeat[:, r, 0] = part * LANES
        qfeat[:, 3 + r, 0] = part
    pos = np.arange(tile)
    d = (pos[None, :] - pos[:, None]).astype(np.float32)
    visible = (pos[:, None] // CHUNK) <= (pos[None, :] // CHUNK)
    dbias = jnp.asarray(np.where(visible[None], 2.0 * c2[:, None, None] * np.minimum(d, 0.0)[None],
                                 MASK_VALUE).astype(np.float32))
    cb = min(256, 2 * tq)
    n_pairs = len(pairs)
    prod = lambda s: jnp.minimum(s, n_pairs - 1)
    cons = lambda s: jnp.maximum(s - 1, 0)
    idx = lambda f: (lambda b_, h, s, ii, jj, la, li: f(b_, h, s, ii, jj))
    grid_spec = pltpu.PrefetchScalarGridSpec(
        num_scalar_prefetch=4,
        grid=(b, DIFF_HEADS, n_pairs + 1),
        in_specs=[
            pl.BlockSpec((None, tq, LANES), idx(lambda b_, h, s, ii, jj: (b_, ii[prod(s)], q_cb + h))),
            pl.BlockSpec((None, tk, 2 * LANES), idx(lambda b_, h, s, ii, jj: (b_, jj[prod(s)], h))),
            pl.BlockSpec((None, LANES, tk), idx(lambda b_, h, s, ii, jj: (b_, h, jj[cons(s)]))),
            pl.BlockSpec((None, tk, tq), idx(lambda b_, h, s, ii, jj: (h, 0, 0))),
            pl.BlockSpec((None, LANES, 1), idx(lambda b_, h, s, ii, jj: (h, 0, 0))),
            pl.BlockSpec((4, DIFF_HEAD_DIM), idx(lambda b_, h, s, ii, jj: (0, 0))),
            pl.BlockSpec((LANES, 1), idx(lambda b_, h, s, ii, jj: (0, 0))),
        ],
        out_specs=pl.BlockSpec((None, tq, LANES), idx(lambda b_, h, s, ii, jj: (b_, ii[cons(s)], h))),
        scratch_shapes=[
            pltpu.VMEM((2 * LANES, 2 * tq), BF16),
            pltpu.VMEM((1, 2 * tq), F32),
            pltpu.VMEM((tk, 2 * tq), F32), pltpu.VMEM((tk, 2 * tq), F32),
            pltpu.VMEM((1, 2 * tq), F32), pltpu.VMEM((1, 2 * tq), F32),
            pltpu.VMEM((1, 2 * tq), F32),
            pltpu.VMEM((LANES + ONES_ROWS, 2 * tq), F32),
        ],
    )
    return pl.pallas_call(
        functools.partial(_diff_prompt_kernel, tq=tq, tk=tk, cb=cb, lambda_init=lambda_init),
        grid_spec=grid_spec,
        out_shape=jax.ShapeDtypeStruct((b, t, DIFF_HEADS * LANES), BF16),
        compiler_params=_params("parallel", "parallel", "arbitrary"),
        name="diff_attn_prompt",
    )(jnp.asarray(ii), jnp.asarray(jj), jnp.asarray(last), jnp.asarray(lin), pbf3, kaug3, vt_arr, dbias,
      jnp.asarray(qfeat), lam_p, subln_g.reshape(LANES, 1))


def _band_attn_kernel(*refs, tq, n_parts, tkp):
    q_ref = refs[0]
    k_refs = refs[1:1 + n_parts]
    v_refs = refs[1 + n_parts:1 + 2 * n_parts]
    bias_ref = refs[1 + 2 * n_parts]
    o_ref = refs[2 + 2 * n_parts]
    i = pl.program_id(2)
    qz = _split_heads_rows(q_ref[...], tq)
    scores = []
    for m in range(n_parts):
        sc = lax.dot_general(qz, k_refs[m][...], NT_DIMS, preferred_element_type=F32)
        sc = sc + bias_ref[:, m * tkp:(m + 1) * tkp]
        if n_parts > 1:
            sc = jnp.where(i - (n_parts - 1) + m >= 0, sc, MASK_VALUE)
        scores.append(sc)
    mx = functools.reduce(jnp.maximum, [jnp.max(sc, axis=1, keepdims=True) for sc in scores])
    den = None
    num = None
    for m in range(n_parts):
        p = jnp.exp2(scores[m] - mx)
        d = jnp.sum(p, axis=1, keepdims=True)
        r = _dot(p.astype(BF16), v_refs[m][...])
        den = d if den is None else den + d
        num = r if num is None else num + r
    r = num / den
    lane = lax.broadcasted_iota(jnp.int32, (tq, LANES), 1)
    o_ref[...] = jnp.where(lane < BAND_HEAD_DIM, r[:tq], r[tq:]).astype(o_ref.dtype)


def _band_bias_kernel(line_ref, o_ref, *, q0, k0):
    tq, nk = o_ref.shape
    line = jnp.broadcast_to(line_ref[...], (tq, line_ref.shape[1]))
    rolled = pltpu.roll(line, 0, 1, stride=1, stride_axis=0)
    qc = (q0 + lax.broadcasted_iota(jnp.int32, (tq, nk), 0)) >> 6
    kc = (k0 + lax.broadcasted_iota(jnp.int32, (tq, nk), 1)) >> 6
    visible = (kc <= qc) & (qc - kc <= BAND_PREV_CHUNKS)
    o_ref[...] = jnp.where(visible, rolled[:, 0:nk], MASK_VALUE)


def _band_bias_t_kernel(line_ref, o_ref, *, q0, k0):
    nk, tq = o_ref.shape
    line = jnp.broadcast_to(line_ref[...], (nk, line_ref.shape[1]))
    rolled = pltpu.roll(line, 0, 1, stride=1, stride_axis=0)
    kc = (k0 + lax.broadcasted_iota(jnp.int32, (nk, tq), 0)) >> 6
    qc = (q0 + lax.broadcasted_iota(jnp.int32, (nk, tq), 1)) >> 6
    visible = (kc <= qc) & (qc - kc <= BAND_PREV_CHUNKS)
    o_ref[...] = jnp.where(visible, rolled[:, 0:tq], MASK_VALUE)


def _band_bias(rel_table, q0, k0, tq, nk, key_major=False):
    assert CHUNK == 64
    width = pl.next_power_of_2(tq + nk - 1)
    y = np.arange(width)
    c_minus_r = np.where(y < nk, y, y - width)
    rel = np.clip((q0 - k0) - c_minus_r, -(CHUNK - 1), REL_CLIP) + (CHUNK - 1)
    if key_major:
        rel = rel[(-y) % width]
    line = (rel_table.astype(F32) * LOG2E)[:, rel].reshape(BAND_HEADS, 1, width)
    if key_major:
        kern, block, shape = _band_bias_t_kernel, (None, nk, tq), (BAND_HEADS // 2, nk, 2 * tq)
        index = lambda h: (h // 2, 0, h % 2)
    else:
        kern, block, shape = _band_bias_kernel, (None, tq, nk), (BAND_HEADS // 2, 2 * tq, nk)
        index = lambda h: (h // 2, h % 2, 0)
    return pl.pallas_call(
        functools.partial(kern, q0=q0, k0=k0),
        grid=(BAND_HEADS,),
        in_specs=[pl.BlockSpec((None, 1, width), lambda h: (h, 0, 0))],
        out_specs=pl.BlockSpec(block, index),
        out_shape=jax.ShapeDtypeStruct(shape, F32),
        compiler_params=_params("parallel"),
        name="band_bias",
    )(line)


def _band_prompt_kernel(*refs, tq, n_parts, tkp):
    q_ref = refs[0]
    k_refs = refs[1:1 + n_parts]
    vt_refs = refs[1 + n_parts:1 + 2 * n_parts]
    bias_ref = refs[1 + 2 * n_parts]
    o_ref = refs[2 + 2 * n_parts]
    nk = n_parts * tkp
    hd = BAND_HEAD_DIM
    i = pl.program_id(1)
    ones = jnp.ones((ONES_ROWS, nk), BF16)
    for pr in range(BAND_HEADS // 2):
        ln = slice(pr * LANES, (pr + 1) * LANES)
        qf = q_ref[:, ln].astype(F32)
        lane = lax.broadcasted_iota(jnp.int32, qf.shape, 1)
        qzt = jnp.concatenate([jnp.where(lane < hd, qf, 0.0).T, jnp.where(lane >= hd, qf, 0.0).T],
                              axis=1).astype(BF16)
        k = jnp.concatenate([r[:, ln] for r in k_refs], axis=0)
        sc = _dot(k, qzt) + bias_ref[pr]
        row = lax.broadcasted_iota(jnp.int32, sc.shape, 0)
        sc = jnp.where(row >= (n_parts - 1 - i) * tkp, sc, MASK_VALUE)
        mx = jnp.max(sc, axis=0, keepdims=True)
        p = jnp.exp2((sc - mx).astype(BF16))
        vt = jnp.concatenate([r[ln, :] for r in vt_refs] , axis=1)
        r = _dot(jnp.concatenate([vt, ones], axis=0), p)
        o = r[0:LANES, :] / r[LANES:LANES + 1, :]
        o = jnp.concatenate([o[0:hd, 0:tq], o[hd:2 * hd, tq:2 * tq]], axis=0)
        o_ref[:, ln] = o.T.astype(o_ref.dtype)


def _band_attn_prompt(pbf3, vt_arr, q_cb, k_cb, vt_rb, bias_t, *, tq, n_parts, tkp):
    b, t, _ = pbf3.shape
    width = (BAND_HEADS // 2) * LANES
    part = lambda i, m: jnp.maximum(i - (n_parts - 1) + m, 0)
    k_spec = lambda m: pl.BlockSpec((None, tkp, width), lambda b_, i: (b_, part(i, m), k_cb))
    vt_spec = lambda m: pl.BlockSpec((None, width, tkp), lambda b_, i: (b_, vt_rb, part(i, m)))
    return pl.pallas_call(
        functools.partial(_band_prompt_kernel, tq=tq, n_parts=n_parts, tkp=tkp),
        grid=(b, t // tq),
        in_specs=[pl.BlockSpec((None, tq, width), lambda b_, i: (b_, i, q_cb))]
        + [k_spec(m) for m in range(n_parts)] + [vt_spec(m) for m in range(n_parts)]
        + [_const_spec(bias_t.shape)],
        out_specs=pl.BlockSpec((None, tq, width), lambda b_, i: (b_, i, 0)),
        out_shape=jax.ShapeDtypeStruct((b, t, width), BF16),
        compiler_params=_params("parallel", "parallel"),
        name="band_attn_prompt",
    )(pbf3, *([pbf3] * n_parts), *([vt_arr] * n_parts), bias_t)


def _band_attn(q_arr, k_arr, v_arr, q_cb, k_cb, v_cb, bias, *, tq, n_parts, tkp):
    b, t_q, _ = q_arr.shape
    n_pairs = BAND_HEADS // 2

    def kv_spec(cb, m):
        return pl.BlockSpec((None, tkp, LANES),
                            lambda p, b_, i: (b_, jnp.maximum(i - (n_parts - 1) + m, 0), cb + p))

    return pl.pallas_call(
        functools.partial(_band_attn_kernel, tq=tq, n_parts=n_parts, tkp=tkp),
        grid=(n_pairs, b, t_q // tq),
        in_specs=[pl.BlockSpec((None, tq, LANES), lambda p, b_, i: (b_, i, q_cb + p))]
        + [kv_spec(k_cb, m) for m in range(n_parts)]
        + [kv_spec(v_cb, m) for m in range(n_parts)]
        + [pl.BlockSpec((None, 2 * tq, n_parts * tkp), lambda p, b_, i: (p, 0, 0))],
        out_specs=pl.BlockSpec((None, tq, LANES), lambda p, b_, i: (b_, i, p)),
        out_shape=jax.ShapeDtypeStruct((b, t_q, n_pairs * LANES), BF16),
        compiler_params=_params("parallel", "parallel", "parallel"),
        name="band_attn",
    )(q_arr, *([k_arr] * n_parts), *([v_arr] * n_parts), bias)


def _attn_out_ffn_kernel(oa_ref, ob_ref, x_ref, wo_ref, g_ref, wgu_ref, wd_ref, o_ref, *, d_ff, tf):
    half = oa_ref.shape[1]
    x1 = x_ref[...] + _dot(oa_ref[...], wo_ref[0:half, :]) + _dot(ob_ref[...], wo_ref[half:2 * half, :])
    h = _rmsnorm(x1, g_ref[...], RMS_EPS).astype(BF16)
    o_ref[...] = x1 + _swiglu(h, wgu_ref, wd_ref, d_ff, tf)


def _attn_out_ffn(oa, ob, x2d, wo_bf, gain, wgu_bf, wd_bf, tm):
    m, d = x2d.shape
    d_ff = wd_bf.shape[0]
    row = lambda i: (i, 0)
    return pl.pallas_call(
        functools.partial(_attn_out_ffn_kernel, d_ff=d_ff, tf=256),
        grid=(m // tm,),
        in_specs=[pl.BlockSpec((tm, oa.shape[1]), row), pl.BlockSpec((tm, ob.shape[1]), row),
                  pl.BlockSpec((tm, d), row), _const_spec(wo_bf.shape), _const_spec((1, d)),
                  _const_spec(wgu_bf.shape), _const_spec(wd_bf.shape)],
        out_specs=pl.BlockSpec((tm, d), row),
        out_shape=jax.ShapeDtypeStruct((m, d), F32),
        compiler_params=_params("parallel"),
        name="attn_out_ffn",
    )(oa, ob, x2d, wo_bf, gain.reshape(1, d), wgu_bf, wd_bf)


def _gelu(z):
    return 0.5 * z * (1.0 + lax.erf(z * (2.0 ** -0.5)))


def _gmlp_kernel(x_ref, g_ref, win_ref, bin_ref, lng_ref, lnb_ref, ws_ref, bs_ref, wout_ref,
                 o_ref, v_ref, u_s, act_s, *, seg, cw):
    tm, gd = v_ref.shape
    gw = gd // GMLP_GROUPS
    x = x_ref[...]
    h = _rmsnorm(x, g_ref[...], RMS_EPS).astype(BF16)
    for c in range(gd // cw):
        lo, hi = c * cw, (c + 1) * cw
        u_s[:, lo:hi] = _gelu(_dot(h, win_ref[:, lo:hi]) + bin_ref[:, lo:hi])
        v_ref[:, lo:hi] = _gelu(_dot(h, win_ref[:, gd + lo:gd + hi]) + bin_ref[:, gd + lo:gd + hi])
    v = v_ref[...]
    mu = jnp.mean(v, axis=-1, keepdims=True)
    var = jnp.mean(jnp.square(v - mu), axis=-1, keepdims=True)
    v_ref[...] = (v - mu) * lax.rsqrt(var + LN_EPS) * lng_ref[...] + lnb_ref[...]
    r_i = lax.broadcasted_iota(jnp.int32, (seg, seg), 0)
    c_i = lax.broadcasted_iota(jnp.int32, (seg, seg), 1)
    for g in range(GMLP_GROUPS):
        w = jnp.where(r_i >= c_i, ws_ref[g], 0.0).astype(BF16)
        for n in range(tm // seg):
            rows = slice(n * seg, (n + 1) * seg)
            cols = slice(g * gw, (g + 1) * gw)
            sv = _dot(w, v_ref[rows, cols].astype(BF16)) + bs_ref[g]
            act_s[rows, cols] = (u_s[rows, cols] * sv).astype(BF16)
    o_ref[...] = x + _dot(act_s[...], wout_ref[...])


def _gmlp(x2d, gain, win_bf, b_in, ln_g, ln_b, w_s, b_s, wout_bf, tm, seg, return_v):
    m, d = x2d.shape
    gd = wout_bf.shape[0]
    row = lambda i: (i, 0)
    ws = w_s[:, :seg, :seg]
    bs = b_s[:, :seg, None]
    out_specs = [pl.BlockSpec((tm, d), row)]
    out_shape = [jax.ShapeDtypeStruct((m, d), F32)]
    scratch = [pltpu.VMEM((tm, gd), F32), pltpu.VMEM((tm, gd), BF16)]
    if return_v:
        out_specs.append(pl.BlockSpec((tm, gd), row))
        out_shape.append(jax.ShapeDtypeStruct((m, gd), F32))
    else:
        scratch.insert(0, pltpu.VMEM((tm, gd), F32))
    outs = pl.pallas_call(
        functools.partial(_gmlp_kernel, seg=seg, cw=512),
        grid=(m // tm,),
        in_specs=[pl.BlockSpec((tm, d), row), _const_spec((1, d)), _const_spec(win_bf.shape),
                  _const_spec((1, 2 * gd)), _const_spec((1, gd)), _const_spec((1, gd)),
                  _const_spec(ws.shape), _const_spec(bs.shape), _const_spec(wout_bf.shape)],
        out_specs=out_specs, out_shape=out_shape, scratch_shapes=scratch,
        compiler_params=_params("parallel"),
        name="gmlp",
    )(x2d, gain.reshape(1, d), win_bf, b_in.reshape(1, 2 * gd), ln_g.reshape(1, gd),
      ln_b.reshape(1, gd), ws, bs, wout_bf)
    return (outs[0], outs[1]) if return_v else (outs[0], None)


def _router_kernel(x_ref, g_ref, whi_ref, wlo_ref, b_ref, meta_ref, cnt_ref):
    h = _rmsnorm(x_ref[...], g_ref[...], RMS_EPS)
    h_hi = h.astype(BF16)
    h_lo = (h - h_hi.astype(F32)).astype(BF16)
    logits = (_dot(h_hi, whi_ref[...]) + _dot(h_hi, wlo_ref[...]) + _dot(h_lo, whi_ref[...])
              + b_ref[...])
    tm = logits.shape[0]
    lane = lax.broadcasted_iota(jnp.int32, logits.shape, 1)
    logits = jnp.where(lane < N_EXPERTS, logits, -jnp.inf)
    v1 = jnp.max(logits, axis=1, keepdims=True)
    i1 = jnp.min(jnp.where(logits == v1, lane, LANES), axis=1, keepdims=True)
    rest = jnp.where(lane == i1, -jnp.inf, logits)
    v2 = jnp.max(rest, axis=1, keepdims=True)
    i2 = jnp.min(jnp.where(rest == v2, lane, LANES), axis=1, keepdims=True)
    e2 = jnp.exp(v2 - v1)
    den = 1.0 + e2

    @pl.when(pl.program_id(0) == 0)
    def _init():
        cnt_ref[...] = jnp.zeros(cnt_ref.shape, F32)

    oh1 = lane == i1
    oh2 = lane == i2
    r_i = lax.broadcasted_iota(jnp.int32, (tm, tm), 0)
    c_i = lax.broadcasted_iota(jnp.int32, (tm, tm), 1)
    before = jnp.where(c_i < r_i, 1.0, 0.0).astype(BF16)
    cum1 = _dot(before, jnp.where(oh1, 1.0, 0.0).astype(BF16))
    cum2 = _dot(before, jnp.where(oh2, 1.0, 0.0).astype(BF16))
    n1 = jnp.sum(jnp.where(oh1, 1.0, 0.0), axis=0, keepdims=True)
    n2 = jnp.sum(jnp.where(oh2, 1.0, 0.0), axis=0, keepdims=True)
    base = cnt_ref[...]
    rank1 = jnp.sum(jnp.where(oh1, base + cum1, 0.0), axis=1, keepdims=True)
    rank2 = jnp.sum(jnp.where(oh2, base + n1 + cum2, 0.0), axis=1, keepdims=True)
    cnt_ref[...] = base + n1 + n2
    meta = jnp.where(lane == 0, i1.astype(F32), 0.0)
    meta = jnp.where(lane == 1, i2.astype(F32), meta)
    meta = jnp.where(lane == 2, 1.0 / den, meta)
    meta = jnp.where(lane == 3, e2 / den, meta)
    meta = jnp.where(lane == 4, rank1, meta)
    meta = jnp.where(lane == 5, rank2, meta)
    meta_ref[...] = meta


def _router(x2d, gain, w_router, b_router, tm):
    m, d = x2d.shape
    assert 2 * m < 2 ** 24
    w_pad = jnp.pad(w_router, ((0, 0), (0, LANES - N_EXPERTS)))
    w_hi = w_pad.astype(BF16)
    w_lo = (w_pad - w_hi.astype(F32)).astype(BF16)
    b_pad = jnp.pad(b_router, (0, LANES - N_EXPERTS)).reshape(1, LANES)
    row = lambda i: (i, 0)
    return pl.pallas_call(
        _router_kernel,
        grid=(m // tm,),
        in_specs=[pl.BlockSpec((tm, d), row), _const_spec((1, d)), _const_spec((d, LANES)),
                  _const_spec((d, LANES)), _const_spec((1, LANES))],
        out_specs=[pl.BlockSpec((tm, LANES), row), pl.BlockSpec((1, LANES), lambda i: (0, 0))],
        out_shape=[jax.ShapeDtypeStruct((m, LANES), F32), jax.ShapeDtypeStruct((1, LANES), F32)],
        compiler_params=_params("arbitrary"),
        name="moe_router",
    )(x2d, gain.reshape(1, d), w_hi, w_lo, b_pad)


def _row_copy(src_hbm, src_row, dst, dst_row, sem):
    return pltpu.make_async_copy(src_hbm.at[pl.ds(src_row, 1)], dst.at[pl.ds(dst_row, 1)], sem)


ROW_DMA_UNROLL = 8


def _dispatch_kernel(ends_ref, pos1_ref, pos2_ref, x_ref, g_ref, xs_hbm, h_ref, zero_ref, sem, zsem,
                     *, tm, tmx):
    @pl.when(pl.program_id(0) == 0)
    def _zero_padding():
        zero_ref[...] = jnp.zeros(zero_ref.shape, zero_ref.dtype)

        def fill(row0):
            cp = pltpu.make_async_copy(zero_ref, xs_hbm.at[pl.ds(row0, tmx)], zsem)
            cp.start()
            cp.wait()

        for e in range(N_EXPERTS):
            start_e = ends_ref[e - 1] if e else 0

            @pl.when(ends_ref[e] > start_e)
            def _():
                fill(pl.multiple_of(ends_ref[e] - tmx, tmx))

        def tail(t, carry):
            fill(pl.multiple_of(t * tmx, tmx))
            return carry

        lax.fori_loop(ends_ref[N_EXPERTS - 1] // tmx, xs_hbm.shape[0] // tmx, tail, 0)

    i = pl.program_id(0)
    slot = i % 2
    h_ref.at[slot][...] = _rmsnorm(x_ref[...], g_ref[...], RMS_EPS)

    def start(r, carry):
        _row_copy(h_ref.at[slot], r, xs_hbm, pos1_ref[r], sem.at[slot]).start()
        _row_copy(h_ref.at[slot], r, xs_hbm, pos2_ref[r], sem.at[slot]).start(priority=1)
        return carry

    def wait_slot(sl):
        def wait(r, carry):
            _row_copy(h_ref.at[sl], 0, xs_hbm, 0, sem.at[sl]).wait()
            _row_copy(h_ref.at[sl], 0, xs_hbm, 0, sem.at[sl]).wait()
            return carry
        lax.fori_loop(0, tm, wait, 0, unroll=ROW_DMA_UNROLL)

    lax.fori_loop(0, tm, start, 0, unroll=ROW_DMA_UNROLL)

    @pl.when(i > 0)
    def _previous():
        wait_slot(1 - slot)

    @pl.when(i == pl.num_programs(0) - 1)
    def _own():
        wait_slot(slot)


def _dispatch(x2d, gain, pos1, pos2, ends, n_slots, tm, tmx):
    m, d = x2d.shape
    smem = lambda: pl.BlockSpec((tm,), lambda i, ends: (i,), memory_space=pltpu.SMEM)
    grid_spec = pltpu.PrefetchScalarGridSpec(
        num_scalar_prefetch=1,
        grid=(m // tm,),
        in_specs=[smem(), smem(), pl.BlockSpec((tm, d), lambda i, ends: (i, 0)),
                  pl.BlockSpec((1, d), lambda i, ends: (0, 0))],
        out_specs=pl.BlockSpec(memory_space=pl.ANY),
        scratch_shapes=[pltpu.VMEM((2, tm, d), F32), pltpu.VMEM((tmx, d), F32),
                        pltpu.SemaphoreType.DMA((2,)), pltpu.SemaphoreType.DMA(())],
    )
    return pl.pallas_call(
        functools.partial(_dispatch_kernel, tm=tm, tmx=tmx),
        grid_spec=grid_spec,
        out_shape=jax.ShapeDtypeStruct((n_slots, d), F32),
        compiler_params=_params("arbitrary"),
        name="moe_dispatch",
    )(ends, pos1, pos2, x2d, gain.reshape(1, d))


def _expert_ffn_kernel(te_ref, nt_ref, xs_ref, wgu_ref, wd_ref, ys_ref, *, d_ff, tf):
    @pl.when(pl.program_id(0) < nt_ref[0])
    def _():
        ys_ref[...] = _swiglu(xs_ref[...].astype(BF16), wgu_ref, wd_ref, d_ff, tf)

    @pl.when(pl.program_id(0) >= nt_ref[0])
    def _():
        ys_ref[...] = jnp.zeros(ys_ref.shape, F32)


def _expert_ffn(xs, tile_expert, n_tiles, wgu_bf, wd_bf, tmx):
    n_slots, d = xs.shape
    _, d_ff, _ = wd_bf.shape
    row = lambda i, te, nt: (jnp.minimum(i, nt[0] - 1), 0)
    grid_spec = pltpu.PrefetchScalarGridSpec(
        num_scalar_prefetch=2,
        grid=(n_slots // tmx,),
        in_specs=[pl.BlockSpec((tmx, d), row),
                  pl.BlockSpec((None, d, 2 * d_ff), lambda i, te, nt: (te[i], 0, 0)),
                  pl.BlockSpec((None, d_ff, d), lambda i, te, nt: (te[i], 0, 0))],
        out_specs=pl.BlockSpec((tmx, d), lambda i, te, nt: (i, 0)),
    )
    return pl.pallas_call(
        functools.partial(_expert_ffn_kernel, d_ff=d_ff, tf=256),
        grid_spec=grid_spec,
        out_shape=jax.ShapeDtypeStruct((n_slots, d), F32),
        compiler_params=_params("arbitrary"),
        name="moe_experts",
    )(tile_expert, n_tiles, xs, wgu_bf, wd_bf)


def _combine_kernel(pos1_ref, pos2_ref, nxt1_ref, nxt2_ref, ys_hbm, x_ref, meta_ref, gf_ref, o_ref,
                    y1_ref, y2_ref, sem, *, tm):
    i = pl.program_id(0)
    slot = i % 2

    def gather(p1_ref, p2_ref, sl):
        def start(r, carry):
            _row_copy(ys_hbm, p1_ref[r], y1_ref.at[sl], r, sem.at[sl]).start()
            _row_copy(ys_hbm, p2_ref[r], y2_ref.at[sl], r, sem.at[sl]).start(priority=1)
            return carry
        lax.fori_loop(0, tm, start, 0, unroll=ROW_DMA_UNROLL)

    @pl.when(i == 0)
    def _first():
        gather(pos1_ref, pos2_ref, 0)

    @pl.when(i + 1 < pl.num_programs(0))
    def _next():
        gather(nxt1_ref, nxt2_ref, 1 - slot)

    def wait(r, carry):
        _row_copy(ys_hbm, 0, y1_ref.at[slot], 0, sem.at[slot]).wait()
        _row_copy(ys_hbm, 0, y2_ref.at[slot], 0, sem.at[slot]).wait()
        return carry

    lax.fori_loop(0, tm, wait, 0, unroll=ROW_DMA_UNROLL)
    meta = meta_ref[...]
    out = meta[:, 2:3] * y1_ref[slot] + meta[:, 3:4] * y2_ref[slot]
    o_ref[...] = _rmsnorm(x_ref[...] + out, gf_ref[...], RMS_EPS)


def _combine(ys, pos1, pos2, x2d, meta, gain_final, tm):
    m, d = x2d.shape
    last = m // tm - 1
    smem = lambda off: pl.BlockSpec((tm,), lambda i: (jnp.minimum(i + off, last),), memory_space=pltpu.SMEM)
    row = lambda i: (i, 0)
    return pl.pallas_call(
        functools.partial(_combine_kernel, tm=tm),
        grid=(m // tm,),
        in_specs=[smem(0), smem(0), smem(1), smem(1), pl.BlockSpec(memory_space=pl.ANY),
                  pl.BlockSpec((tm, d), row), pl.BlockSpec((tm, LANES), row), _const_spec((1, d))],
        out_specs=pl.BlockSpec((tm, d), row),
        out_shape=jax.ShapeDtypeStruct((m, d), F32),
        scratch_shapes=[pltpu.VMEM((2, tm, d), F32), pltpu.VMEM((2, tm, d), F32),
                        pltpu.SemaphoreType.DMA((2,))],
        compiler_params=_params("arbitrary"),
        name="moe_combine",
    )(pos1, pos2, pos1, pos2, ys, x2d, meta, gain_final.reshape(1, d))


def _moe(x2d, norm_gain, w_router, b_router, wgu_bf, wd_bf, gain_final, tm):
    m, d = x2d.shape
    tmx = 512 if m >= 4096 else 128
    meta, counts = _router(x2d, norm_gain, w_router, b_router, tm)
    counts = counts[0, :N_EXPERTS].astype(jnp.int32)
    padded = (counts + tmx - 1) // tmx * tmx
    ends = jnp.cumsum(padded)
    starts = ends - padded
    n_slots = 2 * m + N_EXPERTS * tmx
    n_tiles_max = n_slots // tmx
    tile_start = jnp.arange(n_tiles_max, dtype=jnp.int32) * tmx
    tile_expert = jnp.minimum(jnp.sum((tile_start[:, None] >= ends[None, :]).astype(jnp.int32), axis=1),
                              N_EXPERTS - 1).astype(jnp.int32)
    n_tiles = (ends[-1:] // tmx).astype(jnp.int32)
    experts = jnp.arange(N_EXPERTS, dtype=jnp.int32)[None, :]
    start_of = lambda e: jnp.sum(jnp.where(e[:, None] == experts, starts[None, :], 0), axis=1)
    e1, e2 = meta[:, 0].astype(jnp.int32), meta[:, 1].astype(jnp.int32)
    pos1 = start_of(e1) + meta[:, 4].astype(jnp.int32)
    pos2 = start_of(e2) + meta[:, 5].astype(jnp.int32)
    td = min(512, m)
    xs = _dispatch(x2d, norm_gain, pos1, pos2, ends.astype(jnp.int32), n_slots, td, tmx)
    ys = _expert_ffn(xs, tile_expert, n_tiles, wgu_bf, wd_bf, tmx)
    return _combine(ys, pos1, pos2, x2d, meta, gain_final, td)


def _forward(x, caches, w):
    b, t, d = x.shape
    m = b * t
    tm = min(512, m)
    x2d = x.reshape(m, d)
    lambda_init = 0.8 - 0.6 * math.exp(-0.3 * 0)
    width = DIFF_HEADS * 2 * DIFF_HEAD_DIM
    ncb = width // LANES

    if caches is None:
        pbf, ka, va, kb, vb, vt, kaug = _qkv_proj(x2d, w["norm_attn"], w["attn_w_in"], tm, seq=t)
    else:
        pbf, ka, va, kb, vb = _qkv_proj(x2d, w["norm_attn"], w["attn_w_in"], tm)
    pbf3 = pbf.reshape(b, t, 6 * width)
    if caches is None:
        oa = _diff_attn_prompt(pbf3, kaug.reshape(b, t, 2 * width), vt, 0, w["diff_lambda"],
                               w["diff_subln"], tile=min(1024, t), lambda_init=lambda_init)
        tqb = min(256, t)
        n_parts = BAND_PAST_MAX // tqb + 1
        bias = _band_bias(w["band_rel_bias"], 0, -BAND_PAST_MAX, tqb, n_parts * tqb, key_major=True)
        ob = _band_attn_prompt(pbf3, vt, 3, 4, 1, bias, tq=tqb, n_parts=n_parts, tkp=tqb)
        n_band = min(BAND_PAST_MAX, t)
        tail = lambda a: a.reshape(b, t, width)[:, t - n_band:].reshape(b, n_band, BAND_HEADS, BAND_HEAD_DIM)
        new_bk, new_bv = tail(kb), tail(vb)
    else:
        ck_a, cv_a, ck_b, cv_b = caches
        p_len, pb_len = ck_a.shape[1], ck_b.shape[1]
        cat = lambda c, lo: jnp.concatenate(
            [c.reshape(b, c.shape[1], width).astype(BF16), pbf3[:, :, lo:lo + width]], axis=1)
        k_a, v_a = cat(ck_a, width), cat(cv_a, 2 * width)
        k_b, v_b = cat(ck_b, 4 * width), cat(cv_b, 5 * width)
        oa = _diff_attn(pbf3, k_a, v_a, 0, 0, 0, w["diff_lambda"], w["diff_subln"],
                        tq=t, tk=p_len + t, q_off=p_len, lambda_init=lambda_init)
        bias = _band_bias(w["band_rel_bias"], p_len, p_len - pb_len, t, pb_len + t)
        ob = _band_attn(pbf3, k_b, v_b, 3 * ncb, 0, 0, bias, tq=t, n_parts=1, tkp=pb_len + t)
        new_bk = kb.reshape(b, t, BAND_HEADS, BAND_HEAD_DIM)
        new_bv = vb.reshape(b, t, BAND_HEADS, BAND_HEAD_DIM)
    new_dk = ka.reshape(b, t, 2 * DIFF_HEADS, DIFF_HEAD_DIM)
    new_dv = va.reshape(b, t, DIFF_HEADS, 2 * DIFF_HEAD_DIM)

    x2d = _attn_out_ffn(oa.reshape(m, width), ob.reshape(m, width), x2d, w["attn_w_out"],
                        w["norm_ffn"], w["ffn_w_gu"], w["ffn_w_down"], tm)

    seg = min(t, GMLP_CHUNK)
    x2d, v_rows = _gmlp(x2d, w["norm_gmlp"], w["gmlp_w_in"], w["gmlp_b_in"], w["gmlp_ln_g"],
                        w["gmlp_ln_b"], w["gmlp_w_s"], w["gmlp_b_s"], w["gmlp_w_out"], tm, seg,
                        return_v=caches is not None)
    y = _moe(x2d, w["norm_moe"], w["moe_w_router"], w["moe_b_router"], w["moe_w_gu"], w["moe_w_down"],
             w["norm_final"], tm)
    new_gv = None if v_rows is None else v_rows.reshape(b, t, -1)[None]
    return (y.reshape(b, t, d), new_dk[None], new_dv[None], new_bk[None], new_bv[None], new_gv)


def kernel(x_prompt, x_sample, cache_diff_k, cache_diff_v, cache_band_k, cache_band_v,
           norm_attn, attn_w_in, diff_lambda, diff_subln, band_rel_bias, attn_w_out,
           norm_ffn, ffn_w_gu, ffn_w_down,
           norm_gmlp, gmlp_w_in, gmlp_b_in, gmlp_ln_g, gmlp_ln_b, gmlp_w_s, gmlp_b_s, gmlp_w_out,
           norm_moe, moe_w_router, moe_b_router, moe_w_gu, moe_w_down, norm_final):
    w = {
        "norm_attn": norm_attn[0], "attn_w_in": attn_w_in[0].astype(BF16),
        "diff_lambda": diff_lambda[0], "diff_subln": diff_subln[0],
        "band_rel_bias": band_rel_bias[0], "attn_w_out": attn_w_out[0].astype(BF16),
        "norm_ffn": norm_ffn[0], "ffn_w_gu": ffn_w_gu[0].astype(BF16),
        "ffn_w_down": ffn_w_down[0].astype(BF16),
        "norm_gmlp": norm_gmlp[0], "gmlp_w_in": gmlp_w_in[0].astype(BF16),
        "gmlp_b_in": gmlp_b_in[0], "gmlp_ln_g": gmlp_ln_g[0], "gmlp_ln_b": gmlp_ln_b[0],
        "gmlp_w_s": gmlp_w_s[0], "gmlp_b_s": gmlp_b_s[0], "gmlp_w_out": gmlp_w_out[0].astype(BF16),
        "norm_moe": norm_moe[0], "moe_w_router": moe_w_router[0], "moe_b_router": moe_b_router[0],
        "moe_w_gu": moe_w_gu[0].astype(BF16), "moe_w_down": moe_w_down[0].astype(BF16),
        "norm_final": norm_final,
    }
    y_p, dk_p, dv_p, bk_p, bv_p, _ = _forward(x_prompt, None, w)
    y_s, dk_s, dv_s, bk_s, bv_s, gv_s = _forward(
        x_sample, (cache_diff_k[0], cache_diff_v[0], cache_band_k[0], cache_band_v[0]), w)
    return (y_p, y_s, dk_p, dv_p, bk_p, bv_p, dk_s, dv_s, bk_s, bv_s, gv_s)
```
